```python
import jax
import jax.numpy as jnp
from jax import lax
import numpy as np

D_MODEL = 2048
BATCH = 4
SEQ = 2048
DEPTH = 1
DEC_BATCH = 128
DEC_SEQ = 8
PAST_LEN = 8192
PAGE_SIZE = 128

RW_HEADS = 12
RW_HEAD_DIM = 64
RW_WIDTH = RW_HEADS * RW_HEAD_DIM
RW_DECAY_RANK = 64
RW_A_RANK = 64
RW_GATE_RANK = 128
RW_PROJ = 3 * RW_WIDTH + RW_DECAY_RANK + RW_A_RANK + RW_GATE_RANK
RW_GN_EPS = 6.4e-4
SWA_Q_HEADS = 12
SWA_KV_HEADS = 4
SWA_GROUP = SWA_Q_HEADS // SWA_KV_HEADS
SWA_HEAD_DIM = 64
SWA_Q_WIDTH = SWA_Q_HEADS * SWA_HEAD_DIM
SWA_KV_WIDTH = SWA_KV_HEADS * SWA_HEAD_DIM
WINDOW = 128
MEM_TOKENS = 256
MEM_HEADS = 4
MEM_HEAD_DIM = 128
MEM_WIDTH = MEM_HEADS * MEM_HEAD_DIM
N_BRANCH = 3
P_IN = RW_PROJ + SWA_Q_WIDTH + 2 * SWA_KV_WIDTH + MEM_WIDTH + N_BRANCH * D_MODEL
D_FF = 5632
CONV_W = 3
NORM_EPS = 1e-6

kernel_name = 'hybrid_rwkv7_swa_sink_mem_convglu_step'


def _rmsnorm(x, g):
    xf = x.astype(jnp.float32)
    y = xf * lax.rsqrt(jnp.mean(xf * xf, axis=-1, keepdims=True) + NORM_EPS)
    return (y * g.astype(jnp.float32)).astype(x.dtype)


def _wkv7_step(S, inp):
    r, w, k, v, kk, a = inp
    sa = jnp.einsum('bhvk,bhk->bhv', S, -kk)
    S = S * w[:, :, None, :] + sa[..., None] * (kk * a)[:, :, None, :] + v[..., None] * k[:, :, None, :]
    return S, jnp.einsum('bhvk,bhk->bhv', S, r)


def _rwkv7(p, prev_row, s0, prm):
    b, t, _ = p.shape
    f32 = jnp.float32
    prev = jnp.concatenate([prev_row[:, None, :].astype(p.dtype), p[:, :-1]], axis=1)
    s = p + prm['mu_rwkv'] * (prev - p)
    o1 = RW_WIDTH
    o2 = 2 * RW_WIDTH
    o3 = 3 * RW_WIDTH
    o4 = o3 + RW_DECAY_RANK
    o5 = o4 + RW_A_RANK
    r, k, v, wl, al, gl = jnp.split(s, [o1, o2, o3, o4, o5], axis=-1)
    w_log = -jax.nn.softplus(-(prm['w0_decay'] + jnp.tanh(wl) @ prm['w_decay_up']).astype(f32)) - 0.5
    decay = jnp.exp(-jnp.exp(w_log))
    a = jax.nn.sigmoid((prm['a0'] + al @ prm['w_a_up']).astype(f32))
    g = jax.nn.sigmoid(gl) @ prm['w_gate_up']
    hn = (RW_HEADS, RW_HEAD_DIM)
    heads = lambda z: z.astype(f32).reshape(b, t, RW_HEADS, RW_HEAD_DIM)
    r, k, v, decay, a = heads(r), heads(k), heads(v), heads(decay), heads(a)
    kk = k * prm['k_k'].astype(f32).reshape(hn)
    kk = kk / jnp.maximum(jnp.sqrt(jnp.sum(kk * kk, axis=-1, keepdims=True)), 1e-12)
    k = k * (1.0 + (a - 1.0) * prm['k_a'].astype(f32).reshape(hn))
    xs = tuple(jnp.swapaxes(z, 0, 1) for z in (r, decay, k, v, kk, a))
    s_fin, o = lax.scan(_wkv7_step, s0.astype(f32), xs)
    o = jnp.swapaxes(o, 0, 1)
    mean = jnp.mean(o, axis=-1, keepdims=True)
    var = jnp.mean(jnp.square(o - mean), axis=-1, keepdims=True)
    o = (o - mean) * lax.rsqrt(var + RW_GN_EPS)
    o = o.reshape(b, t, RW_WIDTH) * prm['ln_x_w'].astype(f32) + prm['ln_x_b'].astype(f32)
    bonus = jnp.sum(r * k * prm['r_k'].astype(f32), axis=-1, keepdims=True) * v
    o = (o + bonus.reshape(b, t, RW_WIDTH)) * g.astype(f32)
    return o.astype(p.dtype), p[:, -1], s_fin


def _sink_softmax(s, mask, sinks):
    s = jnp.where(mask, s, -jnp.inf)
    sk = sinks.astype(jnp.float32)[:, :, None, None]
    m = jnp.maximum(jnp.max(s, axis=-1, keepdims=True), sk)
    e = jnp.exp(s - m)
    return e / (jnp.sum(e, axis=-1, keepdims=True) + jnp.exp(sk - m))


def _swa_prompt(q, k, v, sinks):
    b, t = q.shape[:2]
    nb = t // WINDOW
    qb = q.reshape(b, nb, WINDOW, SWA_KV_HEADS, SWA_GROUP, SWA_HEAD_DIM)
    pad = jnp.zeros((b, WINDOW, SWA_KV_HEADS, SWA_HEAD_DIM), k.dtype)
    kb = jnp.concatenate([pad, k], axis=1).reshape(b, nb + 1, WINDOW, SWA_KV_HEADS, SWA_HEAD_DIM)
    vb = jnp.concatenate([pad, v], axis=1).reshape(b, nb + 1, WINDOW, SWA_KV_HEADS, SWA_HEAD_DIM)
    kw = jnp.concatenate([kb[:, :-1], kb[:, 1:]], axis=2)
    vw = jnp.concatenate([vb[:, :-1], vb[:, 1:]], axis=2)
    s = jnp.einsum('bnqhgd,bnkhd->bnhgqk', qb, kw, preferred_element_type=jnp.float32) * (SWA_HEAD_DIM ** -0.5)
    i = jnp.arange(WINDOW)[:, None]
    j = jnp.arange(2 * WINDOW)[None, :]
    blk = jnp.arange(nb)[:, None, None]
    mask = (j > i) & (j <= i + WINDOW) & (blk * WINDOW + j >= WINDOW)
    pr = _sink_softmax(s, mask[None, :, None, None], sinks.reshape(SWA_KV_HEADS, SWA_GROUP))
    o = jnp.einsum('bnhgqk,bnkhd->bnqhgd', pr.astype(v.dtype), vw)
    keep = min(WINDOW, t)
    return o.reshape(b, t, SWA_Q_WIDTH), k[:, t - keep:], v[:, t - keep:]


def _swa_step(q, k, v, ck, cv, sinks):
    b, t = q.shape[:2]
    wb = ck.shape[1]
    kf = jnp.concatenate([ck.astype(k.dtype), k], axis=1)
    vf = jnp.concatenate([cv.astype(v.dtype), v], axis=1)
    qg = q.reshape(b, t, SWA_KV_HEADS, SWA_GROUP, SWA_HEAD_DIM)
    s = jnp.einsum('bqhgd,bkhd->bhgqk', qg, kf, preferred_element_type=jnp.float32) * (SWA_HEAD_DIM ** -0.5)
    rel = jnp.arange(t)[:, None] + wb - jnp.arange(wb + t)[None, :]
    mask = (rel >= 0) & (rel < WINDOW)
    pr = _sink_softmax(s, mask, sinks.reshape(SWA_KV_HEADS, SWA_GROUP))
    o = jnp.einsum('bhgqk,bkhd->bqhgd', pr.astype(vf.dtype), vf)
    return o.reshape(b, t, SWA_Q_WIDTH), kf[:, -wb:], vf[:, -wb:]


def _mem_kv(mem, g_mem, w_mem_kv):
    b, m, _ = mem.shape
    kv = _rmsnorm(mem, g_mem) @ w_mem_kv
    mk, mv = jnp.split(kv, 2, axis=-1)
    return mk.reshape(b, m, MEM_HEADS, MEM_HEAD_DIM), mv.reshape(b, m, MEM_HEADS, MEM_HEAD_DIM)


def _mem_attend(q, mk, mv):
    b, t = q.shape[:2]
    qh = q.reshape(b, t, MEM_HEADS, MEM_HEAD_DIM)
    s = jnp.einsum('bthd,bmhd->bhtm', qh, mk.astype(q.dtype), preferred_element_type=jnp.float32) * (MEM_HEAD_DIM ** -0.5)
    pr = jax.nn.softmax(s, axis=-1)
    o = jnp.einsum('bhtm,bmhd->bthd', pr.astype(q.dtype), mv.astype(q.dtype))
    return o.reshape(b, t, MEM_WIDTH)


def _layer(x, swa_fn, mem_k, mem_v, shift_prev, wkv_prev, conv_prev, prm):
    b, t, _ = x.shape
    xn = _rmsnorm(x, prm['g_pre_mix'])
    p = xn @ prm['w_in']
    o1 = RW_PROJ
    o2 = o1 + SWA_Q_WIDTH
    o3 = o2 + SWA_KV_WIDTH
    o4 = o3 + SWA_KV_WIDTH
    o5 = o4 + MEM_WIDTH
    p_rw, q_sw, k_sw, v_sw, q_mem, p_gate = jnp.split(p, [o1, o2, o3, o4, o5], axis=-1)
    o_rw, shift_new, wkv_new = _rwkv7(p_rw, shift_prev, wkv_prev, prm)
    k_sw = k_sw.reshape(b, t, SWA_KV_HEADS, SWA_HEAD_DIM)
    v_sw = v_sw.reshape(b, t, SWA_KV_HEADS, SWA_HEAD_DIM)
    o_sw, kbuf, vbuf = swa_fn(q_sw, k_sw, v_sw)
    o_mem = _mem_attend(q_mem, mem_k, mem_v)
    gates = jax.nn.sigmoid(p_gate.astype(jnp.float32)).astype(x.dtype).reshape(b, t, N_BRANCH, D_MODEL)
    merged = (gates[:, :, 0] * (o_rw @ prm['w_br_rwkv'])
              + gates[:, :, 1] * (o_sw @ prm['w_br_swa'])
              + gates[:, :, 2] * (o_mem @ prm['w_br_mem']))
    h = x + _rmsnorm(merged @ prm['w_o'], prm['g_post_mix'])
    hn = _rmsnorm(h, prm['g_pre_ffn'])
    z_gate, z_val = jnp.split(hn @ prm['w_ffn_up'], 2, axis=-1)
    full = jnp.concatenate([conv_prev.astype(z_gate.dtype), z_gate], axis=1)
    conv = prm['conv_b']
    for j in range(CONV_W):
        conv = conv + prm['conv_w'][j] * full[:, j:j + t]
    f = (jax.nn.gelu(conv) * z_val) @ prm['w_ffn_down']
    y = h + _rmsnorm(f, prm['g_post_ffn'])
    return y, kbuf, vbuf, shift_new, wkv_new, full[:, -(CONV_W - 1):]


def setup_inputs(seed: int = 0) -> dict:
    key = jax.random.key(seed)
    ks = iter(jax.random.split(key, 48))
    f32 = jnp.float32
    nrm = lambda shape, scale: jax.random.normal(next(ks), shape, f32) * scale
    swa_buf = min(WINDOW, PAST_LEN)
    return {
        'x_prompt': nrm((BATCH, SEQ, D_MODEL), 1.0),
        'x_sample': nrm((DEC_BATCH, DEC_SEQ, D_MODEL), 1.0),
        'cache_swa_k': nrm((DEC_BATCH, swa_buf, SWA_KV_HEADS, SWA_HEAD_DIM), 1.0),
        'cache_swa_v': nrm((DEC_BATCH, swa_buf, SWA_KV_HEADS, SWA_HEAD_DIM), 1.0),
        'cache_mem_k': nrm((DEC_BATCH, MEM_TOKENS, MEM_HEADS, MEM_HEAD_DIM), 1.0),
        'cache_mem_v': nrm((DEC_BATCH, MEM_TOKENS, MEM_HEADS, MEM_HEAD_DIM), 1.0),
        'state_rwkv_shift': nrm((DEC_BATCH, RW_PROJ), 1.0),
        'state_rwkv_wkv': nrm((DEC_BATCH, RW_HEADS, RW_HEAD_DIM, RW_HEAD_DIM), 0.3),
        'state_ffn_conv': nrm((DEC_BATCH, CONV_W - 1, D_FF), 1.0),
        'mem_prompt': nrm((BATCH, MEM_TOKENS, D_MODEL), 1.0),
        'g_pre_mix': 1.0 + nrm((D_MODEL,), 0.05),
        'w_in': nrm((D_MODEL, P_IN), D_MODEL ** -0.5),
        'mu_rwkv': jax.random.uniform(next(ks), (RW_PROJ,), f32, 0.1, 0.9),
        'w_decay_up': nrm((RW_DECAY_RANK, RW_WIDTH), 0.5 * RW_DECAY_RANK ** -0.5),
        'w0_decay': nrm((RW_WIDTH,), 0.5),
        'w_a_up': nrm((RW_A_RANK, RW_WIDTH), RW_A_RANK ** -0.5),
        'a0': nrm((RW_WIDTH,), 0.3),
        'w_gate_up': nrm((RW_GATE_RANK, RW_WIDTH), RW_GATE_RANK ** -0.5),
        'k_k': 0.85 + nrm((RW_WIDTH,), 0.05),
        'k_a': 1.0 + nrm((RW_WIDTH,), 0.05),
        'r_k': nrm((RW_HEADS, RW_HEAD_DIM), 0.1),
        'ln_x_w': 1.0 + nrm((RW_WIDTH,), 0.05),
        'ln_x_b': nrm((RW_WIDTH,), 0.02),
        'swa_sinks': nrm((SWA_Q_HEADS,), 0.5),
        'g_mem': 1.0 + nrm((D_MODEL,), 0.05),
        'w_mem_kv': nrm((D_MODEL, 2 * MEM_WIDTH), D_MODEL ** -0.5),
        'w_br_rwkv': nrm((RW_WIDTH, D_MODEL), RW_WIDTH ** -0.5),
        'w_br_swa': nrm((SWA_Q_WIDTH, D_MODEL), SWA_Q_WIDTH ** -0.5),
        'w_br_mem': nrm((MEM_WIDTH, D_MODEL), MEM_WIDTH ** -0.5),
        'w_o': nrm((D_MODEL, D_MODEL), D_MODEL ** -0.5),
        'g_post_mix': 1.0 + nrm((D_MODEL,), 0.05),
        'g_pre_ffn': 1.0 + nrm((D_MODEL,), 0.05),
        'w_ffn_up': nrm((D_MODEL, 2 * D_FF), D_MODEL ** -0.5),
        'conv_w': nrm((CONV_W, D_FF), CONV_W ** -0.5),
        'conv_b': nrm((D_FF,), 0.02),
        'w_ffn_down': nrm((D_FF, D_MODEL), D_FF ** -0.5),
        'g_post_ffn': 1.0 + nrm((D_MODEL,), 0.05),
    }


def reference(x_prompt, x_sample, cache_swa_k, cache_swa_v, cache_mem_k, cache_mem_v,
              state_rwkv_shift, state_rwkv_wkv, state_ffn_conv, mem_prompt,
              g_pre_mix, w_in, mu_rwkv, w_decay_up, w0_decay, w_a_up, a0, w_gate_up,
              k_k, k_a, r_k, ln_x_w, ln_x_b, swa_sinks, g_mem, w_mem_kv,
              w_br_rwkv, w_br_swa, w_br_mem, w_o, g_post_mix, g_pre_ffn,
              w_ffn_up, conv_w, conv_b, w_ffn_down, g_post_ffn):
    prm = dict(g_pre_mix=g_pre_mix, w_in=w_in, mu_rwkv=mu_rwkv, w_decay_up=w_decay_up,
               w0_decay=w0_decay, w_a_up=w_a_up, a0=a0, w_gate_up=w_gate_up, k_k=k_k, k_a=k_a,
               r_k=r_k, ln_x_w=ln_x_w, ln_x_b=ln_x_b, w_br_rwkv=w_br_rwkv, w_br_swa=w_br_swa,
               w_br_mem=w_br_mem, w_o=w_o, g_post_mix=g_post_mix, g_pre_ffn=g_pre_ffn,
               w_ffn_up=w_ffn_up, conv_w=conv_w, conv_b=conv_b, w_ffn_down=w_ffn_down,
               g_post_ffn=g_post_ffn)
    bp = x_prompt.shape[0]
    dt = x_prompt.dtype
    swa_prompt_fn = lambda q, k, v: _swa_prompt(q, k, v, swa_sinks)
    swa_sample_fn = lambda q, k, v: _swa_step(q, k, v, cache_swa_k, cache_swa_v, swa_sinks)
    y_p = x_prompt
    y_s = x_sample
    for _ in range(DEPTH):
        mem_k_p, mem_v_p = _mem_kv(mem_prompt, g_mem, w_mem_kv)
        y_p, swa_k_p, swa_v_p, shift_p, wkv_p, conv_p = _layer(
            y_p, swa_prompt_fn, mem_k_p, mem_v_p,
            jnp.zeros((bp, RW_PROJ), dt),
            jnp.zeros((bp, RW_HEADS, RW_HEAD_DIM, RW_HEAD_DIM), jnp.float32),
            jnp.zeros((bp, CONV_W - 1, D_FF), dt), prm)
        y_s, swa_k_s, swa_v_s, shift_s, wkv_s, conv_s = _layer(
            y_s, swa_sample_fn, cache_mem_k, cache_mem_v,
            state_rwkv_shift, state_rwkv_wkv, state_ffn_conv, prm)
    return (y_p, y_s, swa_k_p, swa_v_p, mem_k_p, mem_v_p, shift_p, wkv_p, conv_p,
            swa_k_s, swa_v_s, shift_s, wkv_s, conv_s)
```

```python
import functools
import math

import jax
import jax.numpy as jnp
from jax import lax
from jax.experimental import pallas as pl
from jax.experimental.pallas import tpu as pltpu

F32 = jnp.float32
BF16 = jnp.bfloat16

RW_HEADS = 12
RW_HEAD_DIM = 64
RW_WIDTH = RW_HEADS * RW_HEAD_DIM
RW_DECAY_RANK = 64
RW_A_RANK = 64
RW_GATE_RANK = 128
RW_PROJ = 3 * RW_WIDTH + RW_DECAY_RANK + RW_A_RANK + RW_GATE_RANK
RW_GN_EPS = 6.4e-4
RW_PAIRS = RW_HEADS // 2
SWA_Q_HEADS = 12
SWA_KV_HEADS = 4
SWA_GROUP = SWA_Q_HEADS // SWA_KV_HEADS
SWA_HEAD_DIM = 64
SWA_Q_WIDTH = SWA_Q_HEADS * SWA_HEAD_DIM
SWA_KV_WIDTH = SWA_KV_HEADS * SWA_HEAD_DIM
WINDOW = 128
MEM_HEADS = 4
MEM_HEAD_DIM = 128
MEM_WIDTH = MEM_HEADS * MEM_HEAD_DIM
N_BRANCH = 3
CONV_W = 3
NORM_EPS = 1e-6
ATT_WIDTH = SWA_Q_WIDTH + 2 * SWA_KV_WIDTH + MEM_WIDTH

LANES = 128
MXU_COLS = 256
VMEM_LIMIT_BYTES = 56 * 1024 * 1024

RWKV_ROWS = 64
ROW_TILE = 1024
NORM_ROW_TILE = 512


def _params(*sem):
    return pltpu.CompilerParams(dimension_semantics=sem, vmem_limit_bytes=VMEM_LIMIT_BYTES)


def _row_tile(n_p, n_s, target):
    t = math.gcd(math.gcd(n_p, n_s), target)
    assert t % 16 == 0, (n_p, n_s, target)
    return t


def _div_pow2(x, n):
    assert n & (n - 1) == 0
    return lax.shift_right_logical(x, n.bit_length() - 1)


def _mod_pow2(x, n):
    assert n & (n - 1) == 0
    return jnp.bitwise_and(x, n - 1)


def _rms(x, g):
    ms = jnp.mean(x * x, axis=-1, keepdims=True)
    return x * lax.rsqrt(ms + NORM_EPS) * g


def _norm2_kernel(xp_ref, xs_ref, g_ref, o_ref, *, n_p):
    i = pl.program_id(0)

    @pl.when(i < n_p)
    def _():
        o_ref[...] = _rms(xp_ref[...], g_ref[...]).astype(o_ref.dtype)

    @pl.when(i >= n_p)
    def _():
        o_ref[...] = _rms(xs_ref[...], g_ref[...]).astype(o_ref.dtype)


def _norm_rows2(xp, xs, g):
    n_rows_p, d = xp.shape
    n_rows_s = xs.shape[0]
    tm = _row_tile(n_rows_p, n_rows_s, NORM_ROW_TILE)
    n_p, n_s = n_rows_p // tm, n_rows_s // tm
    return pl.pallas_call(
        functools.partial(_norm2_kernel, n_p=n_p),
        grid=(n_p + n_s,),
        in_specs=[
            pl.BlockSpec((tm, d), lambda i: (jnp.minimum(i, n_p - 1), 0)),
            pl.BlockSpec((tm, d), lambda i: (jnp.maximum(i - n_p, 0), 0)),
            pl.BlockSpec((1, d), lambda i: (0, 0)),
        ],
        out_specs=pl.BlockSpec((tm, d), lambda i: (i, 0)),
        out_shape=jax.ShapeDtypeStruct((n_rows_p + n_rows_s, d), BF16),
        compiler_params=_params("arbitrary"),
        name="norm_rows",
    )(xp, xs, g.reshape(1, d))


def _norm1_kernel(x_ref, g_ref, o_ref):
    o_ref[...] = _rms(x_ref[...], g_ref[...]).astype(o_ref.dtype)


def _norm_rows1(x, g):
    n, d = x.shape
    tm = math.gcd(n, NORM_ROW_TILE)
    return pl.pallas_call(
        _norm1_kernel,
        grid=(n // tm,),
        in_specs=[pl.BlockSpec((tm, d), lambda i: (i, 0)), pl.BlockSpec((1, d), lambda i: (0, 0))],
        out_specs=pl.BlockSpec((tm, d), lambda i: (i, 0)),
        out_shape=jax.ShapeDtypeStruct((n, d), BF16),
        compiler_params=_params("arbitrary"),
        name="norm_mem",
    )(x, g.reshape(1, d))


def _mm_kernel(a_ref, w_ref, o_ref, wb_ref):
    @pl.when(pl.program_id(1) == 0)
    def _():
        wb_ref[...] = w_ref[...].astype(BF16)

    o_ref[...] = jnp.dot(a_ref[...], wb_ref[...], preferred_element_type=F32)


def _matmul(a, w, col0, n_cols, tn, name):
    m, k = a.shape
    tm = math.gcd(m, ROW_TILE)
    assert n_cols % tn == 0 and col0 % tn == 0
    j0 = col0 // tn
    return pl.pallas_call(
        _mm_kernel,
        grid=(n_cols // tn, m // tm),
        in_specs=[
            pl.BlockSpec((tm, k), lambda j, i: (i, 0)),
            pl.BlockSpec((k, tn), lambda j, i: (0, j0 + j)),
        ],
        out_specs=pl.BlockSpec((tm, tn), lambda j, i: (i, j)),
        out_shape=jax.ShapeDtypeStruct((m, n_cols), F32),
        scratch_shapes=[pltpu.VMEM((k, tn), BF16)],
        compiler_params=_params("arbitrary", "arbitrary"),
        name=name,
    )(a, w)


_NN = (((1,), (0,)), ((), ()))
_NT = (((1,), (1,)), ((), ()))
_TN = (((0,), (0,)), ((), ()))


def _bf_parts(x, n):
    parts = []
    r = x
    for i in range(n):
        h = r.astype(BF16)
        parts.append(h)
        if i + 1 < n:
            r = r - h.astype(F32)
    return parts


def _dotp(a, b, dn, pa, pb):
    pa_parts = _bf_parts(a, pa)
    pb_parts = _bf_parts(b, pb)
    acc = None
    for i in range(pa):
        for j in range(pb):
            if i + j < max(pa, pb):
                t = lax.dot_general(pa_parts[i], pb_parts[j], dn, preferred_element_type=F32)
                acc = t if acc is None else acc + t
    return acc


def _softplus(x):
    return jnp.maximum(x, 0.0) + jnp.log(1.0 + jnp.exp(-jnp.abs(x)))


def _sigmoid(x):
    return 1.0 / (1.0 + jnp.exp(-x))


def _rwkv_kernel(*refs, seq_rows, n_seq, has_state, hp):
    if has_state:
        (p_ref, sh0_ref, s0_ref, mu_ref, wda_ref, wa0_ref, wg_ref, vec_ref,
         o_ref, sh_out_ref, s_out_ref, prev_scr, st_scr) = refs
    else:
        (p_ref, mu_ref, wda_ref, wa0_ref, wg_ref, vec_ref,
         o_ref, sh_out_ref, s_out_ref, prev_scr, st_scr) = refs
    c = pl.program_id(1)
    n_c = pl.num_programs(1)
    C = seq_rows
    R = seq_rows * n_seq
    W = RW_WIDTH

    @pl.when(c == 0)
    def _():
        if has_state:
            prev_scr[...] = sh0_ref[...]
            st_scr[...] = s0_ref[...]
        else:
            prev_scr[...] = jnp.zeros(prev_scr.shape, F32)
            st_scr[...] = jnp.zeros(st_scr.shape, F32)

    p = p_ref[...]
    row = lax.broadcasted_iota(jnp.int32, (R, 1), 0)
    prev = pltpu.roll(p, 1, axis=0)
    for s in range(n_seq):
        prev = jnp.where(row == s * C, prev_scr[s], prev)
    for s in range(n_seq):
        prev_scr[s] = p[(s + 1) * C - 1:(s + 1) * C, :]
        sh_out_ref[s] = p[(s + 1) * C - 1:(s + 1) * C, :]
    x = p + mu_ref[...] * (prev - p)

    r = x[:, 0:W]
    k = x[:, W:2 * W]
    v = x[:, 2 * W:3 * W]
    wa_in = x[:, 3 * W:3 * W + LANES]
    g_in = x[:, 3 * W + LANES:3 * W + 2 * LANES]

    lane = lax.broadcasted_iota(jnp.int32, (1, LANES), 1)
    wa_act = jnp.where(lane < RW_DECAY_RANK, jnp.tanh(wa_in), wa_in)
    za = jnp.dot(wa_act.astype(BF16), wda_ref[...], preferred_element_type=F32) + wa0_ref[...]
    w_log = -_softplus(-za[:, 0:W]) - 0.5
    lw = -jnp.exp(w_log)
    a = _sigmoid(za[:, W:2 * W])
    g = jnp.dot(_sigmoid(g_in).astype(BF16), wg_ref[...], preferred_element_type=F32)

    k_k = vec_ref[0:1, :]
    k_a = vec_ref[1:2, :]
    r_k = vec_ref[2:3, :]
    ln_w = vec_ref[3:4, :]
    ln_b = vec_ref[4:5, :]

    li = lax.broadcasted_iota(jnp.int32, (LANES, LANES), 0)
    lj = lax.broadcasted_iota(jnp.int32, (LANES, LANES), 1)
    same_head = _div_pow2(li, RW_HEAD_DIM) == _div_pow2(lj, RW_HEAD_DIM)
    ones_bd = jnp.where(same_head, 1.0, 0.0).astype(BF16)

    def head_sum(z):
        cols = []
        for q in range(RW_PAIRS):
            zq = z[:, q * LANES:(q + 1) * LANES]
            acc = None
            for part in _bf_parts(zq, 3):
                t = jnp.dot(part, ones_bd, preferred_element_type=F32)
                acc = t if acc is None else acc + t
            cols.append(acc)
        return jnp.concatenate(cols, axis=1)

    kk = k * k_k
    kk = kk / jnp.maximum(jnp.sqrt(head_sum(kk * kk)), 1e-12)
    k = k * (1.0 + (a - 1.0) * k_a)
    b = kk * a

    ri = lax.broadcasted_iota(jnp.int32, (R, R), 0)
    rj = lax.broadcasted_iota(jnp.int32, (R, R), 1)
    same_seq = _div_pow2(ri, C) == _div_pow2(rj, C)
    causal = jnp.where(same_seq & (rj <= ri), 1.0, 0.0).astype(BF16)
    whole = jnp.where(same_seq, 1.0, 0.0).astype(BF16)
    cl = None
    ct = None
    for part in _bf_parts(lw, 3):
        t1 = jnp.dot(causal, part, preferred_element_type=F32)
        t2 = jnp.dot(whole, part, preferred_element_type=F32)
        cl = t1 if cl is None else cl + t1
        ct = t2 if ct is None else ct + t2
    e_pos = jnp.exp(cl)
    e_neg = jnp.exp(-cl)
    at = -kk * jnp.exp(cl - lw)
    rt = r * e_pos
    bt = b * e_neg
    kt = k * e_neg
    e_end = jnp.exp(ct - cl)
    bts = b * e_end
    kts = k * e_end
    p_end = jnp.exp(ct)

    strict = jnp.where(same_seq & (rj < ri), 1.0, 0.0)
    incl = jnp.where(same_seq & (rj <= ri), 1.0, 0.0)
    strict2 = jnp.concatenate([strict, strict], axis=1)
    incl2 = jnp.concatenate([incl, incl], axis=1)
    head_mask = [jnp.where(lane < RW_HEAD_DIM, 1.0, 0.0), jnp.where(lane >= RW_HEAD_DIM, 1.0, 0.0)]
    diag = jnp.where(li == lj, 1.0, 0.0)
    bd_mask = jnp.where(same_head, 1.0, 0.0)
    zeros_rl = jnp.zeros((R, LANES), F32)
    n_dbl = max(1, (C - 1).bit_length())

    o_cols = []
    for q in range(RW_PAIRS):
        sl = slice(q * LANES, (q + 1) * LANES)
        xq = jnp.concatenate([at[:, sl], rt[:, sl]], axis=0)
        yq = jnp.concatenate([bt[:, sl], kt[:, sl]], axis=0)
        ysq = jnp.concatenate([bts[:, sl], kts[:, sl]], axis=0)
        vq = v[:, sl]
        rhs1 = jnp.concatenate([zeros_rl, vq], axis=0)
        v0 = jnp.concatenate([vq, zeros_rl], axis=1)
        rhat = None
        o0 = None
        w_heads = []
        for h in range(2):
            m = head_mask[h]
            xh = xq * m
            z = _dotp(xh, yq, _NT, hp, hp)
            zt = z[0:R] * strict2
            zb = z[R:2 * R] * incl2
            lmat = zt[:, 0:R]
            wmat = jnp.concatenate([_dotp(zt, rhs1, _NN, hp, hp), xh[0:R]], axis=1)
            pows = [lmat]
            for _ in range(n_dbl - 1):
                pows.append(_dotp(pows[-1], pows[-1], _NN, hp, hp))
            for lp in reversed(pows):
                wmat = wmat + _dotp(lp, wmat, _NN, hp, hp)
            rhs2 = jnp.concatenate([wmat, v0], axis=0)
            om = _dotp(zb, rhs2, _NN, hp, hp)
            rh = xh[R:2 * R] + om[:, LANES:]
            rhat = rh if rhat is None else rhat + rh
            o0h = om[:, 0:LANES] * m
            o0 = o0h if o0 is None else o0 + o0h
            w_heads.append(wmat)
        o_seq = []
        for s in range(n_seq):
            rs = slice(s * C, (s + 1) * C)
            hg = None
            for h in range(2):
                ys = jnp.concatenate([ysq[s * C:(s + 1) * C], ysq[R + s * C:R + (s + 1) * C]], axis=0)
                ys = ys * head_mask[h]
                r2 = jnp.concatenate([w_heads[h][rs], v0[rs]], axis=0)
                t = _dotp(ys, r2, _TN, hp, hp)
                hg = t if hg is None else hg + t
            hmat = hg[:, 0:LANES] * bd_mask
            gmat = hg[:, LANES:] + diag * p_end[s * C:s * C + 1, sl]
            a0 = st_scr[s, q]
            o_seq.append(_dotp(rhat[rs], a0, _NN, hp, hp) + o0[rs])
            st_scr[s, q] = _dotp(gmat, a0, _NN, hp, hp) + hmat
        o_cols.append(o_seq[0] if n_seq == 1 else jnp.concatenate(o_seq, axis=0))
    o = jnp.concatenate(o_cols, axis=1)

    inv_n = 1.0 / RW_HEAD_DIM
    mean = head_sum(o) * inv_n
    d = o - mean
    var = head_sum(d * d) * inv_n
    o = d * lax.rsqrt(var + RW_GN_EPS) * ln_w + ln_b
    bonus = head_sum(r * k * r_k) * v
    o_ref[...] = ((o + bonus) * g).astype(o_ref.dtype)

    @pl.when(c == n_c - 1)
    def _():
        s_out_ref[...] = st_scr[...]


def _rwkv(p_rw, row0, n_b, t_len, shift0, s0, prm, hp):
    if t_len >= RWKV_ROWS:
        seq_rows, n_seq = RWKV_ROWS, 1
    else:
        seq_rows, n_seq = t_len, RWKV_ROWS // t_len
    assert t_len % seq_rows == 0 and n_b % n_seq == 0 and row0 % (seq_rows * n_seq) == 0
    rows = seq_rows * n_seq
    n_c = t_len // seq_rows
    blk0 = row0 // rows
    has_state = s0 is not None

    wda = jnp.zeros((LANES, 2 * RW_WIDTH), F32)
    wda = wda.at[0:RW_DECAY_RANK, 0:RW_WIDTH].set(prm["w_decay_up"])
    wda = wda.at[RW_DECAY_RANK:, RW_WIDTH:].set(prm["w_a_up"])
    wa0 = jnp.concatenate([prm["w0_decay"], prm["a0"]]).reshape(1, 2 * RW_WIDTH)
    vecs = jnp.zeros((8, RW_WIDTH), F32)
    for i, name in enumerate(("k_k", "k_a", "r_k", "ln_x_w", "ln_x_b")):
        vecs = vecs.at[i].set(prm[name].reshape(RW_WIDTH))

    const = lambda *shape: pl.BlockSpec(shape, lambda bi, c: (0,) * len(shape))
    in_specs = [pl.BlockSpec((rows, RW_PROJ), lambda bi, c: (blk0 + bi * n_c + c, 0))]
    args = [p_rw]
    if has_state:
        st = jnp.swapaxes(s0, -1, -2).reshape(n_b, RW_PAIRS, 2, RW_HEAD_DIM, RW_HEAD_DIM)
        a0 = jnp.zeros((n_b, RW_PAIRS, 2, RW_HEAD_DIM, 2, RW_HEAD_DIM), F32)
        a0 = a0.at[:, :, 0, :, 0, :].set(st[:, :, 0]).at[:, :, 1, :, 1, :].set(st[:, :, 1])
        a0 = a0.reshape(n_b, RW_PAIRS, LANES, LANES)
        in_specs += [pl.BlockSpec((n_seq, 1, RW_PROJ), lambda bi, c: (bi, 0, 0)),
                     pl.BlockSpec((n_seq, RW_PAIRS, LANES, LANES), lambda bi, c: (bi, 0, 0, 0))]
        args += [shift0.reshape(n_b, 1, RW_PROJ), a0]
    in_specs += [const(1, RW_PROJ), const(LANES, 2 * RW_WIDTH), const(1, 2 * RW_WIDTH),
                 const(RW_GATE_RANK, RW_WIDTH), const(8, RW_WIDTH)]
    args += [prm["mu_rwkv"].reshape(1, RW_PROJ), wda.astype(BF16), wa0,
             prm["w_gate_up"].astype(BF16), vecs]

    o, sh, st = pl.pallas_call(
        functools.partial(_rwkv_kernel, seq_rows=seq_rows, n_seq=n_seq, has_state=has_state, hp=hp),
        grid=(n_b // n_seq, n_c),
        in_specs=in_specs,
        out_specs=[
            pl.BlockSpec((rows, RW_WIDTH), lambda bi, c: (bi * n_c + c, 0)),
            pl.BlockSpec((n_seq, 1, RW_PROJ), lambda bi, c: (bi, 0, 0)),
            pl.BlockSpec((n_seq, RW_PAIRS, LANES, LANES), lambda bi, c: (bi, 0, 0, 0)),
        ],
        out_shape=[
            jax.ShapeDtypeStruct((n_b * t_len, RW_WIDTH), BF16),
            jax.ShapeDtypeStruct((n_b, 1, RW_PROJ), F32),
            jax.ShapeDtypeStruct((n_b, RW_PAIRS, LANES, LANES), F32),
        ],
        scratch_shapes=[pltpu.VMEM((n_seq, 1, RW_PROJ), F32),
                        pltpu.VMEM((n_seq, RW_PAIRS, LANES, LANES), F32)],
        compiler_params=_params("arbitrary", "arbitrary"),
        name="rwkv_state" if has_state else "rwkv_fresh",
    )(*args)
    st = st.reshape(n_b, RW_PAIRS, 2, RW_HEAD_DIM, 2, RW_HEAD_DIM)
    st = jnp.stack([st[:, :, 0, :, 0, :], st[:, :, 1, :, 1, :]], axis=2)
    st = jnp.swapaxes(st.reshape(n_b, RW_HEADS, RW_HEAD_DIM, RW_HEAD_DIM), -1, -2)
    return o, sh.reshape(n_b, RW_PROJ), st


def _sink_attend(s_parts, v_parts, sink):
    m = sink
    for s in s_parts:
        m = jnp.maximum(m, jnp.max(s, axis=-1, keepdims=True))
    den = jnp.exp(sink - m)
    es = []
    for s in s_parts:
        e = jnp.exp(s - m)
        den = den + jnp.sum(e, axis=-1, keepdims=True)
        es.append(e)
    inv = 1.0 / den
    out = None
    for e, vv in zip(es, v_parts):
        t = jnp.dot((e * inv).astype(BF16), vv, preferred_element_type=F32)
        out = t if out is None else out + t
    return out


def _swa_prompt_kernel(sink_ref, q_ref, kp_ref, kc_ref, vp_ref, vc_ref, o_ref):
    n = pl.program_id(1)
    Wn = WINDOW
    q = q_ref[...]
    kf = jnp.concatenate([kp_ref[...], kc_ref[...]], axis=0).astype(BF16)
    vf = jnp.concatenate([vp_ref[...], vc_ref[...]], axis=0).astype(BF16)
    gi = lax.broadcasted_iota(jnp.int32, (SWA_GROUP * Wn, 2 * Wn), 0)
    i = _mod_pow2(gi, Wn)
    j = lax.broadcasted_iota(jnp.int32, (SWA_GROUP * Wn, 2 * Wn), 1)
    first_key = jnp.where(n > 0, 0, Wn)
    ok = (j > i) & (j <= i + Wn) & (j >= first_key)
    grp = _div_pow2(lax.broadcasted_iota(jnp.int32, (SWA_GROUP * Wn, 1), 0), Wn)
    scale = SWA_HEAD_DIM ** -0.5
    for hk in range(SWA_KV_HEADS):
        ks = slice(hk * SWA_HEAD_DIM, (hk + 1) * SWA_HEAD_DIM)
        qh = jnp.concatenate(
            [q[:, (hk * SWA_GROUP + g) * SWA_HEAD_DIM:(hk * SWA_GROUP + g + 1) * SWA_HEAD_DIM]
             for g in range(SWA_GROUP)], axis=0).astype(BF16)
        s = lax.dot_general(qh, kf[:, ks], _NT, preferred_element_type=F32) * scale
        s = jnp.where(ok, s, -jnp.inf)
        sink = jnp.zeros((SWA_GROUP * Wn, 1), F32)
        for g in range(SWA_GROUP):
            sink = jnp.where(grp == g, sink_ref[hk * SWA_GROUP + g], sink)
        o = _sink_attend([s], [vf[:, ks]], sink)
        for g in range(SWA_GROUP):
            h = hk * SWA_GROUP + g
            o_ref[:, h * SWA_HEAD_DIM:(h + 1) * SWA_HEAD_DIM] = o[g * Wn:(g + 1) * Wn].astype(o_ref.dtype)


def _swa_prompt(p_att, n_b, t_len, sinks):
    nb = t_len // WINDOW
    kcol = SWA_Q_WIDTH // SWA_KV_WIDTH
    vcol = kcol + 1
    cur = lambda col: (lambda b, n: (b * nb + n, col))
    prv = lambda col: (lambda b, n: (b * nb + jnp.maximum(n - 1, 0), col))
    return pl.pallas_call(
        _swa_prompt_kernel,
        grid=(n_b, nb),
        in_specs=[
            pl.BlockSpec(memory_space=pltpu.SMEM),
            pl.BlockSpec((WINDOW, SWA_Q_WIDTH), cur(0)),
            pl.BlockSpec((WINDOW, SWA_KV_WIDTH), prv(kcol)),
            pl.BlockSpec((WINDOW, SWA_KV_WIDTH), cur(kcol)),
            pl.BlockSpec((WINDOW, SWA_KV_WIDTH), prv(vcol)),
            pl.BlockSpec((WINDOW, SWA_KV_WIDTH), cur(vcol)),
        ],
        out_specs=pl.BlockSpec((WINDOW, SWA_Q_WIDTH), lambda b, n: (b * nb + n, 0)),
        out_shape=jax.ShapeDtypeStruct((n_b * t_len, SWA_Q_WIDTH), BF16),
        compiler_params=_params("arbitrary", "arbitrary"),
        name="swa_prompt",
    )(sinks, p_att, p_att, p_att, p_att, p_att)


def _swa_step_kernel(sink_ref, q_ref, kn_ref, vn_ref, ck_ref, cv_ref, o_ref, ok_ref, ov_ref, *, n_seq, t_len):
    T = t_len
    wb = ck_ref.shape[1]
    scale = SWA_HEAD_DIM ** -0.5
    t_row = _mod_pow2(lax.broadcasted_iota(jnp.int32, (SWA_GROUP * T, 1), 0), T)
    grp = _div_pow2(lax.broadcasted_iota(jnp.int32, (SWA_GROUP * T, 1), 0), T)
    jc = lax.broadcasted_iota(jnp.int32, (SWA_GROUP * T, wb), 1)
    jn = lax.broadcasted_iota(jnp.int32, (SWA_GROUP * T, T), 1)
    ok_c = (t_row + wb - jc < WINDOW)
    ok_n = (jn <= t_row)
    for s in range(n_seq):
        rs = slice(s * T, (s + 1) * T)
        q = q_ref[rs, :]
        kn = kn_ref[rs, :]
        vn = vn_ref[rs, :]
        ck = ck_ref[s]
        cv = cv_ref[s]
        ok_ref[s, 0:wb - T, :] = ck[T:wb, :]
        ok_ref[s, wb - T:wb, :] = kn
        ov_ref[s, 0:wb - T, :] = cv[T:wb, :]
        ov_ref[s, wb - T:wb, :] = vn
        knb, vnb, ckb, cvb = kn.astype(BF16), vn.astype(BF16), ck.astype(BF16), cv.astype(BF16)
        for hk in range(SWA_KV_HEADS):
            ks = slice(hk * SWA_HEAD_DIM, (hk + 1) * SWA_HEAD_DIM)
            qh = jnp.concatenate(
                [q[:, (hk * SWA_GROUP + g) * SWA_HEAD_DIM:(hk * SWA_GROUP + g + 1) * SWA_HEAD_DIM]
                 for g in range(SWA_GROUP)], axis=0).astype(BF16)
            sc = lax.dot_general(qh, ckb[:, ks], _NT, preferred_element_type=F32) * scale
            sn = lax.dot_general(qh, knb[:, ks], _NT, preferred_element_type=F32) * scale
            sc = jnp.where(ok_c, sc, -jnp.inf)
            sn = jnp.where(ok_n, sn, -jnp.inf)
            sink = jnp.zeros((SWA_GROUP * T, 1), F32)
            for g in range(SWA_GROUP):
                sink = jnp.where(grp == g, sink_ref[hk * SWA_GROUP + g], sink)
            o = _sink_attend([sc, sn], [cvb[:, ks], vnb[:, ks]], sink)
            for g in range(SWA_GROUP):
                h = hk * SWA_GROUP + g
                o_ref[rs, h * SWA_HEAD_DIM:(h + 1) * SWA_HEAD_DIM] = o[g * T:(g + 1) * T].astype(o_ref.dtype)


def _swa_step(p_att, row0, n_b, t_len, cache_k, cache_v, sinks):
    wb = cache_k.shape[1]
    assert wb == WINDOW and t_len % 8 == 0 and t_len < wb
    n_seq = max(1, 16 // t_len)
    while n_b % n_seq:
        n_seq //= 2
    rows = n_seq * t_len
    assert rows % 16 == 0 and row0 % rows == 0
    blk0 = row0 // rows
    kcol = SWA_Q_WIDTH // SWA_KV_WIDTH
    ck = cache_k.reshape(n_b, wb, SWA_KV_WIDTH)
    cv = cache_v.reshape(n_b, wb, SWA_KV_WIDTH)
    o, nk, nv = pl.pallas_call(
        functools.partial(_swa_step_kernel, n_seq=n_seq, t_len=t_len),
        grid=(n_b // n_seq,),
        in_specs=[
            pl.BlockSpec(memory_space=pltpu.SMEM),
            pl.BlockSpec((rows, SWA_Q_WIDTH), lambda i: (blk0 + i, 0)),
            pl.BlockSpec((rows, SWA_KV_WIDTH), lambda i: (blk0 + i, kcol)),
            pl.BlockSpec((rows, SWA_KV_WIDTH), lambda i: (blk0 + i, kcol + 1)),
            pl.BlockSpec((n_seq, wb, SWA_KV_WIDTH), lambda i: (i, 0, 0)),
            pl.BlockSpec((n_seq, wb, SWA_KV_WIDTH), lambda i: (i, 0, 0)),
        ],
        out_specs=[
            pl.BlockSpec((rows, SWA_Q_WIDTH), lambda i: (i, 0)),
            pl.BlockSpec((n_seq, wb, SWA_KV_WIDTH), lambda i: (i, 0, 0)),
            pl.BlockSpec((n_seq, wb, SWA_KV_WIDTH), lambda i: (i, 0, 0)),
        ],
        out_shape=[
            jax.ShapeDtypeStruct((n_b * t_len, SWA_Q_WIDTH), BF16),
            jax.ShapeDtypeStruct((n_b, wb, SWA_KV_WIDTH), F32),
            jax.ShapeDtypeStruct((n_b, wb, SWA_KV_WIDTH), F32),
        ],
        compiler_params=_params("arbitrary"),
        name="swa_step",
    )(sinks, p_att, p_att, p_att, ck, cv)
    shape = (n_b, wb, SWA_KV_HEADS, SWA_HEAD_DIM)
    return o, nk.reshape(shape), nv.reshape(shape)


def _softmax_attend(q, kh, vh, scale):
    s = lax.dot_general(q, kh, _NT, preferred_element_type=F32) * scale
    m = jnp.max(s, axis=-1, keepdims=True)
    e = jnp.exp(s - m)
    pr = e / jnp.sum(e, axis=-1, keepdims=True)
    return jnp.dot(pr.astype(BF16), vh, preferred_element_type=F32)


def _mem_prompt_kernel(qa_ref, qb_ref, k_ref, v_ref, o_ref):
    scale = MEM_HEAD_DIM ** -0.5
    kb = k_ref[...].astype(BF16)
    vb = v_ref[...].astype(BF16)
    half = MEM_HEADS // 2
    for h in range(MEM_HEADS):
        q_ref = qa_ref if h < half else qb_ref
        hs = slice((h % half) * MEM_HEAD_DIM, (h % half + 1) * MEM_HEAD_DIM)
        ms = slice(h * MEM_HEAD_DIM, (h + 1) * MEM_HEAD_DIM)
        o = _softmax_attend(q_ref[:, hs].astype(BF16), kb[:, ms], vb[:, ms], scale)
        o_ref[:, ms] = o.astype(o_ref.dtype)


def _mem_prompt(p_att, n_b, t_len, mem_kv):
    m_tok = mem_kv.shape[0] // n_b
    tq = math.gcd(t_len, 512)
    nq = t_len // tq
    half_w = MEM_WIDTH // 2
    qcol = (SWA_Q_WIDTH + 2 * SWA_KV_WIDTH) // half_w
    return pl.pallas_call(
        _mem_prompt_kernel,
        grid=(n_b, nq),
        in_specs=[
            pl.BlockSpec((tq, half_w), lambda b, n: (b * nq + n, qcol)),
            pl.BlockSpec((tq, half_w), lambda b, n: (b * nq + n, qcol + 1)),
            pl.BlockSpec((m_tok, MEM_WIDTH), lambda b, n: (b, 0)),
            pl.BlockSpec((m_tok, MEM_WIDTH), lambda b, n: (b, 1)),
        ],
        out_specs=pl.BlockSpec((tq, MEM_WIDTH), lambda b, n: (b * nq + n, 0)),
        out_shape=jax.ShapeDtypeStruct((n_b * t_len, MEM_WIDTH), BF16),
        compiler_params=_params("arbitrary", "arbitrary"),
        name="mem_prompt",
    )(p_att, p_att, mem_kv, mem_kv)


def _mem_step_kernel(qa_ref, qb_ref, k_ref, v_ref, o_ref, *, n_seq, t_len):
    scale = MEM_HEAD_DIM ** -0.5
    half = MEM_HEADS // 2
    for s in range(n_seq):
        rs = slice(s * t_len, (s + 1) * t_len)
        kb = k_ref[s].astype(BF16)
        vb = v_ref[s].astype(BF16)
        for h in range(MEM_HEADS):
            q_ref = qa_ref if h < half else qb_ref
            hs = slice((h % half) * MEM_HEAD_DIM, (h % half + 1) * MEM_HEAD_DIM)
            ms = slice(h * MEM_HEAD_DIM, (h + 1) * MEM_HEAD_DIM)
            o = _softmax_attend(q_ref[rs, hs].astype(BF16), kb[:, ms], vb[:, ms], scale)
            o_ref[rs, ms] = o.astype(o_ref.dtype)


def _mem_step(p_att, row0, n_b, t_len, cache_k, cache_v):
    m_tok = cache_k.shape[1]
    n_seq = max(1, 64 // t_len)
    while n_b % n_seq:
        n_seq //= 2
    rows = n_seq * t_len
    assert rows % 16 == 0 and row0 % rows == 0
    blk0 = row0 // rows
    half_w = MEM_WIDTH // 2
    qcol = (SWA_Q_WIDTH + 2 * SWA_KV_WIDTH) // half_w
    ck = cache_k.reshape(n_b, m_tok, MEM_WIDTH)
    cv = cache_v.reshape(n_b, m_tok, MEM_WIDTH)
    return pl.pallas_call(
        functools.partial(_mem_step_kernel, n_seq=n_seq, t_len=t_len),
        grid=(n_b // n_seq,),
        in_specs=[
            pl.BlockSpec((rows, half_w), lambda i: (blk0 + i, qcol)),
            pl.BlockSpec((rows, half_w), lambda i: (blk0 + i, qcol + 1)),
            pl.BlockSpec((n_seq, m_tok, MEM_WIDTH), lambda i: (i, 0, 0)),
            pl.BlockSpec((n_seq, m_tok, MEM_WIDTH), lambda i: (i, 0, 0)),
        ],
        out_specs=pl.BlockSpec((rows, MEM_WIDTH), lambda i: (i, 0)),
        out_shape=jax.ShapeDtypeStruct((n_b * t_len, MEM_WIDTH), BF16),
        compiler_params=_params("arbitrary"),
        name="mem_step",
    )(p_att, p_att, ck, cv)


def _merge_kernel(xn_ref, rwp_ref, rws_ref, swp_ref, sws_ref, mmp_ref, mms_ref,
                  wg0_ref, wg1_ref, wg2_ref, wb0_ref, wb1_ref, wb2_ref, o_ref,
                  g0_scr, g1_scr, g2_scr, b0_scr, b1_scr, b2_scr, *, n_p):
    i = pl.program_id(1)

    @pl.when(i == 0)
    def _():
        for src, dst in ((wg0_ref, g0_scr), (wg1_ref, g1_scr), (wg2_ref, g2_scr),
                         (wb0_ref, b0_scr), (wb1_ref, b1_scr), (wb2_ref, b2_scr)):
            dst[...] = src[...].astype(BF16)

    def run(o_rw_ref, o_sw_ref, o_mm_ref):
        xn = xn_ref[...]
        acc = None
        for g_scr, b_scr, br_ref in ((g0_scr, b0_scr, o_rw_ref), (g1_scr, b1_scr, o_sw_ref),
                                     (g2_scr, b2_scr, o_mm_ref)):
            gate = _sigmoid(jnp.dot(xn, g_scr[...], preferred_element_type=F32))
            t = gate * jnp.dot(br_ref[...], b_scr[...], preferred_element_type=F32)
            acc = t if acc is None else acc + t
        o_ref[...] = acc.astype(o_ref.dtype)

    @pl.when(i < n_p)
    def _():
        run(rwp_ref, swp_ref, mmp_ref)

    @pl.when(i >= n_p)
    def _():
        run(rws_ref, sws_ref, mms_ref)


def _merge(xn, o_p, o_s, w_in, gate_col0, w_br, n_rows_p):
    m, d = xn.shape
    tn = MXU_COLS
    tm = _row_tile(n_rows_p, m - n_rows_p, ROW_TILE)
    n_p = n_rows_p // tm
    n_i = m // tm
    assert gate_col0 % tn == 0 and d % tn == 0
    g0 = gate_col0 // tn
    nj = d // tn
    p_spec = lambda w: pl.BlockSpec((tm, w), lambda j, i: (jnp.minimum(i, n_p - 1), 0))
    s_spec = lambda w: pl.BlockSpec((tm, w), lambda j, i: (jnp.maximum(i - n_p, 0), 0))
    in_specs = [pl.BlockSpec((tm, d), lambda j, i: (i, 0))]
    args = [xn]
    for op, os_ in zip(o_p, o_s):
        in_specs += [p_spec(op.shape[1]), s_spec(os_.shape[1])]
        args += [op, os_]
    for br in range(N_BRANCH):
        in_specs.append(pl.BlockSpec((d, tn), lambda j, i, br=br: (0, g0 + br * nj + j)))
        args.append(w_in)
    for w in w_br:
        in_specs.append(pl.BlockSpec((w.shape[0], tn), lambda j, i: (0, j)))
        args.append(w)
    scratch = [pltpu.VMEM((d, tn), BF16) for _ in range(N_BRANCH)]
    scratch += [pltpu.VMEM((w.shape[0], tn), BF16) for w in w_br]
    return pl.pallas_call(
        functools.partial(_merge_kernel, n_p=n_p),
        grid=(nj, n_i),
        in_specs=in_specs,
        out_specs=pl.BlockSpec((tm, tn), lambda j, i: (i, j)),
        out_shape=jax.ShapeDtypeStruct((m, d), BF16),
        scratch_shapes=scratch,
        compiler_params=_params("arbitrary", "arbitrary"),
        name="gated_merge",
    )(*args)


def _wo_kernel(m_ref, w_ref, xp_ref, xs_ref, g1_ref, g2_ref, h_ref, hn_ref, acc_ref, *, n_p):
    i = pl.program_id(0)
    kk = pl.program_id(1)

    @pl.when(kk == 0)
    def _():
        acc_ref[...] = jnp.zeros(acc_ref.shape, F32)

    acc_ref[...] += jnp.dot(m_ref[...], w_ref[...].astype(BF16), preferred_element_type=F32)

    def finish(x_ref):
        h = x_ref[...] + _rms(acc_ref[...], g1_ref[...])
        h_ref[...] = h
        hn_ref[...] = _rms(h, g2_ref[...]).astype(hn_ref.dtype)

    last = kk == pl.num_programs(1) - 1

    @pl.when(last & (i < n_p))
    def _():
        finish(xp_ref)

    @pl.when(last & (i >= n_p))
    def _():
        finish(xs_ref)


def _wo_block(merged, w_o, xp, xs, g_post, g_pre_ffn):
    m, d = merged.shape
    n_rows_p = xp.shape[0]
    tm = _row_tile(n_rows_p, m - n_rows_p, NORM_ROW_TILE)
    n_p = n_rows_p // tm
    tk = math.gcd(d, 512)
    return pl.pallas_call(
        functools.partial(_wo_kernel, n_p=n_p),
        grid=(m // tm, d // tk),
        in_specs=[
            pl.BlockSpec((tm, tk), lambda i, k: (i, k)),
            pl.BlockSpec((tk, d), lambda i, k: (k, 0)),
            pl.BlockSpec((tm, d), lambda i, k: (jnp.minimum(i, n_p - 1), 0)),
            pl.BlockSpec((tm, d), lambda i, k: (jnp.maximum(i - n_p, 0), 0)),
            pl.BlockSpec((1, d), lambda i, k: (0, 0)),
            pl.BlockSpec((1, d), lambda i, k: (0, 0)),
        ],
        out_specs=[pl.BlockSpec((tm, d), lambda i, k: (i, 0)), pl.BlockSpec((tm, d), lambda i, k: (i, 0))],
        out_shape=[jax.ShapeDtypeStruct((m, d), F32), jax.ShapeDtypeStruct((m, d), BF16)],
        scratch_shapes=[pltpu.VMEM((tm, d), F32)],
        compiler_params=_params("arbitrary", "arbitrary"),
        name="wo_residual",
    )(merged, w_o, xp, xs, g_post.reshape(1, d), g_pre_ffn.reshape(1, d))


def _gelu_tanh(x):
    return 0.5 * x * (1.0 + jnp.tanh(math.sqrt(2.0 / math.pi) * (x + 0.044715 * (x * x * x))))


def _ffn_up_kernel(hn_ref, wg_ref, wv_ref, cw_ref, cb_ref, st_ref, act_ref, cp_ref, cs_ref,
                   wgb_scr, wvb_scr, carry_scr, *, n_p, blocks_per_seq, t_s):
    i = pl.program_id(1)

    @pl.when(i == 0)
    def _():
        wgb_scr[...] = wg_ref[...].astype(BF16)
        wvb_scr[...] = wv_ref[...].astype(BF16)

    hn = hn_ref[...]
    zg = jnp.dot(hn, wgb_scr[...], preferred_element_type=F32)
    zv = jnp.dot(hn, wvb_scr[...], preferred_element_type=F32)
    tm, tn = zg.shape
    w0 = cw_ref[0:1, :]
    w1 = cw_ref[1:2, :]
    w2 = cw_ref[2:3, :]
    cb = cb_ref[...]

    @pl.when(i < n_p)
    def _():
        @pl.when(i % blocks_per_seq == 0)
        def _():
            carry_scr[...] = jnp.zeros(carry_scr.shape, F32)

        p1 = carry_scr[1:2, :]
        p2 = carry_scr[0:1, :]
        row = lax.broadcasted_iota(jnp.int32, (tm, 1), 0)
        zm1 = jnp.where(row == 0, p1, pltpu.roll(zg, 1, axis=0))
        zm2 = jnp.where(row == 0, p2, jnp.where(row == 1, p1, pltpu.roll(zg, 2, axis=0)))
        conv = cb + w0 * zm2 + w1 * zm1 + w2 * zg
        act_ref[...] = (_gelu_tanh(conv) * zv).astype(act_ref.dtype)
        carry_scr[0:2, :] = zg[tm - 2:tm, :]
        cp_ref[0] = zg[tm - 2:tm, :]

    @pl.when(i >= n_p)
    def _():
        z3 = zg.reshape(tm // t_s, t_s, tn)
        st = st_ref[...]
        t = lax.broadcasted_iota(jnp.int32, (1, t_s, 1), 1)
        s1 = st[:, 1:2, :]
        s0 = st[:, 0:1, :]
        zm1 = jnp.where(t == 0, s1, pltpu.roll(z3, 1, axis=1))
        zm2 = jnp.where(t == 0, s0, jnp.where(t == 1, s1, pltpu.roll(z3, 2, axis=1)))
        conv = cb + w0 * zm2 + w1 * zm1 + w2 * z3
        act = _gelu_tanh(conv) * zv.reshape(tm // t_s, t_s, tn)
        act_ref[...] = act.reshape(tm, tn).astype(act_ref.dtype)
        cs_ref[...] = z3[:, t_s - 2:t_s, :]


def _ffn_up(hn, w_up, conv_w, conv_b, conv_state, n_b_p, t_p, n_b_s, t_s):
    m, d = hn.shape
    d_ff = conv_b.shape[0]
    n_rows_p = n_b_p * t_p
    n_rows_s = n_b_s * t_s
    tm = _row_tile(math.gcd(n_rows_p, t_p), n_rows_s, ROW_TILE)
    assert tm == n_rows_s and t_s % 8 == 0 and t_s >= 2, "sample rows must form one row block"
    n_p = n_rows_p // tm
    blocks_per_seq = t_p // tm
    tn = math.gcd(d_ff, 512)
    nj = d_ff // tn
    last_seq = n_b_p - 1
    return pl.pallas_call(
        functools.partial(_ffn_up_kernel, n_p=n_p, blocks_per_seq=blocks_per_seq, t_s=t_s),
        grid=(nj, n_p + 1),
        in_specs=[
            pl.BlockSpec((tm, d), lambda j, i: (i, 0)),
            pl.BlockSpec((d, tn), lambda j, i: (0, j)),
            pl.BlockSpec((d, tn), lambda j, i: (0, nj + j)),
            pl.BlockSpec((CONV_W, tn), lambda j, i: (0, j)),
            pl.BlockSpec((1, tn), lambda j, i: (0, j)),
            pl.BlockSpec((n_b_s, CONV_W - 1, tn), lambda j, i: (0, 0, j)),
        ],
        out_specs=[
            pl.BlockSpec((tm, tn), lambda j, i: (i, j)),
            pl.BlockSpec((1, CONV_W - 1, tn), lambda j, i: (jnp.minimum(i // blocks_per_seq, last_seq), 0, j)),
            pl.BlockSpec((n_b_s, CONV_W - 1, tn), lambda j, i: (0, 0, j)),
        ],
        out_shape=[
            jax.ShapeDtypeStruct((m, d_ff), BF16),
            jax.ShapeDtypeStruct((n_b_p, CONV_W - 1, d_ff), F32),
            jax.ShapeDtypeStruct((n_b_s, CONV_W - 1, d_ff), F32),
        ],
        scratch_shapes=[pltpu.VMEM((d, tn), BF16), pltpu.VMEM((d, tn), BF16), pltpu.VMEM((8, tn), F32)],
        compiler_params=_params("arbitrary", "arbitrary"),
        name="ffn_up_convglu",
    )(hn, w_up, w_up, conv_w, conv_b.reshape(1, d_ff), conv_state)


def _ffn_down_kernel(a_ref, w_ref, h_ref, g_ref, y_ref):
    kk = pl.program_id(1)

    @pl.when(kk == 0)
    def _():
        y_ref[...] = jnp.zeros(y_ref.shape, F32)

    y_ref[...] += jnp.dot(a_ref[...], w_ref[...].astype(BF16), preferred_element_type=F32)

    @pl.when(kk == pl.num_programs(1) - 1)
    def _():
        y_ref[...] = h_ref[...] + _rms(y_ref[...], g_ref[...])


def _ffn_down(act, w_down, h, g, row0, n_rows):
    d_ff, d = w_down.shape
    tm = math.gcd(math.gcd(n_rows, row0) if row0 else n_rows, ROW_TILE)
    tk = math.gcd(d_ff, 512)
    i0 = row0 // tm
    return pl.pallas_call(
        _ffn_down_kernel,
        grid=(n_rows // tm, d_ff // tk),
        in_specs=[
            pl.BlockSpec((tm, tk), lambda i, k: (i0 + i, k)),
            pl.BlockSpec((tk, d), lambda i, k: (k, 0)),
            pl.BlockSpec((tm, d), lambda i, k: (i0 + i, 0)),
            pl.BlockSpec((1, d), lambda i, k: (0, 0)),
        ],
        out_specs=pl.BlockSpec((tm, d), lambda i, k: (i, 0)),
        out_shape=jax.ShapeDtypeStruct((n_rows, d), F32),
        compiler_params=_params("arbitrary", "arbitrary"),
        name="ffn_down",
    )(act, w_down, h, g.reshape(1, d))


def kernel(x_prompt, x_sample, cache_swa_k, cache_swa_v, cache_mem_k, cache_mem_v, state_rwkv_shift, state_rwkv_wkv, state_ffn_conv, mem_prompt, g_pre_mix, w_in, mu_rwkv, w_decay_up, w0_decay, w_a_up, a0, w_gate_up, k_k, k_a, r_k, ln_x_w, ln_x_b, swa_sinks, g_mem, w_mem_kv, w_br_rwkv, w_br_swa, w_br_mem, w_o, g_post_mix, g_pre_ffn, w_ffn_up, conv_w, conv_b, w_ffn_down, g_post_ffn):
    b_p, t_p, d = x_prompt.shape
    b_s, t_s, _ = x_sample.shape
    n_p, n_s = b_p * t_p, b_s * t_s
    m_tok = mem_prompt.shape[1]
    rw_prm = dict(mu_rwkv=mu_rwkv, w_decay_up=w_decay_up, w0_decay=w0_decay, w_a_up=w_a_up, a0=a0,
                  w_gate_up=w_gate_up, k_k=k_k, k_a=k_a, r_k=r_k, ln_x_w=ln_x_w, ln_x_b=ln_x_b)
    xp = x_prompt.reshape(n_p, d)
    xs = x_sample.reshape(n_s, d)

    xn = _norm_rows2(xp, xs, g_pre_mix)
    p_rw = _matmul(xn, w_in, 0, RW_PROJ, 512, "proj_rwkv")
    p_att = _matmul(xn, w_in, RW_PROJ, ATT_WIDTH, MXU_COLS, "proj_attn")

    mem_kv = _matmul(_norm_rows1(mem_prompt.reshape(b_p * m_tok, d), g_mem), w_mem_kv,
                     0, 2 * MEM_WIDTH, MXU_COLS, "proj_mem_kv")

    o_rw_p, shift_p, wkv_p = _rwkv(p_rw, 0, b_p, t_p, None, None, rw_prm, 2)
    o_rw_s, shift_s, wkv_s = _rwkv(p_rw, n_p, b_s, t_s, state_rwkv_shift, state_rwkv_wkv, rw_prm, 2)

    o_sw_p = _swa_prompt(p_att, b_p, t_p, swa_sinks)
    o_sw_s, swa_k_s, swa_v_s = _swa_step(p_att, n_p, b_s, t_s, cache_swa_k, cache_swa_v, swa_sinks)

    o_mm_p = _mem_prompt(p_att, b_p, t_p, mem_kv)
    o_mm_s = _mem_step(p_att, n_p, b_s, t_s, cache_mem_k, cache_mem_v)

    merged = _merge(xn, (o_rw_p, o_sw_p, o_mm_p), (o_rw_s, o_sw_s, o_mm_s), w_in,
                    RW_PROJ + ATT_WIDTH, (w_br_rwkv, w_br_swa, w_br_mem), n_p)
    h, hn = _wo_block(merged, w_o, xp, xs, g_post_mix, g_pre_ffn)
    act, conv_p, conv_s = _ffn_up(hn, w_ffn_up, conv_w, conv_b, state_ffn_conv, b_p, t_p, b_s, t_s)
    y_p = _ffn_down(act, w_ffn_down, h, g_post_ffn, 0, n_p)
    y_s = _ffn_down(act, w_ffn_down, h, g_post_ffn, n_p, n_s)

    keep = min(WINDOW, t_p)
    kv_p = p_att[:n_p].reshape(b_p, t_p, ATT_WIDTH)[:, t_p - keep:, SWA_Q_WIDTH:SWA_Q_WIDTH + 2 * SWA_KV_WIDTH]
    swa_k_p = kv_p[..., :SWA_KV_WIDTH].reshape(b_p, keep, SWA_KV_HEADS, SWA_HEAD_DIM)
    swa_v_p = kv_p[..., SWA_KV_WIDTH:].reshape(b_p, keep, SWA_KV_HEADS, SWA_HEAD_DIM)
    mem_k_p = mem_kv[:, :MEM_WIDTH].reshape(b_p, m_tok, MEM_HEADS, MEM_HEAD_DIM)
    mem_v_p = mem_kv[:, MEM_WIDTH:].reshape(b_p, m_tok, MEM_HEADS, MEM_HEAD_DIM)
    return (y_p.reshape(b_p, t_p, d), y_s.reshape(b_s, t_s, d), swa_k_p, swa_v_p, mem_k_p, mem_v_p,
            shift_p, wkv_p, conv_p, swa_k_s, swa_v_s, shift_s, wkv_s, conv_s)
```

```python
import functools
import math

import jax
import jax.numpy as jnp
from jax import lax
from jax.experimental import pallas as pl
from jax.experimental.pallas import tpu as pltpu

F32 = jnp.float32
BF16 = jnp.bfloat16

RW_HEADS = 12
RW_HEAD_DIM = 64
RW_WIDTH = RW_HEADS * RW_HEAD_DIM
RW_DECAY_RANK = 64
RW_A_RANK = 64
RW_GATE_RANK = 128
RW_PROJ = 3 * RW_WIDTH + RW_DECAY_RANK + RW_A_RANK + RW_GATE_RANK
RW_GN_EPS = 6.4e-4
RW_PAIRS = RW_HEADS // 2
SWA_Q_HEADS = 12
SWA_KV_HEADS = 4
SWA_GROUP = SWA_Q_HEADS // SWA_KV_HEADS
SWA_HEAD_DIM = 64
SWA_Q_WIDTH = SWA_Q_HEADS * SWA_HEAD_DIM
SWA_KV_WIDTH = SWA_KV_HEADS * SWA_HEAD_DIM
WINDOW = 128
MEM_HEADS = 4
MEM_HEAD_DIM = 128
MEM_WIDTH = MEM_HEADS * MEM_HEAD_DIM
N_BRANCH = 3
CONV_W = 3
NORM_EPS = 1e-6
ATT_WIDTH = SWA_Q_WIDTH + 2 * SWA_KV_WIDTH + MEM_WIDTH

LANES = 128
MXU_COLS = 256
VMEM_LIMIT_BYTES = 56 * 1024 * 1024

RWKV_ROWS = 64
RWKV_PARTS = dict(z=2, lv=2, pow=1, app=1, om=2, hg=2, out=2, st=2)
ROW_TILE = 1024
NORM_ROW_TILE = 512


def _params(*sem):
    return pltpu.CompilerParams(dimension_semantics=sem, vmem_limit_bytes=VMEM_LIMIT_BYTES)


def _row_tile(n_p, n_s, target):
    t = math.gcd(math.gcd(n_p, n_s), target)
    assert t % 16 == 0, (n_p, n_s, target)
    return t


def _div_pow2(x, n):
    assert n & (n - 1) == 0
    return lax.shift_right_logical(x, n.bit_length() - 1)


def _mod_pow2(x, n):
    assert n & (n - 1) == 0
    return jnp.bitwise_and(x, n - 1)


def _rms(x, g):
    ms = jnp.mean(x * x, axis=-1, keepdims=True)
    return x * lax.rsqrt(ms + NORM_EPS) * g


def _norm2_kernel(xp_ref, xs_ref, g_ref, o_ref, *, n_p):
    i = pl.program_id(0)

    @pl.when(i < n_p)
    def _():
        o_ref[...] = _rms(xp_ref[...], g_ref[...]).astype(o_ref.dtype)

    @pl.when(i >= n_p)
    def _():
        o_ref[...] = _rms(xs_ref[...], g_ref[...]).astype(o_ref.dtype)


def _norm_rows2(xp, xs, g):
    n_rows_p, d = xp.shape
    n_rows_s = xs.shape[0]
    tm = _row_tile(n_rows_p, n_rows_s, NORM_ROW_TILE)
    n_p, n_s = n_rows_p // tm, n_rows_s // tm
    return pl.pallas_call(
        functools.partial(_norm2_kernel, n_p=n_p),
        grid=(n_p + n_s,),
        in_specs=[
            pl.BlockSpec((tm, d), lambda i: (jnp.minimum(i, n_p - 1), 0)),
            pl.BlockSpec((tm, d), lambda i: (jnp.maximum(i - n_p, 0), 0)),
            pl.BlockSpec((1, d), lambda i: (0, 0)),
        ],
        out_specs=pl.BlockSpec((tm, d), lambda i: (i, 0)),
        out_shape=jax.ShapeDtypeStruct((n_rows_p + n_rows_s, d), BF16),
        compiler_params=_params("arbitrary"),
        name="norm_rows",
    )(xp, xs, g.reshape(1, d))


def _norm1_kernel(x_ref, g_ref, o_ref):
    o_ref[...] = _rms(x_ref[...], g_ref[...]).astype(o_ref.dtype)


def _norm_rows1(x, g):
    n, d = x.shape
    tm = math.gcd(n, NORM_ROW_TILE)
    return pl.pallas_call(
        _norm1_kernel,
        grid=(n // tm,),
        in_specs=[pl.BlockSpec((tm, d), lambda i: (i, 0)), pl.BlockSpec((1, d), lambda i: (0, 0))],
        out_specs=pl.BlockSpec((tm, d), lambda i: (i, 0)),
        out_shape=jax.ShapeDtypeStruct((n, d), BF16),
        compiler_params=_params("arbitrary"),
        name="norm_mem",
    )(x, g.reshape(1, d))


def _mm_kernel(a_ref, w_ref, o_ref, wb_ref):
    @pl.when(pl.program_id(1) == 0)
    def _():
        wb_ref[...] = w_ref[...].astype(BF16)

    o_ref[...] = jnp.dot(a_ref[...], wb_ref[...], preferred_element_type=F32)


def _matmul(a, w, col0, n_cols, tn, tm, name):
    m, k = a.shape
    tm = math.gcd(m, tm)
    assert n_cols % tn == 0 and col0 % LANES == 0 and tn % LANES == 0
    return pl.pallas_call(
        _mm_kernel,
        grid=(n_cols // tn, m // tm),
        in_specs=[
            pl.BlockSpec((tm, k), lambda j, i: (i, 0)),
            pl.BlockSpec((pl.Element(k), pl.Element(tn)),
                         lambda j, i: (0, pl.multiple_of(col0 + j * tn, LANES))),
        ],
        out_specs=pl.BlockSpec((tm, tn), lambda j, i: (i, j)),
        out_shape=jax.ShapeDtypeStruct((m, n_cols), F32),
        scratch_shapes=[pltpu.VMEM((k, tn), BF16)],
        compiler_params=_params("arbitrary", "arbitrary"),
        name=name,
    )(a, w)


_NN = (((1,), (0,)), ((), ()))
_NT = (((1,), (1,)), ((), ()))
_TN = (((0,), (0,)), ((), ()))


def _bf_parts(x, n):
    parts = []
    r = x
    for i in range(n):
        h = r.astype(BF16)
        parts.append(h)
        if i + 1 < n:
            r = r - h.astype(F32)
    return parts


def _dotp(a, b, dn, pa, pb):
    pa_parts = _bf_parts(a, pa)
    pb_parts = _bf_parts(b, pb)
    acc = None
    for i in range(pa):
        for j in range(pb):
            if i + j < max(pa, pb):
                t = lax.dot_general(pa_parts[i], pb_parts[j], dn, preferred_element_type=F32)
                acc = t if acc is None else acc + t
    return acc


def _softplus(x):
    return jnp.maximum(x, 0.0) + jnp.log(1.0 + jnp.exp(-jnp.abs(x)))


def _sigmoid(x):
    return 1.0 / (1.0 + jnp.exp(-x))


def _rwkv_kernel(*refs, seq_rows, n_seq, has_state, pp):
    if has_state:
        (p_ref, sh0_ref, s0_ref, mu_ref, wda_ref, wa0_ref, wg_ref, vec_ref,
         o_ref, sh_out_ref, s_out_ref, prev_scr, st_scr) = refs
    else:
        (p_ref, mu_ref, wda_ref, wa0_ref, wg_ref, vec_ref,
         o_ref, sh_out_ref, s_out_ref, prev_scr, st_scr) = refs
    c = pl.program_id(1)
    n_c = pl.num_programs(1)
    C = seq_rows
    R = seq_rows * n_seq
    W = RW_WIDTH
    HD = RW_HEAD_DIM

    @pl.when(c == 0)
    def _():
        st_scr[...] = jnp.zeros(st_scr.shape, F32)
        if has_state:
            prev_scr[...] = sh0_ref[...]
            for s in range(n_seq):
                for q in range(RW_PAIRS):
                    st_scr[s, q, 0:HD, 0:HD] = s0_ref[s, 2 * q]
                    st_scr[s, q, HD:2 * HD, HD:2 * HD] = s0_ref[s, 2 * q + 1]
        else:
            prev_scr[...] = jnp.zeros(prev_scr.shape, F32)

    p = p_ref[...]
    row = lax.broadcasted_iota(jnp.int32, (R, 1), 0)
    prev = pltpu.roll(p, 1, axis=0)
    for s in range(n_seq):
        prev = jnp.where(row == s * C, prev_scr[s], prev)
    for s in range(n_seq):
        prev_scr[s] = p[(s + 1) * C - 1:(s + 1) * C, :]
        sh_out_ref[s] = p[(s + 1) * C - 1:(s + 1) * C, :]
    x = p + mu_ref[...] * (prev - p)

    r = x[:, 0:W]
    k = x[:, W:2 * W]
    v = x[:, 2 * W:3 * W]
    wa_in = x[:, 3 * W:3 * W + LANES]
    g_in = x[:, 3 * W + LANES:3 * W + 2 * LANES]

    lane = lax.broadcasted_iota(jnp.int32, (1, LANES), 1)
    wa_act = jnp.where(lane < RW_DECAY_RANK, jnp.tanh(wa_in), wa_in)
    za = jnp.dot(wa_act.astype(BF16), wda_ref[...], preferred_element_type=F32) + wa0_ref[...]
    w_log = -_softplus(-za[:, 0:W]) - 0.5
    lw = -jnp.exp(w_log)
    a = _sigmoid(za[:, W:2 * W])
    g = jnp.dot(_sigmoid(g_in).astype(BF16), wg_ref[...], preferred_element_type=F32)

    k_k = vec_ref[0:1, :]
    k_a = vec_ref[1:2, :]
    r_k = vec_ref[2:3, :]
    ln_w = vec_ref[3:4, :]
    ln_b = vec_ref[4:5, :]

    li = lax.broadcasted_iota(jnp.int32, (LANES, LANES), 0)
    lj = lax.broadcasted_iota(jnp.int32, (LANES, LANES), 1)
    same_head = _div_pow2(li, RW_HEAD_DIM) == _div_pow2(lj, RW_HEAD_DIM)
    ones_bd = jnp.where(same_head, 1.0, 0.0).astype(BF16)

    def head_sum(z):
        cols = []
        for q in range(RW_PAIRS):
            zq = z[:, q * LANES:(q + 1) * LANES]
            acc = None
            for part in _bf_parts(zq, 3):
                t = jnp.dot(part, ones_bd, preferred_element_type=F32)
                acc = t if acc is None else acc + t
            cols.append(acc)
        return jnp.concatenate(cols, axis=1)

    kk = k * k_k
    kk = kk / jnp.maximum(jnp.sqrt(head_sum(kk * kk)), 1e-12)
    k = k * (1.0 + (a - 1.0) * k_a)
    b = kk * a

    ri = lax.broadcasted_iota(jnp.int32, (R, R), 0)
    rj = lax.broadcasted_iota(jnp.int32, (R, R), 1)
    same_seq = _div_pow2(ri, C) == _div_pow2(rj, C)
    causal = jnp.where(same_seq & (rj <= ri), 1.0, 0.0).astype(BF16)
    whole = jnp.where(same_seq, 1.0, 0.0).astype(BF16)
    cl = None
    ct = None
    for part in _bf_parts(lw, 3):
        t1 = jnp.dot(causal, part, preferred_element_type=F32)
        t2 = jnp.dot(whole, part, preferred_element_type=F32)
        cl = t1 if cl is None else cl + t1
        ct = t2 if ct is None else ct + t2
    e_pos = jnp.exp(cl)
    e_neg = jnp.exp(-cl)
    at = -kk * jnp.exp(cl - lw)
    rt = r * e_pos
    bt = b * e_neg
    kt = k * e_neg
    e_end = jnp.exp(ct - cl)
    bts = b * e_end
    kts = k * e_end
    p_end = jnp.exp(ct)

    strict = jnp.where(same_seq & (rj < ri), 1.0, 0.0)
    incl = jnp.where(same_seq & (rj <= ri), 1.0, 0.0)
    strict2 = jnp.concatenate([strict, strict], axis=1)
    incl2 = jnp.concatenate([incl, incl], axis=1)
    m0 = jnp.where(lane < RW_HEAD_DIM, 1.0, 0.0)
    m1 = jnp.where(lane >= RW_HEAD_DIM, 1.0, 0.0)
    diag = jnp.where(li == lj, 1.0, 0.0)
    bd_mask = jnp.where(same_head, 1.0, 0.0)
    zeros_rl = jnp.zeros((R, LANES), F32)
    zeros_r2 = jnp.zeros((R, 2 * R), F32)
    n_dbl = max(1, (C - 1).bit_length())
    pairs = range(RW_PAIRS)
    sls = [slice(q * LANES, (q + 1) * LANES) for q in pairs]

    vq = [v[:, sl] for sl in sls]
    v0 = [jnp.concatenate([vv, zeros_rl], axis=1) for vv in vq]
    zs = []
    for q in pairs:
        sl = sls[q]
        xa, xr = at[:, sl], rt[:, sl]
        x4 = jnp.concatenate([xa * m0, xr * m0, xa * m1, xr * m1], axis=0)
        y4 = jnp.concatenate([bt[:, sl], kt[:, sl], kt[:, sl], bt[:, sl]], axis=0)
        zs.append(_dotp(x4, y4, _NT, pp["z"], pp["z"]))
    zt0 = [zs[q][0:R, 0:2 * R] * strict2 for q in pairs]
    zb0 = [zs[q][R:2 * R, 0:2 * R] * incl2 for q in pairs]
    zt1 = [zs[q][2 * R:3 * R, 2 * R:4 * R] * strict2 for q in pairs]
    zb1 = [zs[q][3 * R:4 * R, 2 * R:4 * R] * incl2 for q in pairs]
    half0 = jnp.where(lax.broadcasted_iota(jnp.int32, (1, 2 * R), 1) < R, 1.0, 0.0)
    half1 = 1.0 - half0
    lbd = [jnp.concatenate([zt0[q] * half0, zt1[q] * half1], axis=0) for q in pairs]
    wmat = []
    for q in pairs:
        lhs = jnp.concatenate([jnp.concatenate([zt0[q], zeros_r2], axis=1),
                               jnp.concatenate([zeros_r2, zt1[q]], axis=1)], axis=0)
        rhs = jnp.concatenate([zeros_rl, vq[q], vq[q], zeros_rl], axis=0)
        lv = _dotp(lhs, rhs, _NN, pp["lv"], pp["lv"])
        xa = at[:, sls[q]]
        wmat.append(jnp.concatenate([lv, jnp.concatenate([xa * m0, xa * m1], axis=0)], axis=1))
    pw = lbd
    for step in range(n_dbl):
        wmat = [wmat[q] + _dotp(pw[q], wmat[q], _NN, pp["app"], pp["app"]) for q in pairs]
        if step + 1 < n_dbl:
            pw = [_dotp(pw[q], pw[q], _NN, pp["pow"], pp["pow"]) for q in pairs]
    rhat, o0 = [], []
    for q in pairs:
        lhs = jnp.concatenate([jnp.concatenate([zb0[q], zeros_r2], axis=1),
                               jnp.concatenate([zeros_r2, zb1[q]], axis=1)], axis=0)
        rhs = jnp.concatenate([wmat[q][0:R], v0[q], v0[q], wmat[q][R:2 * R]], axis=0)
        om = _dotp(lhs, rhs, _NN, pp["om"], pp["om"])
        rhat.append(rt[:, sls[q]] + om[0:R, LANES:] + om[R:2 * R, LANES:])
        o0.append(om[0:R, 0:LANES] * m0 + om[R:2 * R, 0:LANES] * m1)
    hgt = {}
    for q in pairs:
        bq, kq = bts[:, sls[q]], kts[:, sls[q]]
        for s in range(n_seq):
            rs = slice(s * C, (s + 1) * C)
            ys4 = jnp.concatenate([bq[rs] * m0, kq[rs] * m0, kq[rs] * m1, bq[rs] * m1], axis=0)
            r4 = jnp.concatenate([wmat[q][rs], v0[q][rs], v0[q][rs],
                                  wmat[q][R + s * C:R + (s + 1) * C]], axis=0)
            hgt[q, s] = _dotp(r4, ys4, _TN, pp["hg"], pp["hg"])
    o_cols = []
    for q in pairs:
        o_seq = []
        for s in range(n_seq):
            rs = slice(s * C, (s + 1) * C)
            s_old = st_scr[s, q]
            ht = hgt[q, s][0:LANES] * bd_mask
            gt = hgt[q, s][LANES:] + diag * p_end[s * C:s * C + 1, sls[q]]
            o_seq.append(_dotp(rhat[q][rs], s_old, _NT, pp["out"], pp["out"]) + o0[q][rs])
            st_scr[s, q] = _dotp(s_old, gt, _NN, pp["st"], pp["st"]) + ht
        o_cols.append(o_seq[0] if n_seq == 1 else jnp.concatenate(o_seq, axis=0))
    o = jnp.concatenate(o_cols, axis=1)

    inv_n = 1.0 / RW_HEAD_DIM
    mean = head_sum(o) * inv_n
    d = o - mean
    var = head_sum(d * d) * inv_n
    o = d * lax.rsqrt(var + RW_GN_EPS) * ln_w + ln_b
    bonus = head_sum(r * k * r_k) * v
    o_ref[...] = ((o + bonus) * g).astype(o_ref.dtype)

    @pl.when(c == n_c - 1)
    def _():
        for s in range(n_seq):
            for q in range(RW_PAIRS):
                s_out_ref[s, 2 * q] = st_scr[s, q, 0:HD, 0:HD]
                s_out_ref[s, 2 * q + 1] = st_scr[s, q, HD:2 * HD, HD:2 * HD]


def _rwkv(p_rw, row0, n_b, t_len, shift0, s0, prm, pp):
    if t_len >= RWKV_ROWS:
        seq_rows, n_seq = RWKV_ROWS, 1
    else:
        seq_rows, n_seq = t_len, RWKV_ROWS // t_len
    assert t_len % seq_rows == 0 and n_b % n_seq == 0 and row0 % (seq_rows * n_seq) == 0
    rows = seq_rows * n_seq
    n_c = t_len // seq_rows
    blk0 = row0 // rows
    has_state = s0 is not None

    wda = jnp.zeros((LANES, 2 * RW_WIDTH), F32)
    wda = wda.at[0:RW_DECAY_RANK, 0:RW_WIDTH].set(prm["w_decay_up"])
    wda = wda.at[RW_DECAY_RANK:, RW_WIDTH:].set(prm["w_a_up"])
    wa0 = jnp.concatenate([prm["w0_decay"], prm["a0"]]).reshape(1, 2 * RW_WIDTH)
    vecs = jnp.zeros((8, RW_WIDTH), F32)
    for i, name in enumerate(("k_k", "k_a", "r_k", "ln_x_w", "ln_x_b")):
        vecs = vecs.at[i].set(prm[name].reshape(RW_WIDTH))

    const = lambda *shape: pl.BlockSpec(shape, lambda bi, c: (0,) * len(shape))
    in_specs = [pl.BlockSpec((rows, RW_PROJ), lambda bi, c: (blk0 + bi * n_c + c, 0))]
    args = [p_rw]
    st_block = (n_seq, RW_HEADS, RW_HEAD_DIM, RW_HEAD_DIM)
    if has_state:
        in_specs += [pl.BlockSpec((n_seq, 1, RW_PROJ), lambda bi, c: (bi, 0, 0)),
                     pl.BlockSpec(st_block, lambda bi, c: (bi, 0, 0, 0))]
        args += [shift0.reshape(n_b, 1, RW_PROJ), s0]
    in_specs += [const(1, RW_PROJ), const(LANES, 2 * RW_WIDTH), const(1, 2 * RW_WIDTH),
                 const(RW_GATE_RANK, RW_WIDTH), const(8, RW_WIDTH)]
    args += [prm["mu_rwkv"].reshape(1, RW_PROJ), wda.astype(BF16), wa0,
             prm["w_gate_up"].astype(BF16), vecs]

    o, sh, st = pl.pallas_call(
        functools.partial(_rwkv_kernel, seq_rows=seq_rows, n_seq=n_seq, has_state=has_state, pp=pp),
        grid=(n_b // n_seq, n_c),
        in_specs=in_specs,
        out_specs=[
            pl.BlockSpec((rows, RW_WIDTH), lambda bi, c: (bi * n_c + c, 0)),
            pl.BlockSpec((n_seq, 1, RW_PROJ), lambda bi, c: (bi, 0, 0)),
            pl.BlockSpec(st_block, lambda bi, c: (bi, 0, 0, 0)),
        ],
        out_shape=[
            jax.ShapeDtypeStruct((n_b * t_len, RW_WIDTH), BF16),
            jax.ShapeDtypeStruct((n_b, 1, RW_PROJ), F32),
            jax.ShapeDtypeStruct((n_b, RW_HEADS, RW_HEAD_DIM, RW_HEAD_DIM), F32),
        ],
        scratch_shapes=[pltpu.VMEM((n_seq, 1, RW_PROJ), F32),
                        pltpu.VMEM((n_seq, RW_PAIRS, LANES, LANES), F32)],
        compiler_params=_params("arbitrary", "arbitrary"),
        name="rwkv_state" if has_state else "rwkv_fresh",
    )(*args)
    return o, sh.reshape(n_b, RW_PROJ), st


def _sink_probs(s_parts, sink):
    m = sink
    for s in s_parts:
        m = jnp.maximum(m, jnp.max(s, axis=-1, keepdims=True))
    den = jnp.exp(sink - m)
    es = []
    for s in s_parts:
        e = jnp.exp(s - m)
        den = den + jnp.sum(e, axis=-1, keepdims=True)
        es.append(e)
    inv = 1.0 / den
    return [(e * inv).astype(BF16) for e in es]


def _group_queries(q, hk):
    return jnp.concatenate(
        [q[:, (hk * SWA_GROUP + g) * SWA_HEAD_DIM:(hk * SWA_GROUP + g + 1) * SWA_HEAD_DIM]
         for g in range(SWA_GROUP)], axis=0).astype(BF16)


def _group_sinks(sink_ref, hk, grp):
    sink = jnp.zeros(grp.shape, F32)
    for g in range(SWA_GROUP):
        sink = jnp.where(grp == g, sink_ref[hk * SWA_GROUP + g], sink)
    return sink


def _swa_prompt_kernel(sink_ref, q_ref, kp_ref, kc_ref, vp_ref, vc_ref, o_ref):
    n = pl.program_id(1)
    Wn = WINDOW
    q = q_ref[...]
    kf = jnp.concatenate([kp_ref[...], kc_ref[...]], axis=0).astype(BF16)
    vf = jnp.concatenate([vp_ref[...], vc_ref[...]], axis=0).astype(BF16)
    gi = lax.broadcasted_iota(jnp.int32, (SWA_GROUP * Wn, 2 * Wn), 0)
    i = _mod_pow2(gi, Wn)
    j = lax.broadcasted_iota(jnp.int32, (SWA_GROUP * Wn, 2 * Wn), 1)
    first_key = jnp.where(n > 0, 0, Wn)
    ok = (j > i) & (j <= i + Wn) & (j >= first_key)
    grp = _div_pow2(lax.broadcasted_iota(jnp.int32, (SWA_GROUP * Wn, 1), 0), Wn)
    scale = SWA_HEAD_DIM ** -0.5
    heads = range(SWA_KV_HEADS)
    ks = [slice(hk * SWA_HEAD_DIM, (hk + 1) * SWA_HEAD_DIM) for hk in heads]
    sc = [lax.dot_general(_group_queries(q, hk), kf[:, ks[hk]], _NT, preferred_element_type=F32) * scale
          for hk in heads]
    pr = [_sink_probs([jnp.where(ok, sc[hk], -jnp.inf)], _group_sinks(sink_ref, hk, grp))[0] for hk in heads]
    ov = [jnp.dot(pr[hk], vf[:, ks[hk]], preferred_element_type=F32) for hk in heads]
    for hk in heads:
        for g in range(SWA_GROUP):
            h = hk * SWA_GROUP + g
            o_ref[:, h * SWA_HEAD_DIM:(h + 1) * SWA_HEAD_DIM] = ov[hk][g * Wn:(g + 1) * Wn].astype(o_ref.dtype)


def _swa_prompt(p_att, n_b, t_len, sinks):
    nb = t_len // WINDOW
    kcol = SWA_Q_WIDTH // SWA_KV_WIDTH
    vcol = kcol + 1
    cur = lambda col: (lambda b, n: (b * nb + n, col))
    prv = lambda col: (lambda b, n: (b * nb + jnp.maximum(n - 1, 0), col))
    return pl.pallas_call(
        _swa_prompt_kernel,
        grid=(n_b, nb),
        in_specs=[
            pl.BlockSpec(memory_space=pltpu.SMEM),
            pl.BlockSpec((WINDOW, SWA_Q_WIDTH), cur(0)),
            pl.BlockSpec((WINDOW, SWA_KV_WIDTH), prv(kcol)),
            pl.BlockSpec((WINDOW, SWA_KV_WIDTH), cur(kcol)),
            pl.BlockSpec((WINDOW, SWA_KV_WIDTH), prv(vcol)),
            pl.BlockSpec((WINDOW, SWA_KV_WIDTH), cur(vcol)),
        ],
        out_specs=pl.BlockSpec((WINDOW, SWA_Q_WIDTH), lambda b, n: (b * nb + n, 0)),
        out_shape=jax.ShapeDtypeStruct((n_b * t_len, SWA_Q_WIDTH), BF16),
        compiler_params=_params("arbitrary", "arbitrary"),
        name="swa_prompt",
    )(sinks, p_att, p_att, p_att, p_att, p_att)


def _swa_step_kernel(sink_ref, q_ref, kn_ref, vn_ref, ck_ref, cv_ref, o_ref, ok_ref, ov_ref, *, n_seq, t_len):
    T = t_len
    wb = ck_ref.shape[1]
    scale = SWA_HEAD_DIM ** -0.5
    t_row = _mod_pow2(lax.broadcasted_iota(jnp.int32, (SWA_GROUP * T, 1), 0), T)
    grp = _div_pow2(lax.broadcasted_iota(jnp.int32, (SWA_GROUP * T, 1), 0), T)
    jc = lax.broadcasted_iota(jnp.int32, (SWA_GROUP * T, wb), 1)
    jn = lax.broadcasted_iota(jnp.int32, (SWA_GROUP * T, T), 1)
    ok_c = (t_row + wb - jc < WINDOW)
    ok_n = (jn <= t_row)
    ks = [slice(hk * SWA_HEAD_DIM, (hk + 1) * SWA_HEAD_DIM) for hk in range(SWA_KV_HEADS)]
    sinks = [_group_sinks(sink_ref, hk, grp) for hk in range(SWA_KV_HEADS)]
    work = [(s, hk) for s in range(n_seq) for hk in range(SWA_KV_HEADS)]
    rows = [slice(s * T, (s + 1) * T) for s in range(n_seq)]
    knb, vnb, ckb, cvb = [], [], [], []
    for s in range(n_seq):
        kn = kn_ref[rows[s], :]
        vn = vn_ref[rows[s], :]
        ck = ck_ref[s]
        cv = cv_ref[s]
        ok_ref[s, 0:wb - T, :] = ck[T:wb, :]
        ok_ref[s, wb - T:wb, :] = kn
        ov_ref[s, 0:wb - T, :] = cv[T:wb, :]
        ov_ref[s, wb - T:wb, :] = vn
        knb.append(kn.astype(BF16))
        vnb.append(vn.astype(BF16))
        ckb.append(ck.astype(BF16))
        cvb.append(cv.astype(BF16))
    qh = {(s, hk): _group_queries(q_ref[rows[s], :], hk) for s, hk in work}
    sc = {w: lax.dot_general(qh[w], ckb[w[0]][:, ks[w[1]]], _NT, preferred_element_type=F32) * scale for w in work}
    sn = {w: lax.dot_general(qh[w], knb[w[0]][:, ks[w[1]]], _NT, preferred_element_type=F32) * scale for w in work}
    pr = {w: _sink_probs([jnp.where(ok_c, sc[w], -jnp.inf), jnp.where(ok_n, sn[w], -jnp.inf)], sinks[w[1]])
          for w in work}
    ov = {w: jnp.dot(pr[w][0], cvb[w[0]][:, ks[w[1]]], preferred_element_type=F32)
          + jnp.dot(pr[w][1], vnb[w[0]][:, ks[w[1]]], preferred_element_type=F32) for w in work}
    for s, hk in work:
        for g in range(SWA_GROUP):
            h = hk * SWA_GROUP + g
            o_ref[rows[s], h * SWA_HEAD_DIM:(h + 1) * SWA_HEAD_DIM] = ov[s, hk][g * T:(g + 1) * T].astype(o_ref.dtype)


def _swa_step(p_att, row0, n_b, t_len, cache_k, cache_v, sinks):
    wb = cache_k.shape[1]
    assert wb == WINDOW and t_len % 8 == 0 and t_len < wb
    n_seq = max(1, 64 // t_len)
    while n_b % n_seq:
        n_seq //= 2
    rows = n_seq * t_len
    assert rows % 16 == 0 and row0 % rows == 0
    blk0 = row0 // rows
    kcol = SWA_Q_WIDTH // SWA_KV_WIDTH
    ck = cache_k.reshape(n_b, wb, SWA_KV_WIDTH)
    cv = cache_v.reshape(n_b, wb, SWA_KV_WIDTH)
    o, nk, nv = pl.pallas_call(
        functools.partial(_swa_step_kernel, n_seq=n_seq, t_len=t_len),
        grid=(n_b // n_seq,),
        in_specs=[
            pl.BlockSpec(memory_space=pltpu.SMEM),
            pl.BlockSpec((rows, SWA_Q_WIDTH), lambda i: (blk0 + i, 0)),
            pl.BlockSpec((rows, SWA_KV_WIDTH), lambda i: (blk0 + i, kcol)),
            pl.BlockSpec((rows, SWA_KV_WIDTH), lambda i: (blk0 + i, kcol + 1)),
            pl.BlockSpec((n_seq, wb, SWA_KV_WIDTH), lambda i: (i, 0, 0)),
            pl.BlockSpec((n_seq, wb, SWA_KV_WIDTH), lambda i: (i, 0, 0)),
        ],
        out_specs=[
            pl.BlockSpec((rows, SWA_Q_WIDTH), lambda i: (i, 0)),
            pl.BlockSpec((n_seq, wb, SWA_KV_WIDTH), lambda i: (i, 0, 0)),
            pl.BlockSpec((n_seq, wb, SWA_KV_WIDTH), lambda i: (i, 0, 0)),
        ],
        out_shape=[
            jax.ShapeDtypeStruct((n_b * t_len, SWA_Q_WIDTH), BF16),
            jax.ShapeDtypeStruct((n_b, wb, SWA_KV_WIDTH), F32),
            jax.ShapeDtypeStruct((n_b, wb, SWA_KV_WIDTH), F32),
        ],
        compiler_params=_params("arbitrary"),
        name="swa_step",
    )(sinks, p_att, p_att, p_att, ck, cv)
    shape = (n_b, wb, SWA_KV_HEADS, SWA_HEAD_DIM)
    return o, nk.reshape(shape), nv.reshape(shape)


def _softmax(s):
    m = jnp.max(s, axis=-1, keepdims=True)
    e = jnp.exp(s - m)
    return (e / jnp.sum(e, axis=-1, keepdims=True)).astype(BF16)


def _mem_attend(items, o_ref):
    scale = MEM_HEAD_DIM ** -0.5
    sc = [lax.dot_general(q, k, _NT, preferred_element_type=F32) * scale for q, k, _, _, _ in items]
    pr = [_softmax(s) for s in sc]
    ov = [jnp.dot(p, it[2], preferred_element_type=F32) for p, it in zip(pr, items)]
    for o, it in zip(ov, items):
        o_ref[it[3], it[4]] = o.astype(o_ref.dtype)


def _mem_prompt_kernel(qa_ref, qb_ref, k_ref, v_ref, o_ref):
    kb = k_ref[...].astype(BF16)
    vb = v_ref[...].astype(BF16)
    half = MEM_HEADS // 2
    items = []
    for h in range(MEM_HEADS):
        q_ref = qa_ref if h < half else qb_ref
        hs = slice((h % half) * MEM_HEAD_DIM, (h % half + 1) * MEM_HEAD_DIM)
        ms = slice(h * MEM_HEAD_DIM, (h + 1) * MEM_HEAD_DIM)
        items.append((q_ref[:, hs].astype(BF16), kb[:, ms], vb[:, ms], slice(None), ms))
    _mem_attend(items, o_ref)


def _mem_prompt(p_att, n_b, t_len, mem_kv):
    m_tok = mem_kv.shape[0] // n_b
    tq = math.gcd(t_len, 512)
    nq = t_len // tq
    half_w = MEM_WIDTH // 2
    qcol = (SWA_Q_WIDTH + 2 * SWA_KV_WIDTH) // half_w
    return pl.pallas_call(
        _mem_prompt_kernel,
        grid=(n_b, nq),
        in_specs=[
            pl.BlockSpec((tq, half_w), lambda b, n: (b * nq + n, qcol)),
            pl.BlockSpec((tq, half_w), lambda b, n: (b * nq + n, qcol + 1)),
            pl.BlockSpec((m_tok, MEM_WIDTH), lambda b, n: (b, 0)),
            pl.BlockSpec((m_tok, MEM_WIDTH), lambda b, n: (b, 1)),
        ],
        out_specs=pl.BlockSpec((tq, MEM_WIDTH), lambda b, n: (b * nq + n, 0)),
        out_shape=jax.ShapeDtypeStruct((n_b * t_len, MEM_WIDTH), BF16),
        compiler_params=_params("arbitrary", "arbitrary"),
        name="mem_prompt",
    )(p_att, p_att, mem_kv, mem_kv)


def _mem_step_kernel(qa_ref, qb_ref, k_ref, v_ref, o_ref, *, n_seq, t_len):
    half = MEM_HEADS // 2
    items = []
    for s in range(n_seq):
        rs = slice(s * t_len, (s + 1) * t_len)
        kb = k_ref[s].astype(BF16)
        vb = v_ref[s].astype(BF16)
        for h in range(MEM_HEADS):
            q_ref = qa_ref if h < half else qb_ref
            hs = slice((h % half) * MEM_HEAD_DIM, (h % half + 1) * MEM_HEAD_DIM)
            ms = slice(h * MEM_HEAD_DIM, (h + 1) * MEM_HEAD_DIM)
            items.append((q_ref[rs, hs].astype(BF16), kb[:, ms], vb[:, ms], rs, ms))
    _mem_attend(items, o_ref)


def _mem_step(p_att, row0, n_b, t_len, cache_k, cache_v):
    m_tok = cache_k.shape[1]
    n_seq = max(1, 64 // t_len)
    while n_b % n_seq:
        n_seq //= 2
    rows = n_seq * t_len
    assert rows % 16 == 0 and row0 % rows == 0
    blk0 = row0 // rows
    half_w = MEM_WIDTH // 2
    qcol = (SWA_Q_WIDTH + 2 * SWA_KV_WIDTH) // half_w
    ck = cache_k.reshape(n_b, m_tok, MEM_WIDTH)
    cv = cache_v.reshape(n_b, m_tok, MEM_WIDTH)
    return pl.pallas_call(
        functools.partial(_mem_step_kernel, n_seq=n_seq, t_len=t_len),
        grid=(n_b // n_seq,),
        in_specs=[
            pl.BlockSpec((rows, half_w), lambda i: (blk0 + i, qcol)),
            pl.BlockSpec((rows, half_w), lambda i: (blk0 + i, qcol + 1)),
            pl.BlockSpec((n_seq, m_tok, MEM_WIDTH), lambda i: (i, 0, 0)),
            pl.BlockSpec((n_seq, m_tok, MEM_WIDTH), lambda i: (i, 0, 0)),
        ],
        out_specs=pl.BlockSpec((rows, MEM_WIDTH), lambda i: (i, 0)),
        out_shape=jax.ShapeDtypeStruct((n_b * t_len, MEM_WIDTH), BF16),
        compiler_params=_params("arbitrary"),
        name="mem_step",
    )(p_att, p_att, ck, cv)


def _merge_kernel(xn_ref, rwp_ref, rws_ref, swp_ref, sws_ref, mmp_ref, mms_ref,
                  wg0_ref, wg1_ref, wg2_ref, wb0_ref, wb1_ref, wb2_ref, o_ref,
                  g0_scr, g1_scr, g2_scr, b0_scr, b1_scr, b2_scr, *, n_p):
    i = pl.program_id(1)

    @pl.when(i == 0)
    def _():
        for src, dst in ((wg0_ref, g0_scr), (wg1_ref, g1_scr), (wg2_ref, g2_scr),
                         (wb0_ref, b0_scr), (wb1_ref, b1_scr), (wb2_ref, b2_scr)):
            dst[...] = src[...].astype(BF16)

    def run(o_rw_ref, o_sw_ref, o_mm_ref):
        xn = xn_ref[...]
        acc = None
        for g_scr, b_scr, br_ref in ((g0_scr, b0_scr, o_rw_ref), (g1_scr, b1_scr, o_sw_ref),
                                     (g2_scr, b2_scr, o_mm_ref)):
            gate = _sigmoid(jnp.dot(xn, g_scr[...], preferred_element_type=F32))
            t = gate * jnp.dot(br_ref[...], b_scr[...], preferred_element_type=F32)
            acc = t if acc is None else acc + t
        o_ref[...] = acc.astype(o_ref.dtype)

    @pl.when(i < n_p)
    def _():
        run(rwp_ref, swp_ref, mmp_ref)

    @pl.when(i >= n_p)
    def _():
        run(rws_ref, sws_ref, mms_ref)


def _merge(xn, o_p, o_s, w_in, gate_col0, w_br, n_rows_p):
    m, d = xn.shape
    tn = MXU_COLS
    tm = _row_tile(n_rows_p, m - n_rows_p, ROW_TILE)
    n_p = n_rows_p // tm
    n_i = m // tm
    assert gate_col0 % tn == 0 and d % tn == 0
    g0 = gate_col0 // tn
    nj = d // tn
    p_spec = lambda w: pl.BlockSpec((tm, w), lambda j, i: (jnp.minimum(i, n_p - 1), 0))
    s_spec = lambda w: pl.BlockSpec((tm, w), lambda j, i: (jnp.maximum(i - n_p, 0), 0))
    in_specs = [pl.BlockSpec((tm, d), lambda j, i: (i, 0))]
    args = [xn]
    for op, os_ in zip(o_p, o_s):
        in_specs += [p_spec(op.shape[1]), s_spec(os_.shape[1])]
        args += [op, os_]
    for br in range(N_BRANCH):
        in_specs.append(pl.BlockSpec((d, tn), lambda j, i, br=br: (0, g0 + br * nj + j)))
        args.append(w_in)
    for w in w_br:
        in_specs.append(pl.BlockSpec((w.shape[0], tn), lambda j, i: (0, j)))
        args.append(w)
    scratch = [pltpu.VMEM((d, tn), BF16) for _ in range(N_BRANCH)]
    scratch += [pltpu.VMEM((w.shape[0], tn), BF16) for w in w_br]
    return pl.pallas_call(
        functools.partial(_merge_kernel, n_p=n_p),
        grid=(nj, n_i),
        in_specs=in_specs,
        out_specs=pl.BlockSpec((tm, tn), lambda j, i: (i, j)),
        out_shape=jax.ShapeDtypeStruct((m, d), BF16),
        scratch_shapes=scratch,
        compiler_params=_params("arbitrary", "arbitrary"),
        name="gated_merge",
    )(*args)


def _cast_kernel(x_ref, o_ref):
    o_ref[...] = x_ref[...].astype(o_ref.dtype)


def _to_bf16(w, name):
    r, c = w.shape
    tr = math.gcd(r, 512)
    return pl.pallas_call(
        _cast_kernel,
        grid=(r // tr,),
        in_specs=[pl.BlockSpec((tr, c), lambda i: (i, 0))],
        out_specs=pl.BlockSpec((tr, c), lambda i: (i, 0)),
        out_shape=jax.ShapeDtypeStruct((r, c), BF16),
        compiler_params=_params("arbitrary"),
        name=name,
    )(w)


def _wo_kernel(m_ref, w_ref, xp_ref, xs_ref, g1_ref, g2_ref, h_ref, hn_ref, *, n_p):
    i = pl.program_id(0)
    tm = m_ref.shape[0]
    halves = [slice(0, tm // 2), slice(tm // 2, tm)] if tm % 32 == 0 else [slice(0, tm)]

    def run(x_ref):
        for rs in halves:
            f = jnp.dot(m_ref[rs, :], w_ref[...], preferred_element_type=F32)
            h = x_ref[rs, :] + _rms(f, g1_ref[...])
            h_ref[rs, :] = h
            hn_ref[rs, :] = _rms(h, g2_ref[...]).astype(hn_ref.dtype)

    @pl.when(i < n_p)
    def _():
        run(xp_ref)

    @pl.when(i >= n_p)
    def _():
        run(xs_ref)


def _wo_block(merged, w_o, xp, xs, g_post, g_pre_ffn):
    m, d = merged.shape
    n_rows_p = xp.shape[0]
    tm = _row_tile(n_rows_p, m - n_rows_p, NORM_ROW_TILE)
    n_p = n_rows_p // tm
    return pl.pallas_call(
        functools.partial(_wo_kernel, n_p=n_p),
        grid=(m // tm,),
        in_specs=[
            pl.BlockSpec((tm, d), lambda i: (i, 0)),
            pl.BlockSpec((d, d), lambda i: (0, 0), pipeline_mode=pl.Buffered(1)),
            pl.BlockSpec((tm, d), lambda i: (jnp.minimum(i, n_p - 1), 0)),
            pl.BlockSpec((tm, d), lambda i: (jnp.maximum(i - n_p, 0), 0)),
            pl.BlockSpec((1, d), lambda i: (0, 0)),
            pl.BlockSpec((1, d), lambda i: (0, 0)),
        ],
        out_specs=[pl.BlockSpec((tm, d), lambda i: (i, 0)), pl.BlockSpec((tm, d), lambda i: (i, 0))],
        out_shape=[jax.ShapeDtypeStruct((m, d), F32), jax.ShapeDtypeStruct((m, d), BF16)],
        compiler_params=_params("arbitrary"),
        name="wo_residual",
    )(merged, _to_bf16(w_o, "cast_w_o"), xp, xs, g_post.reshape(1, d), g_pre_ffn.reshape(1, d))


def _gelu_tanh(x):
    return 0.5 * x * (1.0 + jnp.tanh(math.sqrt(2.0 / math.pi) * (x + 0.044715 * (x * x * x))))


def _ffn_up_kernel(hn_ref, wg_ref, wv_ref, cw_ref, cb_ref, st_ref, act_ref, cp_ref, cs_ref,
                   wgb_scr, wvb_scr, carry_scr, *, n_p, blocks_per_seq, t_s):
    i = pl.program_id(1)

    @pl.when(i == 0)
    def _():
        wgb_scr[...] = wg_ref[...].astype(BF16)
        wvb_scr[...] = wv_ref[...].astype(BF16)

    tm = hn_ref.shape[0]
    tn = wgb_scr.shape[1]
    subs = [slice(c0, c0 + MXU_COLS) for c0 in range(0, tn, MXU_COLS)] if tn % MXU_COLS == 0 else [slice(0, tn)]

    def gate_val(cs):
        hn = hn_ref[...]
        zg = jnp.dot(hn, wgb_scr[:, cs], preferred_element_type=F32)
        zv = jnp.dot(hn, wvb_scr[:, cs], preferred_element_type=F32)
        return zg, zv

    @pl.when(i < n_p)
    def _():
        @pl.when(i % blocks_per_seq == 0)
        def _():
            carry_scr[...] = jnp.zeros(carry_scr.shape, F32)

        row = lax.broadcasted_iota(jnp.int32, (tm, 1), 0)
        for cs in subs:
            zg, zv = gate_val(cs)
            p1 = carry_scr[1:2, cs]
            p2 = carry_scr[0:1, cs]
            zm1 = jnp.where(row == 0, p1, pltpu.roll(zg, 1, axis=0))
            zm2 = jnp.where(row == 0, p2, jnp.where(row == 1, p1, pltpu.roll(zg, 2, axis=0)))
            conv = cb_ref[:, cs] + cw_ref[0:1, cs] * zm2 + cw_ref[1:2, cs] * zm1 + cw_ref[2:3, cs] * zg
            act_ref[:, cs] = (_gelu_tanh(conv) * zv).astype(act_ref.dtype)
            carry_scr[0:2, cs] = zg[tm - 2:tm, :]
            cp_ref[0, :, cs] = zg[tm - 2:tm, :]

    @pl.when(i >= n_p)
    def _():
        t = lax.broadcasted_iota(jnp.int32, (1, t_s, 1), 1)
        for cs in subs:
            zg, zv = gate_val(cs)
            tc = zg.shape[1]
            z3 = zg.reshape(tm // t_s, t_s, tc)
            s1 = st_ref[:, 1:2, cs]
            s0 = st_ref[:, 0:1, cs]
            zm1 = jnp.where(t == 0, s1, pltpu.roll(z3, 1, axis=1))
            zm2 = jnp.where(t == 0, s0, jnp.where(t == 1, s1, pltpu.roll(z3, 2, axis=1)))
            conv = cb_ref[:, cs] + cw_ref[0:1, cs] * zm2 + cw_ref[1:2, cs] * zm1 + cw_ref[2:3, cs] * z3
            act = _gelu_tanh(conv) * zv.reshape(tm // t_s, t_s, tc)
            act_ref[:, cs] = act.reshape(tm, tc).astype(act_ref.dtype)
            cs_ref[:, :, cs] = z3[:, t_s - 2:t_s, :]


def _ffn_up(hn, w_up, conv_w, conv_b, conv_state, n_b_p, t_p, n_b_s, t_s):
    m, d = hn.shape
    d_ff = conv_b.shape[0]
    n_rows_p = n_b_p * t_p
    n_rows_s = n_b_s * t_s
    tm = _row_tile(math.gcd(n_rows_p, t_p), n_rows_s, ROW_TILE)
    assert tm == n_rows_s and t_s % 8 == 0 and t_s >= 2, "sample rows must form one row block"
    n_p = n_rows_p // tm
    blocks_per_seq = t_p // tm
    tn = math.gcd(d_ff, 512)
    nj = d_ff // tn
    last_seq = n_b_p - 1
    return pl.pallas_call(
        functools.partial(_ffn_up_kernel, n_p=n_p, blocks_per_seq=blocks_per_seq, t_s=t_s),
        grid=(nj, n_p + 1),
        in_specs=[
            pl.BlockSpec((tm, d), lambda j, i: (i, 0)),
            pl.BlockSpec((d, tn), lambda j, i: (0, j)),
            pl.BlockSpec((d, tn), lambda j, i: (0, nj + j)),
            pl.BlockSpec((CONV_W, tn), lambda j, i: (0, j)),
            pl.BlockSpec((1, tn), lambda j, i: (0, j)),
            pl.BlockSpec((n_b_s, CONV_W - 1, tn), lambda j, i: (0, 0, j)),
        ],
        out_specs=[
            pl.BlockSpec((tm, tn), lambda j, i: (i, j)),
            pl.BlockSpec((1, CONV_W - 1, tn), lambda j, i: (jnp.minimum(i // blocks_per_seq, last_seq), 0, j)),
            pl.BlockSpec((n_b_s, CONV_W - 1, tn), lambda j, i: (0, 0, j)),
        ],
        out_shape=[
            jax.ShapeDtypeStruct((m, d_ff), BF16),
            jax.ShapeDtypeStruct((n_b_p, CONV_W - 1, d_ff), F32),
            jax.ShapeDtypeStruct((n_b_s, CONV_W - 1, d_ff), F32),
        ],
        scratch_shapes=[pltpu.VMEM((d, tn), BF16), pltpu.VMEM((d, tn), BF16), pltpu.VMEM((8, tn), F32)],
        compiler_params=_params("arbitrary", "arbitrary"),
        name="ffn_up_convglu",
    )(hn, w_up, w_up, conv_w, conv_b.reshape(1, d_ff), conv_state)


def _ffn_down_kernel(a_ref, w_ref, h_ref, g_ref, y_ref):
    kk = pl.program_id(1)

    @pl.when(kk == 0)
    def _():
        y_ref[...] = jnp.zeros(y_ref.shape, F32)

    y_ref[...] += jnp.dot(a_ref[...], w_ref[...].astype(BF16), preferred_element_type=F32)

    @pl.when(kk == pl.num_programs(1) - 1)
    def _():
        y_ref[...] = h_ref[...] + _rms(y_ref[...], g_ref[...])


def _ffn_down(act, w_down, h, g, row0, n_rows):
    d_ff, d = w_down.shape
    tm = math.gcd(math.gcd(n_rows, row0) if row0 else n_rows, ROW_TILE)
    tk = math.gcd(d_ff, 512)
    i0 = row0 // tm
    return pl.pallas_call(
        _ffn_down_kernel,
        grid=(n_rows // tm, d_ff // tk),
        in_specs=[
            pl.BlockSpec((tm, tk), lambda i, k: (i0 + i, k)),
            pl.BlockSpec((tk, d), lambda i, k: (k, 0)),
            pl.BlockSpec((tm, d), lambda i, k: (i0 + i, 0)),
            pl.BlockSpec((1, d), lambda i, k: (0, 0)),
        ],
        out_specs=pl.BlockSpec((tm, d), lambda i, k: (i, 0)),
        out_shape=jax.ShapeDtypeStruct((n_rows, d), F32),
        compiler_params=_params("arbitrary", "arbitrary"),
        name="ffn_down",
    )(act, w_down, h, g.reshape(1, d))


def kernel(x_prompt, x_sample, cache_swa_k, cache_swa_v, cache_mem_k, cache_mem_v, state_rwkv_shift, state_rwkv_wkv, state_ffn_conv, mem_prompt, g_pre_mix, w_in, mu_rwkv, w_decay_up, w0_decay, w_a_up, a0, w_gate_up, k_k, k_a, r_k, ln_x_w, ln_x_b, swa_sinks, g_mem, w_mem_kv, w_br_rwkv, w_br_swa, w_br_mem, w_o, g_post_mix, g_pre_ffn, w_ffn_up, conv_w, conv_b, w_ffn_down, g_post_ffn):
    b_p, t_p, d = x_prompt.shape
    b_s, t_s, _ = x_sample.shape
    n_p, n_s = b_p * t_p, b_s * t_s
    m_tok = mem_prompt.shape[1]
    rw_prm = dict(mu_rwkv=mu_rwkv, w_decay_up=w_decay_up, w0_decay=w0_decay, w_a_up=w_a_up, a0=a0,
                  w_gate_up=w_gate_up, k_k=k_k, k_a=k_a, r_k=r_k, ln_x_w=ln_x_w, ln_x_b=ln_x_b)
    xp = x_prompt.reshape(n_p, d)
    xs = x_sample.reshape(n_s, d)

    xn = _norm_rows2(xp, xs, g_pre_mix)
    p_rw = _matmul(xn, w_in, 0, RW_PROJ, RW_PROJ // 2, ROW_TILE, "proj_rwkv")
    p_att = _matmul(xn, w_in, RW_PROJ, ATT_WIDTH, ATT_WIDTH, NORM_ROW_TILE, "proj_attn")

    mem_kv = _matmul(_norm_rows1(mem_prompt.reshape(b_p * m_tok, d), g_mem), w_mem_kv,
                     0, 2 * MEM_WIDTH, MEM_WIDTH, ROW_TILE, "proj_mem_kv")

    o_rw_p, shift_p, wkv_p = _rwkv(p_rw, 0, b_p, t_p, None, None, rw_prm, RWKV_PARTS)
    o_rw_s, shift_s, wkv_s = _rwkv(p_rw, n_p, b_s, t_s, state_rwkv_shift, state_rwkv_wkv, rw_prm, RWKV_PARTS)

    o_sw_p = _swa_prompt(p_att, b_p, t_p, swa_sinks)
    o_sw_s, swa_k_s, swa_v_s = _swa_step(p_att, n_p, b_s, t_s, cache_swa_k, cache_swa_v, swa_sinks)

    o_mm_p = _mem_prompt(p_att, b_p, t_p, mem_kv)
    o_mm_s = _mem_step(p_att, n_p, b_s, t_s, cache_mem_k, cache_mem_v)

    merged = _merge(xn, (o_rw_p, o_sw_p, o_mm_p), (o_rw_s, o_sw_s, o_mm_s), w_in,
                    RW_PROJ + ATT_WIDTH, (w_br_rwkv, w_br_swa, w_br_mem), n_p)
    h, hn = _wo_block(merged, w_o, xp, xs, g_post_mix, g_pre_ffn)
    act, conv_p, conv_s = _ffn_up(hn, w_ffn_up, conv_w, conv_b, state_ffn_conv, b_p, t_p, b_s, t_s)
    y_p = _ffn_down(act, w_ffn_down, h, g_post_ffn, 0, n_p)
    y_s = _ffn_down(act, w_ffn_down, h, g_post_ffn, n_p, n_s)

    keep = min(WINDOW, t_p)
    kv_p = jnp.stack([lax.slice(p_att, ((b + 1) * t_p - keep, SWA_Q_WIDTH),
                                ((b + 1) * t_p, SWA_Q_WIDTH + 2 * SWA_KV_WIDTH)) for b in range(b_p)])
    swa_k_p = kv_p[..., :SWA_KV_WIDTH].reshape(b_p, keep, SWA_KV_HEADS, SWA_HEAD_DIM)
    swa_v_p = kv_p[..., SWA_KV_WIDTH:].reshape(b_p, keep, SWA_KV_HEADS, SWA_HEAD_DIM)
    mem_k_p = mem_kv[:, :MEM_WIDTH].reshape(b_p, m_tok, MEM_HEADS, MEM_HEAD_DIM)
    mem_v_p = mem_kv[:, MEM_WIDTH:].reshape(b_p, m_tok, MEM_HEADS, MEM_HEAD_DIM)
    return (y_p.reshape(b_p, t_p, d), y_s.reshape(b_s, t_s, d), swa_k_p, swa_v_p, mem_k_p, mem_v_p,
            shift_p, wkv_p, conv_p, swa_k_s, swa_v_s, shift_s, wkv_s, conv_s)
```

```python
import functools
import math

import jax
import jax.numpy as jnp
from jax import lax
from jax.experimental import pallas as pl
from jax.experimental.pallas import tpu as pltpu

F32 = jnp.float32
BF16 = jnp.bfloat16

RW_HEADS = 12
RW_HEAD_DIM = 64
RW_WIDTH = RW_HEADS * RW_HEAD_DIM
RW_DECAY_RANK = 64
RW_A_RANK = 64
RW_GATE_RANK = 128
RW_PROJ = 3 * RW_WIDTH + RW_DECAY_RANK + RW_A_RANK + RW_GATE_RANK
RW_GN_EPS = 6.4e-4
RW_PAIRS = RW_HEADS // 2
SWA_Q_HEADS = 12
SWA_KV_HEADS = 4
SWA_GROUP = SWA_Q_HEADS // SWA_KV_HEADS
SWA_HEAD_DIM = 64
SWA_Q_WIDTH = SWA_Q_HEADS * SWA_HEAD_DIM
SWA_KV_WIDTH = SWA_KV_HEADS * SWA_HEAD_DIM
WINDOW = 128
MEM_HEADS = 4
MEM_HEAD_DIM = 128
MEM_WIDTH = MEM_HEADS * MEM_HEAD_DIM
N_BRANCH = 3
CONV_W = 3
NORM_EPS = 1e-6
ATT_WIDTH = SWA_Q_WIDTH + 2 * SWA_KV_WIDTH + MEM_WIDTH

LANES = 128
MXU_COLS = 256
VMEM_LIMIT_BYTES = 56 * 1024 * 1024

RWKV_ROWS = 64
RWKV_PARTS = dict(z=1, lv=1, pow=1, app=1, om=1, hg=1, out=1, st=1)
ROW_TILE = 1024
NORM_ROW_TILE = 512


def _params(*sem):
    return pltpu.CompilerParams(dimension_semantics=sem, vmem_limit_bytes=VMEM_LIMIT_BYTES)


def _row_tile(n_p, n_s, target):
    t = math.gcd(math.gcd(n_p, n_s), target)
    assert t % 16 == 0, (n_p, n_s, target)
    return t


def _div_pow2(x, n):
    assert n & (n - 1) == 0
    return lax.shift_right_logical(x, n.bit_length() - 1)


def _mod_pow2(x, n):
    assert n & (n - 1) == 0
    return jnp.bitwise_and(x, n - 1)


def _rms(x, g):
    ms = jnp.mean(x * x, axis=-1, keepdims=True)
    return x * lax.rsqrt(ms + NORM_EPS) * g


def _norm2_kernel(xp_ref, xs_ref, g_ref, o_ref, *, n_p):
    i = pl.program_id(0)

    @pl.when(i < n_p)
    def _():
        o_ref[...] = _rms(xp_ref[...], g_ref[...]).astype(o_ref.dtype)

    @pl.when(i >= n_p)
    def _():
        o_ref[...] = _rms(xs_ref[...], g_ref[...]).astype(o_ref.dtype)


def _norm_rows2(xp, xs, g):
    n_rows_p, d = xp.shape
    n_rows_s = xs.shape[0]
    tm = _row_tile(n_rows_p, n_rows_s, NORM_ROW_TILE)
    n_p, n_s = n_rows_p // tm, n_rows_s // tm
    return pl.pallas_call(
        functools.partial(_norm2_kernel, n_p=n_p),
        grid=(n_p + n_s,),
        in_specs=[
            pl.BlockSpec((tm, d), lambda i: (jnp.minimum(i, n_p - 1), 0)),
            pl.BlockSpec((tm, d), lambda i: (jnp.maximum(i - n_p, 0), 0)),
            pl.BlockSpec((1, d), lambda i: (0, 0)),
        ],
        out_specs=pl.BlockSpec((tm, d), lambda i: (i, 0)),
        out_shape=jax.ShapeDtypeStruct((n_rows_p + n_rows_s, d), BF16),
        compiler_params=_params("arbitrary"),
        name="norm_rows",
    )(xp, xs, g.reshape(1, d))


def _norm1_kernel(x_ref, g_ref, o_ref):
    o_ref[...] = _rms(x_ref[...], g_ref[...]).astype(o_ref.dtype)


def _norm_rows1(x, g):
    n, d = x.shape
    tm = math.gcd(n, NORM_ROW_TILE)
    return pl.pallas_call(
        _norm1_kernel,
        grid=(n // tm,),
        in_specs=[pl.BlockSpec((tm, d), lambda i: (i, 0)), pl.BlockSpec((1, d), lambda i: (0, 0))],
        out_specs=pl.BlockSpec((tm, d), lambda i: (i, 0)),
        out_shape=jax.ShapeDtypeStruct((n, d), BF16),
        compiler_params=_params("arbitrary"),
        name="norm_mem",
    )(x, g.reshape(1, d))


def _mm_kernel(a_ref, w_ref, o_ref, wb_ref):
    @pl.when(pl.program_id(1) == 0)
    def _():
        wb_ref[...] = w_ref[...].astype(BF16)

    o_ref[...] = jnp.dot(a_ref[...], wb_ref[...], preferred_element_type=F32)


def _matmul(a, w, col0, n_cols, tn, tm, name):
    m, k = a.shape
    tm = math.gcd(m, tm)
    assert n_cols % tn == 0 and col0 % LANES == 0 and tn % LANES == 0
    return pl.pallas_call(
        _mm_kernel,
        grid=(n_cols // tn, m // tm),
        in_specs=[
            pl.BlockSpec((tm, k), lambda j, i: (i, 0)),
            pl.BlockSpec((pl.Element(k), pl.Element(tn)),
                         lambda j, i: (0, pl.multiple_of(col0 + j * tn, LANES))),
        ],
        out_specs=pl.BlockSpec((tm, tn), lambda j, i: (i, j)),
        out_shape=jax.ShapeDtypeStruct((m, n_cols), F32),
        scratch_shapes=[pltpu.VMEM((k, tn), BF16)],
        compiler_params=_params("arbitrary", "arbitrary"),
        name=name,
    )(a, w)


_NN = (((1,), (0,)), ((), ()))
_NT = (((1,), (1,)), ((), ()))
_TN = (((0,), (0,)), ((), ()))


def _bf_parts(x, n):
    parts = []
    r = x
    for i in range(n):
        h = r.astype(BF16)
        parts.append(h)
        if i + 1 < n:
            r = r - h.astype(F32)
    return parts


def _dotp(a, b, dn, pa, pb):
    pa_parts = _bf_parts(a, pa)
    pb_parts = _bf_parts(b, pb)
    acc = None
    for i in range(pa):
        for j in range(pb):
            if i + j < max(pa, pb):
                t = lax.dot_general(pa_parts[i], pb_parts[j], dn, preferred_element_type=F32)
                acc = t if acc is None else acc + t
    return acc


def _softplus(x):
    return jnp.maximum(x, 0.0) + jnp.log(1.0 + jnp.exp(-jnp.abs(x)))


def _sigmoid(x):
    return 1.0 / (1.0 + jnp.exp(-x))


def _rwkv_kernel(*refs, seq_rows, n_seq, has_state, pp):
    if has_state:
        (p_ref, sh0_ref, s0_ref, mu_ref, wda_ref, wa0_ref, wg_ref, vec_ref,
         o_ref, sh_out_ref, s_out_ref, prev_scr, st_scr) = refs
    else:
        (p_ref, mu_ref, wda_ref, wa0_ref, wg_ref, vec_ref,
         o_ref, sh_out_ref, s_out_ref, prev_scr, st_scr) = refs
    c = pl.program_id(1)
    n_c = pl.num_programs(1)
    C = seq_rows
    R = seq_rows * n_seq
    W = RW_WIDTH
    HD = RW_HEAD_DIM

    @pl.when(c == 0)
    def _():
        st_scr[...] = jnp.zeros(st_scr.shape, F32)
        if has_state:
            prev_scr[...] = sh0_ref[...]
            for s in range(n_seq):
                for q in range(RW_PAIRS):
                    st_scr[s, q, 0:HD, 0:HD] = s0_ref[s, 2 * q]
                    st_scr[s, q, HD:2 * HD, HD:2 * HD] = s0_ref[s, 2 * q + 1]
        else:
            prev_scr[...] = jnp.zeros(prev_scr.shape, F32)

    p = p_ref[...]
    row = lax.broadcasted_iota(jnp.int32, (R, 1), 0)
    prev = pltpu.roll(p, 1, axis=0)
    for s in range(n_seq):
        prev = jnp.where(row == s * C, prev_scr[s], prev)
    for s in range(n_seq):
        prev_scr[s] = p[(s + 1) * C - 1:(s + 1) * C, :]
        sh_out_ref[s] = p[(s + 1) * C - 1:(s + 1) * C, :]
    x = p + mu_ref[...] * (prev - p)

    r = x[:, 0:W]
    k = x[:, W:2 * W]
    v = x[:, 2 * W:3 * W]
    wa_in = x[:, 3 * W:3 * W + LANES]
    g_in = x[:, 3 * W + LANES:3 * W + 2 * LANES]

    lane = lax.broadcasted_iota(jnp.int32, (1, LANES), 1)
    wa_act = jnp.where(lane < RW_DECAY_RANK, jnp.tanh(wa_in), wa_in)
    za = jnp.dot(wa_act.astype(BF16), wda_ref[...], preferred_element_type=F32) + wa0_ref[...]
    w_log = -_softplus(-za[:, 0:W]) - 0.5
    lw = -jnp.exp(w_log)
    a = _sigmoid(za[:, W:2 * W])
    g = jnp.dot(_sigmoid(g_in).astype(BF16), wg_ref[...], preferred_element_type=F32)

    k_k = vec_ref[0:1, :]
    k_a = vec_ref[1:2, :]
    r_k = vec_ref[2:3, :]
    ln_w = vec_ref[3:4, :]
    ln_b = vec_ref[4:5, :]

    li = lax.broadcasted_iota(jnp.int32, (LANES, LANES), 0)
    lj = lax.broadcasted_iota(jnp.int32, (LANES, LANES), 1)
    same_head = _div_pow2(li, RW_HEAD_DIM) == _div_pow2(lj, RW_HEAD_DIM)
    ones_bd = jnp.where(same_head, 1.0, 0.0).astype(BF16)

    def head_sum(z):
        cols = []
        for q in range(RW_PAIRS):
            zq = z[:, q * LANES:(q + 1) * LANES]
            acc = None
            for part in _bf_parts(zq, 3):
                t = jnp.dot(part, ones_bd, preferred_element_type=F32)
                acc = t if acc is None else acc + t
            cols.append(acc)
        return jnp.concatenate(cols, axis=1)

    kk = k * k_k
    kk = kk / jnp.maximum(jnp.sqrt(head_sum(kk * kk)), 1e-12)
    k = k * (1.0 + (a - 1.0) * k_a)
    b = kk * a

    ri = lax.broadcasted_iota(jnp.int32, (R, R), 0)
    rj = lax.broadcasted_iota(jnp.int32, (R, R), 1)
    same_seq = _div_pow2(ri, C) == _div_pow2(rj, C)
    causal = jnp.where(same_seq & (rj <= ri), 1.0, 0.0).astype(BF16)
    whole = jnp.where(same_seq, 1.0, 0.0).astype(BF16)
    cl = None
    ct = None
    for part in _bf_parts(lw, 3):
        t1 = jnp.dot(causal, part, preferred_element_type=F32)
        t2 = jnp.dot(whole, part, preferred_element_type=F32)
        cl = t1 if cl is None else cl + t1
        ct = t2 if ct is None else ct + t2
    e_pos = jnp.exp(cl)
    e_neg = jnp.exp(-cl)
    at = -kk * jnp.exp(cl - lw)
    rt = r * e_pos
    bt = b * e_neg
    kt = k * e_neg
    e_end = jnp.exp(ct - cl)
    bts = b * e_end
    kts = k * e_end
    p_end = jnp.exp(ct)

    strict = jnp.where(same_seq & (rj < ri), 1.0, 0.0)
    incl = jnp.where(same_seq & (rj <= ri), 1.0, 0.0)
    strict2 = jnp.concatenate([strict, strict], axis=1)
    incl2 = jnp.concatenate([incl, incl], axis=1)
    m0 = jnp.where(lane < RW_HEAD_DIM, 1.0, 0.0)
    m1 = jnp.where(lane >= RW_HEAD_DIM, 1.0, 0.0)
    diag = jnp.where(li == lj, 1.0, 0.0)
    bd_mask = jnp.where(same_head, 1.0, 0.0)
    zeros_rl = jnp.zeros((R, LANES), F32)
    zeros_r2 = jnp.zeros((R, 2 * R), F32)
    n_dbl = max(1, (C - 1).bit_length())
    pairs = range(RW_PAIRS)
    sls = [slice(q * LANES, (q + 1) * LANES) for q in pairs]

    vq = [v[:, sl] for sl in sls]
    v0 = [jnp.concatenate([vv, zeros_rl], axis=1) for vv in vq]
    zs = []
    for q in pairs:
        sl = sls[q]
        xa, xr = at[:, sl], rt[:, sl]
        x4 = jnp.concatenate([xa * m0, xr * m0, xa * m1, xr * m1], axis=0)
        y4 = jnp.concatenate([bt[:, sl], kt[:, sl], kt[:, sl], bt[:, sl]], axis=0)
        zs.append(_dotp(x4, y4, _NT, pp["z"], pp["z"]))
    zt0 = [zs[q][0:R, 0:2 * R] * strict2 for q in pairs]
    zb0 = [zs[q][R:2 * R, 0:2 * R] * incl2 for q in pairs]
    zt1 = [zs[q][2 * R:3 * R, 2 * R:4 * R] * strict2 for q in pairs]
    zb1 = [zs[q][3 * R:4 * R, 2 * R:4 * R] * incl2 for q in pairs]
    half0 = jnp.where(lax.broadcasted_iota(jnp.int32, (1, 2 * R), 1) < R, 1.0, 0.0)
    half1 = 1.0 - half0
    lbd = [jnp.concatenate([zt0[q] * half0, zt1[q] * half1], axis=0) for q in pairs]
    wmat = []
    for q in pairs:
        lhs = jnp.concatenate([jnp.concatenate([zt0[q], zeros_r2], axis=1),
                               jnp.concatenate([zeros_r2, zt1[q]], axis=1)], axis=0)
        rhs = jnp.concatenate([zeros_rl, vq[q], vq[q], zeros_rl], axis=0)
        lv = _dotp(lhs, rhs, _NN, pp["lv"], pp["lv"])
        xa = at[:, sls[q]]
        wmat.append(jnp.concatenate([lv, jnp.concatenate([xa * m0, xa * m1], axis=0)], axis=1))
    pw = lbd
    for step in range(n_dbl):
        wmat = [wmat[q] + _dotp(pw[q], wmat[q], _NN, pp["app"], pp["app"]) for q in pairs]
        if step + 1 < n_dbl:
            pw = [_dotp(pw[q], pw[q], _NN, pp["pow"], pp["pow"]) for q in pairs]
    rhat, o0 = [], []
    for q in pairs:
        lhs = jnp.concatenate([jnp.concatenate([zb0[q], zeros_r2], axis=1),
                               jnp.concatenate([zeros_r2, zb1[q]], axis=1)], axis=0)
        rhs = jnp.concatenate([wmat[q][0:R], v0[q], v0[q], wmat[q][R:2 * R]], axis=0)
        om = _dotp(lhs, rhs, _NN, pp["om"], pp["om"])
        rhat.append(rt[:, sls[q]] + om[0:R, LANES:] + om[R:2 * R, LANES:])
        o0.append(om[0:R, 0:LANES] * m0 + om[R:2 * R, 0:LANES] * m1)
    hgt = {}
    for q in pairs:
        bq, kq = bts[:, sls[q]], kts[:, sls[q]]
        for s in range(n_seq):
            rs = slice(s * C, (s + 1) * C)
            ys4 = jnp.concatenate([bq[rs] * m0, kq[rs] * m0, kq[rs] * m1, bq[rs] * m1], axis=0)
            r4 = jnp.concatenate([wmat[q][rs], v0[q][rs], v0[q][rs],
                                  wmat[q][R + s * C:R + (s + 1) * C]], axis=0)
            hgt[q, s] = _dotp(r4, ys4, _TN, pp["hg"], pp["hg"])
    o_cols = []
    for q in pairs:
        o_seq = []
        for s in range(n_seq):
            rs = slice(s * C, (s + 1) * C)
            s_old = st_scr[s, q]
            ht = hgt[q, s][0:LANES] * bd_mask
            gt = hgt[q, s][LANES:] + diag * p_end[s * C:s * C + 1, sls[q]]
            o_seq.append(_dotp(rhat[q][rs], s_old, _NT, pp["out"], pp["out"]) + o0[q][rs])
            st_scr[s, q] = _dotp(s_old, gt, _NN, pp["st"], pp["st"]) + ht
        o_cols.append(o_seq[0] if n_seq == 1 else jnp.concatenate(o_seq, axis=0))
    o = jnp.concatenate(o_cols, axis=1)

    inv_n = 1.0 / RW_HEAD_DIM
    mean = head_sum(o) * inv_n
    d = o - mean
    var = head_sum(d * d) * inv_n
    o = d * lax.rsqrt(var + RW_GN_EPS) * ln_w + ln_b
    bonus = head_sum(r * k * r_k) * v
    o_ref[...] = ((o + bonus) * g).astype(o_ref.dtype)

    @pl.when(c == n_c - 1)
    def _():
        for s in range(n_seq):
            for q in range(RW_PAIRS):
                s_out_ref[s, 2 * q] = st_scr[s, q, 0:HD, 0:HD]
                s_out_ref[s, 2 * q + 1] = st_scr[s, q, HD:2 * HD, HD:2 * HD]


def _rwkv(p_rw, row0, n_b, t_len, shift0, s0, prm, pp):
    if t_len >= RWKV_ROWS:
        seq_rows, n_seq = RWKV_ROWS, 1
    else:
        seq_rows, n_seq = t_len, RWKV_ROWS // t_len
    assert t_len % seq_rows == 0 and n_b % n_seq == 0 and row0 % (seq_rows * n_seq) == 0
    rows = seq_rows * n_seq
    n_c = t_len // seq_rows
    blk0 = row0 // rows
    has_state = s0 is not None

    wda = jnp.zeros((LANES, 2 * RW_WIDTH), F32)
    wda = wda.at[0:RW_DECAY_RANK, 0:RW_WIDTH].set(prm["w_decay_up"])
    wda = wda.at[RW_DECAY_RANK:, RW_WIDTH:].set(prm["w_a_up"])
    wa0 = jnp.concatenate([prm["w0_decay"], prm["a0"]]).reshape(1, 2 * RW_WIDTH)
    vecs = jnp.zeros((8, RW_WIDTH), F32)
    for i, name in enumerate(("k_k", "k_a", "r_k", "ln_x_w", "ln_x_b")):
        vecs = vecs.at[i].set(prm[name].reshape(RW_WIDTH))

    const = lambda *shape: pl.BlockSpec(shape, lambda bi, c: (0,) * len(shape))
    in_specs = [pl.BlockSpec((rows, RW_PROJ), lambda bi, c: (blk0 + bi * n_c + c, 0))]
    args = [p_rw]
    st_block = (n_seq, RW_HEADS, RW_HEAD_DIM, RW_HEAD_DIM)
    if has_state:
        in_specs += [pl.BlockSpec((n_seq, 1, RW_PROJ), lambda bi, c: (bi, 0, 0)),
                     pl.BlockSpec(st_block, lambda bi, c: (bi, 0, 0, 0))]
        args += [shift0.reshape(n_b, 1, RW_PROJ), s0]
    in_specs += [const(1, RW_PROJ), const(LANES, 2 * RW_WIDTH), const(1, 2 * RW_WIDTH),
                 const(RW_GATE_RANK, RW_WIDTH), const(8, RW_WIDTH)]
    args += [prm["mu_rwkv"].reshape(1, RW_PROJ), wda.astype(BF16), wa0,
             prm["w_gate_up"].astype(BF16), vecs]

    o, sh, st = pl.pallas_call(
        functools.partial(_rwkv_kernel, seq_rows=seq_rows, n_seq=n_seq, has_state=has_state, pp=pp),
        grid=(n_b // n_seq, n_c),
        in_specs=in_specs,
        out_specs=[
            pl.BlockSpec((rows, RW_WIDTH), lambda bi, c: (bi * n_c + c, 0)),
            pl.BlockSpec((n_seq, 1, RW_PROJ), lambda bi, c: (bi, 0, 0)),
            pl.BlockSpec(st_block, lambda bi, c: (bi, 0, 0, 0)),
        ],
        out_shape=[
            jax.ShapeDtypeStruct((n_b * t_len, RW_WIDTH), BF16),
            jax.ShapeDtypeStruct((n_b, 1, RW_PROJ), F32),
            jax.ShapeDtypeStruct((n_b, RW_HEADS, RW_HEAD_DIM, RW_HEAD_DIM), F32),
        ],
        scratch_shapes=[pltpu.VMEM((n_seq, 1, RW_PROJ), F32),
                        pltpu.VMEM((n_seq, RW_PAIRS, LANES, LANES), F32)],
        compiler_params=_params("arbitrary", "arbitrary"),
        name="rwkv_state" if has_state else "rwkv_fresh",
    )(*args)
    return o, sh.reshape(n_b, RW_PROJ), st


def _sink_probs(s_parts, sink):
    m = sink
    for s in s_parts:
        m = jnp.maximum(m, jnp.max(s, axis=-1, keepdims=True))
    den = jnp.exp(sink - m)
    es = []
    for s in s_parts:
        e = jnp.exp(s - m)
        den = den + jnp.sum(e, axis=-1, keepdims=True)
        es.append(e)
    inv = 1.0 / den
    return [(e * inv).astype(BF16) for e in es]


def _group_queries(q, hk):
    return jnp.concatenate(
        [q[:, (hk * SWA_GROUP + g) * SWA_HEAD_DIM:(hk * SWA_GROUP + g + 1) * SWA_HEAD_DIM]
         for g in range(SWA_GROUP)], axis=0).astype(BF16)


def _group_sinks(sink_ref, hk, grp):
    sink = jnp.zeros(grp.shape, F32)
    for g in range(SWA_GROUP):
        sink = jnp.where(grp == g, sink_ref[hk * SWA_GROUP + g], sink)
    return sink


def _swa_prompt_kernel(sink_ref, q_ref, kp_ref, kc_ref, vp_ref, vc_ref, o_ref):
    n = pl.program_id(1)
    Wn = WINDOW
    q = q_ref[...]
    kf = jnp.concatenate([kp_ref[...], kc_ref[...]], axis=0).astype(BF16)
    vf = jnp.concatenate([vp_ref[...], vc_ref[...]], axis=0).astype(BF16)
    gi = lax.broadcasted_iota(jnp.int32, (SWA_GROUP * Wn, 2 * Wn), 0)
    i = _mod_pow2(gi, Wn)
    j = lax.broadcasted_iota(jnp.int32, (SWA_GROUP * Wn, 2 * Wn), 1)
    first_key = jnp.where(n > 0, 0, Wn)
    ok = (j > i) & (j <= i + Wn) & (j >= first_key)
    grp = _div_pow2(lax.broadcasted_iota(jnp.int32, (SWA_GROUP * Wn, 1), 0), Wn)
    scale = SWA_HEAD_DIM ** -0.5
    heads = range(SWA_KV_HEADS)
    ks = [slice(hk * SWA_HEAD_DIM, (hk + 1) * SWA_HEAD_DIM) for hk in heads]
    sc = [lax.dot_general(_group_queries(q, hk), kf[:, ks[hk]], _NT, preferred_element_type=F32) * scale
          for hk in heads]
    pr = [_sink_probs([jnp.where(ok, sc[hk], -jnp.inf)], _group_sinks(sink_ref, hk, grp))[0] for hk in heads]
    ov = [jnp.dot(pr[hk], vf[:, ks[hk]], preferred_element_type=F32) for hk in heads]
    for hk in heads:
        for g in range(SWA_GROUP):
            h = hk * SWA_GROUP + g
            o_ref[:, h * SWA_HEAD_DIM:(h + 1) * SWA_HEAD_DIM] = ov[hk][g * Wn:(g + 1) * Wn].astype(o_ref.dtype)


def _swa_prompt(p_att, n_b, t_len, sinks):
    nb = t_len // WINDOW
    kcol = SWA_Q_WIDTH // SWA_KV_WIDTH
    vcol = kcol + 1
    cur = lambda col: (lambda b, n: (b * nb + n, col))
    prv = lambda col: (lambda b, n: (b * nb + jnp.maximum(n - 1, 0), col))
    return pl.pallas_call(
        _swa_prompt_kernel,
        grid=(n_b, nb),
        in_specs=[
            pl.BlockSpec(memory_space=pltpu.SMEM),
            pl.BlockSpec((WINDOW, SWA_Q_WIDTH), cur(0)),
            pl.BlockSpec((WINDOW, SWA_KV_WIDTH), prv(kcol)),
            pl.BlockSpec((WINDOW, SWA_KV_WIDTH), cur(kcol)),
            pl.BlockSpec((WINDOW, SWA_KV_WIDTH), prv(vcol)),
            pl.BlockSpec((WINDOW, SWA_KV_WIDTH), cur(vcol)),
        ],
        out_specs=pl.BlockSpec((WINDOW, SWA_Q_WIDTH), lambda b, n: (b * nb + n, 0)),
        out_shape=jax.ShapeDtypeStruct((n_b * t_len, SWA_Q_WIDTH), BF16),
        compiler_params=_params("arbitrary", "arbitrary"),
        name="swa_prompt",
    )(sinks, p_att, p_att, p_att, p_att, p_att)


def _swa_step_kernel(sink_ref, q_ref, kn_ref, vn_ref, ck_ref, cv_ref, o_ref, ok_ref, ov_ref, *, n_seq, t_len):
    T = t_len
    wb = ck_ref.shape[1]
    scale = SWA_HEAD_DIM ** -0.5
    t_row = _mod_pow2(lax.broadcasted_iota(jnp.int32, (SWA_GROUP * T, 1), 0), T)
    grp = _div_pow2(lax.broadcasted_iota(jnp.int32, (SWA_GROUP * T, 1), 0), T)
    jc = lax.broadcasted_iota(jnp.int32, (SWA_GROUP * T, wb), 1)
    jn = lax.broadcasted_iota(jnp.int32, (SWA_GROUP * T, T), 1)
    ok_c = (t_row + wb - jc < WINDOW)
    ok_n = (jn <= t_row)
    ks = [slice(hk * SWA_HEAD_DIM, (hk + 1) * SWA_HEAD_DIM) for hk in range(SWA_KV_HEADS)]
    sinks = [_group_sinks(sink_ref, hk, grp) for hk in range(SWA_KV_HEADS)]
    work = [(s, hk) for s in range(n_seq) for hk in range(SWA_KV_HEADS)]
    rows = [slice(s * T, (s + 1) * T) for s in range(n_seq)]
    knb, vnb, ckb, cvb = [], [], [], []
    for s in range(n_seq):
        kn = kn_ref[rows[s], :]
        vn = vn_ref[rows[s], :]
        ck = ck_ref[s]
        cv = cv_ref[s]
        ok_ref[s, 0:wb - T, :] = ck[T:wb, :]
        ok_ref[s, wb - T:wb, :] = kn
        ov_ref[s, 0:wb - T, :] = cv[T:wb, :]
        ov_ref[s, wb - T:wb, :] = vn
        knb.append(kn.astype(BF16))
        vnb.append(vn.astype(BF16))
        ckb.append(ck.astype(BF16))
        cvb.append(cv.astype(BF16))
    qh = {(s, hk): _group_queries(q_ref[rows[s], :], hk) for s, hk in work}
    sc = {w: lax.dot_general(qh[w], ckb[w[0]][:, ks[w[1]]], _NT, preferred_element_type=F32) * scale for w in work}
    sn = {w: lax.dot_general(qh[w], knb[w[0]][:, ks[w[1]]], _NT, preferred_element_type=F32) * scale for w in work}
    pr = {w: _sink_probs([jnp.where(ok_c, sc[w], -jnp.inf), jnp.where(ok_n, sn[w], -jnp.inf)], sinks[w[1]])
          for w in work}
    ov = {w: jnp.dot(pr[w][0], cvb[w[0]][:, ks[w[1]]], preferred_element_type=F32)
          + jnp.dot(pr[w][1], vnb[w[0]][:, ks[w[1]]], preferred_element_type=F32) for w in work}
    for s, hk in work:
        for g in range(SWA_GROUP):
            h = hk * SWA_GROUP + g
            o_ref[rows[s], h * SWA_HEAD_DIM:(h + 1) * SWA_HEAD_DIM] = ov[s, hk][g * T:(g + 1) * T].astype(o_ref.dtype)


def _swa_step(p_att, row0, n_b, t_len, cache_k, cache_v, sinks):
    wb = cache_k.shape[1]
    assert wb == WINDOW and t_len % 8 == 0 and t_len < wb
    n_seq = max(1, 64 // t_len)
    while n_b % n_seq:
        n_seq //= 2
    rows = n_seq * t_len
    assert rows % 16 == 0 and row0 % rows == 0
    blk0 = row0 // rows
    kcol = SWA_Q_WIDTH // SWA_KV_WIDTH
    ck = cache_k.reshape(n_b, wb, SWA_KV_WIDTH)
    cv = cache_v.reshape(n_b, wb, SWA_KV_WIDTH)
    o, nk, nv = pl.pallas_call(
        functools.partial(_swa_step_kernel, n_seq=n_seq, t_len=t_len),
        grid=(n_b // n_seq,),
        in_specs=[
            pl.BlockSpec(memory_space=pltpu.SMEM),
            pl.BlockSpec((rows, SWA_Q_WIDTH), lambda i: (blk0 + i, 0)),
            pl.BlockSpec((rows, SWA_KV_WIDTH), lambda i: (blk0 + i, kcol)),
            pl.BlockSpec((rows, SWA_KV_WIDTH), lambda i: (blk0 + i, kcol + 1)),
            pl.BlockSpec((n_seq, wb, SWA_KV_WIDTH), lambda i: (i, 0, 0)),
            pl.BlockSpec((n_seq, wb, SWA_KV_WIDTH), lambda i: (i, 0, 0)),
        ],
        out_specs=[
            pl.BlockSpec((rows, SWA_Q_WIDTH), lambda i: (i, 0)),
            pl.BlockSpec((n_seq, wb, SWA_KV_WIDTH), lambda i: (i, 0, 0)),
            pl.BlockSpec((n_seq, wb, SWA_KV_WIDTH), lambda i: (i, 0, 0)),
        ],
        out_shape=[
            jax.ShapeDtypeStruct((n_b * t_len, SWA_Q_WIDTH), BF16),
            jax.ShapeDtypeStruct((n_b, wb, SWA_KV_WIDTH), F32),
            jax.ShapeDtypeStruct((n_b, wb, SWA_KV_WIDTH), F32),
        ],
        compiler_params=_params("arbitrary"),
        name="swa_step",
    )(sinks, p_att, p_att, p_att, ck, cv)
    shape = (n_b, wb, SWA_KV_HEADS, SWA_HEAD_DIM)
    return o, nk.reshape(shape), nv.reshape(shape)


def _softmax(s):
    m = jnp.max(s, axis=-1, keepdims=True)
    e = jnp.exp(s - m)
    return (e / jnp.sum(e, axis=-1, keepdims=True)).astype(BF16)


def _mem_attend(items, o_ref):
    scale = MEM_HEAD_DIM ** -0.5
    sc = [lax.dot_general(q, k, _NT, preferred_element_type=F32) * scale for q, k, _, _, _ in items]
    pr = [_softmax(s) for s in sc]
    ov = [jnp.dot(p, it[2], preferred_element_type=F32) for p, it in zip(pr, items)]
    for o, it in zip(ov, items):
        o_ref[it[3], it[4]] = o.astype(o_ref.dtype)


def _mem_prompt_kernel(qa_ref, qb_ref, k_ref, v_ref, o_ref):
    kb = k_ref[...].astype(BF16)
    vb = v_ref[...].astype(BF16)
    half = MEM_HEADS // 2
    items = []
    for h in range(MEM_HEADS):
        q_ref = qa_ref if h < half else qb_ref
        hs = slice((h % half) * MEM_HEAD_DIM, (h % half + 1) * MEM_HEAD_DIM)
        ms = slice(h * MEM_HEAD_DIM, (h + 1) * MEM_HEAD_DIM)
        items.append((q_ref[:, hs].astype(BF16), kb[:, ms], vb[:, ms], slice(None), ms))
    _mem_attend(items, o_ref)


def _mem_prompt(p_att, n_b, t_len, mem_kv):
    m_tok = mem_kv.shape[0] // n_b
    tq = math.gcd(t_len, 512)
    nq = t_len // tq
    half_w = MEM_WIDTH // 2
    qcol = (SWA_Q_WIDTH + 2 * SWA_KV_WIDTH) // half_w
    return pl.pallas_call(
        _mem_prompt_kernel,
        grid=(n_b, nq),
        in_specs=[
            pl.BlockSpec((tq, half_w), lambda b, n: (b * nq + n, qcol)),
            pl.BlockSpec((tq, half_w), lambda b, n: (b * nq + n, qcol + 1)),
            pl.BlockSpec((m_tok, MEM_WIDTH), lambda b, n: (b, 0)),
            pl.BlockSpec((m_tok, MEM_WIDTH), lambda b, n: (b, 1)),
        ],
        out_specs=pl.BlockSpec((tq, MEM_WIDTH), lambda b, n: (b * nq + n, 0)),
        out_shape=jax.ShapeDtypeStruct((n_b * t_len, MEM_WIDTH), BF16),
        compiler_params=_params("arbitrary", "arbitrary"),
        name="mem_prompt",
    )(p_att, p_att, mem_kv, mem_kv)


def _mem_step_kernel(qa_ref, qb_ref, k_ref, v_ref, o_ref, *, n_seq, t_len):
    half = MEM_HEADS // 2
    m_tok = k_ref.shape[1] // MEM_HEADS
    items = []
    for s in range(n_seq):
        rs = slice(s * t_len, (s + 1) * t_len)
        for h in range(MEM_HEADS):
            q_ref = qa_ref if h < half else qb_ref
            hs = slice((h % half) * MEM_HEAD_DIM, (h % half + 1) * MEM_HEAD_DIM)
            ms = slice(h * MEM_HEAD_DIM, (h + 1) * MEM_HEAD_DIM)
            hrows = pl.ds(h, m_tok, stride=MEM_HEADS)
            items.append((q_ref[rs, hs].astype(BF16), k_ref[s, hrows, :].astype(BF16),
                          v_ref[s, hrows, :].astype(BF16), rs, ms))
    _mem_attend(items, o_ref)


def _mem_step(p_att, row0, n_b, t_len, cache_k, cache_v):
    m_tok = cache_k.shape[1]
    n_seq = max(1, 64 // t_len)
    while n_b % n_seq:
        n_seq //= 2
    rows = n_seq * t_len
    assert rows % 16 == 0 and row0 % rows == 0
    blk0 = row0 // rows
    half_w = MEM_WIDTH // 2
    qcol = (SWA_Q_WIDTH + 2 * SWA_KV_WIDTH) // half_w
    cache_block = (n_seq, m_tok * MEM_HEADS, MEM_HEAD_DIM)
    ck = cache_k.reshape(n_b, m_tok * MEM_HEADS, MEM_HEAD_DIM)
    cv = cache_v.reshape(n_b, m_tok * MEM_HEADS, MEM_HEAD_DIM)
    return pl.pallas_call(
        functools.partial(_mem_step_kernel, n_seq=n_seq, t_len=t_len),
        grid=(n_b // n_seq,),
        in_specs=[
            pl.BlockSpec((rows, half_w), lambda i: (blk0 + i, qcol)),
            pl.BlockSpec((rows, half_w), lambda i: (blk0 + i, qcol + 1)),
            pl.BlockSpec(cache_block, lambda i: (i, 0, 0)),
            pl.BlockSpec(cache_block, lambda i: (i, 0, 0)),
        ],
        out_specs=pl.BlockSpec((rows, MEM_WIDTH), lambda i: (i, 0)),
        out_shape=jax.ShapeDtypeStruct((n_b * t_len, MEM_WIDTH), BF16),
        compiler_params=_params("arbitrary"),
        name="mem_step",
    )(p_att, p_att, ck, cv)


def _merge_kernel(xn_ref, rwp_ref, rws_ref, swp_ref, sws_ref, mmp_ref, mms_ref,
                  wg0_ref, wg1_ref, wg2_ref, wb0_ref, wb1_ref, wb2_ref, o_ref,
                  g0_scr, g1_scr, g2_scr, b0_scr, b1_scr, b2_scr, *, n_p):
    i = pl.program_id(1)

    @pl.when(i == 0)
    def _():
        for src, dst in ((wg0_ref, g0_scr), (wg1_ref, g1_scr), (wg2_ref, g2_scr),
                         (wb0_ref, b0_scr), (wb1_ref, b1_scr), (wb2_ref, b2_scr)):
            dst[...] = src[...].astype(BF16)

    def run(o_rw_ref, o_sw_ref, o_mm_ref):
        xn = xn_ref[...]
        acc = None
        for g_scr, b_scr, br_ref in ((g0_scr, b0_scr, o_rw_ref), (g1_scr, b1_scr, o_sw_ref),
                                     (g2_scr, b2_scr, o_mm_ref)):
            gate = _sigmoid(jnp.dot(xn, g_scr[...], preferred_element_type=F32))
            t = gate * jnp.dot(br_ref[...], b_scr[...], preferred_element_type=F32)
            acc = t if acc is None else acc + t
        o_ref[...] = acc.astype(o_ref.dtype)

    @pl.when(i < n_p)
    def _():
        run(rwp_ref, swp_ref, mmp_ref)

    @pl.when(i >= n_p)
    def _():
        run(rws_ref, sws_ref, mms_ref)


def _merge(xn, o_p, o_s, w_in, gate_col0, w_br, n_rows_p):
    m, d = xn.shape
    tn = MXU_COLS
    tm = _row_tile(n_rows_p, m - n_rows_p, ROW_TILE)
    n_p = n_rows_p // tm
    n_i = m // tm
    assert gate_col0 % tn == 0 and d % tn == 0
    g0 = gate_col0 // tn
    nj = d // tn
    p_spec = lambda w: pl.BlockSpec((tm, w), lambda j, i: (jnp.minimum(i, n_p - 1), 0))
    s_spec = lambda w: pl.BlockSpec((tm, w), lambda j, i: (jnp.maximum(i - n_p, 0), 0))
    in_specs = [pl.BlockSpec((tm, d), lambda j, i: (i, 0))]
    args = [xn]
    for op, os_ in zip(o_p, o_s):
        in_specs += [p_spec(op.shape[1]), s_spec(os_.shape[1])]
        args += [op, os_]
    for br in range(N_BRANCH):
        in_specs.append(pl.BlockSpec((d, tn), lambda j, i, br=br: (0, g0 + br * nj + j)))
        args.append(w_in)
    for w in w_br:
        in_specs.append(pl.BlockSpec((w.shape[0], tn), lambda j, i: (0, j)))
        args.append(w)
    scratch = [pltpu.VMEM((d, tn), BF16) for _ in range(N_BRANCH)]
    scratch += [pltpu.VMEM((w.shape[0], tn), BF16) for w in w_br]
    return pl.pallas_call(
        functools.partial(_merge_kernel, n_p=n_p),
        grid=(nj, n_i),
        in_specs=in_specs,
        out_specs=pl.BlockSpec((tm, tn), lambda j, i: (i, j)),
        out_shape=jax.ShapeDtypeStruct((m, d), BF16),
        scratch_shapes=scratch,
        compiler_params=_params("arbitrary", "arbitrary"),
        name="gated_merge",
    )(*args)


def _cast_kernel(x_ref, o_ref):
    o_ref[...] = x_ref[...].astype(o_ref.dtype)


def _to_bf16(w, name):
    r, c = w.shape
    tr = math.gcd(r, 512)
    return pl.pallas_call(
        _cast_kernel,
        grid=(r // tr,),
        in_specs=[pl.BlockSpec((tr, c), lambda i: (i, 0))],
        out_specs=pl.BlockSpec((tr, c), lambda i: (i, 0)),
        out_shape=jax.ShapeDtypeStruct((r, c), BF16),
        compiler_params=_params("arbitrary"),
        name=name,
    )(w)


def _wo_kernel(m_ref, w_ref, xp_ref, xs_ref, g1_ref, g2_ref, h_ref, hn_ref, *, n_p):
    i = pl.program_id(0)
    tm = m_ref.shape[0]
    halves = [slice(0, tm // 2), slice(tm // 2, tm)] if tm % 32 == 0 else [slice(0, tm)]

    def run(x_ref):
        for rs in halves:
            f = jnp.dot(m_ref[rs, :], w_ref[...], preferred_element_type=F32)
            h = x_ref[rs, :] + _rms(f, g1_ref[...])
            h_ref[rs, :] = h
            hn_ref[rs, :] = _rms(h, g2_ref[...]).astype(hn_ref.dtype)

    @pl.when(i < n_p)
    def _():
        run(xp_ref)

    @pl.when(i >= n_p)
    def _():
        run(xs_ref)


def _wo_block(merged, w_o, xp, xs, g_post, g_pre_ffn):
    m, d = merged.shape
    n_rows_p = xp.shape[0]
    tm = _row_tile(n_rows_p, m - n_rows_p, NORM_ROW_TILE)
    n_p = n_rows_p // tm
    return pl.pallas_call(
        functools.partial(_wo_kernel, n_p=n_p),
        grid=(m // tm,),
        in_specs=[
            pl.BlockSpec((tm, d), lambda i: (i, 0)),
            pl.BlockSpec((d, d), lambda i: (0, 0), pipeline_mode=pl.Buffered(1)),
            pl.BlockSpec((tm, d), lambda i: (jnp.minimum(i, n_p - 1), 0)),
            pl.BlockSpec((tm, d), lambda i: (jnp.maximum(i - n_p, 0), 0)),
            pl.BlockSpec((1, d), lambda i: (0, 0)),
            pl.BlockSpec((1, d), lambda i: (0, 0)),
        ],
        out_specs=[pl.BlockSpec((tm, d), lambda i: (i, 0)), pl.BlockSpec((tm, d), lambda i: (i, 0))],
        out_shape=[jax.ShapeDtypeStruct((m, d), F32), jax.ShapeDtypeStruct((m, d), BF16)],
        compiler_params=_params("arbitrary"),
        name="wo_residual",
    )(merged, _to_bf16(w_o, "cast_w_o"), xp, xs, g_post.reshape(1, d), g_pre_ffn.reshape(1, d))


def _gelu_tanh(x):
    return 0.5 * x * (1.0 + jnp.tanh(math.sqrt(2.0 / math.pi) * (x + 0.044715 * (x * x * x))))


def _ffn_up_kernel(hn_ref, wg_ref, wv_ref, cw_ref, cb_ref, st_ref, act_ref, cp_ref, cs_ref,
                   wgb_scr, wvb_scr, carry_scr, *, n_p, blocks_per_seq, t_s):
    i = pl.program_id(1)

    @pl.when(i == 0)
    def _():
        wgb_scr[...] = wg_ref[...].astype(BF16)
        wvb_scr[...] = wv_ref[...].astype(BF16)

    tm = hn_ref.shape[0]
    tn = wgb_scr.shape[1]
    subs = [slice(c0, c0 + MXU_COLS) for c0 in range(0, tn, MXU_COLS)] if tn % MXU_COLS == 0 else [slice(0, tn)]
    n_pieces = 4 if tm % (4 * 8 * t_s) == 0 else 1
    tr = tm // n_pieces

    def gate_val(rs, cs):
        hn = hn_ref[rs, :]
        zg = jnp.dot(hn, wgb_scr[:, cs], preferred_element_type=F32)
        zv = jnp.dot(hn, wvb_scr[:, cs], preferred_element_type=F32)
        return zg, zv

    def conv_glu(cs, zg, zm1, zm2, zv):
        conv = cb_ref[:, cs] + cw_ref[0:1, cs] * zm2 + cw_ref[1:2, cs] * zm1 + cw_ref[2:3, cs] * zg
        return _gelu_tanh(conv) * zv

    @pl.when(i < n_p)
    def _():
        @pl.when(i % blocks_per_seq == 0)
        def _():
            carry_scr[...] = jnp.zeros(carry_scr.shape, F32)

        row = lax.broadcasted_iota(jnp.int32, (tr, 1), 0)
        for cs in subs:
            p1 = carry_scr[1:2, cs]
            p2 = carry_scr[0:1, cs]
            for r in range(n_pieces):
                rs = slice(r * tr, (r + 1) * tr)
                zg, zv = gate_val(rs, cs)
                zm1 = jnp.where(row == 0, p1, pltpu.roll(zg, 1, axis=0))
                zm2 = jnp.where(row == 0, p2, jnp.where(row == 1, p1, pltpu.roll(zg, 2, axis=0)))
                act_ref[rs, cs] = conv_glu(cs, zg, zm1, zm2, zv).astype(act_ref.dtype)
                p1 = zg[tr - 1:tr, :]
                p2 = zg[tr - 2:tr - 1, :]
            carry_scr[0:1, cs] = p2
            carry_scr[1:2, cs] = p1
            cp_ref[0, 0:1, cs] = p2
            cp_ref[0, 1:2, cs] = p1

    @pl.when(i >= n_p)
    def _():
        t = lax.broadcasted_iota(jnp.int32, (1, t_s, 1), 1)
        nq = tr // t_s
        for cs in subs:
            for r in range(n_pieces):
                rs = slice(r * tr, (r + 1) * tr)
                qs = slice(r * nq, (r + 1) * nq)
                zg, zv = gate_val(rs, cs)
                tc = zg.shape[1]
                z3 = zg.reshape(nq, t_s, tc)
                s1 = st_ref[qs, 1:2, cs]
                s0 = st_ref[qs, 0:1, cs]
                zm1 = jnp.where(t == 0, s1, pltpu.roll(z3, 1, axis=1))
                zm2 = jnp.where(t == 0, s0, jnp.where(t == 1, s1, pltpu.roll(z3, 2, axis=1)))
                act = conv_glu(cs, z3, zm1, zm2, zv.reshape(nq, t_s, tc))
                act_ref[rs, cs] = act.reshape(tr, tc).astype(act_ref.dtype)
                cs_ref[qs, :, cs] = z3[:, t_s - 2:t_s, :]


def _ffn_up(hn, w_up, conv_w, conv_b, conv_state, n_b_p, t_p, n_b_s, t_s):
    m, d = hn.shape
    d_ff = conv_b.shape[0]
    n_rows_p = n_b_p * t_p
    n_rows_s = n_b_s * t_s
    tm = _row_tile(math.gcd(n_rows_p, t_p), n_rows_s, ROW_TILE)
    assert tm == n_rows_s and t_s % 8 == 0 and t_s >= 2, "sample rows must form one row block"
    n_p = n_rows_p // tm
    blocks_per_seq = t_p // tm
    tn = math.gcd(d_ff, 512)
    nj = d_ff // tn
    last_seq = n_b_p - 1
    return pl.pallas_call(
        functools.partial(_ffn_up_kernel, n_p=n_p, blocks_per_seq=blocks_per_seq, t_s=t_s),
        grid=(nj, n_p + 1),
        in_specs=[
            pl.BlockSpec((tm, d), lambda j, i: (i, 0)),
            pl.BlockSpec((d, tn), lambda j, i: (0, j)),
            pl.BlockSpec((d, tn), lambda j, i: (0, nj + j)),
            pl.BlockSpec((CONV_W, tn), lambda j, i: (0, j)),
            pl.BlockSpec((1, tn), lambda j, i: (0, j)),
            pl.BlockSpec((n_b_s, CONV_W - 1, tn), lambda j, i: (0, 0, j)),
        ],
        out_specs=[
            pl.BlockSpec((tm, tn), lambda j, i: (i, j)),
            pl.BlockSpec((1, CONV_W - 1, tn), lambda j, i: (jnp.minimum(i // blocks_per_seq, last_seq), 0, j)),
            pl.BlockSpec((n_b_s, CONV_W - 1, tn), lambda j, i: (0, 0, j)),
        ],
        out_shape=[
            jax.ShapeDtypeStruct((m, d_ff), BF16),
            jax.ShapeDtypeStruct((n_b_p, CONV_W - 1, d_ff), F32),
            jax.ShapeDtypeStruct((n_b_s, CONV_W - 1, d_ff), F32),
        ],
        scratch_shapes=[pltpu.VMEM((d, tn), BF16), pltpu.VMEM((d, tn), BF16), pltpu.VMEM((8, tn), F32)],
        compiler_params=_params("arbitrary", "arbitrary"),
        name="ffn_up_convglu",
    )(hn, w_up, w_up, conv_w, conv_b.reshape(1, d_ff), conv_state)


def _ffn_down_kernel(a_ref, w_ref, h_ref, g_ref, y_ref):
    kk = pl.program_id(1)

    @pl.when(kk == 0)
    def _():
        y_ref[...] = jnp.zeros(y_ref.shape, F32)

    last = kk == pl.num_programs(1) - 1

    @pl.when(jnp.logical_not(last))
    def _():
        y_ref[...] += jnp.dot(a_ref[...], w_ref[...].astype(BF16), preferred_element_type=F32)

    @pl.when(last)
    def _():
        tm = y_ref.shape[0]
        n_pieces = 4 if tm % 64 == 0 else 1
        tr = tm // n_pieces
        wb = w_ref[...].astype(BF16)
        for r in range(n_pieces):
            rs = slice(r * tr, (r + 1) * tr)
            f = y_ref[rs, :] + jnp.dot(a_ref[rs, :], wb, preferred_element_type=F32)
            y_ref[rs, :] = h_ref[rs, :] + _rms(f, g_ref[...])


def _ffn_down(act, w_down, h, g, row0, n_rows):
    d_ff, d = w_down.shape
    tm = math.gcd(math.gcd(n_rows, row0) if row0 else n_rows, ROW_TILE)
    tk = math.gcd(d_ff, 512)
    i0 = row0 // tm
    return pl.pallas_call(
        _ffn_down_kernel,
        grid=(n_rows // tm, d_ff // tk),
        in_specs=[
            pl.BlockSpec((tm, tk), lambda i, k: (i0 + i, k)),
            pl.BlockSpec((tk, d), lambda i, k: (k, 0)),
            pl.BlockSpec((tm, d), lambda i, k: (i0 + i, 0)),
            pl.BlockSpec((1, d), lambda i, k: (0, 0)),
        ],
        out_specs=pl.BlockSpec((tm, d), lambda i, k: (i, 0)),
        out_shape=jax.ShapeDtypeStruct((n_rows, d), F32),
        compiler_params=_params("arbitrary", "arbitrary"),
        name="ffn_down",
    )(act, w_down, h, g.reshape(1, d))


def kernel(x_prompt, x_sample, cache_swa_k, cache_swa_v, cache_mem_k, cache_mem_v, state_rwkv_shift, state_rwkv_wkv, state_ffn_conv, mem_prompt, g_pre_mix, w_in, mu_rwkv, w_decay_up, w0_decay, w_a_up, a0, w_gate_up, k_k, k_a, r_k, ln_x_w, ln_x_b, swa_sinks, g_mem, w_mem_kv, w_br_rwkv, w_br_swa, w_br_mem, w_o, g_post_mix, g_pre_ffn, w_ffn_up, conv_w, conv_b, w_ffn_down, g_post_ffn):
    b_p, t_p, d = x_prompt.shape
    b_s, t_s, _ = x_sample.shape
    n_p, n_s = b_p * t_p, b_s * t_s
    m_tok = mem_prompt.shape[1]
    rw_prm = dict(mu_rwkv=mu_rwkv, w_decay_up=w_decay_up, w0_decay=w0_decay, w_a_up=w_a_up, a0=a0,
                  w_gate_up=w_gate_up, k_k=k_k, k_a=k_a, r_k=r_k, ln_x_w=ln_x_w, ln_x_b=ln_x_b)
    xp = x_prompt.reshape(n_p, d)
    xs = x_sample.reshape(n_s, d)

    xn = _norm_rows2(xp, xs, g_pre_mix)
    p_rw = _matmul(xn, w_in, 0, RW_PROJ, RW_PROJ // 2, ROW_TILE, "proj_rwkv")
    p_att = _matmul(xn, w_in, RW_PROJ, ATT_WIDTH, ATT_WIDTH, NORM_ROW_TILE, "proj_attn")

    mem_kv = _matmul(_norm_rows1(mem_prompt.reshape(b_p * m_tok, d), g_mem), w_mem_kv,
                     0, 2 * MEM_WIDTH, MEM_WIDTH, ROW_TILE, "proj_mem_kv")

    o_rw_p, shift_p, wkv_p = _rwkv(p_rw, 0, b_p, t_p, None, None, rw_prm, RWKV_PARTS)
    o_rw_s, shift_s, wkv_s = _rwkv(p_rw, n_p, b_s, t_s, state_rwkv_shift, state_rwkv_wkv, rw_prm, RWKV_PARTS)

    o_sw_p = _swa_prompt(p_att, b_p, t_p, swa_sinks)
    o_sw_s, swa_k_s, swa_v_s = _swa_step(p_att, n_p, b_s, t_s, cache_swa_k, cache_swa_v, swa_sinks)

    o_mm_p = _mem_prompt(p_att, b_p, t_p, mem_kv)
    o_mm_s = _mem_step(p_att, n_p, b_s, t_s, cache_mem_k, cache_mem_v)

    merged = _merge(xn, (o_rw_p, o_sw_p, o_mm_p), (o_rw_s, o_sw_s, o_mm_s), w_in,
                    RW_PROJ + ATT_WIDTH, (w_br_rwkv, w_br_swa, w_br_mem), n_p)
    h, hn = _wo_block(merged, w_o, xp, xs, g_post_mix, g_pre_ffn)
    act, conv_p, conv_s = _ffn_up(hn, w_ffn_up, conv_w, conv_b, state_ffn_conv, b_p, t_p, b_s, t_s)
    y_p = _ffn_down(act, w_ffn_down, h, g_post_ffn, 0, n_p)
    y_s = _ffn_down(act, w_ffn_down, h, g_post_ffn, n_p, n_s)

    keep = min(WINDOW, t_p)
    kv_p = jnp.stack([lax.slice(p_att, ((b + 1) * t_p - keep, SWA_Q_WIDTH),
                                ((b + 1) * t_p, SWA_Q_WIDTH + 2 * SWA_KV_WIDTH)) for b in range(b_p)])
    swa_k_p = kv_p[..., :SWA_KV_WIDTH].reshape(b_p, keep, SWA_KV_HEADS, SWA_HEAD_DIM)
    swa_v_p = kv_p[..., SWA_KV_WIDTH:].reshape(b_p, keep, SWA_KV_HEADS, SWA_HEAD_DIM)
    mem_k_p = mem_kv[:, :MEM_WIDTH].reshape(b_p, m_tok, MEM_HEADS, MEM_HEAD_DIM)
    mem_v_p = mem_kv[:, MEM_WIDTH:].reshape(b_p, m_tok, MEM_HEADS, MEM_HEAD_DIM)
    return (y_p.reshape(b_p, t_p, d), y_s.reshape(b_s, t_s, d), swa_k_p, swa_v_p, mem_k_p, mem_v_p,
            shift_p, wkv_p, conv_p, swa_k_s, swa_v_s, shift_s, wkv_s, conv_s)
```

```python
import functools
import math

import jax
import jax.numpy as jnp
from jax import lax
from jax.experimental import pallas as pl
from jax.experimental.pallas import tpu as pltpu

F32 = jnp.float32
BF16 = jnp.bfloat16

RW_HEADS = 12
RW_HEAD_DIM = 64
RW_WIDTH = RW_HEADS * RW_HEAD_DIM
RW_DECAY_RANK = 64
RW_A_RANK = 64
RW_GATE_RANK = 128
RW_PROJ = 3 * RW_WIDTH + RW_DECAY_RANK + RW_A_RANK + RW_GATE_RANK
RW_GN_EPS = 6.4e-4
RW_PAIRS = RW_HEADS // 2
SWA_Q_HEADS = 12
SWA_KV_HEADS = 4
SWA_GROUP = SWA_Q_HEADS // SWA_KV_HEADS
SWA_HEAD_DIM = 64
SWA_Q_WIDTH = SWA_Q_HEADS * SWA_HEAD_DIM
SWA_KV_WIDTH = SWA_KV_HEADS * SWA_HEAD_DIM
WINDOW = 128
MEM_HEADS = 4
MEM_HEAD_DIM = 128
MEM_WIDTH = MEM_HEADS * MEM_HEAD_DIM
N_BRANCH = 3
CONV_W = 3
NORM_EPS = 1e-6
ATT_WIDTH = SWA_Q_WIDTH + 2 * SWA_KV_WIDTH + MEM_WIDTH
SWA_SCALE = SWA_HEAD_DIM ** -0.5
assert math.frexp(SWA_SCALE)[0] == 0.5

LANES = 128
MXU_COLS = 256
VMEM_LIMIT_BYTES = 56 * 1024 * 1024

RWKV_ROWS = 64
RWKV_GROUPS = 2
RWKV_PARTS = dict(z=1, lv=1, pow=1, app=1, om=1, hg=1, out=1, st=1)
ROW_TILE = 1024
NORM_ROW_TILE = 512


def _params(*sem):
    return pltpu.CompilerParams(dimension_semantics=sem, vmem_limit_bytes=VMEM_LIMIT_BYTES)


def _row_tile(n_p, n_s, target):
    t = math.gcd(math.gcd(n_p, n_s), target)
    assert t % 16 == 0, (n_p, n_s, target)
    return t


def _div_pow2(x, n):
    assert n & (n - 1) == 0
    return lax.shift_right_logical(x, n.bit_length() - 1)


def _mod_pow2(x, n):
    assert n & (n - 1) == 0
    return jnp.bitwise_and(x, n - 1)


def _rms(x, g):
    ms = jnp.mean(x * x, axis=-1, keepdims=True)
    return x * lax.rsqrt(ms + NORM_EPS) * g


def _norm2_kernel(xp_ref, xs_ref, g_ref, o_ref, *, n_p):
    i = pl.program_id(0)

    @pl.when(i < n_p)
    def _():
        o_ref[...] = _rms(xp_ref[...], g_ref[...]).astype(o_ref.dtype)

    @pl.when(i >= n_p)
    def _():
        o_ref[...] = _rms(xs_ref[...], g_ref[...]).astype(o_ref.dtype)


def _norm_rows2(xp, xs, g):
    n_rows_p, d = xp.shape
    n_rows_s = xs.shape[0]
    tm = _row_tile(n_rows_p, n_rows_s, NORM_ROW_TILE)
    n_p, n_s = n_rows_p // tm, n_rows_s // tm
    return pl.pallas_call(
        functools.partial(_norm2_kernel, n_p=n_p),
        grid=(n_p + n_s,),
        in_specs=[
            pl.BlockSpec((tm, d), lambda i: (jnp.minimum(i, n_p - 1), 0)),
            pl.BlockSpec((tm, d), lambda i: (jnp.maximum(i - n_p, 0), 0)),
            pl.BlockSpec((1, d), lambda i: (0, 0)),
        ],
        out_specs=pl.BlockSpec((tm, d), lambda i: (i, 0)),
        out_shape=jax.ShapeDtypeStruct((n_rows_p + n_rows_s, d), BF16),
        compiler_params=_params("arbitrary"),
        name="norm_rows",
    )(xp, xs, g.reshape(1, d))


def _norm1_kernel(x_ref, g_ref, o_ref):
    o_ref[...] = _rms(x_ref[...], g_ref[...]).astype(o_ref.dtype)


def _norm_rows1(x, g):
    n, d = x.shape
    tm = math.gcd(n, NORM_ROW_TILE)
    return pl.pallas_call(
        _norm1_kernel,
        grid=(n // tm,),
        in_specs=[pl.BlockSpec((tm, d), lambda i: (i, 0)), pl.BlockSpec((1, d), lambda i: (0, 0))],
        out_specs=pl.BlockSpec((tm, d), lambda i: (i, 0)),
        out_shape=jax.ShapeDtypeStruct((n, d), BF16),
        compiler_params=_params("arbitrary"),
        name="norm_mem",
    )(x, g.reshape(1, d))


def _mm_kernel(a_ref, w_ref, o_ref, wb_ref):
    @pl.when(pl.program_id(1) == 0)
    def _():
        wb_ref[...] = w_ref[...].astype(BF16)

    o_ref[...] = jnp.dot(a_ref[...], wb_ref[...], preferred_element_type=F32)


def _matmul(a, w, col0, n_cols, tn, tm, name):
    m, k = a.shape
    tm = math.gcd(m, tm)
    assert n_cols % tn == 0 and col0 % LANES == 0 and tn % LANES == 0
    return pl.pallas_call(
        _mm_kernel,
        grid=(n_cols // tn, m // tm),
        in_specs=[
            pl.BlockSpec((tm, k), lambda j, i: (i, 0)),
            pl.BlockSpec((pl.Element(k), pl.Element(tn)),
                         lambda j, i: (0, pl.multiple_of(col0 + j * tn, LANES))),
        ],
        out_specs=pl.BlockSpec((tm, tn), lambda j, i: (i, j)),
        out_shape=jax.ShapeDtypeStruct((m, n_cols), F32),
        scratch_shapes=[pltpu.VMEM((k, tn), BF16)],
        compiler_params=_params("arbitrary", "arbitrary"),
        name=name,
    )(a, w)


_NN = (((1,), (0,)), ((), ()))
_NT = (((1,), (1,)), ((), ()))
_TN = (((0,), (0,)), ((), ()))


def _bf_parts(x, n):
    parts = []
    r = x
    for i in range(n):
        h = r.astype(BF16)
        parts.append(h)
        if i + 1 < n:
            r = r - h.astype(F32)
    return parts


def _dotp(a, b, dn, pa, pb):
    pa_parts = _bf_parts(a, pa)
    pb_parts = _bf_parts(b, pb)
    acc = None
    for i in range(pa):
        for j in range(pb):
            if i + j < max(pa, pb):
                t = lax.dot_general(pa_parts[i], pb_parts[j], dn, preferred_element_type=F32)
                acc = t if acc is None else acc + t
    return acc


def _softplus(x):
    return jnp.maximum(x, 0.0) + jnp.log(1.0 + jnp.exp(-jnp.abs(x)))


def _sigmoid(x):
    return 1.0 / (1.0 + jnp.exp(-x))


def _rwkv_kernel(*refs, seq_rows, n_seq, n_par, has_state, pp):
    n_in = n_par + (2 if has_state else 0)
    p_refs = refs[0:n_par]
    if has_state:
        sh0_ref, s0_ref = refs[n_par:n_in]
    (mu_ref, wda_ref, wa0_ref, wg_ref, vec_ref,
     o_ref, sh_out_ref, s_out_ref, prev_scr, st_scr) = refs[n_in:]
    c = pl.program_id(1)
    n_c = pl.num_programs(1)
    C = seq_rows
    R = seq_rows * n_seq
    W = RW_WIDTH
    HD = RW_HEAD_DIM
    groups = range(n_par)

    @pl.when(c == 0)
    def _():
        st_scr[...] = jnp.zeros(st_scr.shape, F32)
        if has_state:
            prev_scr[...] = sh0_ref[...]
            for s in range(n_par * n_seq):
                for q in range(RW_PAIRS):
                    st_scr[s, q, 0:HD, 0:HD] = s0_ref[s, 2 * q]
                    st_scr[s, q, HD:2 * HD, HD:2 * HD] = s0_ref[s, 2 * q + 1]
        else:
            prev_scr[...] = jnp.zeros(prev_scr.shape, F32)

    k_k = vec_ref[0:1, :]
    k_a = vec_ref[1:2, :]
    r_k = vec_ref[2:3, :]
    ln_w = vec_ref[3:4, :]
    ln_b = vec_ref[4:5, :]
    lane = lax.broadcasted_iota(jnp.int32, (1, LANES), 1)
    row = lax.broadcasted_iota(jnp.int32, (R, 1), 0)

    li = lax.broadcasted_iota(jnp.int32, (LANES, LANES), 0)
    lj = lax.broadcasted_iota(jnp.int32, (LANES, LANES), 1)
    same_head = _div_pow2(li, RW_HEAD_DIM) == _div_pow2(lj, RW_HEAD_DIM)
    ones_bd = jnp.where(same_head, 1.0, 0.0).astype(BF16)

    def head_sum(z):
        cols = []
        for q in range(RW_PAIRS):
            zq = z[:, q * LANES:(q + 1) * LANES]
            acc = None
            for part in _bf_parts(zq, 2):
                t = jnp.dot(part, ones_bd, preferred_element_type=F32)
                acc = t if acc is None else acc + t
            cols.append(acc)
        return jnp.concatenate(cols, axis=1)

    ri = lax.broadcasted_iota(jnp.int32, (R, R), 0)
    rj = lax.broadcasted_iota(jnp.int32, (R, R), 1)
    same_seq = _div_pow2(ri, C) == _div_pow2(rj, C)
    causal = jnp.where(same_seq & (rj <= ri), 1.0, 0.0).astype(BF16)
    whole = jnp.where(same_seq, 1.0, 0.0).astype(BF16)

    def prepare(u):
        p = p_refs[u][...]
        prev = pltpu.roll(p, 1, axis=0)
        for s in range(n_seq):
            prev = jnp.where(row == s * C, prev_scr[u * n_seq + s], prev)
        for s in range(n_seq):
            prev_scr[u * n_seq + s] = p[(s + 1) * C - 1:(s + 1) * C, :]
            sh_out_ref[u * n_seq + s] = p[(s + 1) * C - 1:(s + 1) * C, :]
        x = p + mu_ref[...] * (prev - p)
        r = x[:, 0:W]
        k = x[:, W:2 * W]
        v = x[:, 2 * W:3 * W]
        wa_in = x[:, 3 * W:3 * W + LANES]
        g_in = x[:, 3 * W + LANES:3 * W + 2 * LANES]
        wa_act = jnp.where(lane < RW_DECAY_RANK, jnp.tanh(wa_in), wa_in)
        za = jnp.dot(wa_act.astype(BF16), wda_ref[...], preferred_element_type=F32) + wa0_ref[...]
        w_log = -_softplus(-za[:, 0:W]) - 0.5
        lw = -jnp.exp(w_log)
        a = _sigmoid(za[:, W:2 * W])
        g = jnp.dot(_sigmoid(g_in).astype(BF16), wg_ref[...], preferred_element_type=F32)
        kk = k * k_k
        kk = kk / jnp.maximum(jnp.sqrt(head_sum(kk * kk)), 1e-12)
        k = k * (1.0 + (a - 1.0) * k_a)
        b = kk * a
        cl = None
        ct = None
        for part in _bf_parts(lw, 3):
            t1 = jnp.dot(causal, part, preferred_element_type=F32)
            t2 = jnp.dot(whole, part, preferred_element_type=F32)
            cl = t1 if cl is None else cl + t1
            ct = t2 if ct is None else ct + t2
        e_neg = jnp.exp(-cl)
        e_end = jnp.exp(ct - cl)
        return dict(r=r, k=k, v=v, g=g, at=-kk * jnp.exp(cl - lw), rt=r * jnp.exp(cl), bt=b * e_neg,
                    kt=k * e_neg, bts=b * e_end, kts=k * e_end,
                    p_end=jnp.exp(ct))

    prep = [prepare(u) for u in groups]

    strict = jnp.where(same_seq & (rj < ri), 1.0, 0.0)
    incl = jnp.where(same_seq & (rj <= ri), 1.0, 0.0)
    strict2 = jnp.concatenate([strict, strict], axis=1)
    incl2 = jnp.concatenate([incl, incl], axis=1)
    m0 = jnp.where(lane < RW_HEAD_DIM, 1.0, 0.0)
    m1 = jnp.where(lane >= RW_HEAD_DIM, 1.0, 0.0)
    diag = jnp.where(li == lj, 1.0, 0.0)
    bd_mask = jnp.where(same_head, 1.0, 0.0)
    zeros_rl = jnp.zeros((R, LANES), F32)
    zeros_r2 = jnp.zeros((R, 2 * R), F32)
    n_dbl = max(1, (C - 1).bit_length())
    sls = [slice(q * LANES, (q + 1) * LANES) for q in range(RW_PAIRS)]
    items = [(u, q) for u in groups for q in range(RW_PAIRS)]

    def col(name, it):
        return prep[it[0]][name][:, sls[it[1]]]

    v0 = {it: jnp.concatenate([col("v", it), zeros_rl], axis=1) for it in items}
    zs = {}
    for it in items:
        xa, xr, bq, kq = col("at", it), col("rt", it), col("bt", it), col("kt", it)
        x4 = jnp.concatenate([xa * m0, xr * m0, xa * m1, xr * m1], axis=0)
        y4 = jnp.concatenate([bq, kq, kq, bq], axis=0)
        zs[it] = _dotp(x4, y4, _NT, pp["z"], pp["z"])
    zt0 = {it: zs[it][0:R, 0:2 * R] * strict2 for it in items}
    zb0 = {it: zs[it][R:2 * R, 0:2 * R] * incl2 for it in items}
    zt1 = {it: zs[it][2 * R:3 * R, 2 * R:4 * R] * strict2 for it in items}
    zb1 = {it: zs[it][3 * R:4 * R, 2 * R:4 * R] * incl2 for it in items}
    half0 = jnp.where(lax.broadcasted_iota(jnp.int32, (1, 2 * R), 1) < R, 1.0, 0.0)
    half1 = 1.0 - half0
    pw = {it: jnp.concatenate([zt0[it] * half0, zt1[it] * half1], axis=0) for it in items}
    wmat = {}
    for it in items:
        lhs = jnp.concatenate([jnp.concatenate([zt0[it], zeros_r2], axis=1),
                               jnp.concatenate([zeros_r2, zt1[it]], axis=1)], axis=0)
        vq = col("v", it)
        rhs = jnp.concatenate([zeros_rl, vq, vq, zeros_rl], axis=0)
        lv = _dotp(lhs, rhs, _NN, pp["lv"], pp["lv"])
        xa = col("at", it)
        wmat[it] = jnp.concatenate([lv, jnp.concatenate([xa * m0, xa * m1], axis=0)], axis=1)
    for step in range(n_dbl):
        wmat = {it: wmat[it] + _dotp(pw[it], wmat[it], _NN, pp["app"], pp["app"]) for it in items}
        if step + 1 < n_dbl:
            pw = {it: _dotp(pw[it], pw[it], _NN, pp["pow"], pp["pow"]) for it in items}
    rhat, o0 = {}, {}
    for it in items:
        lhs = jnp.concatenate([jnp.concatenate([zb0[it], zeros_r2], axis=1),
                               jnp.concatenate([zeros_r2, zb1[it]], axis=1)], axis=0)
        rhs = jnp.concatenate([wmat[it][0:R], v0[it], v0[it], wmat[it][R:2 * R]], axis=0)
        om = _dotp(lhs, rhs, _NN, pp["om"], pp["om"])
        rhat[it] = col("rt", it) + om[0:R, LANES:] + om[R:2 * R, LANES:]
        o0[it] = om[0:R, 0:LANES] * m0 + om[R:2 * R, 0:LANES] * m1
    hgt = {}
    for it in items:
        bq, kq = col("bts", it), col("kts", it)
        for s in range(n_seq):
            rs = slice(s * C, (s + 1) * C)
            ys4 = jnp.concatenate([bq[rs] * m0, kq[rs] * m0, kq[rs] * m1, bq[rs] * m1], axis=0)
            r4 = jnp.concatenate([wmat[it][rs], v0[it][rs], v0[it][rs],
                                  wmat[it][R + s * C:R + (s + 1) * C]], axis=0)
            hgt[it, s] = _dotp(r4, ys4, _TN, pp["hg"], pp["hg"])
    o_pair = {}
    for it in items:
        u, q = it
        o_seq = []
        for s in range(n_seq):
            rs = slice(s * C, (s + 1) * C)
            s_old = st_scr[u * n_seq + s, q]
            ht = hgt[it, s][0:LANES] * bd_mask
            gt = hgt[it, s][LANES:] + diag * prep[u]["p_end"][s * C:s * C + 1, sls[q]]
            o_seq.append(_dotp(rhat[it][rs], s_old, _NT, pp["out"], pp["out"]) + o0[it][rs])
            st_scr[u * n_seq + s, q] = _dotp(s_old, gt, _NN, pp["st"], pp["st"]) + ht
        o_pair[it] = o_seq[0] if n_seq == 1 else jnp.concatenate(o_seq, axis=0)

    inv_n = 1.0 / RW_HEAD_DIM
    for u in groups:
        o = jnp.concatenate([o_pair[u, q] for q in range(RW_PAIRS)], axis=1)
        mean = head_sum(o) * inv_n
        d = o - mean
        var = head_sum(d * d) * inv_n
        o = d * lax.rsqrt(var + RW_GN_EPS) * ln_w + ln_b
        bonus = head_sum(prep[u]["r"] * prep[u]["k"] * r_k) * prep[u]["v"]
        o_ref[u] = ((o + bonus) * prep[u]["g"]).astype(o_ref.dtype)

    @pl.when(c == n_c - 1)
    def _():
        for s in range(n_par * n_seq):
            for q in range(RW_PAIRS):
                s_out_ref[s, 2 * q] = st_scr[s, q, 0:HD, 0:HD]
                s_out_ref[s, 2 * q + 1] = st_scr[s, q, HD:2 * HD, HD:2 * HD]


def _rwkv(p_rw, row0, n_b, t_len, shift0, s0, prm, pp):
    if t_len >= RWKV_ROWS:
        seq_rows, n_seq = RWKV_ROWS, 1
    else:
        seq_rows, n_seq = t_len, RWKV_ROWS // t_len
    assert t_len % seq_rows == 0 and n_b % n_seq == 0 and row0 % (seq_rows * n_seq) == 0
    rows = seq_rows * n_seq
    n_c = t_len // seq_rows
    assert n_seq == 1 or n_c == 1
    blk0 = row0 // rows
    has_state = s0 is not None
    n_grp = n_b // n_seq
    n_par = math.gcd(n_grp, RWKV_GROUPS)
    seqs = n_par * n_seq

    wda = jnp.zeros((LANES, 2 * RW_WIDTH), F32)
    wda = wda.at[0:RW_DECAY_RANK, 0:RW_WIDTH].set(prm["w_decay_up"])
    wda = wda.at[RW_DECAY_RANK:, RW_WIDTH:].set(prm["w_a_up"])
    wa0 = jnp.concatenate([prm["w0_decay"], prm["a0"]]).reshape(1, 2 * RW_WIDTH)
    vecs = jnp.zeros((8, RW_WIDTH), F32)
    for i, name in enumerate(("k_k", "k_a", "r_k", "ln_x_w", "ln_x_b")):
        vecs = vecs.at[i].set(prm[name].reshape(RW_WIDTH))

    const = lambda *shape: pl.BlockSpec(shape, lambda bi, c: (0,) * len(shape))
    in_specs = [pl.BlockSpec((rows, RW_PROJ), lambda bi, c, u=u: (blk0 + (bi * n_par + u) * n_c + c, 0))
                for u in range(n_par)]
    args = [p_rw] * n_par
    st_block = (seqs, RW_HEADS, RW_HEAD_DIM, RW_HEAD_DIM)
    if has_state:
        in_specs += [pl.BlockSpec((seqs, 1, RW_PROJ), lambda bi, c: (bi, 0, 0)),
                     pl.BlockSpec(st_block, lambda bi, c: (bi, 0, 0, 0))]
        args += [shift0.reshape(n_b, 1, RW_PROJ), s0]
    in_specs += [const(1, RW_PROJ), const(LANES, 2 * RW_WIDTH), const(1, 2 * RW_WIDTH),
                 const(RW_GATE_RANK, RW_WIDTH), const(8, RW_WIDTH)]
    args += [prm["mu_rwkv"].reshape(1, RW_PROJ), wda.astype(BF16), wa0,
             prm["w_gate_up"].astype(BF16), vecs]

    o, sh, st = pl.pallas_call(
        functools.partial(_rwkv_kernel, seq_rows=seq_rows, n_seq=n_seq, n_par=n_par, has_state=has_state, pp=pp),
        grid=(n_grp // n_par, n_c),
        in_specs=in_specs,
        out_specs=[
            pl.BlockSpec((n_par, rows, RW_WIDTH), lambda bi, c: (bi, c, 0)),
            pl.BlockSpec((seqs, 1, RW_PROJ), lambda bi, c: (bi, 0, 0)),
            pl.BlockSpec(st_block, lambda bi, c: (bi, 0, 0, 0)),
        ],
        out_shape=[
            jax.ShapeDtypeStruct((n_grp, n_c * rows, RW_WIDTH), BF16),
            jax.ShapeDtypeStruct((n_b, 1, RW_PROJ), F32),
            jax.ShapeDtypeStruct((n_b, RW_HEADS, RW_HEAD_DIM, RW_HEAD_DIM), F32),
        ],
        scratch_shapes=[pltpu.VMEM((seqs, 1, RW_PROJ), F32),
                        pltpu.VMEM((seqs, RW_PAIRS, LANES, LANES), F32)],
        compiler_params=_params("arbitrary", "arbitrary"),
        name="rwkv_state" if has_state else "rwkv_fresh",
    )(*args)
    return o.reshape(n_b * t_len, RW_WIDTH), sh.reshape(n_b, RW_PROJ), st


def _sink_probs(s_parts, sink):
    m = sink
    for s in s_parts:
        m = jnp.maximum(m, jnp.max(s, axis=-1, keepdims=True))
    den = jnp.exp(sink - m)
    es = []
    for s in s_parts:
        e = jnp.exp(s - m)
        den = den + jnp.sum(e, axis=-1, keepdims=True)
        es.append(e)
    inv = 1.0 / den
    return [(e * inv).astype(BF16) for e in es]


def _group_queries(q, hk):
    return jnp.concatenate(
        [q[:, (hk * SWA_GROUP + g) * SWA_HEAD_DIM:(hk * SWA_GROUP + g + 1) * SWA_HEAD_DIM]
         for g in range(SWA_GROUP)], axis=0).astype(BF16)


def _group_sinks(sink_ref, hk, grp):
    sink = jnp.zeros(grp.shape, F32)
    for g in range(SWA_GROUP):
        sink = jnp.where(grp == g, sink_ref[hk * SWA_GROUP + g], sink)
    return sink


def _swa_prompt_kernel(sink_ref, q_ref, kp_ref, kc_ref, vp_ref, vc_ref, o_ref, *, n_win):
    n = pl.program_id(1)
    Wn = WINDOW
    kf = jnp.concatenate([kp_ref[...], kc_ref[...]], axis=0).astype(BF16)
    vf = jnp.concatenate([vp_ref[...], vc_ref[...]], axis=0).astype(BF16)
    gi = lax.broadcasted_iota(jnp.int32, (SWA_GROUP * Wn, 2 * Wn), 0)
    i = _mod_pow2(gi, Wn)
    j = lax.broadcasted_iota(jnp.int32, (SWA_GROUP * Wn, 2 * Wn), 1)
    ok = (j > i) & (j <= i + Wn)
    first_key = jnp.where(n > 0, 0, Wn)
    ok_first = ok & (j >= first_key)
    grp = _div_pow2(lax.broadcasted_iota(jnp.int32, (SWA_GROUP * Wn, 1), 0), Wn)
    ks = [slice(hk * SWA_HEAD_DIM, (hk + 1) * SWA_HEAD_DIM) for hk in range(SWA_KV_HEADS)]
    sinks = [_group_sinks(sink_ref, hk, grp) for hk in range(SWA_KV_HEADS)]
    work = [(w, hk) for w in range(n_win) for hk in range(SWA_KV_HEADS)]
    qw = [q_ref[w * Wn:(w + 1) * Wn, :] * SWA_SCALE for w in range(n_win)]
    sc = {(w, hk): lax.dot_general(_group_queries(qw[w], hk), kf[w * Wn:(w + 2) * Wn, ks[hk]], _NT,
                                   preferred_element_type=F32) for w, hk in work}
    pr = {(w, hk): _sink_probs([jnp.where(ok_first if w == 0 else ok, sc[w, hk], -jnp.inf)], sinks[hk])[0]
          for w, hk in work}
    ov = {(w, hk): jnp.dot(pr[w, hk], vf[w * Wn:(w + 2) * Wn, ks[hk]], preferred_element_type=F32)
          for w, hk in work}
    for w, hk in work:
        for g in range(SWA_GROUP):
            h = hk * SWA_GROUP + g
            o_ref[w * Wn:(w + 1) * Wn, h * SWA_HEAD_DIM:(h + 1) * SWA_HEAD_DIM] = (
                ov[w, hk][g * Wn:(g + 1) * Wn].astype(o_ref.dtype))


def _swa_prompt(p_att, n_b, t_len, sinks):
    nb = t_len // WINDOW
    n_win = 2 if nb % 2 == 0 else 1
    ns = nb // n_win
    rows = n_win * WINDOW
    kcol = SWA_Q_WIDTH // SWA_KV_WIDTH
    vcol = kcol + 1
    cur = lambda col: (lambda b, n: (b * ns + n, col))
    prv = lambda col: (lambda b, n: (b * nb + jnp.maximum(n * n_win - 1, 0), col))
    return pl.pallas_call(
        functools.partial(_swa_prompt_kernel, n_win=n_win),
        grid=(n_b, ns),
        in_specs=[
            pl.BlockSpec(memory_space=pltpu.SMEM),
            pl.BlockSpec((rows, SWA_Q_WIDTH), cur(0)),
            pl.BlockSpec((WINDOW, SWA_KV_WIDTH), prv(kcol)),
            pl.BlockSpec((rows, SWA_KV_WIDTH), cur(kcol)),
            pl.BlockSpec((WINDOW, SWA_KV_WIDTH), prv(vcol)),
            pl.BlockSpec((rows, SWA_KV_WIDTH), cur(vcol)),
        ],
        out_specs=pl.BlockSpec((rows, SWA_Q_WIDTH), lambda b, n: (b * ns + n, 0)),
        out_shape=jax.ShapeDtypeStruct((n_b * t_len, SWA_Q_WIDTH), BF16),
        compiler_params=_params("arbitrary", "arbitrary"),
        name="swa_prompt",
    )(sinks, p_att, p_att, p_att, p_att, p_att)


def _swa_step_kernel(sink_ref, q_ref, kn_ref, vn_ref, ck_ref, cv_ref, o_ref, ok_ref, ov_ref, *, n_seq, t_len):
    T = t_len
    wb = ck_ref.shape[1]
    t_row = _mod_pow2(lax.broadcasted_iota(jnp.int32, (SWA_GROUP * T, 1), 0), T)
    grp = _div_pow2(lax.broadcasted_iota(jnp.int32, (SWA_GROUP * T, 1), 0), T)
    jc = lax.broadcasted_iota(jnp.int32, (SWA_GROUP * T, wb), 1)
    jn = lax.broadcasted_iota(jnp.int32, (SWA_GROUP * T, T), 1)
    ok_c = (t_row + wb - jc < WINDOW)
    ok_n = (jn <= t_row)
    ks = [slice(hk * SWA_HEAD_DIM, (hk + 1) * SWA_HEAD_DIM) for hk in range(SWA_KV_HEADS)]
    sinks = [_group_sinks(sink_ref, hk, grp) for hk in range(SWA_KV_HEADS)]
    work = [(s, hk) for s in range(n_seq) for hk in range(SWA_KV_HEADS)]
    rows = [slice(s * T, (s + 1) * T) for s in range(n_seq)]
    knb, vnb, ckb, cvb = [], [], [], []
    for s in range(n_seq):
        kn = kn_ref[rows[s], :]
        vn = vn_ref[rows[s], :]
        ck = ck_ref[s]
        cv = cv_ref[s]
        ok_ref[s, 0:wb - T, :] = ck[T:wb, :]
        ok_ref[s, wb - T:wb, :] = kn
        ov_ref[s, 0:wb - T, :] = cv[T:wb, :]
        ov_ref[s, wb - T:wb, :] = vn
        knb.append(kn.astype(BF16))
        vnb.append(vn.astype(BF16))
        ckb.append(ck.astype(BF16))
        cvb.append(cv.astype(BF16))
    qs = [q_ref[rows[s], :] * SWA_SCALE for s in range(n_seq)]
    qh = {(s, hk): _group_queries(qs[s], hk) for s, hk in work}
    sc = {w: lax.dot_general(qh[w], ckb[w[0]][:, ks[w[1]]], _NT, preferred_element_type=F32) for w in work}
    sn = {w: lax.dot_general(qh[w], knb[w[0]][:, ks[w[1]]], _NT, preferred_element_type=F32) for w in work}
    pr = {w: _sink_probs([jnp.where(ok_c, sc[w], -jnp.inf), jnp.where(ok_n, sn[w], -jnp.inf)], sinks[w[1]])
          for w in work}
    ov = {w: jnp.dot(pr[w][0], cvb[w[0]][:, ks[w[1]]], preferred_element_type=F32)
          + jnp.dot(pr[w][1], vnb[w[0]][:, ks[w[1]]], preferred_element_type=F32) for w in work}
    for s, hk in work:
        for g in range(SWA_GROUP):
            h = hk * SWA_GROUP + g
            o_ref[rows[s], h * SWA_HEAD_DIM:(h + 1) * SWA_HEAD_DIM] = ov[s, hk][g * T:(g + 1) * T].astype(o_ref.dtype)


def _swa_step(p_att, row0, n_b, t_len, cache_k, cache_v, sinks):
    wb = cache_k.shape[1]
    assert wb == WINDOW and t_len % 8 == 0 and t_len < wb
    n_seq = max(1, 64 // t_len)
    while n_b % n_seq:
        n_seq //= 2
    rows = n_seq * t_len
    assert rows % 16 == 0 and row0 % rows == 0
    blk0 = row0 // rows
    kcol = SWA_Q_WIDTH // SWA_KV_WIDTH
    ck = cache_k.reshape(n_b, wb, SWA_KV_WIDTH)
    cv = cache_v.reshape(n_b, wb, SWA_KV_WIDTH)
    o, nk, nv = pl.pallas_call(
        functools.partial(_swa_step_kernel, n_seq=n_seq, t_len=t_len),
        grid=(n_b // n_seq,),
        in_specs=[
            pl.BlockSpec(memory_space=pltpu.SMEM),
            pl.BlockSpec((rows, SWA_Q_WIDTH), lambda i: (blk0 + i, 0)),
            pl.BlockSpec((rows, SWA_KV_WIDTH), lambda i: (blk0 + i, kcol)),
            pl.BlockSpec((rows, SWA_KV_WIDTH), lambda i: (blk0 + i, kcol + 1)),
            pl.BlockSpec((n_seq, wb, SWA_KV_WIDTH), lambda i: (i, 0, 0)),
            pl.BlockSpec((n_seq, wb, SWA_KV_WIDTH), lambda i: (i, 0, 0)),
        ],
        out_specs=[
            pl.BlockSpec((rows, SWA_Q_WIDTH), lambda i: (i, 0)),
            pl.BlockSpec((n_seq, wb, SWA_KV_WIDTH), lambda i: (i, 0, 0)),
            pl.BlockSpec((n_seq, wb, SWA_KV_WIDTH), lambda i: (i, 0, 0)),
        ],
        out_shape=[
            jax.ShapeDtypeStruct((n_b * t_len, SWA_Q_WIDTH), BF16),
            jax.ShapeDtypeStruct((n_b, wb, SWA_KV_WIDTH), F32),
            jax.ShapeDtypeStruct((n_b, wb, SWA_KV_WIDTH), F32),
        ],
        compiler_params=_params("arbitrary"),
        name="swa_step",
    )(sinks, p_att, p_att, p_att, ck, cv)
    shape = (n_b, wb, SWA_KV_HEADS, SWA_HEAD_DIM)
    return o, nk.reshape(shape), nv.reshape(shape)


def _softmax(s):
    m = jnp.max(s, axis=-1, keepdims=True)
    e = jnp.exp(s - m)
    return (e / jnp.sum(e, axis=-1, keepdims=True)).astype(BF16)


def _mem_attend(items, o_ref):
    scale = MEM_HEAD_DIM ** -0.5
    sc = [lax.dot_general(q, k, _NT, preferred_element_type=F32) * scale for q, k, _, _, _ in items]
    pr = [_softmax(s) for s in sc]
    ov = [jnp.dot(p, it[2], preferred_element_type=F32) for p, it in zip(pr, items)]
    for o, it in zip(ov, items):
        o_ref[it[3], it[4]] = o.astype(o_ref.dtype)


def _mem_prompt_kernel(qa_ref, qb_ref, k_ref, v_ref, o_ref):
    kb = k_ref[...].astype(BF16)
    vb = v_ref[...].astype(BF16)
    half = MEM_HEADS // 2
    items = []
    for h in range(MEM_HEADS):
        q_ref = qa_ref if h < half else qb_ref
        hs = slice((h % half) * MEM_HEAD_DIM, (h % half + 1) * MEM_HEAD_DIM)
        ms = slice(h * MEM_HEAD_DIM, (h + 1) * MEM_HEAD_DIM)
        items.append((q_ref[:, hs].astype(BF16), kb[:, ms], vb[:, ms], slice(None), ms))
    _mem_attend(items, o_ref)


def _mem_prompt(p_att, n_b, t_len, mem_kv):
    m_tok = mem_kv.shape[0] // n_b
    tq = math.gcd(t_len, 512)
    nq = t_len // tq
    half_w = MEM_WIDTH // 2
    qcol = (SWA_Q_WIDTH + 2 * SWA_KV_WIDTH) // half_w
    return pl.pallas_call(
        _mem_prompt_kernel,
        grid=(n_b, nq),
        in_specs=[
            pl.BlockSpec((tq, half_w), lambda b, n: (b * nq + n, qcol)),
            pl.BlockSpec((tq, half_w), lambda b, n: (b * nq + n, qcol + 1)),
            pl.BlockSpec((m_tok, MEM_WIDTH), lambda b, n: (b, 0)),
            pl.BlockSpec((m_tok, MEM_WIDTH), lambda b, n: (b, 1)),
        ],
        out_specs=pl.BlockSpec((tq, MEM_WIDTH), lambda b, n: (b * nq + n, 0)),
        out_shape=jax.ShapeDtypeStruct((n_b * t_len, MEM_WIDTH), BF16),
        compiler_params=_params("arbitrary", "arbitrary"),
        name="mem_prompt",
    )(p_att, p_att, mem_kv, mem_kv)


def _mem_step_kernel(qa_ref, qb_ref, k_ref, v_ref, o_ref, *, n_seq, t_len):
    half = MEM_HEADS // 2
    m_tok = k_ref.shape[1] // MEM_HEADS
    items = []
    for s in range(n_seq):
        rs = slice(s * t_len, (s + 1) * t_len)
        for h in range(MEM_HEADS):
            q_ref = qa_ref if h < half else qb_ref
            hs = slice((h % half) * MEM_HEAD_DIM, (h % half + 1) * MEM_HEAD_DIM)
            ms = slice(h * MEM_HEAD_DIM, (h + 1) * MEM_HEAD_DIM)
            hrows = pl.ds(h, m_tok, stride=MEM_HEADS)
            items.append((q_ref[rs, hs].astype(BF16), k_ref[s, hrows, :].astype(BF16),
                          v_ref[s, hrows, :].astype(BF16), rs, ms))
    _mem_attend(items, o_ref)


def _mem_step(p_att, row0, n_b, t_len, cache_k, cache_v):
    m_tok = cache_k.shape[1]
    n_seq = max(1, 64 // t_len)
    while n_b % n_seq:
        n_seq //= 2
    rows = n_seq * t_len
    assert rows % 16 == 0 and row0 % rows == 0
    blk0 = row0 // rows
    half_w = MEM_WIDTH // 2
    qcol = (SWA_Q_WIDTH + 2 * SWA_KV_WIDTH) // half_w
    cache_block = (n_seq, m_tok * MEM_HEADS, MEM_HEAD_DIM)
    ck = cache_k.reshape(n_b, m_tok * MEM_HEADS, MEM_HEAD_DIM)
    cv = cache_v.reshape(n_b, m_tok * MEM_HEADS, MEM_HEAD_DIM)
    return pl.pallas_call(
        functools.partial(_mem_step_kernel, n_seq=n_seq, t_len=t_len),
        grid=(n_b // n_seq,),
        in_specs=[
            pl.BlockSpec((rows, half_w), lambda i: (blk0 + i, qcol)),
            pl.BlockSpec((rows, half_w), lambda i: (blk0 + i, qcol + 1)),
            pl.BlockSpec(cache_block, lambda i: (i, 0, 0)),
            pl.BlockSpec(cache_block, lambda i: (i, 0, 0)),
        ],
        out_specs=pl.BlockSpec((rows, MEM_WIDTH), lambda i: (i, 0)),
        out_shape=jax.ShapeDtypeStruct((n_b * t_len, MEM_WIDTH), BF16),
        compiler_params=_params("arbitrary"),
        name="mem_step",
    )(p_att, p_att, ck, cv)


def _merge_kernel(xn_ref, rwp_ref, rws_ref, swp_ref, sws_ref, mmp_ref, mms_ref,
                  wg0_ref, wg1_ref, wg2_ref, wb0_ref, wb1_ref, wb2_ref, o_ref,
                  g0_scr, g1_scr, g2_scr, b0_scr, b1_scr, b2_scr, *, n_p):
    i = pl.program_id(1)

    @pl.when(i == 0)
    def _():
        for src, dst in ((wg0_ref, g0_scr), (wg1_ref, g1_scr), (wg2_ref, g2_scr),
                         (wb0_ref, b0_scr), (wb1_ref, b1_scr), (wb2_ref, b2_scr)):
            dst[...] = src[...].astype(BF16)

    def run(o_rw_ref, o_sw_ref, o_mm_ref):
        xn = xn_ref[...]
        acc = None
        for g_scr, b_scr, br_ref in ((g0_scr, b0_scr, o_rw_ref), (g1_scr, b1_scr, o_sw_ref),
                                     (g2_scr, b2_scr, o_mm_ref)):
            gate = _sigmoid(jnp.dot(xn, g_scr[...], preferred_element_type=F32))
            t = gate * jnp.dot(br_ref[...], b_scr[...], preferred_element_type=F32)
            acc = t if acc is None else acc + t
        o_ref[...] = acc.astype(o_ref.dtype)

    @pl.when(i < n_p)
    def _():
        run(rwp_ref, swp_ref, mmp_ref)

    @pl.when(i >= n_p)
    def _():
        run(rws_ref, sws_ref, mms_ref)


def _merge(xn, o_p, o_s, w_in, gate_col0, w_br, n_rows_p):
    m, d = xn.shape
    tn = MXU_COLS
    tm = _row_tile(n_rows_p, m - n_rows_p, ROW_TILE)
    n_p = n_rows_p // tm
    n_i = m // tm
    assert gate_col0 % tn == 0 and d % tn == 0
    g0 = gate_col0 // tn
    nj = d // tn
    p_spec = lambda w: pl.BlockSpec((tm, w), lambda j, i: (jnp.minimum(i, n_p - 1), 0))
    s_spec = lambda w: pl.BlockSpec((tm, w), lambda j, i: (jnp.maximum(i - n_p, 0), 0))
    in_specs = [pl.BlockSpec((tm, d), lambda j, i: (i, 0))]
    args = [xn]
    for op, os_ in zip(o_p, o_s):
        in_specs += [p_spec(op.shape[1]), s_spec(os_.shape[1])]
        args += [op, os_]
    for br in range(N_BRANCH):
        in_specs.append(pl.BlockSpec((d, tn), lambda j, i, br=br: (0, g0 + br * nj + j)))
        args.append(w_in)
    for w in w_br:
        in_specs.append(pl.BlockSpec((w.shape[0], tn), lambda j, i: (0, j)))
        args.append(w)
    scratch = [pltpu.VMEM((d, tn), BF16) for _ in range(N_BRANCH)]
    scratch += [pltpu.VMEM((w.shape[0], tn), BF16) for w in w_br]
    return pl.pallas_call(
        functools.partial(_merge_kernel, n_p=n_p),
        grid=(nj, n_i),
        in_specs=in_specs,
        out_specs=pl.BlockSpec((tm, tn), lambda j, i: (i, j)),
        out_shape=jax.ShapeDtypeStruct((m, d), BF16),
        scratch_shapes=scratch,
        compiler_params=_params("arbitrary", "arbitrary"),
        name="gated_merge",
    )(*args)


def _cast_kernel(x_ref, o_ref):
    o_ref[...] = x_ref[...].astype(o_ref.dtype)


def _to_bf16(w, name):
    r, c = w.shape
    tr = math.gcd(r, 512)
    return pl.pallas_call(
        _cast_kernel,
        grid=(r // tr,),
        in_specs=[pl.BlockSpec((tr, c), lambda i: (i, 0))],
        out_specs=pl.BlockSpec((tr, c), lambda i: (i, 0)),
        out_shape=jax.ShapeDtypeStruct((r, c), BF16),
        compiler_params=_params("arbitrary"),
        name=name,
    )(w)


def _wo_kernel(m_ref, w_ref, xp_ref, xs_ref, g1_ref, g2_ref, h_ref, hn_ref, *, n_p):
    i = pl.program_id(0)
    tm = m_ref.shape[0]
    halves = [slice(0, tm // 2), slice(tm // 2, tm)] if tm % 32 == 0 else [slice(0, tm)]

    def run(x_ref):
        for rs in halves:
            f = jnp.dot(m_ref[rs, :], w_ref[...], preferred_element_type=F32)
            h = x_ref[rs, :] + _rms(f, g1_ref[...])
            h_ref[rs, :] = h
            hn_ref[rs, :] = _rms(h, g2_ref[...]).astype(hn_ref.dtype)

    @pl.when(i < n_p)
    def _():
        run(xp_ref)

    @pl.when(i >= n_p)
    def _():
        run(xs_ref)


def _wo_block(merged, w_o, xp, xs, g_post, g_pre_ffn):
    m, d = merged.shape
    n_rows_p = xp.shape[0]
    tm = _row_tile(n_rows_p, m - n_rows_p, NORM_ROW_TILE)
    n_p = n_rows_p // tm
    return pl.pallas_call(
        functools.partial(_wo_kernel, n_p=n_p),
        grid=(m // tm,),
        in_specs=[
            pl.BlockSpec((tm, d), lambda i: (i, 0)),
            pl.BlockSpec((d, d), lambda i: (0, 0), pipeline_mode=pl.Buffered(1)),
            pl.BlockSpec((tm, d), lambda i: (jnp.minimum(i, n_p - 1), 0)),
            pl.BlockSpec((tm, d), lambda i: (jnp.maximum(i - n_p, 0), 0)),
            pl.BlockSpec((1, d), lambda i: (0, 0)),
            pl.BlockSpec((1, d), lambda i: (0, 0)),
        ],
        out_specs=[pl.BlockSpec((tm, d), lambda i: (i, 0)), pl.BlockSpec((tm, d), lambda i: (i, 0))],
        out_shape=[jax.ShapeDtypeStruct((m, d), F32), jax.ShapeDtypeStruct((m, d), BF16)],
        compiler_params=_params("arbitrary"),
        name="wo_residual",
    )(merged, _to_bf16(w_o, "cast_w_o"), xp, xs, g_post.reshape(1, d), g_pre_ffn.reshape(1, d))


def _gelu_tanh(x):
    return 0.5 * x * (1.0 + jnp.tanh(math.sqrt(2.0 / math.pi) * (x + 0.044715 * (x * x * x))))


def _ffn_up_kernel(hn_ref, wg_ref, wv_ref, cw_ref, cb_ref, st_ref, act_ref, cp_ref, cs_ref,
                   wgb_scr, wvb_scr, carry_scr, *, n_p, blocks_per_seq, t_s):
    i = pl.program_id(1)

    @pl.when(i == 0)
    def _():
        wgb_scr[...] = wg_ref[...].astype(BF16)
        wvb_scr[...] = wv_ref[...].astype(BF16)

    tm = hn_ref.shape[0]
    tn = wgb_scr.shape[1]
    subs = [slice(c0, c0 + MXU_COLS) for c0 in range(0, tn, MXU_COLS)] if tn % MXU_COLS == 0 else [slice(0, tn)]
    n_pieces = 2 if tm % (2 * 8 * t_s) == 0 else 1
    tr = tm // n_pieces
    items = [(r, cs) for cs in subs for r in range(n_pieces)]

    def gate_val(rs, cs):
        hn = hn_ref[rs, :]
        zg = jnp.dot(hn, wgb_scr[:, cs], preferred_element_type=F32)
        zv = jnp.dot(hn, wvb_scr[:, cs], preferred_element_type=F32)
        return zg, zv

    def conv_glu(cs, zg, zm1, zm2, zv):
        conv = cb_ref[:, cs] + cw_ref[0:1, cs] * zm2 + cw_ref[1:2, cs] * zm1 + cw_ref[2:3, cs] * zg
        return _gelu_tanh(conv) * zv

    @pl.when(i < n_p)
    def _():
        @pl.when(i % blocks_per_seq == 0)
        def _():
            carry_scr[...] = jnp.zeros(carry_scr.shape, F32)

        row = lax.broadcasted_iota(jnp.int32, (tr, 1), 0)
        z = {items[0]: gate_val(slice(0, tr), items[0][1])}
        for n, (r, cs) in enumerate(items):
            if n + 1 < len(items):
                r1, cs1 = items[n + 1]
                z[items[n + 1]] = gate_val(slice(r1 * tr, (r1 + 1) * tr), cs1)
            zg, zv = z.pop((r, cs))
            if r == 0:
                p1 = carry_scr[1:2, cs]
                p2 = carry_scr[0:1, cs]
            rs = slice(r * tr, (r + 1) * tr)
            zm1 = jnp.where(row == 0, p1, pltpu.roll(zg, 1, axis=0))
            zm2 = jnp.where(row == 0, p2, jnp.where(row == 1, p1, pltpu.roll(zg, 2, axis=0)))
            act_ref[rs, cs] = conv_glu(cs, zg, zm1, zm2, zv).astype(act_ref.dtype)
            p1 = zg[tr - 1:tr, :]
            p2 = zg[tr - 2:tr - 1, :]
            if r == n_pieces - 1:
                carry_scr[0:1, cs] = p2
                carry_scr[1:2, cs] = p1
                cp_ref[0, 0:1, cs] = p2
                cp_ref[0, 1:2, cs] = p1

    @pl.when(i >= n_p)
    def _():
        t = lax.broadcasted_iota(jnp.int32, (1, t_s, 1), 1)
        nq = tr // t_s
        z = {items[0]: gate_val(slice(0, tr), items[0][1])}
        for n, (r, cs) in enumerate(items):
            if n + 1 < len(items):
                r1, cs1 = items[n + 1]
                z[items[n + 1]] = gate_val(slice(r1 * tr, (r1 + 1) * tr), cs1)
            zg, zv = z.pop((r, cs))
            rs = slice(r * tr, (r + 1) * tr)
            qs = slice(r * nq, (r + 1) * nq)
            tc = zg.shape[1]
            z3 = zg.reshape(nq, t_s, tc)
            s1 = st_ref[qs, 1:2, cs]
            s0 = st_ref[qs, 0:1, cs]
            zm1 = jnp.where(t == 0, s1, pltpu.roll(z3, 1, axis=1))
            zm2 = jnp.where(t == 0, s0, jnp.where(t == 1, s1, pltpu.roll(z3, 2, axis=1)))
            act = conv_glu(cs, z3, zm1, zm2, zv.reshape(nq, t_s, tc))
            act_ref[rs, cs] = act.reshape(tr, tc).astype(act_ref.dtype)
            cs_ref[qs, :, cs] = z3[:, t_s - 2:t_s, :]


def _ffn_up(hn, w_up, conv_w, conv_b, conv_state, n_b_p, t_p, n_b_s, t_s):
    m, d = hn.shape
    d_ff = conv_b.shape[0]
    n_rows_p = n_b_p * t_p
    n_rows_s = n_b_s * t_s
    tm = _row_tile(math.gcd(n_rows_p, t_p), n_rows_s, ROW_TILE)
    assert tm == n_rows_s and t_s % 8 == 0 and t_s >= 2, "sample rows must form one row block"
    n_p = n_rows_p // tm
    blocks_per_seq = t_p // tm
    tn = math.gcd(d_ff, 512)
    nj = d_ff // tn
    last_seq = n_b_p - 1
    return pl.pallas_call(
        functools.partial(_ffn_up_kernel, n_p=n_p, blocks_per_seq=blocks_per_seq, t_s=t_s),
        grid=(nj, n_p + 1),
        in_specs=[
            pl.BlockSpec((tm, d), lambda j, i: (i, 0)),
            pl.BlockSpec((d, tn), lambda j, i: (0, j)),
            pl.BlockSpec((d, tn), lambda j, i: (0, nj + j)),
            pl.BlockSpec((CONV_W, tn), lambda j, i: (0, j)),
            pl.BlockSpec((1, tn), lambda j, i: (0, j)),
            pl.BlockSpec((n_b_s, CONV_W - 1, tn), lambda j, i: (0, 0, j)),
        ],
        out_specs=[
            pl.BlockSpec((tm, tn), lambda j, i: (i, j)),
            pl.BlockSpec((1, CONV_W - 1, tn), lambda j, i: (jnp.minimum(i // blocks_per_seq, last_seq), 0, j)),
            pl.BlockSpec((n_b_s, CONV_W - 1, tn), lambda j, i: (0, 0, j)),
        ],
        out_shape=[
            jax.ShapeDtypeStruct((m, d_ff), BF16),
            jax.ShapeDtypeStruct((n_b_p, CONV_W - 1, d_ff), F32),
            jax.ShapeDtypeStruct((n_b_s, CONV_W - 1, d_ff), F32),
        ],
        scratch_shapes=[pltpu.VMEM((d, tn), BF16), pltpu.VMEM((d, tn), BF16), pltpu.VMEM((8, tn), F32)],
        compiler_params=_params("arbitrary", "arbitrary"),
        name="ffn_up_convglu",
    )(hn, w_up, w_up, conv_w, conv_b.reshape(1, d_ff), conv_state)


def _ffn_down_kernel(a_ref, w_ref, h_ref, g_ref, y_ref):
    kk = pl.program_id(1)

    @pl.when(kk == 0)
    def _():
        y_ref[...] = jnp.zeros(y_ref.shape, F32)

    last = kk == pl.num_programs(1) - 1

    @pl.when(jnp.logical_not(last))
    def _():
        y_ref[...] += jnp.dot(a_ref[...], w_ref[...].astype(BF16), preferred_element_type=F32)

    @pl.when(last)
    def _():
        tm = y_ref.shape[0]
        n_pieces = 4 if tm % 64 == 0 else 1
        tr = tm // n_pieces
        wb = w_ref[...].astype(BF16)
        for r in range(n_pieces):
            rs = slice(r * tr, (r + 1) * tr)
            f = y_ref[rs, :] + jnp.dot(a_ref[rs, :], wb, preferred_element_type=F32)
            y_ref[rs, :] = h_ref[rs, :] + _rms(f, g_ref[...])


def _ffn_down(act, w_down, h, g, row0, n_rows):
    d_ff, d = w_down.shape
    tm = math.gcd(math.gcd(n_rows, row0) if row0 else n_rows, ROW_TILE)
    tk = math.gcd(d_ff, 512)
    i0 = row0 // tm
    return pl.pallas_call(
        _ffn_down_kernel,
        grid=(n_rows // tm, d_ff // tk),
        in_specs=[
            pl.BlockSpec((tm, tk), lambda i, k: (i0 + i, k)),
            pl.BlockSpec((tk, d), lambda i, k: (k, 0)),
            pl.BlockSpec((tm, d), lambda i, k: (i0 + i, 0)),
            pl.BlockSpec((1, d), lambda i, k: (0, 0)),
        ],
        out_specs=pl.BlockSpec((tm, d), lambda i, k: (i, 0)),
        out_shape=jax.ShapeDtypeStruct((n_rows, d), F32),
        compiler_params=_params("arbitrary", "arbitrary"),
        name="ffn_down",
    )(act, w_down, h, g.reshape(1, d))


def kernel(x_prompt, x_sample, cache_swa_k, cache_swa_v, cache_mem_k, cache_mem_v, state_rwkv_shift, state_rwkv_wkv, state_ffn_conv, mem_prompt, g_pre_mix, w_in, mu_rwkv, w_decay_up, w0_decay, w_a_up, a0, w_gate_up, k_k, k_a, r_k, ln_x_w, ln_x_b, swa_sinks, g_mem, w_mem_kv, w_br_rwkv, w_br_swa, w_br_mem, w_o, g_post_mix, g_pre_ffn, w_ffn_up, conv_w, conv_b, w_ffn_down, g_post_ffn):
    b_p, t_p, d = x_prompt.shape
    b_s, t_s, _ = x_sample.shape
    n_p, n_s = b_p * t_p, b_s * t_s
    m_tok = mem_prompt.shape[1]
    rw_prm = dict(mu_rwkv=mu_rwkv, w_decay_up=w_decay_up, w0_decay=w0_decay, w_a_up=w_a_up, a0=a0,
                  w_gate_up=w_gate_up, k_k=k_k, k_a=k_a, r_k=r_k, ln_x_w=ln_x_w, ln_x_b=ln_x_b)
    xp = x_prompt.reshape(n_p, d)
    xs = x_sample.reshape(n_s, d)

    xn = _norm_rows2(xp, xs, g_pre_mix)
    p_rw = _matmul(xn, w_in, 0, RW_PROJ, RW_PROJ // 2, ROW_TILE, "proj_rwkv")
    p_att = _matmul(xn, w_in, RW_PROJ, ATT_WIDTH, ATT_WIDTH, NORM_ROW_TILE, "proj_attn")

    mem_kv = _matmul(_norm_rows1(mem_prompt.reshape(b_p * m_tok, d), g_mem), w_mem_kv,
                     0, 2 * MEM_WIDTH, MEM_WIDTH, ROW_TILE, "proj_mem_kv")

    o_rw_p, shift_p, wkv_p = _rwkv(p_rw, 0, b_p, t_p, None, None, rw_prm, RWKV_PARTS)
    o_rw_s, shift_s, wkv_s = _rwkv(p_rw, n_p, b_s, t_s, state_rwkv_shift, state_rwkv_wkv, rw_prm, RWKV_PARTS)

    o_sw_p = _swa_prompt(p_att, b_p, t_p, swa_sinks)
    o_sw_s, swa_k_s, swa_v_s = _swa_step(p_att, n_p, b_s, t_s, cache_swa_k, cache_swa_v, swa_sinks)

    o_mm_p = _mem_prompt(p_att, b_p, t_p, mem_kv)
    o_mm_s = _mem_step(p_att, n_p, b_s, t_s, cache_mem_k, cache_mem_v)

    merged = _merge(xn, (o_rw_p, o_sw_p, o_mm_p), (o_rw_s, o_sw_s, o_mm_s), w_in,
                    RW_PROJ + ATT_WIDTH, (w_br_rwkv, w_br_swa, w_br_mem), n_p)
    h, hn = _wo_block(merged, w_o, xp, xs, g_post_mix, g_pre_ffn)
    act, conv_p, conv_s = _ffn_up(hn, w_ffn_up, conv_w, conv_b, state_ffn_conv, b_p, t_p, b_s, t_s)
    y_p = _ffn_down(act, w_ffn_down, h, g_post_ffn, 0, n_p)
    y_s = _ffn_down(act, w_ffn_down, h, g_post_ffn, n_p, n_s)

    keep = min(WINDOW, t_p)
    kv_p = jnp.stack([lax.slice(p_att, ((b + 1) * t_p - keep, SWA_Q_WIDTH),
                                ((b + 1) * t_p, SWA_Q_WIDTH + 2 * SWA_KV_WIDTH)) for b in range(b_p)])
    swa_k_p = kv_p[..., :SWA_KV_WIDTH].reshape(b_p, keep, SWA_KV_HEADS, SWA_HEAD_DIM)
    swa_v_p = kv_p[..., SWA_KV_WIDTH:].reshape(b_p, keep, SWA_KV_HEADS, SWA_HEAD_DIM)
    mem_k_p = mem_kv[:, :MEM_WIDTH].reshape(b_p, m_tok, MEM_HEADS, MEM_HEAD_DIM)
    mem_v_p = mem_kv[:, MEM_WIDTH:].reshape(b_p, m_tok, MEM_HEADS, MEM_HEAD_DIM)
    return (y_p.reshape(b_p, t_p, d), y_s.reshape(b_s, t_s, d), swa_k_p, swa_v_p, mem_k_p, mem_v_p,
            shift_p, wkv_p, conv_p, swa_k_s, swa_v_s, shift_s, wkv_s, conv_s)
```

```python
import functools
import math

import jax
import jax.numpy as jnp
from jax import lax
from jax.experimental import pallas as pl
from jax.experimental.pallas import tpu as pltpu

F32 = jnp.float32
BF16 = jnp.bfloat16

RW_HEADS = 12
RW_HEAD_DIM = 64
RW_WIDTH = RW_HEADS * RW_HEAD_DIM
RW_DECAY_RANK = 64
RW_A_RANK = 64
RW_GATE_RANK = 128
RW_PROJ = 3 * RW_WIDTH + RW_DECAY_RANK + RW_A_RANK + RW_GATE_RANK
RW_GN_EPS = 6.4e-4
RW_PAIRS = RW_HEADS // 2
SWA_Q_HEADS = 12
SWA_KV_HEADS = 4
SWA_GROUP = SWA_Q_HEADS // SWA_KV_HEADS
SWA_HEAD_DIM = 64
SWA_Q_WIDTH = SWA_Q_HEADS * SWA_HEAD_DIM
SWA_KV_WIDTH = SWA_KV_HEADS * SWA_HEAD_DIM
WINDOW = 128
MEM_HEADS = 4
MEM_HEAD_DIM = 128
MEM_WIDTH = MEM_HEADS * MEM_HEAD_DIM
N_BRANCH = 3
CONV_W = 3
NORM_EPS = 1e-6
ATT_WIDTH = SWA_Q_WIDTH + 2 * SWA_KV_WIDTH + MEM_WIDTH
SWA_SCALE = SWA_HEAD_DIM ** -0.5
assert math.frexp(SWA_SCALE)[0] == 0.5

LANES = 128
MXU_COLS = 256
VMEM_LIMIT_BYTES = 56 * 1024 * 1024

RWKV_ROWS = 64
RWKV_GROUPS = 2
RWKV_PARTS = dict(z=1, lv=1, pow=1, app=1, om=1, hg=1, out=1, st=1)
ROW_TILE = 1024
NORM_ROW_TILE = 512


def _params(*sem):
    return pltpu.CompilerParams(dimension_semantics=sem, vmem_limit_bytes=VMEM_LIMIT_BYTES)


def _row_tile(n_p, n_s, target):
    t = math.gcd(math.gcd(n_p, n_s), target)
    assert t % 16 == 0, (n_p, n_s, target)
    return t


def _div_pow2(x, n):
    assert n & (n - 1) == 0
    return lax.shift_right_logical(x, n.bit_length() - 1)


def _mod_pow2(x, n):
    assert n & (n - 1) == 0
    return jnp.bitwise_and(x, n - 1)


def _rms(x, g):
    ms = jnp.mean(x * x, axis=-1, keepdims=True)
    return x * lax.rsqrt(ms + NORM_EPS) * g


def _norm2_kernel(xp_ref, xs_ref, g_ref, o_ref, *, n_p):
    i = pl.program_id(0)

    @pl.when(i < n_p)
    def _():
        o_ref[...] = _rms(xp_ref[...], g_ref[...]).astype(o_ref.dtype)

    @pl.when(i >= n_p)
    def _():
        o_ref[...] = _rms(xs_ref[...], g_ref[...]).astype(o_ref.dtype)


def _norm_rows2(xp, xs, g):
    n_rows_p, d = xp.shape
    n_rows_s = xs.shape[0]
    tm = _row_tile(n_rows_p, n_rows_s, NORM_ROW_TILE)
    n_p, n_s = n_rows_p // tm, n_rows_s // tm
    return pl.pallas_call(
        functools.partial(_norm2_kernel, n_p=n_p),
        grid=(n_p + n_s,),
        in_specs=[
            pl.BlockSpec((tm, d), lambda i: (jnp.minimum(i, n_p - 1), 0)),
            pl.BlockSpec((tm, d), lambda i: (jnp.maximum(i - n_p, 0), 0)),
            pl.BlockSpec((1, d), lambda i: (0, 0)),
        ],
        out_specs=pl.BlockSpec((tm, d), lambda i: (i, 0)),
        out_shape=jax.ShapeDtypeStruct((n_rows_p + n_rows_s, d), BF16),
        compiler_params=_params("arbitrary"),
        name="norm_rows",
    )(xp, xs, g.reshape(1, d))


def _norm1_kernel(x_ref, g_ref, o_ref):
    o_ref[...] = _rms(x_ref[...], g_ref[...]).astype(o_ref.dtype)


def _norm_rows1(x, g):
    n, d = x.shape
    tm = math.gcd(n, NORM_ROW_TILE)
    return pl.pallas_call(
        _norm1_kernel,
        grid=(n // tm,),
        in_specs=[pl.BlockSpec((tm, d), lambda i: (i, 0)), pl.BlockSpec((1, d), lambda i: (0, 0))],
        out_specs=pl.BlockSpec((tm, d), lambda i: (i, 0)),
        out_shape=jax.ShapeDtypeStruct((n, d), BF16),
        compiler_params=_params("arbitrary"),
        name="norm_mem",
    )(x, g.reshape(1, d))


def _mm_kernel(a_ref, w_ref, o_ref, wb_ref):
    @pl.when(pl.program_id(1) == 0)
    def _():
        wb_ref[...] = w_ref[...].astype(BF16)

    o_ref[...] = jnp.dot(a_ref[...], wb_ref[...], preferred_element_type=F32)


def _matmul(a, w, col0, n_cols, tn, tm, name):
    m, k = a.shape
    tm = math.gcd(m, tm)
    assert n_cols % tn == 0 and col0 % LANES == 0 and tn % LANES == 0
    return pl.pallas_call(
        _mm_kernel,
        grid=(n_cols // tn, m // tm),
        in_specs=[
            pl.BlockSpec((tm, k), lambda j, i: (i, 0)),
            pl.BlockSpec((pl.Element(k), pl.Element(tn)),
                         lambda j, i: (0, pl.multiple_of(col0 + j * tn, LANES))),
        ],
        out_specs=pl.BlockSpec((tm, tn), lambda j, i: (i, j)),
        out_shape=jax.ShapeDtypeStruct((m, n_cols), F32),
        scratch_shapes=[pltpu.VMEM((k, tn), BF16)],
        compiler_params=_params("arbitrary", "arbitrary"),
        name=name,
    )(a, w)


_NN = (((1,), (0,)), ((), ()))
_NT = (((1,), (1,)), ((), ()))
_TN = (((0,), (0,)), ((), ()))


def _bf_parts(x, n):
    parts = []
    r = x
    for i in range(n):
        h = r.astype(BF16)
        parts.append(h)
        if i + 1 < n:
            r = r - h.astype(F32)
    return parts


def _dotp(a, b, dn, pa, pb):
    pa_parts = _bf_parts(a, pa)
    pb_parts = _bf_parts(b, pb)
    acc = None
    for i in range(pa):
        for j in range(pb):
            if i + j < max(pa, pb):
                t = lax.dot_general(pa_parts[i], pb_parts[j], dn, preferred_element_type=F32)
                acc = t if acc is None else acc + t
    return acc


def _softplus(x):
    return jnp.maximum(x, 0.0) + jnp.log(1.0 + jnp.exp(-jnp.abs(x)))


def _sigmoid(x):
    return 1.0 / (1.0 + jnp.exp(-x))


def _rwkv_kernel(*refs, seq_rows, n_seq, n_par, has_state, pp):
    n_in = n_par + (2 if has_state else 0)
    p_refs = refs[0:n_par]
    if has_state:
        sh0_ref, s0_ref = refs[n_par:n_in]
    (mu_ref, wda_ref, wa0_ref, wg_ref, vec_ref,
     o_ref, sh_out_ref, s_out_ref, prev_scr, st_scr) = refs[n_in:]
    c = pl.program_id(1)
    n_c = pl.num_programs(1)
    C = seq_rows
    R = seq_rows * n_seq
    W = RW_WIDTH
    HD = RW_HEAD_DIM
    groups = range(n_par)

    @pl.when(c == 0)
    def _():
        st_scr[...] = jnp.zeros(st_scr.shape, F32)
        if has_state:
            prev_scr[...] = sh0_ref[...]
            for s in range(n_par * n_seq):
                for q in range(RW_PAIRS):
                    st_scr[s, q, 0:HD, 0:HD] = s0_ref[s, 2 * q]
                    st_scr[s, q, HD:2 * HD, HD:2 * HD] = s0_ref[s, 2 * q + 1]
        else:
            prev_scr[...] = jnp.zeros(prev_scr.shape, F32)

    k_k = vec_ref[0:1, :]
    k_a = vec_ref[1:2, :]
    r_k = vec_ref[2:3, :]
    ln_w = vec_ref[3:4, :]
    ln_b = vec_ref[4:5, :]
    lane = lax.broadcasted_iota(jnp.int32, (1, LANES), 1)
    row = lax.broadcasted_iota(jnp.int32, (R, 1), 0)

    li = lax.broadcasted_iota(jnp.int32, (LANES, LANES), 0)
    lj = lax.broadcasted_iota(jnp.int32, (LANES, LANES), 1)
    same_head = _div_pow2(li, RW_HEAD_DIM) == _div_pow2(lj, RW_HEAD_DIM)
    ones_bd = jnp.where(same_head, 1.0, 0.0).astype(BF16)

    def head_sum(z):
        cols = []
        for q in range(RW_PAIRS):
            zq = z[:, q * LANES:(q + 1) * LANES]
            acc = None
            for part in _bf_parts(zq, 2):
                t = jnp.dot(part, ones_bd, preferred_element_type=F32)
                acc = t if acc is None else acc + t
            cols.append(acc)
        return jnp.concatenate(cols, axis=1)

    ri = lax.broadcasted_iota(jnp.int32, (R, R), 0)
    rj = lax.broadcasted_iota(jnp.int32, (R, R), 1)
    same_seq = _div_pow2(ri, C) == _div_pow2(rj, C)
    causal = jnp.where(same_seq & (rj <= ri), 1.0, 0.0).astype(BF16)

    def prepare(u):
        p = p_refs[u][...]
        prev = pltpu.roll(p, 1, axis=0)
        for s in range(n_seq):
            prev = jnp.where(row == s * C, prev_scr[u * n_seq + s], prev)
        for s in range(n_seq):
            prev_scr[u * n_seq + s] = p[(s + 1) * C - 1:(s + 1) * C, :]
            sh_out_ref[u * n_seq + s] = p[(s + 1) * C - 1:(s + 1) * C, :]
        x = p + mu_ref[...] * (prev - p)
        r = x[:, 0:W]
        k = x[:, W:2 * W]
        v = x[:, 2 * W:3 * W]
        wa_in = x[:, 3 * W:3 * W + LANES]
        g_in = x[:, 3 * W + LANES:3 * W + 2 * LANES]
        wa_act = jnp.where(lane < RW_DECAY_RANK, jnp.tanh(wa_in), wa_in)
        za = jnp.dot(wa_act.astype(BF16), wda_ref[...], preferred_element_type=F32) + wa0_ref[...]
        w_log = -_softplus(-za[:, 0:W]) - 0.5
        lw = -jnp.exp(w_log)
        a = _sigmoid(za[:, W:2 * W])
        g = jnp.dot(_sigmoid(g_in).astype(BF16), wg_ref[...], preferred_element_type=F32)
        kk = k * k_k
        kk = kk / jnp.maximum(jnp.sqrt(head_sum(kk * kk)), 1e-12)
        k = k * (1.0 + (a - 1.0) * k_a)
        b = kk * a
        cl = None
        for part in _bf_parts(lw, 2):
            t1 = jnp.dot(causal, part, preferred_element_type=F32)
            cl = t1 if cl is None else cl + t1
        if n_seq == 1:
            ct = cl[R - 1:R, :]
        else:
            last = cl.reshape(n_seq, C, W)[:, C - 1:C, :]
            ct = jnp.broadcast_to(last, (n_seq, C, W)).reshape(R, W)
        e_neg = jnp.exp(-cl)
        e_end = jnp.exp(ct - cl)
        return dict(r=r, k=k, v=v, g=g, at=-kk * jnp.exp(cl - lw), rt=r * jnp.exp(cl), bt=b * e_neg,
                    kt=k * e_neg, bts=b * e_end, kts=k * e_end,
                    p_end=jnp.exp(ct))

    prep = [prepare(u) for u in groups]

    strict = jnp.where(same_seq & (rj < ri), 1.0, 0.0)
    incl = jnp.where(same_seq & (rj <= ri), 1.0, 0.0)
    strict2 = jnp.concatenate([strict, strict], axis=1)
    incl2 = jnp.concatenate([incl, incl], axis=1)
    m0 = jnp.where(lane < RW_HEAD_DIM, 1.0, 0.0)
    m1 = jnp.where(lane >= RW_HEAD_DIM, 1.0, 0.0)
    diag = jnp.where(li == lj, 1.0, 0.0)
    bd_mask = jnp.where(same_head, 1.0, 0.0)
    zeros_rl = jnp.zeros((R, LANES), F32)
    zeros_r2 = jnp.zeros((R, 2 * R), F32)
    n_dbl = max(1, (C - 1).bit_length())
    sls = [slice(q * LANES, (q + 1) * LANES) for q in range(RW_PAIRS)]
    items = [(u, q) for u in groups for q in range(RW_PAIRS)]

    def col(name, it):
        return prep[it[0]][name][:, sls[it[1]]]

    v0 = {it: jnp.concatenate([col("v", it), zeros_rl], axis=1) for it in items}
    zs = {}
    for it in items:
        xa, xr, bq, kq = col("at", it), col("rt", it), col("bt", it), col("kt", it)
        x4 = jnp.concatenate([xa * m0, xr * m0, xa * m1, xr * m1], axis=0)
        y4 = jnp.concatenate([bq, kq, kq, bq], axis=0)
        zs[it] = _dotp(x4, y4, _NT, pp["z"], pp["z"])
    zt0 = {it: zs[it][0:R, 0:2 * R] * strict2 for it in items}
    zb0 = {it: zs[it][R:2 * R, 0:2 * R] * incl2 for it in items}
    zt1 = {it: zs[it][2 * R:3 * R, 2 * R:4 * R] * strict2 for it in items}
    zb1 = {it: zs[it][3 * R:4 * R, 2 * R:4 * R] * incl2 for it in items}
    half0 = jnp.where(lax.broadcasted_iota(jnp.int32, (1, 2 * R), 1) < R, 1.0, 0.0)
    half1 = 1.0 - half0
    pw = {it: jnp.concatenate([zt0[it] * half0, zt1[it] * half1], axis=0) for it in items}
    wmat = {}
    for it in items:
        lhs = jnp.concatenate([jnp.concatenate([zt0[it], zeros_r2], axis=1),
                               jnp.concatenate([zeros_r2, zt1[it]], axis=1)], axis=0)
        vq = col("v", it)
        rhs = jnp.concatenate([zeros_rl, vq, vq, zeros_rl], axis=0)
        lv = _dotp(lhs, rhs, _NN, pp["lv"], pp["lv"])
        xa = col("at", it)
        wmat[it] = jnp.concatenate([lv, jnp.concatenate([xa * m0, xa * m1], axis=0)], axis=1)
    for step in range(n_dbl):
        wmat = {it: wmat[it] + _dotp(pw[it], wmat[it], _NN, pp["app"], pp["app"]) for it in items}
        if step + 1 < n_dbl:
            pw = {it: _dotp(pw[it], pw[it], _NN, pp["pow"], pp["pow"]) for it in items}
    rhat, o0 = {}, {}
    for it in items:
        lhs = jnp.concatenate([jnp.concatenate([zb0[it], zeros_r2], axis=1),
                               jnp.concatenate([zeros_r2, zb1[it]], axis=1)], axis=0)
        rhs = jnp.concatenate([wmat[it][0:R], v0[it], v0[it], wmat[it][R:2 * R]], axis=0)
        om = _dotp(lhs, rhs, _NN, pp["om"], pp["om"])
        rhat[it] = col("rt", it) + om[0:R, LANES:] + om[R:2 * R, LANES:]
        o0[it] = om[0:R, 0:LANES] * m0 + om[R:2 * R, 0:LANES] * m1
    hgt = {}
    for it in items:
        bq, kq = col("bts", it), col("kts", it)
        for s in range(n_seq):
            rs = slice(s * C, (s + 1) * C)
            ys4 = jnp.concatenate([bq[rs] * m0, kq[rs] * m0, kq[rs] * m1, bq[rs] * m1], axis=0)
            r4 = jnp.concatenate([wmat[it][rs], v0[it][rs], v0[it][rs],
                                  wmat[it][R + s * C:R + (s + 1) * C]], axis=0)
            hgt[it, s] = _dotp(r4, ys4, _TN, pp["hg"], pp["hg"])
    o_pair = {}
    for it in items:
        u, q = it
        o_seq = []
        for s in range(n_seq):
            rs = slice(s * C, (s + 1) * C)
            s_old = st_scr[u * n_seq + s, q]
            ht = hgt[it, s][0:LANES] * bd_mask
            gt = hgt[it, s][LANES:] + diag * prep[u]["p_end"][s * C:s * C + 1, sls[q]]
            o_seq.append(_dotp(rhat[it][rs], s_old, _NT, pp["out"], pp["out"]) + o0[it][rs])
            st_scr[u * n_seq + s, q] = _dotp(s_old, gt, _NN, pp["st"], pp["st"]) + ht
        o_pair[it] = o_seq[0] if n_seq == 1 else jnp.concatenate(o_seq, axis=0)

    inv_n = 1.0 / RW_HEAD_DIM
    for u in groups:
        o = jnp.concatenate([o_pair[u, q] for q in range(RW_PAIRS)], axis=1)
        mean = head_sum(o) * inv_n
        d = o - mean
        var = head_sum(d * d) * inv_n
        o = d * lax.rsqrt(var + RW_GN_EPS) * ln_w + ln_b
        bonus = head_sum(prep[u]["r"] * prep[u]["k"] * r_k) * prep[u]["v"]
        o_ref[u] = ((o + bonus) * prep[u]["g"]).astype(o_ref.dtype)

    @pl.when(c == n_c - 1)
    def _():
        for s in range(n_par * n_seq):
            for q in range(RW_PAIRS):
                s_out_ref[s, 2 * q] = st_scr[s, q, 0:HD, 0:HD]
                s_out_ref[s, 2 * q + 1] = st_scr[s, q, HD:2 * HD, HD:2 * HD]


def _rwkv(p_rw, row0, n_b, t_len, shift0, s0, prm, pp):
    if t_len >= RWKV_ROWS:
        seq_rows, n_seq = RWKV_ROWS, 1
    else:
        seq_rows, n_seq = t_len, RWKV_ROWS // t_len
    assert t_len % seq_rows == 0 and n_b % n_seq == 0 and row0 % (seq_rows * n_seq) == 0
    rows = seq_rows * n_seq
    n_c = t_len // seq_rows
    assert n_seq == 1 or n_c == 1
    blk0 = row0 // rows
    has_state = s0 is not None
    n_grp = n_b // n_seq
    n_par = math.gcd(n_grp, RWKV_GROUPS)
    seqs = n_par * n_seq

    wda = jnp.zeros((LANES, 2 * RW_WIDTH), F32)
    wda = wda.at[0:RW_DECAY_RANK, 0:RW_WIDTH].set(prm["w_decay_up"])
    wda = wda.at[RW_DECAY_RANK:, RW_WIDTH:].set(prm["w_a_up"])
    wa0 = jnp.concatenate([prm["w0_decay"], prm["a0"]]).reshape(1, 2 * RW_WIDTH)
    vecs = jnp.zeros((8, RW_WIDTH), F32)
    for i, name in enumerate(("k_k", "k_a", "r_k", "ln_x_w", "ln_x_b")):
        vecs = vecs.at[i].set(prm[name].reshape(RW_WIDTH))

    const = lambda *shape: pl.BlockSpec(shape, lambda bi, c: (0,) * len(shape))
    in_specs = [pl.BlockSpec((rows, RW_PROJ), lambda bi, c, u=u: (blk0 + (bi * n_par + u) * n_c + c, 0))
                for u in range(n_par)]
    args = [p_rw] * n_par
    st_block = (seqs, RW_HEADS, RW_HEAD_DIM, RW_HEAD_DIM)
    if has_state:
        in_specs += [pl.BlockSpec((seqs, 1, RW_PROJ), lambda bi, c: (bi, 0, 0)),
                     pl.BlockSpec(st_block, lambda bi, c: (bi, 0, 0, 0))]
        args += [shift0.reshape(n_b, 1, RW_PROJ), s0]
    in_specs += [const(1, RW_PROJ), const(LANES, 2 * RW_WIDTH), const(1, 2 * RW_WIDTH),
                 const(RW_GATE_RANK, RW_WIDTH), const(8, RW_WIDTH)]
    args += [prm["mu_rwkv"].reshape(1, RW_PROJ), wda.astype(BF16), wa0,
             prm["w_gate_up"].astype(BF16), vecs]

    o, sh, st = pl.pallas_call(
        functools.partial(_rwkv_kernel, seq_rows=seq_rows, n_seq=n_seq, n_par=n_par, has_state=has_state, pp=pp),
        grid=(n_grp // n_par, n_c),
        in_specs=in_specs,
        out_specs=[
            pl.BlockSpec((n_par, rows, RW_WIDTH), lambda bi, c: (bi, c, 0)),
            pl.BlockSpec((seqs, 1, RW_PROJ), lambda bi, c: (bi, 0, 0)),
            pl.BlockSpec(st_block, lambda bi, c: (bi, 0, 0, 0)),
        ],
        out_shape=[
            jax.ShapeDtypeStruct((n_grp, n_c * rows, RW_WIDTH), BF16),
            jax.ShapeDtypeStruct((n_b, 1, RW_PROJ), F32),
            jax.ShapeDtypeStruct((n_b, RW_HEADS, RW_HEAD_DIM, RW_HEAD_DIM), F32),
        ],
        scratch_shapes=[pltpu.VMEM((seqs, 1, RW_PROJ), F32),
                        pltpu.VMEM((seqs, RW_PAIRS, LANES, LANES), F32)],
        compiler_params=_params("arbitrary", "arbitrary"),
        name="rwkv_state" if has_state else "rwkv_fresh",
    )(*args)
    return o.reshape(n_b * t_len, RW_WIDTH), sh.reshape(n_b, RW_PROJ), st


def _sink_probs(s_parts, sink):
    m = sink
    for s in s_parts:
        m = jnp.maximum(m, jnp.max(s, axis=-1, keepdims=True))
    den = jnp.exp(sink - m)
    es = []
    for s in s_parts:
        e = jnp.exp(s - m)
        den = den + jnp.sum(e, axis=-1, keepdims=True)
        es.append(e)
    inv = 1.0 / den
    return [(e * inv).astype(BF16) for e in es]


def _group_queries(q, hk):
    return jnp.concatenate(
        [q[:, (hk * SWA_GROUP + g) * SWA_HEAD_DIM:(hk * SWA_GROUP + g + 1) * SWA_HEAD_DIM]
         for g in range(SWA_GROUP)], axis=0).astype(BF16)


def _group_sinks(sink_ref, hk, grp):
    sink = jnp.zeros(grp.shape, F32)
    for g in range(SWA_GROUP):
        sink = jnp.where(grp == g, sink_ref[hk * SWA_GROUP + g], sink)
    return sink


def _swa_prompt_kernel(sink_ref, q_ref, kp_ref, kc_ref, vp_ref, vc_ref, o_ref, *, n_win):
    n = pl.program_id(1)
    Wn = WINDOW
    kf = jnp.concatenate([kp_ref[...], kc_ref[...]], axis=0).astype(BF16)
    vf = jnp.concatenate([vp_ref[...], vc_ref[...]], axis=0).astype(BF16)
    gi = lax.broadcasted_iota(jnp.int32, (SWA_GROUP * Wn, 2 * Wn), 0)
    i = _mod_pow2(gi, Wn)
    j = lax.broadcasted_iota(jnp.int32, (SWA_GROUP * Wn, 2 * Wn), 1)
    ok = (j > i) & (j <= i + Wn)
    first_key = jnp.where(n > 0, 0, Wn)
    ok_first = ok & (j >= first_key)
    grp = _div_pow2(lax.broadcasted_iota(jnp.int32, (SWA_GROUP * Wn, 1), 0), Wn)
    ks = [slice(hk * SWA_HEAD_DIM, (hk + 1) * SWA_HEAD_DIM) for hk in range(SWA_KV_HEADS)]
    sinks = [_group_sinks(sink_ref, hk, grp) for hk in range(SWA_KV_HEADS)]
    work = [(w, hk) for w in range(n_win) for hk in range(SWA_KV_HEADS)]
    qw = [q_ref[w * Wn:(w + 1) * Wn, :] * SWA_SCALE for w in range(n_win)]
    sc = {(w, hk): lax.dot_general(_group_queries(qw[w], hk), kf[w * Wn:(w + 2) * Wn, ks[hk]], _NT,
                                   preferred_element_type=F32) for w, hk in work}
    pr = {(w, hk): _sink_probs([jnp.where(ok_first if w == 0 else ok, sc[w, hk], -jnp.inf)], sinks[hk])[0]
          for w, hk in work}
    ov = {(w, hk): jnp.dot(pr[w, hk], vf[w * Wn:(w + 2) * Wn, ks[hk]], preferred_element_type=F32)
          for w, hk in work}
    for w, hk in work:
        for g in range(SWA_GROUP):
            h = hk * SWA_GROUP + g
            o_ref[w * Wn:(w + 1) * Wn, h * SWA_HEAD_DIM:(h + 1) * SWA_HEAD_DIM] = (
                ov[w, hk][g * Wn:(g + 1) * Wn].astype(o_ref.dtype))


def _swa_prompt(p_att, n_b, t_len, sinks):
    nb = t_len // WINDOW
    n_win = 2 if nb % 2 == 0 else 1
    ns = nb // n_win
    rows = n_win * WINDOW
    kcol = SWA_Q_WIDTH // SWA_KV_WIDTH
    vcol = kcol + 1
    cur = lambda col: (lambda b, n: (b * ns + n, col))
    prv = lambda col: (lambda b, n: (b * nb + jnp.maximum(n * n_win - 1, 0), col))
    return pl.pallas_call(
        functools.partial(_swa_prompt_kernel, n_win=n_win),
        grid=(n_b, ns),
        in_specs=[
            pl.BlockSpec(memory_space=pltpu.SMEM),
            pl.BlockSpec((rows, SWA_Q_WIDTH), cur(0)),
            pl.BlockSpec((WINDOW, SWA_KV_WIDTH), prv(kcol)),
            pl.BlockSpec((rows, SWA_KV_WIDTH), cur(kcol)),
            pl.BlockSpec((WINDOW, SWA_KV_WIDTH), prv(vcol)),
            pl.BlockSpec((rows, SWA_KV_WIDTH), cur(vcol)),
        ],
        out_specs=pl.BlockSpec((rows, SWA_Q_WIDTH), lambda b, n: (b * ns + n, 0)),
        out_shape=jax.ShapeDtypeStruct((n_b * t_len, SWA_Q_WIDTH), BF16),
        compiler_params=_params("arbitrary", "arbitrary"),
        name="swa_prompt",
    )(sinks, p_att, p_att, p_att, p_att, p_att)


def _swa_step_kernel(sink_ref, q_ref, kn_ref, vn_ref, ck_ref, cv_ref, o_ref, ok_ref, ov_ref, *, n_seq, t_len):
    T = t_len
    wb = ck_ref.shape[2]
    t_row = _mod_pow2(lax.broadcasted_iota(jnp.int32, (SWA_GROUP * T, 1), 0), T)
    grp = _div_pow2(lax.broadcasted_iota(jnp.int32, (SWA_GROUP * T, 1), 0), T)
    jc = lax.broadcasted_iota(jnp.int32, (SWA_GROUP * T, wb), 1)
    jn = lax.broadcasted_iota(jnp.int32, (SWA_GROUP * T, T), 1)
    ok_c = (t_row + wb - jc < WINDOW)
    ok_n = (jn <= t_row)
    ks = [slice(hk * SWA_HEAD_DIM, (hk + 1) * SWA_HEAD_DIM) for hk in range(SWA_KV_HEADS)]
    sinks = [_group_sinks(sink_ref, hk, grp) for hk in range(SWA_KV_HEADS)]
    work = [(s, hk) for s in range(n_seq) for hk in range(SWA_KV_HEADS)]
    rows = [slice(s * T, (s + 1) * T) for s in range(n_seq)]
    lane_w = lax.broadcasted_iota(jnp.int32, (1, wb), 1)
    pad = jnp.zeros((wb - T, SWA_KV_WIDTH), F32)
    knb, vnb, ckb, cvb = [], [], [], []
    for s in range(n_seq):
        kn = kn_ref[rows[s], :]
        vn = vn_ref[rows[s], :]
        ck = ck_ref[s]
        cv = cv_ref[s]
        kn_t = jnp.transpose(jnp.concatenate([pad, kn], axis=0))
        vn_t = jnp.transpose(jnp.concatenate([pad, vn], axis=0))
        ok_ref[s] = jnp.where(lane_w >= wb - T, kn_t, pltpu.roll(ck, wb - T, axis=1))
        ov_ref[s] = jnp.where(lane_w >= wb - T, vn_t, pltpu.roll(cv, wb - T, axis=1))
        knb.append(kn.astype(BF16))
        vnb.append(vn.astype(BF16))
        ckb.append(ck.astype(BF16))
        cvb.append(cv.astype(BF16))
    qs = [q_ref[rows[s], :] * SWA_SCALE for s in range(n_seq)]
    qh = {(s, hk): _group_queries(qs[s], hk) for s, hk in work}
    sc = {w: jnp.dot(qh[w], ckb[w[0]][ks[w[1]], :], preferred_element_type=F32) for w in work}
    sn = {w: lax.dot_general(qh[w], knb[w[0]][:, ks[w[1]]], _NT, preferred_element_type=F32) for w in work}
    pr = {w: _sink_probs([jnp.where(ok_c, sc[w], -jnp.inf), jnp.where(ok_n, sn[w], -jnp.inf)], sinks[w[1]])
          for w in work}
    ov = {w: lax.dot_general(pr[w][0], cvb[w[0]][ks[w[1]], :], _NT, preferred_element_type=F32)
          + jnp.dot(pr[w][1], vnb[w[0]][:, ks[w[1]]], preferred_element_type=F32) for w in work}
    for s, hk in work:
        for g in range(SWA_GROUP):
            h = hk * SWA_GROUP + g
            o_ref[rows[s], h * SWA_HEAD_DIM:(h + 1) * SWA_HEAD_DIM] = ov[s, hk][g * T:(g + 1) * T].astype(o_ref.dtype)


def _swa_step(p_att, row0, n_b, t_len, cache_k, cache_v, sinks):
    wb = cache_k.shape[1]
    assert wb == WINDOW and t_len % 8 == 0 and t_len < wb
    n_seq = max(1, 64 // t_len)
    while n_b % n_seq:
        n_seq //= 2
    rows = n_seq * t_len
    assert rows % 16 == 0 and row0 % rows == 0
    blk0 = row0 // rows
    kcol = SWA_Q_WIDTH // SWA_KV_WIDTH
    to_cm = lambda c: jnp.transpose(c, (0, 2, 3, 1)).reshape(n_b, SWA_KV_WIDTH, wb)
    from_cm = lambda c: jnp.transpose(c.reshape(n_b, SWA_KV_HEADS, SWA_HEAD_DIM, wb), (0, 3, 1, 2))
    ck = to_cm(cache_k)
    cv = to_cm(cache_v)
    o, nk, nv = pl.pallas_call(
        functools.partial(_swa_step_kernel, n_seq=n_seq, t_len=t_len),
        grid=(n_b // n_seq,),
        in_specs=[
            pl.BlockSpec(memory_space=pltpu.SMEM),
            pl.BlockSpec((rows, SWA_Q_WIDTH), lambda i: (blk0 + i, 0)),
            pl.BlockSpec((rows, SWA_KV_WIDTH), lambda i: (blk0 + i, kcol)),
            pl.BlockSpec((rows, SWA_KV_WIDTH), lambda i: (blk0 + i, kcol + 1)),
            pl.BlockSpec((n_seq, SWA_KV_WIDTH, wb), lambda i: (i, 0, 0)),
            pl.BlockSpec((n_seq, SWA_KV_WIDTH, wb), lambda i: (i, 0, 0)),
        ],
        out_specs=[
            pl.BlockSpec((rows, SWA_Q_WIDTH), lambda i: (i, 0)),
            pl.BlockSpec((n_seq, SWA_KV_WIDTH, wb), lambda i: (i, 0, 0)),
            pl.BlockSpec((n_seq, SWA_KV_WIDTH, wb), lambda i: (i, 0, 0)),
        ],
        out_shape=[
            jax.ShapeDtypeStruct((n_b * t_len, SWA_Q_WIDTH), BF16),
            jax.ShapeDtypeStruct((n_b, SWA_KV_WIDTH, wb), F32),
            jax.ShapeDtypeStruct((n_b, SWA_KV_WIDTH, wb), F32),
        ],
        compiler_params=_params("arbitrary"),
        name="swa_step",
    )(sinks, p_att, p_att, p_att, ck, cv)
    return o, from_cm(nk), from_cm(nv)


def _softmax(s):
    m = jnp.max(s, axis=-1, keepdims=True)
    e = jnp.exp(s - m)
    return (e / jnp.sum(e, axis=-1, keepdims=True)).astype(BF16)


def _mem_attend(items, o_ref):
    scale = MEM_HEAD_DIM ** -0.5
    sc = [lax.dot_general(q, k, _NT, preferred_element_type=F32) * scale for q, k, _, _, _ in items]
    pr = [_softmax(s) for s in sc]
    ov = [jnp.dot(p, it[2], preferred_element_type=F32) for p, it in zip(pr, items)]
    for o, it in zip(ov, items):
        o_ref[it[3], it[4]] = o.astype(o_ref.dtype)


def _mem_prompt_kernel(qa_ref, qb_ref, k_ref, v_ref, o_ref):
    kb = k_ref[...].astype(BF16)
    vb = v_ref[...].astype(BF16)
    half = MEM_HEADS // 2
    items = []
    for h in range(MEM_HEADS):
        q_ref = qa_ref if h < half else qb_ref
        hs = slice((h % half) * MEM_HEAD_DIM, (h % half + 1) * MEM_HEAD_DIM)
        ms = slice(h * MEM_HEAD_DIM, (h + 1) * MEM_HEAD_DIM)
        items.append((q_ref[:, hs].astype(BF16), kb[:, ms], vb[:, ms], slice(None), ms))
    _mem_attend(items, o_ref)


def _mem_prompt(p_att, n_b, t_len, mem_kv):
    m_tok = mem_kv.shape[0] // n_b
    tq = math.gcd(t_len, 512)
    nq = t_len // tq
    half_w = MEM_WIDTH // 2
    qcol = (SWA_Q_WIDTH + 2 * SWA_KV_WIDTH) // half_w
    return pl.pallas_call(
        _mem_prompt_kernel,
        grid=(n_b, nq),
        in_specs=[
            pl.BlockSpec((tq, half_w), lambda b, n: (b * nq + n, qcol)),
            pl.BlockSpec((tq, half_w), lambda b, n: (b * nq + n, qcol + 1)),
            pl.BlockSpec((m_tok, MEM_WIDTH), lambda b, n: (b, 0)),
            pl.BlockSpec((m_tok, MEM_WIDTH), lambda b, n: (b, 1)),
        ],
        out_specs=pl.BlockSpec((tq, MEM_WIDTH), lambda b, n: (b * nq + n, 0)),
        out_shape=jax.ShapeDtypeStruct((n_b * t_len, MEM_WIDTH), BF16),
        compiler_params=_params("arbitrary", "arbitrary"),
        name="mem_prompt",
    )(p_att, p_att, mem_kv, mem_kv)


def _mem_step_kernel(qa_ref, qb_ref, k_ref, v_ref, o_ref, *, n_seq, t_len):
    half = MEM_HEADS // 2
    m_tok = k_ref.shape[1] // MEM_HEADS
    items = []
    for s in range(n_seq):
        rs = slice(s * t_len, (s + 1) * t_len)
        for h in range(MEM_HEADS):
            q_ref = qa_ref if h < half else qb_ref
            hs = slice((h % half) * MEM_HEAD_DIM, (h % half + 1) * MEM_HEAD_DIM)
            ms = slice(h * MEM_HEAD_DIM, (h + 1) * MEM_HEAD_DIM)
            hrows = pl.ds(h, m_tok, stride=MEM_HEADS)
            items.append((q_ref[rs, hs].astype(BF16), k_ref[s, hrows, :].astype(BF16),
                          v_ref[s, hrows, :].astype(BF16), rs, ms))
    _mem_attend(items, o_ref)


def _mem_step(p_att, row0, n_b, t_len, cache_k, cache_v):
    m_tok = cache_k.shape[1]
    n_seq = max(1, 64 // t_len)
    while n_b % n_seq:
        n_seq //= 2
    rows = n_seq * t_len
    assert rows % 16 == 0 and row0 % rows == 0
    blk0 = row0 // rows
    half_w = MEM_WIDTH // 2
    qcol = (SWA_Q_WIDTH + 2 * SWA_KV_WIDTH) // half_w
    cache_block = (n_seq, m_tok * MEM_HEADS, MEM_HEAD_DIM)
    ck = cache_k.reshape(n_b, m_tok * MEM_HEADS, MEM_HEAD_DIM)
    cv = cache_v.reshape(n_b, m_tok * MEM_HEADS, MEM_HEAD_DIM)
    return pl.pallas_call(
        functools.partial(_mem_step_kernel, n_seq=n_seq, t_len=t_len),
        grid=(n_b // n_seq,),
        in_specs=[
            pl.BlockSpec((rows, half_w), lambda i: (blk0 + i, qcol)),
            pl.BlockSpec((rows, half_w), lambda i: (blk0 + i, qcol + 1)),
            pl.BlockSpec(cache_block, lambda i: (i, 0, 0)),
            pl.BlockSpec(cache_block, lambda i: (i, 0, 0)),
        ],
        out_specs=pl.BlockSpec((rows, MEM_WIDTH), lambda i: (i, 0)),
        out_shape=jax.ShapeDtypeStruct((n_b * t_len, MEM_WIDTH), BF16),
        compiler_params=_params("arbitrary"),
        name="mem_step",
    )(p_att, p_att, ck, cv)


def _merge_kernel(xn_ref, rwp_ref, rws_ref, swp_ref, sws_ref, mmp_ref, mms_ref,
                  wg0_ref, wg1_ref, wg2_ref, wb0_ref, wb1_ref, wb2_ref, o_ref,
                  g0_scr, g1_scr, g2_scr, b0_scr, b1_scr, b2_scr, *, n_p):
    i = pl.program_id(1)

    @pl.when(i == 0)
    def _():
        for src, dst in ((wg0_ref, g0_scr), (wg1_ref, g1_scr), (wg2_ref, g2_scr),
                         (wb0_ref, b0_scr), (wb1_ref, b1_scr), (wb2_ref, b2_scr)):
            dst[...] = src[...].astype(BF16)

    def run(o_rw_ref, o_sw_ref, o_mm_ref):
        xn = xn_ref[...]
        acc = None
        for g_scr, b_scr, br_ref in ((g0_scr, b0_scr, o_rw_ref), (g1_scr, b1_scr, o_sw_ref),
                                     (g2_scr, b2_scr, o_mm_ref)):
            gate = _sigmoid(jnp.dot(xn, g_scr[...], preferred_element_type=F32))
            t = gate * jnp.dot(br_ref[...], b_scr[...], preferred_element_type=F32)
            acc = t if acc is None else acc + t
        o_ref[...] = acc.astype(o_ref.dtype)

    @pl.when(i < n_p)
    def _():
        run(rwp_ref, swp_ref, mmp_ref)

    @pl.when(i >= n_p)
    def _():
        run(rws_ref, sws_ref, mms_ref)


def _merge(xn, o_p, o_s, w_in, gate_col0, w_br, n_rows_p):
    m, d = xn.shape
    tn = MXU_COLS
    tm = _row_tile(n_rows_p, m - n_rows_p, ROW_TILE)
    n_p = n_rows_p // tm
    n_i = m // tm
    assert gate_col0 % tn == 0 and d % tn == 0
    g0 = gate_col0 // tn
    nj = d // tn
    p_spec = lambda w: pl.BlockSpec((tm, w), lambda j, i: (jnp.minimum(i, n_p - 1), 0))
    s_spec = lambda w: pl.BlockSpec((tm, w), lambda j, i: (jnp.maximum(i - n_p, 0), 0))
    in_specs = [pl.BlockSpec((tm, d), lambda j, i: (i, 0))]
    args = [xn]
    for op, os_ in zip(o_p, o_s):
        in_specs += [p_spec(op.shape[1]), s_spec(os_.shape[1])]
        args += [op, os_]
    for br in range(N_BRANCH):
        in_specs.append(pl.BlockSpec((d, tn), lambda j, i, br=br: (0, g0 + br * nj + j)))
        args.append(w_in)
    for w in w_br:
        in_specs.append(pl.BlockSpec((w.shape[0], tn), lambda j, i: (0, j)))
        args.append(w)
    scratch = [pltpu.VMEM((d, tn), BF16) for _ in range(N_BRANCH)]
    scratch += [pltpu.VMEM((w.shape[0], tn), BF16) for w in w_br]
    return pl.pallas_call(
        functools.partial(_merge_kernel, n_p=n_p),
        grid=(nj, n_i),
        in_specs=in_specs,
        out_specs=pl.BlockSpec((tm, tn), lambda j, i: (i, j)),
        out_shape=jax.ShapeDtypeStruct((m, d), BF16),
        scratch_shapes=scratch,
        compiler_params=_params("arbitrary", "arbitrary"),
        name="gated_merge",
    )(*args)


def _cast_kernel(x_ref, o_ref):
    o_ref[...] = x_ref[...].astype(o_ref.dtype)


def _to_bf16(w, name):
    r, c = w.shape
    tr = math.gcd(r, 512)
    return pl.pallas_call(
        _cast_kernel,
        grid=(r // tr,),
        in_specs=[pl.BlockSpec((tr, c), lambda i: (i, 0))],
        out_specs=pl.BlockSpec((tr, c), lambda i: (i, 0)),
        out_shape=jax.ShapeDtypeStruct((r, c), BF16),
        compiler_params=_params("arbitrary"),
        name=name,
    )(w)


def _wo_kernel(m_ref, w_ref, xp_ref, xs_ref, g1_ref, g2_ref, h_ref, hn_ref, *, n_p):
    i = pl.program_id(0)
    tm = m_ref.shape[0]
    halves = [slice(0, tm // 2), slice(tm // 2, tm)] if tm % 32 == 0 else [slice(0, tm)]

    def run(x_ref):
        for rs in halves:
            f = jnp.dot(m_ref[rs, :], w_ref[...], preferred_element_type=F32)
            h = x_ref[rs, :] + _rms(f, g1_ref[...])
            h_ref[rs, :] = h
            hn_ref[rs, :] = _rms(h, g2_ref[...]).astype(hn_ref.dtype)

    @pl.when(i < n_p)
    def _():
        run(xp_ref)

    @pl.when(i >= n_p)
    def _():
        run(xs_ref)


def _wo_block(merged, w_o, xp, xs, g_post, g_pre_ffn):
    m, d = merged.shape
    n_rows_p = xp.shape[0]
    tm = _row_tile(n_rows_p, m - n_rows_p, NORM_ROW_TILE)
    n_p = n_rows_p // tm
    return pl.pallas_call(
        functools.partial(_wo_kernel, n_p=n_p),
        grid=(m // tm,),
        in_specs=[
            pl.BlockSpec((tm, d), lambda i: (i, 0)),
            pl.BlockSpec((d, d), lambda i: (0, 0), pipeline_mode=pl.Buffered(1)),
            pl.BlockSpec((tm, d), lambda i: (jnp.minimum(i, n_p - 1), 0)),
            pl.BlockSpec((tm, d), lambda i: (jnp.maximum(i - n_p, 0), 0)),
            pl.BlockSpec((1, d), lambda i: (0, 0)),
            pl.BlockSpec((1, d), lambda i: (0, 0)),
        ],
        out_specs=[pl.BlockSpec((tm, d), lambda i: (i, 0)), pl.BlockSpec((tm, d), lambda i: (i, 0))],
        out_shape=[jax.ShapeDtypeStruct((m, d), F32), jax.ShapeDtypeStruct((m, d), BF16)],
        compiler_params=_params("arbitrary"),
        name="wo_residual",
    )(merged, _to_bf16(w_o, "cast_w_o"), xp, xs, g_post.reshape(1, d), g_pre_ffn.reshape(1, d))


def _gelu_tanh(x):
    return 0.5 * x * (1.0 + jnp.tanh(math.sqrt(2.0 / math.pi) * (x + 0.044715 * (x * x * x))))


def _ffn_up_kernel(hn_ref, wg_ref, wv_ref, cw_ref, cb_ref, st_ref, act_ref, cp_ref, cs_ref,
                   wgb_scr, wvb_scr, carry_scr, *, n_p, blocks_per_seq, t_s):
    i = pl.program_id(1)

    @pl.when(i == 0)
    def _():
        wgb_scr[...] = wg_ref[...].astype(BF16)
        wvb_scr[...] = wv_ref[...].astype(BF16)

    tm = hn_ref.shape[0]
    tn = wgb_scr.shape[1]
    subs = [slice(c0, c0 + MXU_COLS) for c0 in range(0, tn, MXU_COLS)] if tn % MXU_COLS == 0 else [slice(0, tn)]
    cut = tm // 2
    pieces = [(0, cut), (cut, tm)] if tm % (2 * 8 * t_s) == 0 else [(0, tm)]
    items = [(pc, cs) for cs in subs for pc in pieces]

    def gate_val(rs, cs):
        hn = hn_ref[rs, :]
        zg = jnp.dot(hn, wgb_scr[:, cs], preferred_element_type=F32)
        zv = jnp.dot(hn, wvb_scr[:, cs], preferred_element_type=F32)
        return zg, zv

    def conv_glu(cs, zg, zm1, zm2, zv):
        conv = cb_ref[:, cs] + cw_ref[0:1, cs] * zm2 + cw_ref[1:2, cs] * zm1 + cw_ref[2:3, cs] * zg
        return _gelu_tanh(conv) * zv

    @pl.when(i < n_p)
    def _():
        @pl.when(i % blocks_per_seq == 0)
        def _():
            carry_scr[...] = jnp.zeros(carry_scr.shape, F32)

        z = {items[0]: gate_val(slice(*items[0][0]), items[0][1])}
        for n, (pc, cs) in enumerate(items):
            if n + 1 < len(items):
                z[items[n + 1]] = gate_val(slice(*items[n + 1][0]), items[n + 1][1])
            zg, zv = z.pop((pc, cs))
            if pc[0] == 0:
                p1 = carry_scr[1:2, cs]
                p2 = carry_scr[0:1, cs]
            tr = pc[1] - pc[0]
            row = lax.broadcasted_iota(jnp.int32, (tr, 1), 0)
            zm1 = jnp.where(row == 0, p1, pltpu.roll(zg, 1, axis=0))
            zm2 = jnp.where(row == 0, p2, jnp.where(row == 1, p1, pltpu.roll(zg, 2, axis=0)))
            act_ref[slice(*pc), cs] = conv_glu(cs, zg, zm1, zm2, zv).astype(act_ref.dtype)
            p1 = zg[tr - 1:tr, :]
            p2 = zg[tr - 2:tr - 1, :]
            if pc[1] == tm:
                carry_scr[0:1, cs] = p2
                carry_scr[1:2, cs] = p1
                cp_ref[0, 0:1, cs] = p2
                cp_ref[0, 1:2, cs] = p1

    @pl.when(i >= n_p)
    def _():
        t = lax.broadcasted_iota(jnp.int32, (1, t_s, 1), 1)
        z = {items[0]: gate_val(slice(*items[0][0]), items[0][1])}
        for n, (pc, cs) in enumerate(items):
            if n + 1 < len(items):
                z[items[n + 1]] = gate_val(slice(*items[n + 1][0]), items[n + 1][1])
            zg, zv = z.pop((pc, cs))
            rs = slice(*pc)
            tr = pc[1] - pc[0]
            nq = tr // t_s
            qs = slice(pc[0] // t_s, pc[1] // t_s)
            tc = zg.shape[1]
            z3 = zg.reshape(nq, t_s, tc)
            s1 = st_ref[qs, 1:2, cs]
            s0 = st_ref[qs, 0:1, cs]
            zm1 = jnp.where(t == 0, s1, pltpu.roll(z3, 1, axis=1))
            zm2 = jnp.where(t == 0, s0, jnp.where(t == 1, s1, pltpu.roll(z3, 2, axis=1)))
            act = conv_glu(cs, z3, zm1, zm2, zv.reshape(nq, t_s, tc))
            act_ref[rs, cs] = act.reshape(tr, tc).astype(act_ref.dtype)
            cs_ref[qs, :, cs] = z3[:, t_s - 2:t_s, :]


def _ffn_up(hn, w_up, conv_w, conv_b, conv_state, n_b_p, t_p, n_b_s, t_s):
    m, d = hn.shape
    d_ff = conv_b.shape[0]
    n_rows_p = n_b_p * t_p
    n_rows_s = n_b_s * t_s
    tm = _row_tile(math.gcd(n_rows_p, t_p), n_rows_s, ROW_TILE)
    assert tm == n_rows_s and t_s % 8 == 0 and t_s >= 2, "sample rows must form one row block"
    n_p = n_rows_p // tm
    blocks_per_seq = t_p // tm
    tn = math.gcd(d_ff, 512)
    nj = d_ff // tn
    last_seq = n_b_p - 1
    return pl.pallas_call(
        functools.partial(_ffn_up_kernel, n_p=n_p, blocks_per_seq=blocks_per_seq, t_s=t_s),
        grid=(nj, n_p + 1),
        in_specs=[
            pl.BlockSpec((tm, d), lambda j, i: (i, 0)),
            pl.BlockSpec((d, tn), lambda j, i: (0, j)),
            pl.BlockSpec((d, tn), lambda j, i: (0, nj + j)),
            pl.BlockSpec((CONV_W, tn), lambda j, i: (0, j)),
            pl.BlockSpec((1, tn), lambda j, i: (0, j)),
            pl.BlockSpec((n_b_s, CONV_W - 1, tn), lambda j, i: (0, 0, j)),
        ],
        out_specs=[
            pl.BlockSpec((tm, tn), lambda j, i: (i, j)),
            pl.BlockSpec((1, CONV_W - 1, tn), lambda j, i: (jnp.minimum(i // blocks_per_seq, last_seq), 0, j)),
            pl.BlockSpec((n_b_s, CONV_W - 1, tn), lambda j, i: (0, 0, j)),
        ],
        out_shape=[
            jax.ShapeDtypeStruct((m, d_ff), BF16),
            jax.ShapeDtypeStruct((n_b_p, CONV_W - 1, d_ff), F32),
            jax.ShapeDtypeStruct((n_b_s, CONV_W - 1, d_ff), F32),
        ],
        scratch_shapes=[pltpu.VMEM((d, tn), BF16), pltpu.VMEM((d, tn), BF16), pltpu.VMEM((8, tn), F32)],
        compiler_params=_params("arbitrary", "arbitrary"),
        name="ffn_up_convglu",
    )(hn, w_up, w_up, conv_w, conv_b.reshape(1, d_ff), conv_state)


def _ffn_down_kernel(a_ref, w_ref, h_ref, g_ref, y_ref):
    kk = pl.program_id(1)

    @pl.when(kk == 0)
    def _():
        y_ref[...] = jnp.zeros(y_ref.shape, F32)

    last = kk == pl.num_programs(1) - 1

    @pl.when(jnp.logical_not(last))
    def _():
        y_ref[...] += jnp.dot(a_ref[...], w_ref[...].astype(BF16), preferred_element_type=F32)

    @pl.when(last)
    def _():
        tm = y_ref.shape[0]
        n_pieces = 4 if tm % 64 == 0 else 1
        tr = tm // n_pieces
        wb = w_ref[...].astype(BF16)
        for r in range(n_pieces):
            rs = slice(r * tr, (r + 1) * tr)
            f = y_ref[rs, :] + jnp.dot(a_ref[rs, :], wb, preferred_element_type=F32)
            y_ref[rs, :] = h_ref[rs, :] + _rms(f, g_ref[...])


def _ffn_down(act, w_down, h, g, row0, n_rows):
    d_ff, d = w_down.shape
    tm = math.gcd(math.gcd(n_rows, row0) if row0 else n_rows, ROW_TILE)
    tk = math.gcd(d_ff, 512)
    i0 = row0 // tm
    return pl.pallas_call(
        _ffn_down_kernel,
        grid=(n_rows // tm, d_ff // tk),
        in_specs=[
            pl.BlockSpec((tm, tk), lambda i, k: (i0 + i, k)),
            pl.BlockSpec((tk, d), lambda i, k: (k, 0)),
            pl.BlockSpec((tm, d), lambda i, k: (i0 + i, 0)),
            pl.BlockSpec((1, d), lambda i, k: (0, 0)),
        ],
        out_specs=pl.BlockSpec((tm, d), lambda i, k: (i, 0)),
        out_shape=jax.ShapeDtypeStruct((n_rows, d), F32),
        compiler_params=_params("arbitrary", "arbitrary"),
        name="ffn_down",
    )(act, w_down, h, g.reshape(1, d))


def kernel(x_prompt, x_sample, cache_swa_k, cache_swa_v, cache_mem_k, cache_mem_v, state_rwkv_shift, state_rwkv_wkv, state_ffn_conv, mem_prompt, g_pre_mix, w_in, mu_rwkv, w_decay_up, w0_decay, w_a_up, a0, w_gate_up, k_k, k_a, r_k, ln_x_w, ln_x_b, swa_sinks, g_mem, w_mem_kv, w_br_rwkv, w_br_swa, w_br_mem, w_o, g_post_mix, g_pre_ffn, w_ffn_up, conv_w, conv_b, w_ffn_down, g_post_ffn):
    b_p, t_p, d = x_prompt.shape
    b_s, t_s, _ = x_sample.shape
    n_p, n_s = b_p * t_p, b_s * t_s
    m_tok = mem_prompt.shape[1]
    rw_prm = dict(mu_rwkv=mu_rwkv, w_decay_up=w_decay_up, w0_decay=w0_decay, w_a_up=w_a_up, a0=a0,
                  w_gate_up=w_gate_up, k_k=k_k, k_a=k_a, r_k=r_k, ln_x_w=ln_x_w, ln_x_b=ln_x_b)
    xp = x_prompt.reshape(n_p, d)
    xs = x_sample.reshape(n_s, d)

    xn = _norm_rows2(xp, xs, g_pre_mix)
    p_rw = _matmul(xn, w_in, 0, RW_PROJ, RW_PROJ // 2, ROW_TILE, "proj_rwkv")
    p_att = _matmul(xn, w_in, RW_PROJ, ATT_WIDTH, ATT_WIDTH, NORM_ROW_TILE, "proj_attn")

    mem_kv = _matmul(_norm_rows1(mem_prompt.reshape(b_p * m_tok, d), g_mem), w_mem_kv,
                     0, 2 * MEM_WIDTH, MEM_WIDTH, ROW_TILE, "proj_mem_kv")

    o_rw_p, shift_p, wkv_p = _rwkv(p_rw, 0, b_p, t_p, None, None, rw_prm, RWKV_PARTS)
    o_rw_s, shift_s, wkv_s = _rwkv(p_rw, n_p, b_s, t_s, state_rwkv_shift, state_rwkv_wkv, rw_prm, RWKV_PARTS)

    o_sw_p = _swa_prompt(p_att, b_p, t_p, swa_sinks)
    o_sw_s, swa_k_s, swa_v_s = _swa_step(p_att, n_p, b_s, t_s, cache_swa_k, cache_swa_v, swa_sinks)

    o_mm_p = _mem_prompt(p_att, b_p, t_p, mem_kv)
    o_mm_s = _mem_step(p_att, n_p, b_s, t_s, cache_mem_k, cache_mem_v)

    merged = _merge(xn, (o_rw_p, o_sw_p, o_mm_p), (o_rw_s, o_sw_s, o_mm_s), w_in,
                    RW_PROJ + ATT_WIDTH, (w_br_rwkv, w_br_swa, w_br_mem), n_p)
    h, hn = _wo_block(merged, w_o, xp, xs, g_post_mix, g_pre_ffn)
    act, conv_p, conv_s = _ffn_up(hn, w_ffn_up, conv_w, conv_b, state_ffn_conv, b_p, t_p, b_s, t_s)
    y_p = _ffn_down(act, w_ffn_down, h, g_post_ffn, 0, n_p)
    y_s = _ffn_down(act, w_ffn_down, h, g_post_ffn, n_p, n_s)

    keep = min(WINDOW, t_p)
    kv_p = jnp.stack([lax.slice(p_att, ((b + 1) * t_p - keep, SWA_Q_WIDTH),
                                ((b + 1) * t_p, SWA_Q_WIDTH + 2 * SWA_KV_WIDTH)) for b in range(b_p)])
    swa_k_p = kv_p[..., :SWA_KV_WIDTH].reshape(b_p, keep, SWA_KV_HEADS, SWA_HEAD_DIM)
    swa_v_p = kv_p[..., SWA_KV_WIDTH:].reshape(b_p, keep, SWA_KV_HEADS, SWA_HEAD_DIM)
    mem_k_p = mem_kv[:, :MEM_WIDTH].reshape(b_p, m_tok, MEM_HEADS, MEM_HEAD_DIM)
    mem_v_p = mem_kv[:, MEM_WIDTH:].reshape(b_p, m_tok, MEM_HEADS, MEM_HEAD_DIM)
    return (y_p.reshape(b_p, t_p, d), y_s.reshape(b_s, t_s, d), swa_k_p, swa_v_p, mem_k_p, mem_v_p,
            shift_p, wkv_p, conv_p, swa_k_s, swa_v_s, shift_s, wkv_s, conv_s)
```

```python
import functools
import math

import jax
import jax.numpy as jnp
from jax import lax
from jax.experimental import pallas as pl
from jax.experimental.pallas import tpu as pltpu

F32 = jnp.float32
BF16 = jnp.bfloat16

RW_HEADS = 12
RW_HEAD_DIM = 64
RW_WIDTH = RW_HEADS * RW_HEAD_DIM
RW_DECAY_RANK = 64
RW_A_RANK = 64
RW_GATE_RANK = 128
RW_PROJ = 3 * RW_WIDTH + RW_DECAY_RANK + RW_A_RANK + RW_GATE_RANK
RW_GN_EPS = 6.4e-4
RW_PAIRS = RW_HEADS // 2
SWA_Q_HEADS = 12
SWA_KV_HEADS = 4
SWA_GROUP = SWA_Q_HEADS // SWA_KV_HEADS
SWA_HEAD_DIM = 64
SWA_Q_WIDTH = SWA_Q_HEADS * SWA_HEAD_DIM
SWA_KV_WIDTH = SWA_KV_HEADS * SWA_HEAD_DIM
WINDOW = 128
MEM_HEADS = 4
MEM_HEAD_DIM = 128
MEM_WIDTH = MEM_HEADS * MEM_HEAD_DIM
N_BRANCH = 3
CONV_W = 3
NORM_EPS = 1e-6
ATT_WIDTH = SWA_Q_WIDTH + 2 * SWA_KV_WIDTH + MEM_WIDTH
SWA_SCALE = SWA_HEAD_DIM ** -0.5
assert math.frexp(SWA_SCALE)[0] == 0.5

LANES = 128
MXU_COLS = 256
VMEM_LIMIT_BYTES = 56 * 1024 * 1024

RWKV_ROWS = 64
RWKV_GROUPS = 2
RWKV_PARTS = dict(z=1, lv=1, pow=1, app=1, om=1, hg=1, out=1, st=1)
RWKV_SUM_PARTS = 1
ROW_TILE = 1024
NORM_ROW_TILE = 512


def _params(*sem):
    return pltpu.CompilerParams(dimension_semantics=sem, vmem_limit_bytes=VMEM_LIMIT_BYTES)


def _row_tile(n_p, n_s, target):
    t = math.gcd(math.gcd(n_p, n_s), target)
    assert t % 16 == 0, (n_p, n_s, target)
    return t


def _div_pow2(x, n):
    assert n & (n - 1) == 0
    return lax.shift_right_logical(x, n.bit_length() - 1)


def _mod_pow2(x, n):
    assert n & (n - 1) == 0
    return jnp.bitwise_and(x, n - 1)


def _rms(x, g):
    ms = jnp.mean(x * x, axis=-1, keepdims=True)
    return x * lax.rsqrt(ms + NORM_EPS) * g


def _norm_mm_kernel(*refs, n_p, two):
    if two:
        xp_ref, xs_ref, g_ref, w_ref, xn_ref, o_ref, wb_scr = refs
    else:
        xp_ref, g_ref, w_ref, xn_ref, o_ref, wb_scr = refs
    i = pl.program_id(0)

    @pl.when(i == 0)
    def _():
        wb_scr[...] = w_ref[...].astype(BF16)

    def run(x_ref):
        xn = _rms(x_ref[...], g_ref[...]).astype(BF16)
        xn_ref[...] = xn
        o_ref[...] = jnp.dot(xn, wb_scr[...], preferred_element_type=F32)

    if two:
        @pl.when(i < n_p)
        def _():
            run(xp_ref)

        @pl.when(i >= n_p)
        def _():
            run(xs_ref)
    else:
        run(xp_ref)


def _norm_matmul(xp, xs, g, w, col0, n_cols, name):
    n_rows_p, d = xp.shape
    two = xs is not None
    n_rows_s = xs.shape[0] if two else 0
    tm = _row_tile(n_rows_p, n_rows_s, NORM_ROW_TILE) if two else math.gcd(n_rows_p, NORM_ROW_TILE)
    n_p, n_s = n_rows_p // tm, n_rows_s // tm
    assert col0 % LANES == 0 and n_cols % LANES == 0
    once = pl.Buffered(1)
    in_specs = [pl.BlockSpec((tm, d), lambda i: (jnp.minimum(i, n_p - 1), 0))]
    args = [xp]
    if two:
        in_specs.append(pl.BlockSpec((tm, d), lambda i: (jnp.maximum(i - n_p, 0), 0)))
        args.append(xs)
    in_specs += [pl.BlockSpec((1, d), lambda i: (0, 0)),
                 pl.BlockSpec((pl.Element(d), pl.Element(n_cols)), lambda i: (0, col0), pipeline_mode=once)]
    args += [g.reshape(1, d), w]
    m = n_rows_p + n_rows_s
    return pl.pallas_call(
        functools.partial(_norm_mm_kernel, n_p=n_p, two=two),
        grid=(n_p + n_s,),
        in_specs=in_specs,
        out_specs=[pl.BlockSpec((tm, d), lambda i: (i, 0)), pl.BlockSpec((tm, n_cols), lambda i: (i, 0))],
        out_shape=[jax.ShapeDtypeStruct((m, d), BF16), jax.ShapeDtypeStruct((m, n_cols), F32)],
        scratch_shapes=[pltpu.VMEM((d, n_cols), BF16)],
        compiler_params=_params("arbitrary"),
        name=name,
    )(*args)


def _mm_kernel(a_ref, w_ref, o_ref, wb_ref):
    @pl.when(pl.program_id(1) == 0)
    def _():
        wb_ref[...] = w_ref[...].astype(BF16)

    o_ref[...] = jnp.dot(a_ref[...], wb_ref[...], preferred_element_type=F32)


def _matmul(a, w, col0, n_cols, tn, tm, name):
    m, k = a.shape
    tm = math.gcd(m, tm)
    assert n_cols % tn == 0 and col0 % LANES == 0 and tn % LANES == 0
    return pl.pallas_call(
        _mm_kernel,
        grid=(n_cols // tn, m // tm),
        in_specs=[
            pl.BlockSpec((tm, k), lambda j, i: (i, 0)),
            pl.BlockSpec((pl.Element(k), pl.Element(tn)),
                         lambda j, i: (0, pl.multiple_of(col0 + j * tn, LANES))),
        ],
        out_specs=pl.BlockSpec((tm, tn), lambda j, i: (i, j)),
        out_shape=jax.ShapeDtypeStruct((m, n_cols), F32),
        scratch_shapes=[pltpu.VMEM((k, tn), BF16)],
        compiler_params=_params("arbitrary", "arbitrary"),
        name=name,
    )(a, w)


_NN = (((1,), (0,)), ((), ()))
_NT = (((1,), (1,)), ((), ()))
_TN = (((0,), (0,)), ((), ()))


def _bf_parts(x, n):
    parts = []
    r = x
    for i in range(n):
        h = r.astype(BF16)
        parts.append(h)
        if i + 1 < n:
            r = r - h.astype(F32)
    return parts


def _dotp(a, b, dn, pa, pb):
    pa_parts = _bf_parts(a, pa)
    pb_parts = _bf_parts(b, pb)
    acc = None
    for i in range(pa):
        for j in range(pb):
            if i + j < max(pa, pb):
                t = lax.dot_general(pa_parts[i], pb_parts[j], dn, preferred_element_type=F32)
                acc = t if acc is None else acc + t
    return acc


def _softplus(x):
    return jnp.maximum(x, 0.0) + jnp.log(1.0 + jnp.exp(-jnp.abs(x)))


def _sigmoid(x):
    return 1.0 / (1.0 + jnp.exp(-x))


def _rwkv_kernel(*refs, seq_rows, n_seq, n_par, has_state, pp):
    n_in = n_par + (2 if has_state else 0)
    p_refs = refs[0:n_par]
    if has_state:
        sh0_ref, s0_ref = refs[n_par:n_in]
    (mu_ref, wda_ref, wa0_ref, wg_ref, vec_ref,
     o_ref, sh_out_ref, s_out_ref, prev_scr, st_scr) = refs[n_in:]
    c = pl.program_id(1)
    n_c = pl.num_programs(1)
    C = seq_rows
    R = seq_rows * n_seq
    W = RW_WIDTH
    HD = RW_HEAD_DIM
    groups = range(n_par)

    @pl.when(c == 0)
    def _():
        st_scr[...] = jnp.zeros(st_scr.shape, F32)
        if has_state:
            prev_scr[...] = sh0_ref[...]
            for s in range(n_par * n_seq):
                for q in range(RW_PAIRS):
                    st_scr[s, q, 0:HD, 0:HD] = s0_ref[s, 2 * q]
                    st_scr[s, q, HD:2 * HD, HD:2 * HD] = s0_ref[s, 2 * q + 1]
        else:
            prev_scr[...] = jnp.zeros(prev_scr.shape, F32)

    k_k = vec_ref[0:1, :]
    k_a = vec_ref[1:2, :]
    r_k = vec_ref[2:3, :]
    ln_w = vec_ref[3:4, :]
    ln_b = vec_ref[4:5, :]
    lane = lax.broadcasted_iota(jnp.int32, (1, LANES), 1)
    row = lax.broadcasted_iota(jnp.int32, (R, 1), 0)

    li = lax.broadcasted_iota(jnp.int32, (LANES, LANES), 0)
    lj = lax.broadcasted_iota(jnp.int32, (LANES, LANES), 1)
    same_head = _div_pow2(li, RW_HEAD_DIM) == _div_pow2(lj, RW_HEAD_DIM)
    ones_bd = jnp.where(same_head, 1.0, 0.0).astype(BF16)

    def head_sum(z):
        cols = []
        for q in range(RW_PAIRS):
            zq = z[:, q * LANES:(q + 1) * LANES]
            acc = None
            for part in _bf_parts(zq, RWKV_SUM_PARTS):
                t = jnp.dot(part, ones_bd, preferred_element_type=F32)
                acc = t if acc is None else acc + t
            cols.append(acc)
        return jnp.concatenate(cols, axis=1)

    ri = lax.broadcasted_iota(jnp.int32, (R, R), 0)
    rj = lax.broadcasted_iota(jnp.int32, (R, R), 1)
    same_seq = _div_pow2(ri, C) == _div_pow2(rj, C)
    causal = jnp.where(same_seq & (rj <= ri), 1.0, 0.0).astype(BF16)

    def prepare(u):
        p = p_refs[u][...]
        prev = pltpu.roll(p, 1, axis=0)
        for s in range(n_seq):
            prev = jnp.where(row == s * C, prev_scr[u * n_seq + s], prev)
        for s in range(n_seq):
            prev_scr[u * n_seq + s] = p[(s + 1) * C - 1:(s + 1) * C, :]
            sh_out_ref[u * n_seq + s] = p[(s + 1) * C - 1:(s + 1) * C, :]
        x = p + mu_ref[...] * (prev - p)
        r = x[:, 0:W]
        k = x[:, W:2 * W]
        v = x[:, 2 * W:3 * W]
        wa_in = x[:, 3 * W:3 * W + LANES]
        g_in = x[:, 3 * W + LANES:3 * W + 2 * LANES]
        wa_act = jnp.where(lane < RW_DECAY_RANK, jnp.tanh(wa_in), wa_in)
        za = jnp.dot(wa_act.astype(BF16), wda_ref[...], preferred_element_type=F32) + wa0_ref[...]
        w_log = -_softplus(-za[:, 0:W]) - 0.5
        lw = -jnp.exp(w_log)
        a = _sigmoid(za[:, W:2 * W])
        g = jnp.dot(_sigmoid(g_in).astype(BF16), wg_ref[...], preferred_element_type=F32)
        kk = k * k_k
        kk = kk / jnp.maximum(jnp.sqrt(head_sum(kk * kk)), 1e-12)
        k = k * (1.0 + (a - 1.0) * k_a)
        b = kk * a
        cl = None
        for part in _bf_parts(lw, 2):
            t1 = jnp.dot(causal, part, preferred_element_type=F32)
            cl = t1 if cl is None else cl + t1
        if n_seq == 1:
            ct = cl[R - 1:R, :]
        else:
            last = cl.reshape(n_seq, C, W)[:, C - 1:C, :]
            ct = jnp.broadcast_to(last, (n_seq, C, W)).reshape(R, W)
        e_neg = jnp.exp(-cl)
        e_end = jnp.exp(ct - cl)
        return dict(r=r, k=k, v=v, g=g, at=-kk * jnp.exp(cl - lw), rt=r * jnp.exp(cl), bt=b * e_neg,
                    kt=k * e_neg, bts=b * e_end, kts=k * e_end,
                    p_end=jnp.exp(ct))

    prep = [prepare(u) for u in groups]

    strict = jnp.where(same_seq & (rj < ri), 1.0, 0.0)
    incl = jnp.where(same_seq & (rj <= ri), 1.0, 0.0)
    strict2 = jnp.concatenate([strict, strict], axis=1)
    incl2 = jnp.concatenate([incl, incl], axis=1)
    m0 = jnp.where(lane < RW_HEAD_DIM, 1.0, 0.0)
    m1 = jnp.where(lane >= RW_HEAD_DIM, 1.0, 0.0)
    diag = jnp.where(li == lj, 1.0, 0.0)
    bd_mask = jnp.where(same_head, 1.0, 0.0)
    zeros_rl = jnp.zeros((R, LANES), F32)
    zeros_r2 = jnp.zeros((R, 2 * R), F32)
    n_dbl = max(1, (C - 1).bit_length())
    sls = [slice(q * LANES, (q + 1) * LANES) for q in range(RW_PAIRS)]
    items = [(u, q) for u in groups for q in range(RW_PAIRS)]

    def col(name, it):
        return prep[it[0]][name][:, sls[it[1]]]

    v0 = {it: jnp.concatenate([col("v", it), zeros_rl], axis=1) for it in items}
    zs = {}
    for it in items:
        xa, xr, bq, kq = col("at", it), col("rt", it), col("bt", it), col("kt", it)
        x4 = jnp.concatenate([xa * m0, xr * m0, xa * m1, xr * m1], axis=0)
        y4 = jnp.concatenate([bq, kq, kq, bq], axis=0)
        zs[it] = _dotp(x4, y4, _NT, pp["z"], pp["z"])
    zt0 = {it: zs[it][0:R, 0:2 * R] * strict2 for it in items}
    zb0 = {it: zs[it][R:2 * R, 0:2 * R] * incl2 for it in items}
    zt1 = {it: zs[it][2 * R:3 * R, 2 * R:4 * R] * strict2 for it in items}
    zb1 = {it: zs[it][3 * R:4 * R, 2 * R:4 * R] * incl2 for it in items}
    half0 = jnp.where(lax.broadcasted_iota(jnp.int32, (1, 2 * R), 1) < R, 1.0, 0.0)
    half1 = 1.0 - half0
    pw = {it: jnp.concatenate([zt0[it] * half0, zt1[it] * half1], axis=0) for it in items}
    wmat = {}
    for it in items:
        lhs = jnp.concatenate([jnp.concatenate([zt0[it], zeros_r2], axis=1),
                               jnp.concatenate([zeros_r2, zt1[it]], axis=1)], axis=0)
        vq = col("v", it)
        rhs = jnp.concatenate([zeros_rl, vq, vq, zeros_rl], axis=0)
        lv = _dotp(lhs, rhs, _NN, pp["lv"], pp["lv"])
        xa = col("at", it)
        wmat[it] = jnp.concatenate([lv, jnp.concatenate([xa * m0, xa * m1], axis=0)], axis=1)
    for step in range(n_dbl):
        wmat = {it: wmat[it] + _dotp(pw[it], wmat[it], _NN, pp["app"], pp["app"]) for it in items}
        if step + 1 < n_dbl:
            pw = {it: _dotp(pw[it], pw[it], _NN, pp["pow"], pp["pow"]) for it in items}
    rhat, o0 = {}, {}
    for it in items:
        lhs = jnp.concatenate([jnp.concatenate([zb0[it], zeros_r2], axis=1),
                               jnp.concatenate([zeros_r2, zb1[it]], axis=1)], axis=0)
        rhs = jnp.concatenate([wmat[it][0:R], v0[it], v0[it], wmat[it][R:2 * R]], axis=0)
        om = _dotp(lhs, rhs, _NN, pp["om"], pp["om"])
        rhat[it] = col("rt", it) + om[0:R, LANES:] + om[R:2 * R, LANES:]
        o0[it] = om[0:R, 0:LANES] * m0 + om[R:2 * R, 0:LANES] * m1
    hgt = {}
    for it in items:
        bq, kq = col("bts", it), col("kts", it)
        for s in range(n_seq):
            rs = slice(s * C, (s + 1) * C)
            ys4 = jnp.concatenate([bq[rs] * m0, kq[rs] * m0, kq[rs] * m1, bq[rs] * m1], axis=0)
            r4 = jnp.concatenate([wmat[it][rs], v0[it][rs], v0[it][rs],
                                  wmat[it][R + s * C:R + (s + 1) * C]], axis=0)
            hgt[it, s] = _dotp(r4, ys4, _TN, pp["hg"], pp["hg"])
    o_pair = {}
    for it in items:
        u, q = it
        o_seq = []
        for s in range(n_seq):
            rs = slice(s * C, (s + 1) * C)
            s_old = st_scr[u * n_seq + s, q]
            ht = hgt[it, s][0:LANES] * bd_mask
            gt = hgt[it, s][LANES:] + diag * prep[u]["p_end"][s * C:s * C + 1, sls[q]]
            o_seq.append(_dotp(rhat[it][rs], s_old, _NT, pp["out"], pp["out"]) + o0[it][rs])
            st_scr[u * n_seq + s, q] = _dotp(s_old, gt, _NN, pp["st"], pp["st"]) + ht
        o_pair[it] = o_seq[0] if n_seq == 1 else jnp.concatenate(o_seq, axis=0)

    inv_n = 1.0 / RW_HEAD_DIM
    for u in groups:
        o = jnp.concatenate([o_pair[u, q] for q in range(RW_PAIRS)], axis=1)
        mean = head_sum(o) * inv_n
        d = o - mean
        var = head_sum(d * d) * inv_n
        o = d * lax.rsqrt(var + RW_GN_EPS) * ln_w + ln_b
        bonus = head_sum(prep[u]["r"] * prep[u]["k"] * r_k) * prep[u]["v"]
        o_ref[u] = ((o + bonus) * prep[u]["g"]).astype(o_ref.dtype)

    @pl.when(c == n_c - 1)
    def _():
        for s in range(n_par * n_seq):
            for q in range(RW_PAIRS):
                s_out_ref[s, 2 * q] = st_scr[s, q, 0:HD, 0:HD]
                s_out_ref[s, 2 * q + 1] = st_scr[s, q, HD:2 * HD, HD:2 * HD]


def _rwkv(p_rw, row0, n_b, t_len, shift0, s0, prm, pp):
    if t_len >= RWKV_ROWS:
        seq_rows, n_seq = RWKV_ROWS, 1
    else:
        seq_rows, n_seq = t_len, RWKV_ROWS // t_len
    assert t_len % seq_rows == 0 and n_b % n_seq == 0 and row0 % (seq_rows * n_seq) == 0
    rows = seq_rows * n_seq
    n_c = t_len // seq_rows
    assert n_seq == 1 or n_c == 1
    blk0 = row0 // rows
    has_state = s0 is not None
    n_grp = n_b // n_seq
    n_par = math.gcd(n_grp, RWKV_GROUPS)
    seqs = n_par * n_seq

    wda = jnp.zeros((LANES, 2 * RW_WIDTH), F32)
    wda = wda.at[0:RW_DECAY_RANK, 0:RW_WIDTH].set(prm["w_decay_up"])
    wda = wda.at[RW_DECAY_RANK:, RW_WIDTH:].set(prm["w_a_up"])
    wa0 = jnp.concatenate([prm["w0_decay"], prm["a0"]]).reshape(1, 2 * RW_WIDTH)
    vecs = jnp.zeros((8, RW_WIDTH), F32)
    for i, name in enumerate(("k_k", "k_a", "r_k", "ln_x_w", "ln_x_b")):
        vecs = vecs.at[i].set(prm[name].reshape(RW_WIDTH))

    const = lambda *shape: pl.BlockSpec(shape, lambda bi, c: (0,) * len(shape))
    in_specs = [pl.BlockSpec((rows, RW_PROJ), lambda bi, c, u=u: (blk0 + (bi * n_par + u) * n_c + c, 0))
                for u in range(n_par)]
    args = [p_rw] * n_par
    st_block = (seqs, RW_HEADS, RW_HEAD_DIM, RW_HEAD_DIM)
    if has_state:
        in_specs += [pl.BlockSpec((seqs, 1, RW_PROJ), lambda bi, c: (bi, 0, 0)),
                     pl.BlockSpec(st_block, lambda bi, c: (bi, 0, 0, 0))]
        args += [shift0.reshape(n_b, 1, RW_PROJ), s0]
    in_specs += [const(1, RW_PROJ), const(LANES, 2 * RW_WIDTH), const(1, 2 * RW_WIDTH),
                 const(RW_GATE_RANK, RW_WIDTH), const(8, RW_WIDTH)]
    args += [prm["mu_rwkv"].reshape(1, RW_PROJ), wda.astype(BF16), wa0,
             prm["w_gate_up"].astype(BF16), vecs]

    o, sh, st = pl.pallas_call(
        functools.partial(_rwkv_kernel, seq_rows=seq_rows, n_seq=n_seq, n_par=n_par, has_state=has_state, pp=pp),
        grid=(n_grp // n_par, n_c),
        in_specs=in_specs,
        out_specs=[
            pl.BlockSpec((n_par, rows, RW_WIDTH), lambda bi, c: (bi, c, 0)),
            pl.BlockSpec((seqs, 1, RW_PROJ), lambda bi, c: (bi, 0, 0)),
            pl.BlockSpec(st_block, lambda bi, c: (bi, 0, 0, 0)),
        ],
        out_shape=[
            jax.ShapeDtypeStruct((n_grp, n_c * rows, RW_WIDTH), BF16),
            jax.ShapeDtypeStruct((n_b, 1, RW_PROJ), F32),
            jax.ShapeDtypeStruct((n_b, RW_HEADS, RW_HEAD_DIM, RW_HEAD_DIM), F32),
        ],
        scratch_shapes=[pltpu.VMEM((seqs, 1, RW_PROJ), F32),
                        pltpu.VMEM((seqs, RW_PAIRS, LANES, LANES), F32)],
        compiler_params=_params("arbitrary", "arbitrary"),
        name="rwkv_state" if has_state else "rwkv_fresh",
    )(*args)
    return o.reshape(n_b * t_len, RW_WIDTH), sh.reshape(n_b, RW_PROJ), st


def _sink_probs(s_parts, sink):
    m = sink
    for s in s_parts:
        m = jnp.maximum(m, jnp.max(s, axis=-1, keepdims=True))
    den = jnp.exp(sink - m)
    es = []
    for s in s_parts:
        e = jnp.exp(s - m)
        den = den + jnp.sum(e, axis=-1, keepdims=True)
        es.append(e)
    inv = 1.0 / den
    return [(e * inv).astype(BF16) for e in es]


def _group_queries(q, hk):
    return jnp.concatenate(
        [q[:, (hk * SWA_GROUP + g) * SWA_HEAD_DIM:(hk * SWA_GROUP + g + 1) * SWA_HEAD_DIM]
         for g in range(SWA_GROUP)], axis=0).astype(BF16)


def _group_sinks(sink_ref, hk, grp):
    sink = jnp.zeros(grp.shape, F32)
    for g in range(SWA_GROUP):
        sink = jnp.where(grp == g, sink_ref[hk * SWA_GROUP + g], sink)
    return sink


def _swa_prompt_kernel(sink_ref, q_ref, kp_ref, kc_ref, vp_ref, vc_ref, o_ref, *, n_win):
    n = pl.program_id(1)
    Wn = WINDOW
    kf = jnp.concatenate([kp_ref[...], kc_ref[...]], axis=0).astype(BF16)
    vf = jnp.concatenate([vp_ref[...], vc_ref[...]], axis=0).astype(BF16)
    gi = lax.broadcasted_iota(jnp.int32, (SWA_GROUP * Wn, 2 * Wn), 0)
    i = _mod_pow2(gi, Wn)
    j = lax.broadcasted_iota(jnp.int32, (SWA_GROUP * Wn, 2 * Wn), 1)
    ok = (j > i) & (j <= i + Wn)
    first_key = jnp.where(n > 0, 0, Wn)
    ok_first = ok & (j >= first_key)
    grp = _div_pow2(lax.broadcasted_iota(jnp.int32, (SWA_GROUP * Wn, 1), 0), Wn)
    ks = [slice(hk * SWA_HEAD_DIM, (hk + 1) * SWA_HEAD_DIM) for hk in range(SWA_KV_HEADS)]
    sinks = [_group_sinks(sink_ref, hk, grp) for hk in range(SWA_KV_HEADS)]
    work = [(w, hk) for w in range(n_win) for hk in range(SWA_KV_HEADS)]
    qw = [q_ref[w * Wn:(w + 1) * Wn, :] * SWA_SCALE for w in range(n_win)]
    sc = {(w, hk): lax.dot_general(_group_queries(qw[w], hk), kf[w * Wn:(w + 2) * Wn, ks[hk]], _NT,
                                   preferred_element_type=F32) for w, hk in work}
    pr = {(w, hk): _sink_probs([jnp.where(ok_first if w == 0 else ok, sc[w, hk], -jnp.inf)], sinks[hk])[0]
          for w, hk in work}
    ov = {(w, hk): jnp.dot(pr[w, hk], vf[w * Wn:(w + 2) * Wn, ks[hk]], preferred_element_type=F32)
          for w, hk in work}
    for w, hk in work:
        for g in range(SWA_GROUP):
            h = hk * SWA_GROUP + g
            o_ref[w * Wn:(w + 1) * Wn, h * SWA_HEAD_DIM:(h + 1) * SWA_HEAD_DIM] = (
                ov[w, hk][g * Wn:(g + 1) * Wn].astype(o_ref.dtype))


def _swa_prompt(p_att, n_b, t_len, sinks):
    nb = t_len // WINDOW
    n_win = 2 if nb % 2 == 0 else 1
    ns = nb // n_win
    rows = n_win * WINDOW
    kcol = SWA_Q_WIDTH // SWA_KV_WIDTH
    vcol = kcol + 1
    cur = lambda col: (lambda b, n: (b * ns + n, col))
    prv = lambda col: (lambda b, n: (b * nb + jnp.maximum(n * n_win - 1, 0), col))
    return pl.pallas_call(
        functools.partial(_swa_prompt_kernel, n_win=n_win),
        grid=(n_b, ns),
        in_specs=[
            pl.BlockSpec(memory_space=pltpu.SMEM),
            pl.BlockSpec((rows, SWA_Q_WIDTH), cur(0)),
            pl.BlockSpec((WINDOW, SWA_KV_WIDTH), prv(kcol)),
            pl.BlockSpec((rows, SWA_KV_WIDTH), cur(kcol)),
            pl.BlockSpec((WINDOW, SWA_KV_WIDTH), prv(vcol)),
            pl.BlockSpec((rows, SWA_KV_WIDTH), cur(vcol)),
        ],
        out_specs=pl.BlockSpec((rows, SWA_Q_WIDTH), lambda b, n: (b * ns + n, 0)),
        out_shape=jax.ShapeDtypeStruct((n_b * t_len, SWA_Q_WIDTH), BF16),
        compiler_params=_params("arbitrary", "arbitrary"),
        name="swa_prompt",
    )(sinks, p_att, p_att, p_att, p_att, p_att)


def _swa_step_kernel(sink_ref, q_ref, kn_ref, vn_ref, ck_ref, cv_ref, o_ref, ok_ref, ov_ref, *, n_seq, t_len):
    T = t_len
    wb = ck_ref.shape[2]
    t_row = _mod_pow2(lax.broadcasted_iota(jnp.int32, (SWA_GROUP * T, 1), 0), T)
    grp = _div_pow2(lax.broadcasted_iota(jnp.int32, (SWA_GROUP * T, 1), 0), T)
    jc = lax.broadcasted_iota(jnp.int32, (SWA_GROUP * T, wb), 1)
    jn = lax.broadcasted_iota(jnp.int32, (SWA_GROUP * T, T), 1)
    ok_c = (t_row + wb - jc < WINDOW)
    ok_n = (jn <= t_row)
    ks = [slice(hk * SWA_HEAD_DIM, (hk + 1) * SWA_HEAD_DIM) for hk in range(SWA_KV_HEADS)]
    sinks = [_group_sinks(sink_ref, hk, grp) for hk in range(SWA_KV_HEADS)]
    work = [(s, hk) for s in range(n_seq) for hk in range(SWA_KV_HEADS)]
    rows = [slice(s * T, (s + 1) * T) for s in range(n_seq)]
    lane_w = lax.broadcasted_iota(jnp.int32, (1, wb), 1)
    pad = jnp.zeros((wb - T, SWA_KV_WIDTH), F32)
    knb, vnb, ckb, cvb = [], [], [], []
    for s in range(n_seq):
        kn = kn_ref[rows[s], :]
        vn = vn_ref[rows[s], :]
        ck = ck_ref[s]
        cv = cv_ref[s]
        kn_t = jnp.transpose(jnp.concatenate([pad, kn], axis=0))
        vn_t = jnp.transpose(jnp.concatenate([pad, vn], axis=0))
        ok_ref[s] = jnp.where(lane_w >= wb - T, kn_t, pltpu.roll(ck, wb - T, axis=1))
        ov_ref[s] = jnp.where(lane_w >= wb - T, vn_t, pltpu.roll(cv, wb - T, axis=1))
        knb.append(kn.astype(BF16))
        vnb.append(vn.astype(BF16))
        ckb.append(ck.astype(BF16))
        cvb.append(cv.astype(BF16))
    qs = [q_ref[rows[s], :] * SWA_SCALE for s in range(n_seq)]
    qh = {(s, hk): _group_queries(qs[s], hk) for s, hk in work}
    sc = {w: jnp.dot(qh[w], ckb[w[0]][ks[w[1]], :], preferred_element_type=F32) for w in work}
    sn = {w: lax.dot_general(qh[w], knb[w[0]][:, ks[w[1]]], _NT, preferred_element_type=F32) for w in work}
    pr = {w: _sink_probs([jnp.where(ok_c, sc[w], -jnp.inf), jnp.where(ok_n, sn[w], -jnp.inf)], sinks[w[1]])
          for w in work}
    ov = {w: lax.dot_general(pr[w][0], cvb[w[0]][ks[w[1]], :], _NT, preferred_element_type=F32)
          + jnp.dot(pr[w][1], vnb[w[0]][:, ks[w[1]]], preferred_element_type=F32) for w in work}
    for s, hk in work:
        for g in range(SWA_GROUP):
            h = hk * SWA_GROUP + g
            o_ref[rows[s], h * SWA_HEAD_DIM:(h + 1) * SWA_HEAD_DIM] = ov[s, hk][g * T:(g + 1) * T].astype(o_ref.dtype)


def _swa_step(p_att, row0, n_b, t_len, cache_k, cache_v, sinks):
    wb = cache_k.shape[1]
    assert wb == WINDOW and t_len % 8 == 0 and t_len < wb
    n_seq = max(1, 64 // t_len)
    while n_b % n_seq:
        n_seq //= 2
    rows = n_seq * t_len
    assert rows % 16 == 0 and row0 % rows == 0
    blk0 = row0 // rows
    kcol = SWA_Q_WIDTH // SWA_KV_WIDTH
    to_cm = lambda c: jnp.transpose(c, (0, 2, 3, 1)).reshape(n_b, SWA_KV_WIDTH, wb)
    from_cm = lambda c: jnp.transpose(c.reshape(n_b, SWA_KV_HEADS, SWA_HEAD_DIM, wb), (0, 3, 1, 2))
    ck = to_cm(cache_k)
    cv = to_cm(cache_v)
    o, nk, nv = pl.pallas_call(
        functools.partial(_swa_step_kernel, n_seq=n_seq, t_len=t_len),
        grid=(n_b // n_seq,),
        in_specs=[
            pl.BlockSpec(memory_space=pltpu.SMEM),
            pl.BlockSpec((rows, SWA_Q_WIDTH), lambda i: (blk0 + i, 0)),
            pl.BlockSpec((rows, SWA_KV_WIDTH), lambda i: (blk0 + i, kcol)),
            pl.BlockSpec((rows, SWA_KV_WIDTH), lambda i: (blk0 + i, kcol + 1)),
            pl.BlockSpec((n_seq, SWA_KV_WIDTH, wb), lambda i: (i, 0, 0)),
            pl.BlockSpec((n_seq, SWA_KV_WIDTH, wb), lambda i: (i, 0, 0)),
        ],
        out_specs=[
            pl.BlockSpec((rows, SWA_Q_WIDTH), lambda i: (i, 0)),
            pl.BlockSpec((n_seq, SWA_KV_WIDTH, wb), lambda i: (i, 0, 0)),
            pl.BlockSpec((n_seq, SWA_KV_WIDTH, wb), lambda i: (i, 0, 0)),
        ],
        out_shape=[
            jax.ShapeDtypeStruct((n_b * t_len, SWA_Q_WIDTH), BF16),
            jax.ShapeDtypeStruct((n_b, SWA_KV_WIDTH, wb), F32),
            jax.ShapeDtypeStruct((n_b, SWA_KV_WIDTH, wb), F32),
        ],
        compiler_params=_params("arbitrary"),
        name="swa_step",
    )(sinks, p_att, p_att, p_att, ck, cv)
    return o, from_cm(nk), from_cm(nv)


def _softmax(s):
    m = jnp.max(s, axis=-1, keepdims=True)
    e = jnp.exp(s - m)
    return (e / jnp.sum(e, axis=-1, keepdims=True)).astype(BF16)


def _mem_attend(items, o_ref):
    scale = MEM_HEAD_DIM ** -0.5
    sc = [lax.dot_general(q, k, _NT, preferred_element_type=F32) * scale for q, k, _, _, _ in items]
    pr = [_softmax(s) for s in sc]
    ov = [jnp.dot(p, it[2], preferred_element_type=F32) for p, it in zip(pr, items)]
    for o, it in zip(ov, items):
        o_ref[it[3], it[4]] = o.astype(o_ref.dtype)


def _mem_prompt_kernel(qa_ref, qb_ref, k_ref, v_ref, o_ref):
    kb = k_ref[...].astype(BF16)
    vb = v_ref[...].astype(BF16)
    half = MEM_HEADS // 2
    items = []
    for h in range(MEM_HEADS):
        q_ref = qa_ref if h < half else qb_ref
        hs = slice((h % half) * MEM_HEAD_DIM, (h % half + 1) * MEM_HEAD_DIM)
        ms = slice(h * MEM_HEAD_DIM, (h + 1) * MEM_HEAD_DIM)
        items.append((q_ref[:, hs].astype(BF16), kb[:, ms], vb[:, ms], slice(None), ms))
    _mem_attend(items, o_ref)


def _mem_prompt(p_att, n_b, t_len, mem_kv):
    m_tok = mem_kv.shape[0] // n_b
    tq = math.gcd(t_len, 512)
    nq = t_len // tq
    half_w = MEM_WIDTH // 2
    qcol = (SWA_Q_WIDTH + 2 * SWA_KV_WIDTH) // half_w
    return pl.pallas_call(
        _mem_prompt_kernel,
        grid=(n_b, nq),
        in_specs=[
            pl.BlockSpec((tq, half_w), lambda b, n: (b * nq + n, qcol)),
            pl.BlockSpec((tq, half_w), lambda b, n: (b * nq + n, qcol + 1)),
            pl.BlockSpec((m_tok, MEM_WIDTH), lambda b, n: (b, 0)),
            pl.BlockSpec((m_tok, MEM_WIDTH), lambda b, n: (b, 1)),
        ],
        out_specs=pl.BlockSpec((tq, MEM_WIDTH), lambda b, n: (b * nq + n, 0)),
        out_shape=jax.ShapeDtypeStruct((n_b * t_len, MEM_WIDTH), BF16),
        compiler_params=_params("arbitrary", "arbitrary"),
        name="mem_prompt",
    )(p_att, p_att, mem_kv, mem_kv)


def _mem_step_kernel(qa_ref, qb_ref, k_ref, v_ref, o_ref, *, n_seq, t_len):
    half = MEM_HEADS // 2
    m_tok = k_ref.shape[1] // MEM_HEADS
    items = []
    for s in range(n_seq):
        rs = slice(s * t_len, (s + 1) * t_len)
        for h in range(MEM_HEADS):
            q_ref = qa_ref if h < half else qb_ref
            hs = slice((h % half) * MEM_HEAD_DIM, (h % half + 1) * MEM_HEAD_DIM)
            ms = slice(h * MEM_HEAD_DIM, (h + 1) * MEM_HEAD_DIM)
            hrows = pl.ds(h, m_tok, stride=MEM_HEADS)
            items.append((q_ref[rs, hs].astype(BF16), k_ref[s, hrows, :].astype(BF16),
                          v_ref[s, hrows, :].astype(BF16), rs, ms))
    _mem_attend(items, o_ref)


def _mem_step(p_att, row0, n_b, t_len, cache_k, cache_v):
    m_tok = cache_k.shape[1]
    n_seq = max(1, 64 // t_len)
    while n_b % n_seq:
        n_seq //= 2
    rows = n_seq * t_len
    assert rows % 16 == 0 and row0 % rows == 0
    blk0 = row0 // rows
    half_w = MEM_WIDTH // 2
    qcol = (SWA_Q_WIDTH + 2 * SWA_KV_WIDTH) // half_w
    cache_block = (n_seq, m_tok * MEM_HEADS, MEM_HEAD_DIM)
    ck = cache_k.reshape(n_b, m_tok * MEM_HEADS, MEM_HEAD_DIM)
    cv = cache_v.reshape(n_b, m_tok * MEM_HEADS, MEM_HEAD_DIM)
    return pl.pallas_call(
        functools.partial(_mem_step_kernel, n_seq=n_seq, t_len=t_len),
        grid=(n_b // n_seq,),
        in_specs=[
            pl.BlockSpec((rows, half_w), lambda i: (blk0 + i, qcol)),
            pl.BlockSpec((rows, half_w), lambda i: (blk0 + i, qcol + 1)),
            pl.BlockSpec(cache_block, lambda i: (i, 0, 0)),
            pl.BlockSpec(cache_block, lambda i: (i, 0, 0)),
        ],
        out_specs=pl.BlockSpec((rows, MEM_WIDTH), lambda i: (i, 0)),
        out_shape=jax.ShapeDtypeStruct((n_b * t_len, MEM_WIDTH), BF16),
        compiler_params=_params("arbitrary"),
        name="mem_step",
    )(p_att, p_att, ck, cv)


def _merge_kernel(xn_ref, rwp_ref, rws_ref, swp_ref, sws_ref, mmp_ref, mms_ref,
                  wg0_ref, wg1_ref, wg2_ref, wb0_ref, wb1_ref, wb2_ref, o_ref,
                  g0_scr, g1_scr, g2_scr, b0_scr, b1_scr, b2_scr, *, n_p):
    i = pl.program_id(1)

    @pl.when(i == 0)
    def _():
        for src, dst in ((wg0_ref, g0_scr), (wg1_ref, g1_scr), (wg2_ref, g2_scr),
                         (wb0_ref, b0_scr), (wb1_ref, b1_scr), (wb2_ref, b2_scr)):
            dst[...] = src[...].astype(BF16)

    def run(o_rw_ref, o_sw_ref, o_mm_ref):
        xn = xn_ref[...]
        acc = None
        for g_scr, b_scr, br_ref in ((g0_scr, b0_scr, o_rw_ref), (g1_scr, b1_scr, o_sw_ref),
                                     (g2_scr, b2_scr, o_mm_ref)):
            gate = _sigmoid(jnp.dot(xn, g_scr[...], preferred_element_type=F32))
            t = gate * jnp.dot(br_ref[...], b_scr[...], preferred_element_type=F32)
            acc = t if acc is None else acc + t
        o_ref[...] = acc.astype(o_ref.dtype)

    @pl.when(i < n_p)
    def _():
        run(rwp_ref, swp_ref, mmp_ref)

    @pl.when(i >= n_p)
    def _():
        run(rws_ref, sws_ref, mms_ref)


def _merge(xn, o_p, o_s, w_in, gate_col0, w_br, n_rows_p):
    m, d = xn.shape
    tn = MXU_COLS
    tm = _row_tile(n_rows_p, m - n_rows_p, ROW_TILE)
    n_p = n_rows_p // tm
    n_i = m // tm
    assert gate_col0 % tn == 0 and d % tn == 0
    g0 = gate_col0 // tn
    nj = d // tn
    p_spec = lambda w: pl.BlockSpec((tm, w), lambda j, i: (jnp.minimum(i, n_p - 1), 0))
    s_spec = lambda w: pl.BlockSpec((tm, w), lambda j, i: (jnp.maximum(i - n_p, 0), 0))
    in_specs = [pl.BlockSpec((tm, d), lambda j, i: (i, 0))]
    args = [xn]
    for op, os_ in zip(o_p, o_s):
        in_specs += [p_spec(op.shape[1]), s_spec(os_.shape[1])]
        args += [op, os_]
    for br in range(N_BRANCH):
        in_specs.append(pl.BlockSpec((d, tn), lambda j, i, br=br: (0, g0 + br * nj + j)))
        args.append(w_in)
    for w in w_br:
        in_specs.append(pl.BlockSpec((w.shape[0], tn), lambda j, i: (0, j)))
        args.append(w)
    scratch = [pltpu.VMEM((d, tn), BF16) for _ in range(N_BRANCH)]
    scratch += [pltpu.VMEM((w.shape[0], tn), BF16) for w in w_br]
    return pl.pallas_call(
        functools.partial(_merge_kernel, n_p=n_p),
        grid=(nj, n_i),
        in_specs=in_specs,
        out_specs=pl.BlockSpec((tm, tn), lambda j, i: (i, j)),
        out_shape=jax.ShapeDtypeStruct((m, d), BF16),
        scratch_shapes=scratch,
        compiler_params=_params("arbitrary", "arbitrary"),
        name="gated_merge",
    )(*args)


def _cast_kernel(x_ref, o_ref):
    o_ref[...] = x_ref[...].astype(o_ref.dtype)


def _to_bf16(w, name):
    r, c = w.shape
    tr = math.gcd(r, 512)
    return pl.pallas_call(
        _cast_kernel,
        grid=(r // tr,),
        in_specs=[pl.BlockSpec((tr, c), lambda i: (i, 0))],
        out_specs=pl.BlockSpec((tr, c), lambda i: (i, 0)),
        out_shape=jax.ShapeDtypeStruct((r, c), BF16),
        compiler_params=_params("arbitrary"),
        name=name,
    )(w)


def _wo_kernel(m_ref, w_ref, xp_ref, xs_ref, g1_ref, g2_ref, h_ref, hn_ref, *, n_p):
    i = pl.program_id(0)
    tm = m_ref.shape[0]
    halves = [slice(0, tm // 2), slice(tm // 2, tm)] if tm % 32 == 0 else [slice(0, tm)]

    def run(x_ref):
        for rs in halves:
            f = jnp.dot(m_ref[rs, :], w_ref[...], preferred_element_type=F32)
            h = x_ref[rs, :] + _rms(f, g1_ref[...])
            h_ref[rs, :] = h
            hn_ref[rs, :] = _rms(h, g2_ref[...]).astype(hn_ref.dtype)

    @pl.when(i < n_p)
    def _():
        run(xp_ref)

    @pl.when(i >= n_p)
    def _():
        run(xs_ref)


def _wo_block(merged, w_o, xp, xs, g_post, g_pre_ffn):
    m, d = merged.shape
    n_rows_p = xp.shape[0]
    tm = _row_tile(n_rows_p, m - n_rows_p, NORM_ROW_TILE)
    n_p = n_rows_p // tm
    return pl.pallas_call(
        functools.partial(_wo_kernel, n_p=n_p),
        grid=(m // tm,),
        in_specs=[
            pl.BlockSpec((tm, d), lambda i: (i, 0)),
            pl.BlockSpec((d, d), lambda i: (0, 0), pipeline_mode=pl.Buffered(1)),
            pl.BlockSpec((tm, d), lambda i: (jnp.minimum(i, n_p - 1), 0)),
            pl.BlockSpec((tm, d), lambda i: (jnp.maximum(i - n_p, 0), 0)),
            pl.BlockSpec((1, d), lambda i: (0, 0)),
            pl.BlockSpec((1, d), lambda i: (0, 0)),
        ],
        out_specs=[pl.BlockSpec((tm, d), lambda i: (i, 0)), pl.BlockSpec((tm, d), lambda i: (i, 0))],
        out_shape=[jax.ShapeDtypeStruct((m, d), F32), jax.ShapeDtypeStruct((m, d), BF16)],
        compiler_params=_params("arbitrary"),
        name="wo_residual",
    )(merged, _to_bf16(w_o, "cast_w_o"), xp, xs, g_post.reshape(1, d), g_pre_ffn.reshape(1, d))


def _gelu_tanh(x):
    return 0.5 * x * (1.0 + jnp.tanh(math.sqrt(2.0 / math.pi) * (x + 0.044715 * (x * x * x))))


def _ffn_up_kernel(hn_ref, wg_ref, wv_ref, cw_ref, cb_ref, st_ref, act_ref, cp_ref, cs_ref,
                   wgb_scr, wvb_scr, carry_scr, *, n_p, blocks_per_seq, t_s):
    i = pl.program_id(1)

    @pl.when(i == 0)
    def _():
        wgb_scr[...] = wg_ref[...].astype(BF16)
        wvb_scr[...] = wv_ref[...].astype(BF16)

    tm = hn_ref.shape[0]
    tn = wgb_scr.shape[1]
    subs = [slice(c0, c0 + MXU_COLS) for c0 in range(0, tn, MXU_COLS)] if tn % MXU_COLS == 0 else [slice(0, tn)]
    cut = tm // 2
    pieces = [(0, cut), (cut, tm)] if tm % (2 * 8 * t_s) == 0 else [(0, tm)]
    items = [(pc, cs) for cs in subs for pc in pieces]

    def gate_val(rs, cs):
        hn = hn_ref[rs, :]
        zg = jnp.dot(hn, wgb_scr[:, cs], preferred_element_type=F32)
        zv = jnp.dot(hn, wvb_scr[:, cs], preferred_element_type=F32)
        return zg, zv

    def conv_glu(cs, zg, zm1, zm2, zv):
        conv = cb_ref[:, cs] + cw_ref[0:1, cs] * zm2 + cw_ref[1:2, cs] * zm1 + cw_ref[2:3, cs] * zg
        return _gelu_tanh(conv) * zv

    @pl.when(i < n_p)
    def _():
        @pl.when(i % blocks_per_seq == 0)
        def _():
            carry_scr[...] = jnp.zeros(carry_scr.shape, F32)

        z = {items[0]: gate_val(slice(*items[0][0]), items[0][1])}
        for n, (pc, cs) in enumerate(items):
            if n + 1 < len(items):
                z[items[n + 1]] = gate_val(slice(*items[n + 1][0]), items[n + 1][1])
            zg, zv = z.pop((pc, cs))
            if pc[0] == 0:
                p1 = carry_scr[1:2, cs]
                p2 = carry_scr[0:1, cs]
            tr = pc[1] - pc[0]
            row = lax.broadcasted_iota(jnp.int32, (tr, 1), 0)
            zm1 = jnp.where(row == 0, p1, pltpu.roll(zg, 1, axis=0))
            zm2 = jnp.where(row == 0, p2, jnp.where(row == 1, p1, pltpu.roll(zg, 2, axis=0)))
            act_ref[slice(*pc), cs] = conv_glu(cs, zg, zm1, zm2, zv).astype(act_ref.dtype)
            p1 = zg[tr - 1:tr, :]
            p2 = zg[tr - 2:tr - 1, :]
            if pc[1] == tm:
                carry_scr[0:1, cs] = p2
                carry_scr[1:2, cs] = p1
                cp_ref[0, 0:1, cs] = p2
                cp_ref[0, 1:2, cs] = p1

    @pl.when(i >= n_p)
    def _():
        t = lax.broadcasted_iota(jnp.int32, (1, t_s, 1), 1)
        z = {items[0]: gate_val(slice(*items[0][0]), items[0][1])}
        for n, (pc, cs) in enumerate(items):
            if n + 1 < len(items):
                z[items[n + 1]] = gate_val(slice(*items[n + 1][0]), items[n + 1][1])
            zg, zv = z.pop((pc, cs))
            rs = slice(*pc)
            tr = pc[1] - pc[0]
            nq = tr // t_s
            qs = slice(pc[0] // t_s, pc[1] // t_s)
            tc = zg.shape[1]
            z3 = zg.reshape(nq, t_s, tc)
            s1 = st_ref[qs, 1:2, cs]
            s0 = st_ref[qs, 0:1, cs]
            zm1 = jnp.where(t == 0, s1, pltpu.roll(z3, 1, axis=1))
            zm2 = jnp.where(t == 0, s0, jnp.where(t == 1, s1, pltpu.roll(z3, 2, axis=1)))
            act = conv_glu(cs, z3, zm1, zm2, zv.reshape(nq, t_s, tc))
            act_ref[rs, cs] = act.reshape(tr, tc).astype(act_ref.dtype)
            cs_ref[qs, :, cs] = z3[:, t_s - 2:t_s, :]


def _ffn_up(hn, w_up, conv_w, conv_b, conv_state, n_b_p, t_p, n_b_s, t_s):
    m, d = hn.shape
    d_ff = conv_b.shape[0]
    n_rows_p = n_b_p * t_p
    n_rows_s = n_b_s * t_s
    tm = _row_tile(math.gcd(n_rows_p, t_p), n_rows_s, ROW_TILE)
    assert tm == n_rows_s and t_s % 8 == 0 and t_s >= 2, "sample rows must form one row block"
    n_p = n_rows_p // tm
    blocks_per_seq = t_p // tm
    tn = math.gcd(d_ff, 512)
    nj = d_ff // tn
    last_seq = n_b_p - 1
    return pl.pallas_call(
        functools.partial(_ffn_up_kernel, n_p=n_p, blocks_per_seq=blocks_per_seq, t_s=t_s),
        grid=(nj, n_p + 1),
        in_specs=[
            pl.BlockSpec((tm, d), lambda j, i: (i, 0)),
            pl.BlockSpec((d, tn), lambda j, i: (0, j)),
            pl.BlockSpec((d, tn), lambda j, i: (0, nj + j)),
            pl.BlockSpec((CONV_W, tn), lambda j, i: (0, j)),
            pl.BlockSpec((1, tn), lambda j, i: (0, j)),
            pl.BlockSpec((n_b_s, CONV_W - 1, tn), lambda j, i: (0, 0, j)),
        ],
        out_specs=[
            pl.BlockSpec((tm, tn), lambda j, i: (i, j)),
            pl.BlockSpec((1, CONV_W - 1, tn), lambda j, i: (jnp.minimum(i // blocks_per_seq, last_seq), 0, j)),
            pl.BlockSpec((n_b_s, CONV_W - 1, tn), lambda j, i: (0, 0, j)),
        ],
        out_shape=[
            jax.ShapeDtypeStruct((m, d_ff), BF16),
            jax.ShapeDtypeStruct((n_b_p, CONV_W - 1, d_ff), F32),
            jax.ShapeDtypeStruct((n_b_s, CONV_W - 1, d_ff), F32),
        ],
        scratch_shapes=[pltpu.VMEM((d, tn), BF16), pltpu.VMEM((d, tn), BF16), pltpu.VMEM((8, tn), F32)],
        compiler_params=_params("arbitrary", "arbitrary"),
        name="ffn_up_convglu",
    )(hn, w_up, w_up, conv_w, conv_b.reshape(1, d_ff), conv_state)


def _ffn_down_kernel(a_ref, w_ref, h_ref, g_ref, y_ref):
    kk = pl.program_id(1)

    @pl.when(kk == 0)
    def _():
        y_ref[...] = jnp.zeros(y_ref.shape, F32)

    last = kk == pl.num_programs(1) - 1

    @pl.when(jnp.logical_not(last))
    def _():
        y_ref[...] += jnp.dot(a_ref[...], w_ref[...].astype(BF16), preferred_element_type=F32)

    @pl.when(last)
    def _():
        tm = y_ref.shape[0]
        n_pieces = 4 if tm % 64 == 0 else 1
        tr = tm // n_pieces
        wb = w_ref[...].astype(BF16)
        for r in range(n_pieces):
            rs = slice(r * tr, (r + 1) * tr)
            f = y_ref[rs, :] + jnp.dot(a_ref[rs, :], wb, preferred_element_type=F32)
            y_ref[rs, :] = h_ref[rs, :] + _rms(f, g_ref[...])


def _ffn_down(act, w_down, h, g, row0, n_rows):
    d_ff, d = w_down.shape
    tm = math.gcd(math.gcd(n_rows, row0) if row0 else n_rows, ROW_TILE)
    tk = math.gcd(d_ff, 512)
    i0 = row0 // tm
    return pl.pallas_call(
        _ffn_down_kernel,
        grid=(n_rows // tm, d_ff // tk),
        in_specs=[
            pl.BlockSpec((tm, tk), lambda i, k: (i0 + i, k)),
            pl.BlockSpec((tk, d), lambda i, k: (k, 0)),
            pl.BlockSpec((tm, d), lambda i, k: (i0 + i, 0)),
            pl.BlockSpec((1, d), lambda i, k: (0, 0)),
        ],
        out_specs=pl.BlockSpec((tm, d), lambda i, k: (i, 0)),
        out_shape=jax.ShapeDtypeStruct((n_rows, d), F32),
        compiler_params=_params("arbitrary", "arbitrary"),
        name="ffn_down",
    )(act, w_down, h, g.reshape(1, d))


def kernel(x_prompt, x_sample, cache_swa_k, cache_swa_v, cache_mem_k, cache_mem_v, state_rwkv_shift, state_rwkv_wkv, state_ffn_conv, mem_prompt, g_pre_mix, w_in, mu_rwkv, w_decay_up, w0_decay, w_a_up, a0, w_gate_up, k_k, k_a, r_k, ln_x_w, ln_x_b, swa_sinks, g_mem, w_mem_kv, w_br_rwkv, w_br_swa, w_br_mem, w_o, g_post_mix, g_pre_ffn, w_ffn_up, conv_w, conv_b, w_ffn_down, g_post_ffn):
    b_p, t_p, d = x_prompt.shape
    b_s, t_s, _ = x_sample.shape
    n_p, n_s = b_p * t_p, b_s * t_s
    m_tok = mem_prompt.shape[1]
    rw_prm = dict(mu_rwkv=mu_rwkv, w_decay_up=w_decay_up, w0_decay=w0_decay, w_a_up=w_a_up, a0=a0,
                  w_gate_up=w_gate_up, k_k=k_k, k_a=k_a, r_k=r_k, ln_x_w=ln_x_w, ln_x_b=ln_x_b)
    xp = x_prompt.reshape(n_p, d)
    xs = x_sample.reshape(n_s, d)

    xn, p_att = _norm_matmul(xp, xs, g_pre_mix, w_in, RW_PROJ, ATT_WIDTH, "norm_proj_attn")
    p_rw = _matmul(xn, w_in, 0, RW_PROJ, RW_PROJ // 2, ROW_TILE, "proj_rwkv")

    _, mem_kv = _norm_matmul(mem_prompt.reshape(b_p * m_tok, d), None, g_mem, w_mem_kv,
                             0, 2 * MEM_WIDTH, "norm_proj_mem_kv")

    o_rw_p, shift_p, wkv_p = _rwkv(p_rw, 0, b_p, t_p, None, None, rw_prm, RWKV_PARTS)
    o_rw_s, shift_s, wkv_s = _rwkv(p_rw, n_p, b_s, t_s, state_rwkv_shift, state_rwkv_wkv, rw_prm, RWKV_PARTS)

    o_sw_p = _swa_prompt(p_att, b_p, t_p, swa_sinks)
    o_sw_s, swa_k_s, swa_v_s = _swa_step(p_att, n_p, b_s, t_s, cache_swa_k, cache_swa_v, swa_sinks)

    o_mm_p = _mem_prompt(p_att, b_p, t_p, mem_kv)
    o_mm_s = _mem_step(p_att, n_p, b_s, t_s, cache_mem_k, cache_mem_v)

    merged = _merge(xn, (o_rw_p, o_sw_p, o_mm_p), (o_rw_s, o_sw_s, o_mm_s), w_in,
                    RW_PROJ + ATT_WIDTH, (w_br_rwkv, w_br_swa, w_br_mem), n_p)
    h, hn = _wo_block(merged, w_o, xp, xs, g_post_mix, g_pre_ffn)
    act, conv_p, conv_s = _ffn_up(hn, w_ffn_up, conv_w, conv_b, state_ffn_conv, b_p, t_p, b_s, t_s)
    y_p = _ffn_down(act, w_ffn_down, h, g_post_ffn, 0, n_p)
    y_s = _ffn_down(act, w_ffn_down, h, g_post_ffn, n_p, n_s)

    keep = min(WINDOW, t_p)
    kv_p = jnp.stack([lax.slice(p_att, ((b + 1) * t_p - keep, SWA_Q_WIDTH),
                                ((b + 1) * t_p, SWA_Q_WIDTH + 2 * SWA_KV_WIDTH)) for b in range(b_p)])
    swa_k_p = kv_p[..., :SWA_KV_WIDTH].reshape(b_p, keep, SWA_KV_HEADS, SWA_HEAD_DIM)
    swa_v_p = kv_p[..., SWA_KV_WIDTH:].reshape(b_p, keep, SWA_KV_HEADS, SWA_HEAD_DIM)
    mem_k_p = mem_kv[:, :MEM_WIDTH].reshape(b_p, m_tok, MEM_HEADS, MEM_HEAD_DIM)
    mem_v_p = mem_kv[:, MEM_WIDTH:].reshape(b_p, m_tok, MEM_HEADS, MEM_HEAD_DIM)
    return (y_p.reshape(b_p, t_p, d), y_s.reshape(b_s, t_s, d), swa_k_p, swa_v_p, mem_k_p, mem_v_p,
            shift_p, wkv_p, conv_p, swa_k_s, swa_v_s, shift_s, wkv_s, conv_s)
```

```python
import functools
import math

import jax
import jax.numpy as jnp
from jax import lax
from jax.experimental import pallas as pl
from jax.experimental.pallas import tpu as pltpu

F32 = jnp.float32
BF16 = jnp.bfloat16

RW_HEADS = 12
RW_HEAD_DIM = 64
RW_WIDTH = RW_HEADS * RW_HEAD_DIM
RW_DECAY_RANK = 64
RW_A_RANK = 64
RW_GATE_RANK = 128
RW_PROJ = 3 * RW_WIDTH + RW_DECAY_RANK + RW_A_RANK + RW_GATE_RANK
RW_GN_EPS = 6.4e-4
RW_PAIRS = RW_HEADS // 2
SWA_Q_HEADS = 12
SWA_KV_HEADS = 4
SWA_GROUP = SWA_Q_HEADS // SWA_KV_HEADS
SWA_HEAD_DIM = 64
SWA_Q_WIDTH = SWA_Q_HEADS * SWA_HEAD_DIM
SWA_KV_WIDTH = SWA_KV_HEADS * SWA_HEAD_DIM
WINDOW = 128
MEM_HEADS = 4
MEM_HEAD_DIM = 128
MEM_WIDTH = MEM_HEADS * MEM_HEAD_DIM
N_BRANCH = 3
CONV_W = 3
NORM_EPS = 1e-6
ATT_WIDTH = SWA_Q_WIDTH + 2 * SWA_KV_WIDTH + MEM_WIDTH
SWA_SCALE = SWA_HEAD_DIM ** -0.5
assert math.frexp(SWA_SCALE)[0] == 0.5

LANES = 128
MXU_COLS = 256
VMEM_LIMIT_BYTES = 56 * 1024 * 1024

RWKV_ROWS = 64
RWKV_GROUPS = 2
RWKV_PARTS = dict(z=1, lv=1, pow=1, app=1, om=1, hg=1, out=1, st=1)
RWKV_SUM_PARTS = 1
ROW_TILE = 1024
NORM_ROW_TILE = 512


def _params(*sem):
    return pltpu.CompilerParams(dimension_semantics=sem, vmem_limit_bytes=VMEM_LIMIT_BYTES)


def _row_tile(n_p, n_s, target):
    t = math.gcd(math.gcd(n_p, n_s), target)
    assert t % 16 == 0, (n_p, n_s, target)
    return t


def _div_pow2(x, n):
    assert n & (n - 1) == 0
    return lax.shift_right_logical(x, n.bit_length() - 1)


def _mod_pow2(x, n):
    assert n & (n - 1) == 0
    return jnp.bitwise_and(x, n - 1)


def _rms(x, g):
    ms = jnp.mean(x * x, axis=-1, keepdims=True)
    return x * lax.rsqrt(ms + NORM_EPS) * g


def _norm_mm_kernel(*refs, n_p, two):
    if two:
        xp_ref, xs_ref, g_ref, w_ref, xn_ref, o_ref, wb_scr = refs
    else:
        xp_ref, g_ref, w_ref, xn_ref, o_ref, wb_scr = refs
    i = pl.program_id(0)

    @pl.when(i == 0)
    def _():
        wb_scr[...] = w_ref[...].astype(BF16)

    def run(x_ref):
        xn = _rms(x_ref[...], g_ref[...]).astype(BF16)
        xn_ref[...] = xn
        o_ref[...] = jnp.dot(xn, wb_scr[...], preferred_element_type=F32)

    if two:
        @pl.when(i < n_p)
        def _():
            run(xp_ref)

        @pl.when(i >= n_p)
        def _():
            run(xs_ref)
    else:
        run(xp_ref)


def _norm_matmul(xp, xs, g, w, col0, n_cols, name):
    n_rows_p, d = xp.shape
    two = xs is not None
    n_rows_s = xs.shape[0] if two else 0
    tm = _row_tile(n_rows_p, n_rows_s, NORM_ROW_TILE) if two else math.gcd(n_rows_p, NORM_ROW_TILE)
    n_p, n_s = n_rows_p // tm, n_rows_s // tm
    assert col0 % LANES == 0 and n_cols % LANES == 0
    once = pl.Buffered(1)
    in_specs = [pl.BlockSpec((tm, d), lambda i: (jnp.minimum(i, n_p - 1), 0))]
    args = [xp]
    if two:
        in_specs.append(pl.BlockSpec((tm, d), lambda i: (jnp.maximum(i - n_p, 0), 0)))
        args.append(xs)
    in_specs += [pl.BlockSpec((1, d), lambda i: (0, 0)),
                 pl.BlockSpec((pl.Element(d), pl.Element(n_cols)), lambda i: (0, col0), pipeline_mode=once)]
    args += [g.reshape(1, d), w]
    m = n_rows_p + n_rows_s
    return pl.pallas_call(
        functools.partial(_norm_mm_kernel, n_p=n_p, two=two),
        grid=(n_p + n_s,),
        in_specs=in_specs,
        out_specs=[pl.BlockSpec((tm, d), lambda i: (i, 0)), pl.BlockSpec((tm, n_cols), lambda i: (i, 0))],
        out_shape=[jax.ShapeDtypeStruct((m, d), BF16), jax.ShapeDtypeStruct((m, n_cols), F32)],
        scratch_shapes=[pltpu.VMEM((d, n_cols), BF16)],
        compiler_params=_params("arbitrary"),
        name=name,
    )(*args)


def _mm_kernel(a_ref, w_ref, o_ref, wb_ref):
    @pl.when(pl.program_id(1) == 0)
    def _():
        wb_ref[...] = w_ref[...].astype(BF16)

    o_ref[...] = jnp.dot(a_ref[...], wb_ref[...], preferred_element_type=F32)


def _matmul(a, w, col0, n_cols, tn, tm, name):
    m, k = a.shape
    tm = math.gcd(m, tm)
    assert n_cols % tn == 0 and col0 % LANES == 0 and tn % LANES == 0
    return pl.pallas_call(
        _mm_kernel,
        grid=(n_cols // tn, m // tm),
        in_specs=[
            pl.BlockSpec((tm, k), lambda j, i: (i, 0)),
            pl.BlockSpec((pl.Element(k), pl.Element(tn)),
                         lambda j, i: (0, pl.multiple_of(col0 + j * tn, LANES))),
        ],
        out_specs=pl.BlockSpec((tm, tn), lambda j, i: (i, j)),
        out_shape=jax.ShapeDtypeStruct((m, n_cols), F32),
        scratch_shapes=[pltpu.VMEM((k, tn), BF16)],
        compiler_params=_params("arbitrary", "arbitrary"),
        name=name,
    )(a, w)


_NN = (((1,), (0,)), ((), ()))
_NT = (((1,), (1,)), ((), ()))
_TN = (((0,), (0,)), ((), ()))


def _bf_parts(x, n):
    parts = []
    r = x
    for i in range(n):
        h = r.astype(BF16)
        parts.append(h)
        if i + 1 < n:
            r = r - h.astype(F32)
    return parts


def _dotp(a, b, dn, pa, pb):
    pa_parts = _bf_parts(a, pa)
    pb_parts = _bf_parts(b, pb)
    acc = None
    for i in range(pa):
        for j in range(pb):
            if i + j < max(pa, pb):
                t = lax.dot_general(pa_parts[i], pb_parts[j], dn, preferred_element_type=F32)
                acc = t if acc is None else acc + t
    return acc


def _softplus(x):
    return jnp.maximum(x, 0.0) + jnp.log(1.0 + jnp.exp(-jnp.abs(x)))


def _sigmoid(x):
    return 1.0 / (1.0 + jnp.exp(-x))


def _rwkv_kernel(*refs, seq_rows, n_seq, n_par, has_state, pp):
    n_in = n_par + (2 if has_state else 0)
    p_refs = refs[0:n_par]
    if has_state:
        sh0_ref, s0_ref = refs[n_par:n_in]
    (mu_ref, wda_ref, wa0_ref, wg_ref, vec_ref,
     o_ref, sh_out_ref, s_out_ref, prev_scr, st_scr) = refs[n_in:]
    c = pl.program_id(1)
    n_c = pl.num_programs(1)
    C = seq_rows
    R = seq_rows * n_seq
    W = RW_WIDTH
    HD = RW_HEAD_DIM
    groups = range(n_par)

    @pl.when(c == 0)
    def _():
        st_scr[...] = jnp.zeros(st_scr.shape, F32)
        if has_state:
            prev_scr[...] = sh0_ref[...]
            for s in range(n_par * n_seq):
                for q in range(RW_PAIRS):
                    st_scr[s, q, 0:HD, 0:HD] = s0_ref[s, 2 * q]
                    st_scr[s, q, HD:2 * HD, HD:2 * HD] = s0_ref[s, 2 * q + 1]
        else:
            prev_scr[...] = jnp.zeros(prev_scr.shape, F32)

    k_k = vec_ref[0:1, :]
    k_a = vec_ref[1:2, :]
    r_k = vec_ref[2:3, :]
    ln_w = vec_ref[3:4, :]
    ln_b = vec_ref[4:5, :]
    lane = lax.broadcasted_iota(jnp.int32, (1, LANES), 1)
    row = lax.broadcasted_iota(jnp.int32, (R, 1), 0)

    li = lax.broadcasted_iota(jnp.int32, (LANES, LANES), 0)
    lj = lax.broadcasted_iota(jnp.int32, (LANES, LANES), 1)
    same_head = _div_pow2(li, RW_HEAD_DIM) == _div_pow2(lj, RW_HEAD_DIM)
    ones_bd = jnp.where(same_head, 1.0, 0.0).astype(BF16)

    def head_sum(z):
        cols = []
        for q in range(RW_PAIRS):
            zq = z[:, q * LANES:(q + 1) * LANES]
            acc = None
            for part in _bf_parts(zq, RWKV_SUM_PARTS):
                t = jnp.dot(part, ones_bd, preferred_element_type=F32)
                acc = t if acc is None else acc + t
            cols.append(acc)
        return jnp.concatenate(cols, axis=1)

    ri = lax.broadcasted_iota(jnp.int32, (R, R), 0)
    rj = lax.broadcasted_iota(jnp.int32, (R, R), 1)
    same_seq = _div_pow2(ri, C) == _div_pow2(rj, C)
    causal = jnp.where(same_seq & (rj <= ri), 1.0, 0.0).astype(BF16)

    def prepare(u):
        p = p_refs[u][...]
        prev = pltpu.roll(p, 1, axis=0)
        for s in range(n_seq):
            prev = jnp.where(row == s * C, prev_scr[u * n_seq + s], prev)
        for s in range(n_seq):
            prev_scr[u * n_seq + s] = p[(s + 1) * C - 1:(s + 1) * C, :]
            sh_out_ref[u * n_seq + s] = p[(s + 1) * C - 1:(s + 1) * C, :]
        x = p + mu_ref[...] * (prev - p)
        r = x[:, 0:W]
        k = x[:, W:2 * W]
        v = x[:, 2 * W:3 * W]
        wa_in = x[:, 3 * W:3 * W + LANES]
        g_in = x[:, 3 * W + LANES:3 * W + 2 * LANES]
        wa_act = jnp.where(lane < RW_DECAY_RANK, jnp.tanh(wa_in), wa_in)
        za = jnp.dot(wa_act.astype(BF16), wda_ref[...], preferred_element_type=F32) + wa0_ref[...]
        w_log = -_softplus(-za[:, 0:W]) - 0.5
        lw = -jnp.exp(w_log)
        a = _sigmoid(za[:, W:2 * W])
        g = jnp.dot(_sigmoid(g_in).astype(BF16), wg_ref[...], preferred_element_type=F32)
        kk = k * k_k
        kk = kk / jnp.maximum(jnp.sqrt(head_sum(kk * kk)), 1e-12)
        k = k * (1.0 + (a - 1.0) * k_a)
        b = kk * a
        cl = None
        for part in _bf_parts(lw, 2):
            t1 = jnp.dot(causal, part, preferred_element_type=F32)
            cl = t1 if cl is None else cl + t1
        if n_seq == 1:
            ct = cl[R - 1:R, :]
        else:
            last = cl.reshape(n_seq, C, W)[:, C - 1:C, :]
            ct = jnp.broadcast_to(last, (n_seq, C, W)).reshape(R, W)
        e_neg = jnp.exp(-cl)
        e_end = jnp.exp(ct - cl)
        return dict(r=r, k=k, v=v, g=g, at=-kk * jnp.exp(cl - lw), rt=r * jnp.exp(cl), bt=b * e_neg,
                    kt=k * e_neg, bts=b * e_end, kts=k * e_end,
                    p_end=jnp.exp(ct))

    prep = [prepare(u) for u in groups]

    strict = jnp.where(same_seq & (rj < ri), 1.0, 0.0)
    incl = jnp.where(same_seq & (rj <= ri), 1.0, 0.0)
    strict2 = jnp.concatenate([strict, strict], axis=1)
    incl2 = jnp.concatenate([incl, incl], axis=1)
    m0 = jnp.where(lane < RW_HEAD_DIM, 1.0, 0.0)
    m1 = jnp.where(lane >= RW_HEAD_DIM, 1.0, 0.0)
    diag = jnp.where(li == lj, 1.0, 0.0)
    bd_mask = jnp.where(same_head, 1.0, 0.0)
    zeros_rl = jnp.zeros((R, LANES), F32)
    zeros_r2 = jnp.zeros((R, 2 * R), F32)
    n_dbl = max(1, (C - 1).bit_length())
    sls = [slice(q * LANES, (q + 1) * LANES) for q in range(RW_PAIRS)]
    items = [(u, q) for u in groups for q in range(RW_PAIRS)]

    def col(name, it):
        return prep[it[0]][name][:, sls[it[1]]]

    v0 = {it: jnp.concatenate([col("v", it), zeros_rl], axis=1) for it in items}
    zs = {}
    for it in items:
        xa, xr, bq, kq = col("at", it), col("rt", it), col("bt", it), col("kt", it)
        x4 = jnp.concatenate([xa * m0, xr * m0, xa * m1, xr * m1], axis=0)
        y4 = jnp.concatenate([bq, kq, kq, bq], axis=0)
        zs[it] = _dotp(x4, y4, _NT, pp["z"], pp["z"])
    zt0 = {it: zs[it][0:R, 0:2 * R] * strict2 for it in items}
    zb0 = {it: zs[it][R:2 * R, 0:2 * R] * incl2 for it in items}
    zt1 = {it: zs[it][2 * R:3 * R, 2 * R:4 * R] * strict2 for it in items}
    zb1 = {it: zs[it][3 * R:4 * R, 2 * R:4 * R] * incl2 for it in items}
    half0 = jnp.where(lax.broadcasted_iota(jnp.int32, (1, 2 * R), 1) < R, 1.0, 0.0)
    half1 = 1.0 - half0
    pw = {it: jnp.concatenate([zt0[it] * half0, zt1[it] * half1], axis=0) for it in items}
    wmat = {}
    for it in items:
        lhs = jnp.concatenate([jnp.concatenate([zt0[it], zeros_r2], axis=1),
                               jnp.concatenate([zeros_r2, zt1[it]], axis=1)], axis=0)
        vq = col("v", it)
        rhs = jnp.concatenate([zeros_rl, vq, vq, zeros_rl], axis=0)
        lv = _dotp(lhs, rhs, _NN, pp["lv"], pp["lv"])
        xa = col("at", it)
        wmat[it] = jnp.concatenate([lv, jnp.concatenate([xa * m0, xa * m1], axis=0)], axis=1)
    def lower_rows(a, lo):
        return jnp.concatenate([a[lo:R], a[R + lo:2 * R]], axis=0)

    def scatter_rows(base, upd, lo):
        h = R - lo
        top = [jnp.zeros((lo, upd.shape[1]), F32)] * 2 if base is None else [base[0:lo], base[R:R + lo]]
        mid = [upd[0:h], upd[h:2 * h]] if base is None else [base[lo:R] + upd[0:h], base[R + lo:2 * R] + upd[h:2 * h]]
        return jnp.concatenate([top[0], mid[0], top[1], mid[1]], axis=0)

    for step in range(n_dbl):
        lo = 1 << step
        if n_seq == 1 and lo % 8 == 0:
            wmat = {it: scatter_rows(wmat[it], _dotp(lower_rows(pw[it], lo), wmat[it], _NN, pp["app"], pp["app"]), lo)
                    for it in items}
        else:
            wmat = {it: wmat[it] + _dotp(pw[it], wmat[it], _NN, pp["app"], pp["app"]) for it in items}
        if step + 1 < n_dbl:
            if n_seq == 1 and (2 * lo) % 8 == 0:
                pw = {it: scatter_rows(None, _dotp(lower_rows(pw[it], 2 * lo), pw[it], _NN, pp["pow"], pp["pow"]),
                                       2 * lo) for it in items}
            else:
                pw = {it: _dotp(pw[it], pw[it], _NN, pp["pow"], pp["pow"]) for it in items}
    rhat, o0 = {}, {}
    for it in items:
        lhs = jnp.concatenate([jnp.concatenate([zb0[it], zeros_r2], axis=1),
                               jnp.concatenate([zeros_r2, zb1[it]], axis=1)], axis=0)
        rhs = jnp.concatenate([wmat[it][0:R], v0[it], v0[it], wmat[it][R:2 * R]], axis=0)
        om = _dotp(lhs, rhs, _NN, pp["om"], pp["om"])
        rhat[it] = col("rt", it) + om[0:R, LANES:] + om[R:2 * R, LANES:]
        o0[it] = om[0:R, 0:LANES] * m0 + om[R:2 * R, 0:LANES] * m1
    hgt = {}
    for it in items:
        bq, kq = col("bts", it), col("kts", it)
        for s in range(n_seq):
            rs = slice(s * C, (s + 1) * C)
            ys4 = jnp.concatenate([bq[rs] * m0, kq[rs] * m0, kq[rs] * m1, bq[rs] * m1], axis=0)
            r4 = jnp.concatenate([wmat[it][rs], v0[it][rs], v0[it][rs],
                                  wmat[it][R + s * C:R + (s + 1) * C]], axis=0)
            hgt[it, s] = _dotp(r4, ys4, _TN, pp["hg"], pp["hg"])
    o_pair = {}
    for it in items:
        u, q = it
        o_seq = []
        for s in range(n_seq):
            rs = slice(s * C, (s + 1) * C)
            s_old = st_scr[u * n_seq + s, q]
            ht = hgt[it, s][0:LANES] * bd_mask
            gt = hgt[it, s][LANES:] + diag * prep[u]["p_end"][s * C:s * C + 1, sls[q]]
            o_seq.append(_dotp(rhat[it][rs], s_old, _NT, pp["out"], pp["out"]) + o0[it][rs])
            st_scr[u * n_seq + s, q] = _dotp(s_old, gt, _NN, pp["st"], pp["st"]) + ht
        o_pair[it] = o_seq[0] if n_seq == 1 else jnp.concatenate(o_seq, axis=0)

    inv_n = 1.0 / RW_HEAD_DIM
    for u in groups:
        o = jnp.concatenate([o_pair[u, q] for q in range(RW_PAIRS)], axis=1)
        mean = head_sum(o) * inv_n
        d = o - mean
        var = head_sum(d * d) * inv_n
        o = d * lax.rsqrt(var + RW_GN_EPS) * ln_w + ln_b
        bonus = head_sum(prep[u]["r"] * prep[u]["k"] * r_k) * prep[u]["v"]
        o_ref[u] = ((o + bonus) * prep[u]["g"]).astype(o_ref.dtype)

    @pl.when(c == n_c - 1)
    def _():
        for s in range(n_par * n_seq):
            for q in range(RW_PAIRS):
                s_out_ref[s, 2 * q] = st_scr[s, q, 0:HD, 0:HD]
                s_out_ref[s, 2 * q + 1] = st_scr[s, q, HD:2 * HD, HD:2 * HD]


def _rwkv(p_rw, row0, n_b, t_len, shift0, s0, prm, pp):
    if t_len >= RWKV_ROWS:
        seq_rows, n_seq = RWKV_ROWS, 1
    else:
        seq_rows, n_seq = t_len, RWKV_ROWS // t_len
    assert t_len % seq_rows == 0 and n_b % n_seq == 0 and row0 % (seq_rows * n_seq) == 0
    rows = seq_rows * n_seq
    n_c = t_len // seq_rows
    assert n_seq == 1 or n_c == 1
    blk0 = row0 // rows
    has_state = s0 is not None
    n_grp = n_b // n_seq
    n_par = math.gcd(n_grp, RWKV_GROUPS)
    seqs = n_par * n_seq

    wda = jnp.zeros((LANES, 2 * RW_WIDTH), F32)
    wda = wda.at[0:RW_DECAY_RANK, 0:RW_WIDTH].set(prm["w_decay_up"])
    wda = wda.at[RW_DECAY_RANK:, RW_WIDTH:].set(prm["w_a_up"])
    wa0 = jnp.concatenate([prm["w0_decay"], prm["a0"]]).reshape(1, 2 * RW_WIDTH)
    vecs = jnp.zeros((8, RW_WIDTH), F32)
    for i, name in enumerate(("k_k", "k_a", "r_k", "ln_x_w", "ln_x_b")):
        vecs = vecs.at[i].set(prm[name].reshape(RW_WIDTH))

    const = lambda *shape: pl.BlockSpec(shape, lambda bi, c: (0,) * len(shape))
    in_specs = [pl.BlockSpec((rows, RW_PROJ), lambda bi, c, u=u: (blk0 + (bi * n_par + u) * n_c + c, 0))
                for u in range(n_par)]
    args = [p_rw] * n_par
    st_block = (seqs, RW_HEADS, RW_HEAD_DIM, RW_HEAD_DIM)
    if has_state:
        in_specs += [pl.BlockSpec((seqs, 1, RW_PROJ), lambda bi, c: (bi, 0, 0)),
                     pl.BlockSpec(st_block, lambda bi, c: (bi, 0, 0, 0))]
        args += [shift0.reshape(n_b, 1, RW_PROJ), s0]
    in_specs += [const(1, RW_PROJ), const(LANES, 2 * RW_WIDTH), const(1, 2 * RW_WIDTH),
                 const(RW_GATE_RANK, RW_WIDTH), const(8, RW_WIDTH)]
    args += [prm["mu_rwkv"].reshape(1, RW_PROJ), wda.astype(BF16), wa0,
             prm["w_gate_up"].astype(BF16), vecs]

    o, sh, st = pl.pallas_call(
        functools.partial(_rwkv_kernel, seq_rows=seq_rows, n_seq=n_seq, n_par=n_par, has_state=has_state, pp=pp),
        grid=(n_grp // n_par, n_c),
        in_specs=in_specs,
        out_specs=[
            pl.BlockSpec((n_par, rows, RW_WIDTH), lambda bi, c: (bi, c, 0)),
            pl.BlockSpec((seqs, 1, RW_PROJ), lambda bi, c: (bi, 0, 0)),
            pl.BlockSpec(st_block, lambda bi, c: (bi, 0, 0, 0)),
        ],
        out_shape=[
            jax.ShapeDtypeStruct((n_grp, n_c * rows, RW_WIDTH), BF16),
            jax.ShapeDtypeStruct((n_b, 1, RW_PROJ), F32),
            jax.ShapeDtypeStruct((n_b, RW_HEADS, RW_HEAD_DIM, RW_HEAD_DIM), F32),
        ],
        scratch_shapes=[pltpu.VMEM((seqs, 1, RW_PROJ), F32),
                        pltpu.VMEM((seqs, RW_PAIRS, LANES, LANES), F32)],
        compiler_params=_params("arbitrary", "arbitrary"),
        name="rwkv_state" if has_state else "rwkv_fresh",
    )(*args)
    return o.reshape(n_b * t_len, RW_WIDTH), sh.reshape(n_b, RW_PROJ), st


def _sink_probs(s_parts, sink):
    m = sink
    for s in s_parts:
        m = jnp.maximum(m, jnp.max(s, axis=-1, keepdims=True))
    den = jnp.exp(sink - m)
    es = []
    for s in s_parts:
        e = jnp.exp(s - m)
        den = den + jnp.sum(e, axis=-1, keepdims=True)
        es.append(e)
    inv = 1.0 / den
    return [(e * inv).astype(BF16) for e in es]


def _group_queries(q, hk):
    return jnp.concatenate(
        [q[:, (hk * SWA_GROUP + g) * SWA_HEAD_DIM:(hk * SWA_GROUP + g + 1) * SWA_HEAD_DIM]
         for g in range(SWA_GROUP)], axis=0).astype(BF16)


def _group_sinks(sink_ref, hk, grp):
    sink = jnp.zeros(grp.shape, F32)
    for g in range(SWA_GROUP):
        sink = jnp.where(grp == g, sink_ref[hk * SWA_GROUP + g], sink)
    return sink


def _swa_prompt_kernel(sink_ref, q_ref, kp_ref, kc_ref, vp_ref, vc_ref, o_ref, *, n_win):
    n = pl.program_id(1)
    Wn = WINDOW
    kf = jnp.concatenate([kp_ref[...], kc_ref[...]], axis=0).astype(BF16)
    vf = jnp.concatenate([vp_ref[...], vc_ref[...]], axis=0).astype(BF16)
    gi = lax.broadcasted_iota(jnp.int32, (SWA_GROUP * Wn, 2 * Wn), 0)
    i = _mod_pow2(gi, Wn)
    j = lax.broadcasted_iota(jnp.int32, (SWA_GROUP * Wn, 2 * Wn), 1)
    ok = (j > i) & (j <= i + Wn)
    first_key = jnp.where(n > 0, 0, Wn)
    ok_first = ok & (j >= first_key)
    grp = _div_pow2(lax.broadcasted_iota(jnp.int32, (SWA_GROUP * Wn, 1), 0), Wn)
    ks = [slice(hk * SWA_HEAD_DIM, (hk + 1) * SWA_HEAD_DIM) for hk in range(SWA_KV_HEADS)]
    sinks = [_group_sinks(sink_ref, hk, grp) for hk in range(SWA_KV_HEADS)]
    work = [(w, hk) for w in range(n_win) for hk in range(SWA_KV_HEADS)]
    qw = [q_ref[w * Wn:(w + 1) * Wn, :] * SWA_SCALE for w in range(n_win)]
    sc = {(w, hk): lax.dot_general(_group_queries(qw[w], hk), kf[w * Wn:(w + 2) * Wn, ks[hk]], _NT,
                                   preferred_element_type=F32) for w, hk in work}
    pr = {(w, hk): _sink_probs([jnp.where(ok_first if w == 0 else ok, sc[w, hk], -jnp.inf)], sinks[hk])[0]
          for w, hk in work}
    ov = {(w, hk): jnp.dot(pr[w, hk], vf[w * Wn:(w + 2) * Wn, ks[hk]], preferred_element_type=F32)
          for w, hk in work}
    for w, hk in work:
        for g in range(SWA_GROUP):
            h = hk * SWA_GROUP + g
            o_ref[w * Wn:(w + 1) * Wn, h * SWA_HEAD_DIM:(h + 1) * SWA_HEAD_DIM] = (
                ov[w, hk][g * Wn:(g + 1) * Wn].astype(o_ref.dtype))


def _swa_prompt(p_att, n_b, t_len, sinks):
    nb = t_len // WINDOW
    n_win = 2 if nb % 2 == 0 else 1
    ns = nb // n_win
    rows = n_win * WINDOW
    kcol = SWA_Q_WIDTH // SWA_KV_WIDTH
    vcol = kcol + 1
    cur = lambda col: (lambda b, n: (b * ns + n, col))
    prv = lambda col: (lambda b, n: (b * nb + jnp.maximum(n * n_win - 1, 0), col))
    return pl.pallas_call(
        functools.partial(_swa_prompt_kernel, n_win=n_win),
        grid=(n_b, ns),
        in_specs=[
            pl.BlockSpec(memory_space=pltpu.SMEM),
            pl.BlockSpec((rows, SWA_Q_WIDTH), cur(0)),
            pl.BlockSpec((WINDOW, SWA_KV_WIDTH), prv(kcol)),
            pl.BlockSpec((rows, SWA_KV_WIDTH), cur(kcol)),
            pl.BlockSpec((WINDOW, SWA_KV_WIDTH), prv(vcol)),
            pl.BlockSpec((rows, SWA_KV_WIDTH), cur(vcol)),
        ],
        out_specs=pl.BlockSpec((rows, SWA_Q_WIDTH), lambda b, n: (b * ns + n, 0)),
        out_shape=jax.ShapeDtypeStruct((n_b * t_len, SWA_Q_WIDTH), BF16),
        compiler_params=_params("arbitrary", "arbitrary"),
        name="swa_prompt",
    )(sinks, p_att, p_att, p_att, p_att, p_att)


def _swa_step_kernel(sink_ref, q_ref, kn_ref, vn_ref, ck_ref, cv_ref, o_ref, ok_ref, ov_ref, *, n_seq, t_len):
    T = t_len
    wb = ck_ref.shape[2]
    t_row = _mod_pow2(lax.broadcasted_iota(jnp.int32, (SWA_GROUP * T, 1), 0), T)
    grp = _div_pow2(lax.broadcasted_iota(jnp.int32, (SWA_GROUP * T, 1), 0), T)
    jc = lax.broadcasted_iota(jnp.int32, (SWA_GROUP * T, wb), 1)
    jn = lax.broadcasted_iota(jnp.int32, (SWA_GROUP * T, T), 1)
    ok_c = (t_row + wb - jc < WINDOW)
    ok_n = (jn <= t_row)
    ks = [slice(hk * SWA_HEAD_DIM, (hk + 1) * SWA_HEAD_DIM) for hk in range(SWA_KV_HEADS)]
    sinks = [_group_sinks(sink_ref, hk, grp) for hk in range(SWA_KV_HEADS)]
    work = [(s, hk) for s in range(n_seq) for hk in range(SWA_KV_HEADS)]
    rows = [slice(s * T, (s + 1) * T) for s in range(n_seq)]
    lane_w = lax.broadcasted_iota(jnp.int32, (1, wb), 1)
    pad = jnp.zeros((wb - T, SWA_KV_WIDTH), F32)
    knb, vnb, ckb, cvb = [], [], [], []
    for s in range(n_seq):
        kn = kn_ref[rows[s], :]
        vn = vn_ref[rows[s], :]
        ck = ck_ref[s]
        cv = cv_ref[s]
        kn_t = jnp.transpose(jnp.concatenate([pad, kn], axis=0))
        vn_t = jnp.transpose(jnp.concatenate([pad, vn], axis=0))
        ok_ref[s] = jnp.where(lane_w >= wb - T, kn_t, pltpu.roll(ck, wb - T, axis=1))
        ov_ref[s] = jnp.where(lane_w >= wb - T, vn_t, pltpu.roll(cv, wb - T, axis=1))
        knb.append(kn.astype(BF16))
        vnb.append(vn.astype(BF16))
        ckb.append(ck.astype(BF16))
        cvb.append(cv.astype(BF16))
    qs = [q_ref[rows[s], :] * SWA_SCALE for s in range(n_seq)]
    qh = {(s, hk): _group_queries(qs[s], hk) for s, hk in work}
    sc = {w: jnp.dot(qh[w], ckb[w[0]][ks[w[1]], :], preferred_element_type=F32) for w in work}
    sn = {w: lax.dot_general(qh[w], knb[w[0]][:, ks[w[1]]], _NT, preferred_element_type=F32) for w in work}
    pr = {w: _sink_probs([jnp.where(ok_c, sc[w], -jnp.inf), jnp.where(ok_n, sn[w], -jnp.inf)], sinks[w[1]])
          for w in work}
    ov = {w: lax.dot_general(pr[w][0], cvb[w[0]][ks[w[1]], :], _NT, preferred_element_type=F32)
          + jnp.dot(pr[w][1], vnb[w[0]][:, ks[w[1]]], preferred_element_type=F32) for w in work}
    for s, hk in work:
        for g in range(SWA_GROUP):
            h = hk * SWA_GROUP + g
            o_ref[rows[s], h * SWA_HEAD_DIM:(h + 1) * SWA_HEAD_DIM] = ov[s, hk][g * T:(g + 1) * T].astype(o_ref.dtype)


def _swa_step(p_att, row0, n_b, t_len, cache_k, cache_v, sinks):
    wb = cache_k.shape[1]
    assert wb == WINDOW and t_len % 8 == 0 and t_len < wb
    n_seq = max(1, 64 // t_len)
    while n_b % n_seq:
        n_seq //= 2
    rows = n_seq * t_len
    assert rows % 16 == 0 and row0 % rows == 0
    blk0 = row0 // rows
    kcol = SWA_Q_WIDTH // SWA_KV_WIDTH
    to_cm = lambda c: jnp.transpose(c, (0, 2, 3, 1)).reshape(n_b, SWA_KV_WIDTH, wb)
    from_cm = lambda c: jnp.transpose(c.reshape(n_b, SWA_KV_HEADS, SWA_HEAD_DIM, wb), (0, 3, 1, 2))
    ck = to_cm(cache_k)
    cv = to_cm(cache_v)
    o, nk, nv = pl.pallas_call(
        functools.partial(_swa_step_kernel, n_seq=n_seq, t_len=t_len),
        grid=(n_b // n_seq,),
        in_specs=[
            pl.BlockSpec(memory_space=pltpu.SMEM),
            pl.BlockSpec((rows, SWA_Q_WIDTH), lambda i: (blk0 + i, 0)),
            pl.BlockSpec((rows, SWA_KV_WIDTH), lambda i: (blk0 + i, kcol)),
            pl.BlockSpec((rows, SWA_KV_WIDTH), lambda i: (blk0 + i, kcol + 1)),
            pl.BlockSpec((n_seq, SWA_KV_WIDTH, wb), lambda i: (i, 0, 0)),
            pl.BlockSpec((n_seq, SWA_KV_WIDTH, wb), lambda i: (i, 0, 0)),
        ],
        out_specs=[
            pl.BlockSpec((rows, SWA_Q_WIDTH), lambda i: (i, 0)),
            pl.BlockSpec((n_seq, SWA_KV_WIDTH, wb), lambda i: (i, 0, 0)),
            pl.BlockSpec((n_seq, SWA_KV_WIDTH, wb), lambda i: (i, 0, 0)),
        ],
        out_shape=[
            jax.ShapeDtypeStruct((n_b * t_len, SWA_Q_WIDTH), BF16),
            jax.ShapeDtypeStruct((n_b, SWA_KV_WIDTH, wb), F32),
            jax.ShapeDtypeStruct((n_b, SWA_KV_WIDTH, wb), F32),
        ],
        compiler_params=_params("arbitrary"),
        name="swa_step",
    )(sinks, p_att, p_att, p_att, ck, cv)
    return o, from_cm(nk), from_cm(nv)


def _softmax(s):
    m = jnp.max(s, axis=-1, keepdims=True)
    e = jnp.exp(s - m)
    return (e / jnp.sum(e, axis=-1, keepdims=True)).astype(BF16)


def _mem_attend(items, o_ref):
    scale = MEM_HEAD_DIM ** -0.5
    sc = [lax.dot_general(q, k, _NT, preferred_element_type=F32) * scale for q, k, _, _, _ in items]
    pr = [_softmax(s) for s in sc]
    ov = [jnp.dot(p, it[2], preferred_element_type=F32) for p, it in zip(pr, items)]
    for o, it in zip(ov, items):
        o_ref[it[3], it[4]] = o.astype(o_ref.dtype)


def _mem_prompt_kernel(qa_ref, qb_ref, k_ref, v_ref, o_ref):
    kb = k_ref[...].astype(BF16)
    vb = v_ref[...].astype(BF16)
    half = MEM_HEADS // 2
    items = []
    for h in range(MEM_HEADS):
        q_ref = qa_ref if h < half else qb_ref
        hs = slice((h % half) * MEM_HEAD_DIM, (h % half + 1) * MEM_HEAD_DIM)
        ms = slice(h * MEM_HEAD_DIM, (h + 1) * MEM_HEAD_DIM)
        items.append((q_ref[:, hs].astype(BF16), kb[:, ms], vb[:, ms], slice(None), ms))
    _mem_attend(items, o_ref)


def _mem_prompt(p_att, n_b, t_len, mem_kv):
    m_tok = mem_kv.shape[0] // n_b
    tq = math.gcd(t_len, 512)
    nq = t_len // tq
    half_w = MEM_WIDTH // 2
    qcol = (SWA_Q_WIDTH + 2 * SWA_KV_WIDTH) // half_w
    return pl.pallas_call(
        _mem_prompt_kernel,
        grid=(n_b, nq),
        in_specs=[
            pl.BlockSpec((tq, half_w), lambda b, n: (b * nq + n, qcol)),
            pl.BlockSpec((tq, half_w), lambda b, n: (b * nq + n, qcol + 1)),
            pl.BlockSpec((m_tok, MEM_WIDTH), lambda b, n: (b, 0)),
            pl.BlockSpec((m_tok, MEM_WIDTH), lambda b, n: (b, 1)),
        ],
        out_specs=pl.BlockSpec((tq, MEM_WIDTH), lambda b, n: (b * nq + n, 0)),
        out_shape=jax.ShapeDtypeStruct((n_b * t_len, MEM_WIDTH), BF16),
        compiler_params=_params("arbitrary", "arbitrary"),
        name="mem_prompt",
    )(p_att, p_att, mem_kv, mem_kv)


def _mem_step_kernel(qa_ref, qb_ref, k_ref, v_ref, o_ref, *, n_seq, t_len):
    half = MEM_HEADS // 2
    m_tok = k_ref.shape[1] // MEM_HEADS
    items = []
    for s in range(n_seq):
        rs = slice(s * t_len, (s + 1) * t_len)
        for h in range(MEM_HEADS):
            q_ref = qa_ref if h < half else qb_ref
            hs = slice((h % half) * MEM_HEAD_DIM, (h % half + 1) * MEM_HEAD_DIM)
            ms = slice(h * MEM_HEAD_DIM, (h + 1) * MEM_HEAD_DIM)
            hrows = pl.ds(h, m_tok, stride=MEM_HEADS)
            items.append((q_ref[rs, hs].astype(BF16), k_ref[s, hrows, :].astype(BF16),
                          v_ref[s, hrows, :].astype(BF16), rs, ms))
    _mem_attend(items, o_ref)


def _mem_step(p_att, row0, n_b, t_len, cache_k, cache_v):
    m_tok = cache_k.shape[1]
    n_seq = max(1, 64 // t_len)
    while n_b % n_seq:
        n_seq //= 2
    rows = n_seq * t_len
    assert rows % 16 == 0 and row0 % rows == 0
    blk0 = row0 // rows
    half_w = MEM_WIDTH // 2
    qcol = (SWA_Q_WIDTH + 2 * SWA_KV_WIDTH) // half_w
    cache_block = (n_seq, m_tok * MEM_HEADS, MEM_HEAD_DIM)
    ck = cache_k.reshape(n_b, m_tok * MEM_HEADS, MEM_HEAD_DIM)
    cv = cache_v.reshape(n_b, m_tok * MEM_HEADS, MEM_HEAD_DIM)
    return pl.pallas_call(
        functools.partial(_mem_step_kernel, n_seq=n_seq, t_len=t_len),
        grid=(n_b // n_seq,),
        in_specs=[
            pl.BlockSpec((rows, half_w), lambda i: (blk0 + i, qcol)),
            pl.BlockSpec((rows, half_w), lambda i: (blk0 + i, qcol + 1)),
            pl.BlockSpec(cache_block, lambda i: (i, 0, 0)),
            pl.BlockSpec(cache_block, lambda i: (i, 0, 0)),
        ],
        out_specs=pl.BlockSpec((rows, MEM_WIDTH), lambda i: (i, 0)),
        out_shape=jax.ShapeDtypeStruct((n_b * t_len, MEM_WIDTH), BF16),
        compiler_params=_params("arbitrary"),
        name="mem_step",
    )(p_att, p_att, ck, cv)


def _merge_kernel(xn_ref, rwp_ref, rws_ref, swp_ref, sws_ref, mmp_ref, mms_ref,
                  wg0_ref, wg1_ref, wg2_ref, wb0_ref, wb1_ref, wb2_ref, o_ref,
                  g0_scr, g1_scr, g2_scr, b0_scr, b1_scr, b2_scr, *, n_p):
    i = pl.program_id(1)

    @pl.when(i == 0)
    def _():
        for src, dst in ((wg0_ref, g0_scr), (wg1_ref, g1_scr), (wg2_ref, g2_scr),
                         (wb0_ref, b0_scr), (wb1_ref, b1_scr), (wb2_ref, b2_scr)):
            dst[...] = src[...].astype(BF16)

    def run(o_rw_ref, o_sw_ref, o_mm_ref):
        xn = xn_ref[...]
        acc = None
        for g_scr, b_scr, br_ref in ((g0_scr, b0_scr, o_rw_ref), (g1_scr, b1_scr, o_sw_ref),
                                     (g2_scr, b2_scr, o_mm_ref)):
            gate = _sigmoid(jnp.dot(xn, g_scr[...], preferred_element_type=F32))
            t = gate * jnp.dot(br_ref[...], b_scr[...], preferred_element_type=F32)
            acc = t if acc is None else acc + t
        o_ref[...] = acc.astype(o_ref.dtype)

    @pl.when(i < n_p)
    def _():
        run(rwp_ref, swp_ref, mmp_ref)

    @pl.when(i >= n_p)
    def _():
        run(rws_ref, sws_ref, mms_ref)


def _merge(xn, o_p, o_s, w_in, gate_col0, w_br, n_rows_p):
    m, d = xn.shape
    tn = MXU_COLS
    tm = _row_tile(n_rows_p, m - n_rows_p, ROW_TILE)
    n_p = n_rows_p // tm
    n_i = m // tm
    assert gate_col0 % tn == 0 and d % tn == 0
    g0 = gate_col0 // tn
    nj = d // tn
    p_spec = lambda w: pl.BlockSpec((tm, w), lambda j, i: (jnp.minimum(i, n_p - 1), 0))
    s_spec = lambda w: pl.BlockSpec((tm, w), lambda j, i: (jnp.maximum(i - n_p, 0), 0))
    in_specs = [pl.BlockSpec((tm, d), lambda j, i: (i, 0))]
    args = [xn]
    for op, os_ in zip(o_p, o_s):
        in_specs += [p_spec(op.shape[1]), s_spec(os_.shape[1])]
        args += [op, os_]
    for br in range(N_BRANCH):
        in_specs.append(pl.BlockSpec((d, tn), lambda j, i, br=br: (0, g0 + br * nj + j)))
        args.append(w_in)
    for w in w_br:
        in_specs.append(pl.BlockSpec((w.shape[0], tn), lambda j, i: (0, j)))
        args.append(w)
    scratch = [pltpu.VMEM((d, tn), BF16) for _ in range(N_BRANCH)]
    scratch += [pltpu.VMEM((w.shape[0], tn), BF16) for w in w_br]
    return pl.pallas_call(
        functools.partial(_merge_kernel, n_p=n_p),
        grid=(nj, n_i),
        in_specs=in_specs,
        out_specs=pl.BlockSpec((tm, tn), lambda j, i: (i, j)),
        out_shape=jax.ShapeDtypeStruct((m, d), BF16),
        scratch_shapes=scratch,
        compiler_params=_params("arbitrary", "arbitrary"),
        name="gated_merge",
    )(*args)


def _cast_kernel(x_ref, o_ref):
    o_ref[...] = x_ref[...].astype(o_ref.dtype)


def _to_bf16(w, name):
    r, c = w.shape
    tr = math.gcd(r, 512)
    return pl.pallas_call(
        _cast_kernel,
        grid=(r // tr,),
        in_specs=[pl.BlockSpec((tr, c), lambda i: (i, 0))],
        out_specs=pl.BlockSpec((tr, c), lambda i: (i, 0)),
        out_shape=jax.ShapeDtypeStruct((r, c), BF16),
        compiler_params=_params("arbitrary"),
        name=name,
    )(w)


def _wo_kernel(m_ref, w_ref, xp_ref, xs_ref, g1_ref, g2_ref, h_ref, hn_ref, *, n_p):
    i = pl.program_id(0)
    tm = m_ref.shape[0]
    halves = [slice(0, tm // 2), slice(tm // 2, tm)] if tm % 32 == 0 else [slice(0, tm)]

    def run(x_ref):
        for rs in halves:
            f = jnp.dot(m_ref[rs, :], w_ref[...], preferred_element_type=F32)
            h = x_ref[rs, :] + _rms(f, g1_ref[...])
            h_ref[rs, :] = h
            hn_ref[rs, :] = _rms(h, g2_ref[...]).astype(hn_ref.dtype)

    @pl.when(i < n_p)
    def _():
        run(xp_ref)

    @pl.when(i >= n_p)
    def _():
        run(xs_ref)


def _wo_block(merged, w_o, xp, xs, g_post, g_pre_ffn):
    m, d = merged.shape
    n_rows_p = xp.shape[0]
    tm = _row_tile(n_rows_p, m - n_rows_p, NORM_ROW_TILE)
    n_p = n_rows_p // tm
    return pl.pallas_call(
        functools.partial(_wo_kernel, n_p=n_p),
        grid=(m // tm,),
        in_specs=[
            pl.BlockSpec((tm, d), lambda i: (i, 0)),
            pl.BlockSpec((d, d), lambda i: (0, 0), pipeline_mode=pl.Buffered(1)),
            pl.BlockSpec((tm, d), lambda i: (jnp.minimum(i, n_p - 1), 0)),
            pl.BlockSpec((tm, d), lambda i: (jnp.maximum(i - n_p, 0), 0)),
            pl.BlockSpec((1, d), lambda i: (0, 0)),
            pl.BlockSpec((1, d), lambda i: (0, 0)),
        ],
        out_specs=[pl.BlockSpec((tm, d), lambda i: (i, 0)), pl.BlockSpec((tm, d), lambda i: (i, 0))],
        out_shape=[jax.ShapeDtypeStruct((m, d), F32), jax.ShapeDtypeStruct((m, d), BF16)],
        compiler_params=_params("arbitrary"),
        name="wo_residual",
    )(merged, _to_bf16(w_o, "cast_w_o"), xp, xs, g_post.reshape(1, d), g_pre_ffn.reshape(1, d))


def _gelu_tanh(x):
    return 0.5 * x * (1.0 + jnp.tanh(math.sqrt(2.0 / math.pi) * (x + 0.044715 * (x * x * x))))


def _ffn_up_kernel(hp_ref, hs_ref, wg_ref, wv_ref, cw_ref, cb_ref, st_ref, ap_ref, as_ref, cp_ref, cs_ref,
                   wgb_scr, wvb_scr, *, n_p, t_s):
    i = pl.program_id(1)

    @pl.when(i == 0)
    def _():
        wgb_scr[...] = wg_ref[...].astype(BF16)
        wvb_scr[...] = wv_ref[...].astype(BF16)

    tn = wgb_scr.shape[1]
    subs = [slice(c0, c0 + MXU_COLS) for c0 in range(0, tn, MXU_COLS)] if tn % MXU_COLS == 0 else [slice(0, tn)]

    def row_pieces(rows):
        size = NORM_ROW_TILE if rows % NORM_ROW_TILE == 0 and NORM_ROW_TILE % (8 * t_s) == 0 else rows
        return [(r0, r0 + size) for r0 in range(0, rows, size)]

    def conv_glu(cs, zg, zm1, zm2, zv):
        conv = cb_ref[:, cs] + cw_ref[0:1, cs] * zm2 + cw_ref[1:2, cs] * zm1 + cw_ref[2:3, cs] * zg
        return _gelu_tanh(conv) * zv

    def run(h_ref, epilogue):
        items = [(pc, cs) for cs in subs for pc in row_pieces(h_ref.shape[0])]

        def gate_val(pc, cs):
            hn = h_ref[slice(*pc), :]
            return (jnp.dot(hn, wgb_scr[:, cs], preferred_element_type=F32),
                    jnp.dot(hn, wvb_scr[:, cs], preferred_element_type=F32))

        z = {items[0]: gate_val(*items[0])}
        carry = None
        for n, (pc, cs) in enumerate(items):
            if n + 1 < len(items):
                z[items[n + 1]] = gate_val(*items[n + 1])
            zg, zv = z.pop((pc, cs))
            carry = epilogue(pc, cs, zg, zv, carry if pc[0] else None)

    def prompt_epilogue(pc, cs, zg, zv, carry):
        tr = pc[1] - pc[0]
        p2, p1 = carry if carry is not None else (jnp.zeros((1, zg.shape[1]), F32),) * 2
        row = lax.broadcasted_iota(jnp.int32, (tr, 1), 0)
        zm1 = jnp.where(row == 0, p1, pltpu.roll(zg, 1, axis=0))
        zm2 = jnp.where(row == 0, p2, jnp.where(row == 1, p1, pltpu.roll(zg, 2, axis=0)))
        ap_ref[slice(*pc), cs] = conv_glu(cs, zg, zm1, zm2, zv).astype(ap_ref.dtype)
        if pc[1] == ap_ref.shape[0]:
            cp_ref[0, :, cs] = zg[tr - 2:tr, :]
        return zg[tr - 2:tr - 1, :], zg[tr - 1:tr, :]

    def sample_epilogue(pc, cs, zg, zv, carry):
        tr = pc[1] - pc[0]
        nq = tr // t_s
        qs = slice(pc[0] // t_s, pc[1] // t_s)
        tc = zg.shape[1]
        t = lax.broadcasted_iota(jnp.int32, (1, t_s, 1), 1)
        z3 = zg.reshape(nq, t_s, tc)
        s1 = st_ref[qs, 1:2, cs]
        s0 = st_ref[qs, 0:1, cs]
        zm1 = jnp.where(t == 0, s1, pltpu.roll(z3, 1, axis=1))
        zm2 = jnp.where(t == 0, s0, jnp.where(t == 1, s1, pltpu.roll(z3, 2, axis=1)))
        act = conv_glu(cs, z3, zm1, zm2, zv.reshape(nq, t_s, tc))
        as_ref[slice(*pc), cs] = act.reshape(tr, tc).astype(as_ref.dtype)
        cs_ref[qs, :, cs] = z3[:, t_s - 2:t_s, :]
        return None

    @pl.when(i < n_p)
    def _():
        run(hp_ref, prompt_epilogue)

    @pl.when(i >= n_p)
    def _():
        run(hs_ref, sample_epilogue)


def _ffn_up(hn, w_up, conv_w, conv_b, conv_state, n_b_p, t_p, n_b_s, t_s):
    m, d = hn.shape
    d_ff = conv_b.shape[0]
    n_rows_p = n_b_p * t_p
    n_rows_s = n_b_s * t_s
    assert m == n_rows_p + n_rows_s and n_rows_p % n_rows_s == 0 and t_s % 8 == 0 and t_s >= 2
    assert t_p % 16 == 0 and n_rows_s % 16 == 0
    tn = math.gcd(d_ff, 512)
    nj = d_ff // tn
    last = n_b_p - 1
    s_blk = n_rows_p // n_rows_s
    return pl.pallas_call(
        functools.partial(_ffn_up_kernel, n_p=n_b_p, t_s=t_s),
        grid=(nj, n_b_p + 1),
        in_specs=[
            pl.BlockSpec((t_p, d), lambda j, i: (jnp.minimum(i, last), 0)),
            pl.BlockSpec((n_rows_s, d), lambda j, i: (s_blk, 0)),
            pl.BlockSpec((d, tn), lambda j, i: (0, j)),
            pl.BlockSpec((d, tn), lambda j, i: (0, nj + j)),
            pl.BlockSpec((CONV_W, tn), lambda j, i: (0, j)),
            pl.BlockSpec((1, tn), lambda j, i: (0, j)),
            pl.BlockSpec((n_b_s, CONV_W - 1, tn), lambda j, i: (0, 0, j)),
        ],
        out_specs=[
            pl.BlockSpec((t_p, tn), lambda j, i: (jnp.minimum(i, last), j)),
            pl.BlockSpec((n_rows_s, tn), lambda j, i: (0, j)),
            pl.BlockSpec((1, CONV_W - 1, tn), lambda j, i: (jnp.minimum(i, last), 0, j)),
            pl.BlockSpec((n_b_s, CONV_W - 1, tn), lambda j, i: (0, 0, j)),
        ],
        out_shape=[
            jax.ShapeDtypeStruct((n_rows_p, d_ff), BF16),
            jax.ShapeDtypeStruct((n_rows_s, d_ff), BF16),
            jax.ShapeDtypeStruct((n_b_p, CONV_W - 1, d_ff), F32),
            jax.ShapeDtypeStruct((n_b_s, CONV_W - 1, d_ff), F32),
        ],
        scratch_shapes=[pltpu.VMEM((d, tn), BF16), pltpu.VMEM((d, tn), BF16)],
        compiler_params=_params("arbitrary", "arbitrary"),
        name="ffn_up_convglu",
    )(hn, hn, w_up, w_up, conv_w, conv_b.reshape(1, d_ff), conv_state)


def _ffn_down_kernel(a_ref, w_ref, h_ref, g_ref, y_ref):
    kk = pl.program_id(1)

    @pl.when(kk == 0)
    def _():
        y_ref[...] = jnp.zeros(y_ref.shape, F32)

    last = kk == pl.num_programs(1) - 1

    @pl.when(jnp.logical_not(last))
    def _():
        y_ref[...] += jnp.dot(a_ref[...], w_ref[...].astype(BF16), preferred_element_type=F32)

    @pl.when(last)
    def _():
        tm = y_ref.shape[0]
        n_pieces = 4 if tm % 64 == 0 else 1
        tr = tm // n_pieces
        wb = w_ref[...].astype(BF16)
        for r in range(n_pieces):
            rs = slice(r * tr, (r + 1) * tr)
            f = y_ref[rs, :] + jnp.dot(a_ref[rs, :], wb, preferred_element_type=F32)
            y_ref[rs, :] = h_ref[rs, :] + _rms(f, g_ref[...])


def _ffn_down(act, w_down, h, g, row0):
    d_ff, d = w_down.shape
    n_rows = act.shape[0]
    tm = math.gcd(math.gcd(n_rows, row0) if row0 else n_rows, ROW_TILE)
    tk = math.gcd(d_ff, 512)
    i0 = row0 // tm
    return pl.pallas_call(
        _ffn_down_kernel,
        grid=(n_rows // tm, d_ff // tk),
        in_specs=[
            pl.BlockSpec((tm, tk), lambda i, k: (i, k)),
            pl.BlockSpec((tk, d), lambda i, k: (k, 0)),
            pl.BlockSpec((tm, d), lambda i, k: (i0 + i, 0)),
            pl.BlockSpec((1, d), lambda i, k: (0, 0)),
        ],
        out_specs=pl.BlockSpec((tm, d), lambda i, k: (i, 0)),
        out_shape=jax.ShapeDtypeStruct((n_rows, d), F32),
        compiler_params=_params("arbitrary", "arbitrary"),
        name="ffn_down",
    )(act, w_down, h, g.reshape(1, d))


def kernel(x_prompt, x_sample, cache_swa_k, cache_swa_v, cache_mem_k, cache_mem_v, state_rwkv_shift, state_rwkv_wkv, state_ffn_conv, mem_prompt, g_pre_mix, w_in, mu_rwkv, w_decay_up, w0_decay, w_a_up, a0, w_gate_up, k_k, k_a, r_k, ln_x_w, ln_x_b, swa_sinks, g_mem, w_mem_kv, w_br_rwkv, w_br_swa, w_br_mem, w_o, g_post_mix, g_pre_ffn, w_ffn_up, conv_w, conv_b, w_ffn_down, g_post_ffn):
    b_p, t_p, d = x_prompt.shape
    b_s, t_s, _ = x_sample.shape
    n_p, n_s = b_p * t_p, b_s * t_s
    m_tok = mem_prompt.shape[1]
    rw_prm = dict(mu_rwkv=mu_rwkv, w_decay_up=w_decay_up, w0_decay=w0_decay, w_a_up=w_a_up, a0=a0,
                  w_gate_up=w_gate_up, k_k=k_k, k_a=k_a, r_k=r_k, ln_x_w=ln_x_w, ln_x_b=ln_x_b)
    xp = x_prompt.reshape(n_p, d)
    xs = x_sample.reshape(n_s, d)

    xn, p_att = _norm_matmul(xp, xs, g_pre_mix, w_in, RW_PROJ, ATT_WIDTH, "norm_proj_attn")
    p_rw = _matmul(xn, w_in, 0, RW_PROJ, RW_PROJ // 2, ROW_TILE, "proj_rwkv")

    _, mem_kv = _norm_matmul(mem_prompt.reshape(b_p * m_tok, d), None, g_mem, w_mem_kv,
                             0, 2 * MEM_WIDTH, "norm_proj_mem_kv")

    o_rw_p, shift_p, wkv_p = _rwkv(p_rw, 0, b_p, t_p, None, None, rw_prm, RWKV_PARTS)
    o_rw_s, shift_s, wkv_s = _rwkv(p_rw, n_p, b_s, t_s, state_rwkv_shift, state_rwkv_wkv, rw_prm, RWKV_PARTS)

    o_sw_p = _swa_prompt(p_att, b_p, t_p, swa_sinks)
    o_sw_s, swa_k_s, swa_v_s = _swa_step(p_att, n_p, b_s, t_s, cache_swa_k, cache_swa_v, swa_sinks)

    o_mm_p = _mem_prompt(p_att, b_p, t_p, mem_kv)
    o_mm_s = _mem_step(p_att, n_p, b_s, t_s, cache_mem_k, cache_mem_v)

    merged = _merge(xn, (o_rw_p, o_sw_p, o_mm_p), (o_rw_s, o_sw_s, o_mm_s), w_in,
                    RW_PROJ + ATT_WIDTH, (w_br_rwkv, w_br_swa, w_br_mem), n_p)
    h, hn = _wo_block(merged, w_o, xp, xs, g_post_mix, g_pre_ffn)
    act_p, act_s, conv_p, conv_s = _ffn_up(hn, w_ffn_up, conv_w, conv_b, state_ffn_conv, b_p, t_p, b_s, t_s)
    y_p = _ffn_down(act_p, w_ffn_down, h, g_post_ffn, 0)
    y_s = _ffn_down(act_s, w_ffn_down, h, g_post_ffn, n_p)

    keep = min(WINDOW, t_p)
    kv_p = jnp.stack([lax.slice(p_att, ((b + 1) * t_p - keep, SWA_Q_WIDTH),
                                ((b + 1) * t_p, SWA_Q_WIDTH + 2 * SWA_KV_WIDTH)) for b in range(b_p)])
    swa_k_p = kv_p[..., :SWA_KV_WIDTH].reshape(b_p, keep, SWA_KV_HEADS, SWA_HEAD_DIM)
    swa_v_p = kv_p[..., SWA_KV_WIDTH:].reshape(b_p, keep, SWA_KV_HEADS, SWA_HEAD_DIM)
    mem_k_p = mem_kv[:, :MEM_WIDTH].reshape(b_p, m_tok, MEM_HEADS, MEM_HEAD_DIM)
    mem_v_p = mem_kv[:, MEM_WIDTH:].reshape(b_p, m_tok, MEM_HEADS, MEM_HEAD_DIM)
    return (y_p.reshape(b_p, t_p, d), y_s.reshape(b_s, t_s, d), swa_k_p, swa_v_p, mem_k_p, mem_v_p,
            shift_p, wkv_p, conv_p, swa_k_s, swa_v_s, shift_s, wkv_s, conv_s)
```

```python
import functools
import math

import jax
import jax.numpy as jnp
from jax import lax
from jax.experimental import pallas as pl
from jax.experimental.pallas import tpu as pltpu

F32 = jnp.float32
BF16 = jnp.bfloat16

RW_HEADS = 12
RW_HEAD_DIM = 64
RW_WIDTH = RW_HEADS * RW_HEAD_DIM
RW_DECAY_RANK = 64
RW_A_RANK = 64
RW_GATE_RANK = 128
RW_PROJ = 3 * RW_WIDTH + RW_DECAY_RANK + RW_A_RANK + RW_GATE_RANK
RW_GN_EPS = 6.4e-4
RW_PAIRS = RW_HEADS // 2
SWA_Q_HEADS = 12
SWA_KV_HEADS = 4
SWA_GROUP = SWA_Q_HEADS // SWA_KV_HEADS
SWA_HEAD_DIM = 64
SWA_Q_WIDTH = SWA_Q_HEADS * SWA_HEAD_DIM
SWA_KV_WIDTH = SWA_KV_HEADS * SWA_HEAD_DIM
WINDOW = 128
MEM_HEADS = 4
MEM_HEAD_DIM = 128
MEM_WIDTH = MEM_HEADS * MEM_HEAD_DIM
N_BRANCH = 3
CONV_W = 3
NORM_EPS = 1e-6
ATT_WIDTH = SWA_Q_WIDTH + 2 * SWA_KV_WIDTH + MEM_WIDTH
SWA_SCALE = SWA_HEAD_DIM ** -0.5
assert math.frexp(SWA_SCALE)[0] == 0.5

LANES = 128
MXU_COLS = 256
VMEM_LIMIT_BYTES = 56 * 1024 * 1024

RWKV_ROWS = 64
RWKV_GROUPS = 2
RWKV_PARTS = dict(z=1, lv=1, pow=1, app=1, om=1, hg=1, out=1, st=1)
RWKV_SUM_PARTS = 1
ROW_TILE = 1024
NORM_ROW_TILE = 512


def _params(*sem):
    return pltpu.CompilerParams(dimension_semantics=sem, vmem_limit_bytes=VMEM_LIMIT_BYTES)


def _row_tile(n_p, n_s, target):
    t = math.gcd(math.gcd(n_p, n_s), target)
    assert t % 16 == 0, (n_p, n_s, target)
    return t


def _div_pow2(x, n):
    assert n & (n - 1) == 0
    return lax.shift_right_logical(x, n.bit_length() - 1)


def _mod_pow2(x, n):
    assert n & (n - 1) == 0
    return jnp.bitwise_and(x, n - 1)


def _rms(x, g):
    ms = jnp.mean(x * x, axis=-1, keepdims=True)
    return x * lax.rsqrt(ms + NORM_EPS) * g


def _norm_mm_kernel(*refs, n_p, two):
    if two:
        xp_ref, xs_ref, g_ref, w_ref, xn_ref, o_ref, wb_scr = refs
    else:
        xp_ref, g_ref, w_ref, xn_ref, o_ref, wb_scr = refs
    i = pl.program_id(0)

    @pl.when(i == 0)
    def _():
        wb_scr[...] = w_ref[...].astype(BF16)

    def run(x_ref):
        xn = _rms(x_ref[...], g_ref[...]).astype(BF16)
        xn_ref[...] = xn
        o_ref[...] = jnp.dot(xn, wb_scr[...], preferred_element_type=F32)

    if two:
        @pl.when(i < n_p)
        def _():
            run(xp_ref)

        @pl.when(i >= n_p)
        def _():
            run(xs_ref)
    else:
        run(xp_ref)


def _norm_matmul(xp, xs, g, w, col0, n_cols, name):
    n_rows_p, d = xp.shape
    two = xs is not None
    n_rows_s = xs.shape[0] if two else 0
    tm = _row_tile(n_rows_p, n_rows_s, NORM_ROW_TILE) if two else math.gcd(n_rows_p, NORM_ROW_TILE)
    n_p, n_s = n_rows_p // tm, n_rows_s // tm
    assert col0 % LANES == 0 and n_cols % LANES == 0
    once = pl.Buffered(1)
    in_specs = [pl.BlockSpec((tm, d), lambda i: (jnp.minimum(i, n_p - 1), 0))]
    args = [xp]
    if two:
        in_specs.append(pl.BlockSpec((tm, d), lambda i: (jnp.maximum(i - n_p, 0), 0)))
        args.append(xs)
    in_specs += [pl.BlockSpec((1, d), lambda i: (0, 0)),
                 pl.BlockSpec((pl.Element(d), pl.Element(n_cols)), lambda i: (0, col0), pipeline_mode=once)]
    args += [g.reshape(1, d), w]
    m = n_rows_p + n_rows_s
    return pl.pallas_call(
        functools.partial(_norm_mm_kernel, n_p=n_p, two=two),
        grid=(n_p + n_s,),
        in_specs=in_specs,
        out_specs=[pl.BlockSpec((tm, d), lambda i: (i, 0)), pl.BlockSpec((tm, n_cols), lambda i: (i, 0))],
        out_shape=[jax.ShapeDtypeStruct((m, d), BF16), jax.ShapeDtypeStruct((m, n_cols), F32)],
        scratch_shapes=[pltpu.VMEM((d, n_cols), BF16)],
        compiler_params=_params("arbitrary"),
        name=name,
    )(*args)


def _mm_kernel(a_ref, w_ref, o_ref, wb_ref):
    @pl.when(pl.program_id(1) == 0)
    def _():
        wb_ref[...] = w_ref[...].astype(BF16)

    o_ref[...] = jnp.dot(a_ref[...], wb_ref[...], preferred_element_type=F32)


def _matmul(a, w, col0, n_cols, tn, tm, name):
    m, k = a.shape
    tm = math.gcd(m, tm)
    assert n_cols % tn == 0 and col0 % LANES == 0 and tn % LANES == 0
    return pl.pallas_call(
        _mm_kernel,
        grid=(n_cols // tn, m // tm),
        in_specs=[
            pl.BlockSpec((tm, k), lambda j, i: (i, 0)),
            pl.BlockSpec((pl.Element(k), pl.Element(tn)),
                         lambda j, i: (0, pl.multiple_of(col0 + j * tn, LANES))),
        ],
        out_specs=pl.BlockSpec((tm, tn), lambda j, i: (i, j)),
        out_shape=jax.ShapeDtypeStruct((m, n_cols), F32),
        scratch_shapes=[pltpu.VMEM((k, tn), BF16)],
        compiler_params=_params("arbitrary", "arbitrary"),
        name=name,
    )(a, w)


_NN = (((1,), (0,)), ((), ()))
_NT = (((1,), (1,)), ((), ()))
_TN = (((0,), (0,)), ((), ()))


def _bf_parts(x, n):
    parts = []
    r = x
    for i in range(n):
        h = r.astype(BF16)
        parts.append(h)
        if i + 1 < n:
            r = r - h.astype(F32)
    return parts


def _dotp(a, b, dn, pa, pb):
    pa_parts = _bf_parts(a, pa)
    pb_parts = _bf_parts(b, pb)
    acc = None
    for i in range(pa):
        for j in range(pb):
            if i + j < max(pa, pb):
                t = lax.dot_general(pa_parts[i], pb_parts[j], dn, preferred_element_type=F32)
                acc = t if acc is None else acc + t
    return acc


def _softplus(x):
    return jnp.maximum(x, 0.0) + jnp.log(1.0 + jnp.exp(-jnp.abs(x)))


def _sigmoid(x):
    return 1.0 / (1.0 + jnp.exp(-x))


def _rwkv_kernel(*refs, seq_rows, n_seq, n_par, has_state, pp):
    n_in = n_par + (2 if has_state else 0)
    p_refs = refs[0:n_par]
    if has_state:
        sh0_ref, s0_ref = refs[n_par:n_in]
    (mu_ref, wda_ref, wa0_ref, wg_ref, vec_ref,
     o_ref, sh_out_ref, s_out_ref, prev_scr, st_scr) = refs[n_in:]
    c = pl.program_id(1)
    n_c = pl.num_programs(1)
    C = seq_rows
    R = seq_rows * n_seq
    W = RW_WIDTH
    HD = RW_HEAD_DIM
    groups = range(n_par)

    @pl.when(c == 0)
    def _():
        st_scr[...] = jnp.zeros(st_scr.shape, F32)
        if has_state:
            prev_scr[...] = sh0_ref[...]
            for s in range(n_par * n_seq):
                for q in range(RW_PAIRS):
                    st_scr[s, q, 0:HD, 0:HD] = s0_ref[s, 2 * q]
                    st_scr[s, q, HD:2 * HD, HD:2 * HD] = s0_ref[s, 2 * q + 1]
        else:
            prev_scr[...] = jnp.zeros(prev_scr.shape, F32)

    k_k = vec_ref[0:1, :]
    k_a = vec_ref[1:2, :]
    r_k = vec_ref[2:3, :]
    ln_w = vec_ref[3:4, :]
    ln_b = vec_ref[4:5, :]
    lane = lax.broadcasted_iota(jnp.int32, (1, LANES), 1)
    row = lax.broadcasted_iota(jnp.int32, (R, 1), 0)

    li = lax.broadcasted_iota(jnp.int32, (LANES, LANES), 0)
    lj = lax.broadcasted_iota(jnp.int32, (LANES, LANES), 1)
    same_head = _div_pow2(li, RW_HEAD_DIM) == _div_pow2(lj, RW_HEAD_DIM)
    ones_bd = jnp.where(same_head, 1.0, 0.0).astype(BF16)

    def head_sum(*zs):
        stack = jnp.concatenate([z[:, q * LANES:(q + 1) * LANES] for z in zs for q in range(RW_PAIRS)], axis=0)
        acc = None
        for part in _bf_parts(stack, RWKV_SUM_PARTS):
            t = jnp.dot(part, ones_bd, preferred_element_type=F32)
            acc = t if acc is None else acc + t
        outs = [jnp.concatenate([acc[(n * RW_PAIRS + q) * R:(n * RW_PAIRS + q + 1) * R] for q in range(RW_PAIRS)],
                                axis=1) for n in range(len(zs))]
        return outs[0] if len(zs) == 1 else outs

    ri = lax.broadcasted_iota(jnp.int32, (R, R), 0)
    rj = lax.broadcasted_iota(jnp.int32, (R, R), 1)
    same_seq = _div_pow2(ri, C) == _div_pow2(rj, C)
    causal = jnp.where(same_seq & (rj <= ri), 1.0, 0.0).astype(BF16)

    def prepare_proj(u):
        p = p_refs[u][...]
        prev = pltpu.roll(p, 1, axis=0)
        for s in range(n_seq):
            prev = jnp.where(row == s * C, prev_scr[u * n_seq + s], prev)
        for s in range(n_seq):
            prev_scr[u * n_seq + s] = p[(s + 1) * C - 1:(s + 1) * C, :]
            sh_out_ref[u * n_seq + s] = p[(s + 1) * C - 1:(s + 1) * C, :]
        x = p + mu_ref[...] * (prev - p)
        r = x[:, 0:W]
        k = x[:, W:2 * W]
        v = x[:, 2 * W:3 * W]
        wa_in = x[:, 3 * W:3 * W + LANES]
        g_in = x[:, 3 * W + LANES:3 * W + 2 * LANES]
        wa_act = jnp.where(lane < RW_DECAY_RANK, jnp.tanh(wa_in), wa_in)
        za = jnp.dot(wa_act.astype(BF16), wda_ref[...], preferred_element_type=F32) + wa0_ref[...]
        w_log = -_softplus(-za[:, 0:W]) - 0.5
        lw = -jnp.exp(w_log)
        a = _sigmoid(za[:, W:2 * W])
        g = jnp.dot(_sigmoid(g_in).astype(BF16), wg_ref[...], preferred_element_type=F32)
        kk = k * k_k
        k = k * (1.0 + (a - 1.0) * k_a)
        return dict(r=r, k=k, v=v, g=g, a=a, lw=lw, kk=kk)

    def prepare_sums(d):
        d["kk_sq"], d["rk"] = head_sum(d["kk"] * d["kk"], d["r"] * d["k"] * r_k)
        cl = None
        for part in _bf_parts(d["lw"], 2):
            t1 = jnp.dot(causal, part, preferred_element_type=F32)
            cl = t1 if cl is None else cl + t1
        d["cl"] = cl
        return d

    def prepare_factors(d):
        r, k, v, g, a, lw, cl, rk = (d[n] for n in ("r", "k", "v", "g", "a", "lw", "cl", "rk"))
        kk = d["kk"] / jnp.maximum(jnp.sqrt(d["kk_sq"]), 1e-12)
        b = kk * a
        if n_seq == 1:
            ct = cl[R - 1:R, :]
        else:
            last = cl.reshape(n_seq, C, W)[:, C - 1:C, :]
            ct = jnp.broadcast_to(last, (n_seq, C, W)).reshape(R, W)
        e_neg = jnp.exp(-cl)
        e_end = jnp.exp(ct - cl)
        return dict(v=v, g=g, bonus=rk * v, at=-kk * jnp.exp(cl - lw), rt=r * jnp.exp(cl), bt=b * e_neg,
                    kt=k * e_neg, bts=b * e_end, kts=k * e_end,
                    p_end=jnp.exp(ct))

    prep = [prepare_factors(prepare_sums(prepare_proj(u))) for u in groups]

    strict = jnp.where(same_seq & (rj < ri), 1.0, 0.0)
    incl = jnp.where(same_seq & (rj <= ri), 1.0, 0.0)
    strict2 = jnp.concatenate([strict, strict], axis=1)
    incl2 = jnp.concatenate([incl, incl], axis=1)
    m0 = jnp.where(lane < RW_HEAD_DIM, 1.0, 0.0)
    m1 = jnp.where(lane >= RW_HEAD_DIM, 1.0, 0.0)
    diag = jnp.where(li == lj, 1.0, 0.0)
    bd_mask = jnp.where(same_head, 1.0, 0.0)
    zeros_rl = jnp.zeros((R, LANES), F32)
    zeros_r2 = jnp.zeros((R, 2 * R), F32)
    n_dbl = max(1, (C - 1).bit_length())
    sls = [slice(q * LANES, (q + 1) * LANES) for q in range(RW_PAIRS)]
    items = [(u, q) for u in groups for q in range(RW_PAIRS)]

    def col(name, it):
        return prep[it[0]][name][:, sls[it[1]]]

    v0 = {it: jnp.concatenate([col("v", it), zeros_rl], axis=1) for it in items}
    def against_keys(name, it):
        x, bq, kq = col(name, it), col("bt", it), col("kt", it)
        x2 = jnp.concatenate([x * m0, x * m1], axis=0)
        y4 = jnp.concatenate([bq, kq, kq, bq], axis=0)
        return _dotp(x2, y4, _NT, pp["z"], pp["z"])

    zs = {it: against_keys("at", it) for it in items}
    zt0 = {it: zs[it][0:R, 0:2 * R] * strict2 for it in items}
    zt1 = {it: zs[it][R:2 * R, 2 * R:4 * R] * strict2 for it in items}
    half0 = jnp.where(lax.broadcasted_iota(jnp.int32, (1, 2 * R), 1) < R, 1.0, 0.0)
    half1 = 1.0 - half0
    pw = {it: jnp.concatenate([zt0[it] * half0, zt1[it] * half1], axis=0) for it in items}
    wmat = {}
    for it in items:
        lhs = jnp.concatenate([jnp.concatenate([zt0[it], zeros_r2], axis=1),
                               jnp.concatenate([zeros_r2, zt1[it]], axis=1)], axis=0)
        vq = col("v", it)
        rhs = jnp.concatenate([zeros_rl, vq, vq, zeros_rl], axis=0)
        lv = _dotp(lhs, rhs, _NN, pp["lv"], pp["lv"])
        xa = col("at", it)
        wmat[it] = jnp.concatenate([lv, jnp.concatenate([xa * m0, xa * m1], axis=0)], axis=1)
    def lower_rows(a, lo):
        return jnp.concatenate([a[lo:R], a[R + lo:2 * R]], axis=0)

    def scatter_rows(base, upd, lo):
        h = R - lo
        top = [jnp.zeros((lo, upd.shape[1]), F32)] * 2 if base is None else [base[0:lo], base[R:R + lo]]
        mid = [upd[0:h], upd[h:2 * h]] if base is None else [base[lo:R] + upd[0:h], base[R + lo:2 * R] + upd[h:2 * h]]
        return jnp.concatenate([top[0], mid[0], top[1], mid[1]], axis=0)

    for step in range(n_dbl):
        lo = 1 << step
        if n_seq == 1 and lo % 8 == 0:
            wmat = {it: scatter_rows(wmat[it], _dotp(lower_rows(pw[it], lo), wmat[it], _NN, pp["app"], pp["app"]), lo)
                    for it in items}
        else:
            wmat = {it: wmat[it] + _dotp(pw[it], wmat[it], _NN, pp["app"], pp["app"]) for it in items}
        if step + 1 < n_dbl:
            if n_seq == 1 and (2 * lo) % 8 == 0:
                pw = {it: scatter_rows(None, _dotp(lower_rows(pw[it], 2 * lo), pw[it], _NN, pp["pow"], pp["pow"]),
                                       2 * lo) for it in items}
            else:
                pw = {it: _dotp(pw[it], pw[it], _NN, pp["pow"], pp["pow"]) for it in items}
    zs = {it: against_keys("rt", it) for it in items}
    rhat, o0 = {}, {}
    for it in items:
        zb0 = zs[it][0:R, 0:2 * R] * incl2
        zb1 = zs[it][R:2 * R, 2 * R:4 * R] * incl2
        lhs = jnp.concatenate([jnp.concatenate([zb0, zeros_r2], axis=1),
                               jnp.concatenate([zeros_r2, zb1], axis=1)], axis=0)
        rhs = jnp.concatenate([wmat[it][0:R], v0[it], v0[it], wmat[it][R:2 * R]], axis=0)
        om = _dotp(lhs, rhs, _NN, pp["om"], pp["om"])
        rhat[it] = col("rt", it) + om[0:R, LANES:] + om[R:2 * R, LANES:]
        o0[it] = om[0:R, 0:LANES] * m0 + om[R:2 * R, 0:LANES] * m1
    hgt = {}
    for it in items:
        bq, kq = col("bts", it), col("kts", it)
        for s in range(n_seq):
            rs = slice(s * C, (s + 1) * C)
            ys4 = jnp.concatenate([bq[rs] * m0, kq[rs] * m0, kq[rs] * m1, bq[rs] * m1], axis=0)
            r4 = jnp.concatenate([wmat[it][rs], v0[it][rs], v0[it][rs],
                                  wmat[it][R + s * C:R + (s + 1) * C]], axis=0)
            hgt[it, s] = _dotp(r4, ys4, _TN, pp["hg"], pp["hg"])
    o_pair = {}
    for it in items:
        u, q = it
        o_seq = []
        for s in range(n_seq):
            rs = slice(s * C, (s + 1) * C)
            s_old = st_scr[u * n_seq + s, q]
            ht = hgt[it, s][0:LANES] * bd_mask
            gt = hgt[it, s][LANES:] + diag * prep[u]["p_end"][s * C:s * C + 1, sls[q]]
            o_seq.append(_dotp(rhat[it][rs], s_old, _NT, pp["out"], pp["out"]) + o0[it][rs])
            st_scr[u * n_seq + s, q] = _dotp(s_old, gt, _NN, pp["st"], pp["st"]) + ht
        o_pair[it] = o_seq[0] if n_seq == 1 else jnp.concatenate(o_seq, axis=0)

    inv_n = 1.0 / RW_HEAD_DIM
    o_all = [jnp.concatenate([o_pair[u, q] for q in range(RW_PAIRS)], axis=1) for u in groups]
    sums = head_sum(*o_all)
    dev = [o_all[u] - (sums if n_par == 1 else sums[u]) * inv_n for u in groups]
    sums = head_sum(*[d * d for d in dev])
    for u in groups:
        var = (sums if n_par == 1 else sums[u]) * inv_n
        o = dev[u] * lax.rsqrt(var + RW_GN_EPS) * ln_w + ln_b
        o_ref[u] = ((o + prep[u]["bonus"]) * prep[u]["g"]).astype(o_ref.dtype)

    @pl.when(c == n_c - 1)
    def _():
        for s in range(n_par * n_seq):
            for q in range(RW_PAIRS):
                s_out_ref[s, 2 * q] = st_scr[s, q, 0:HD, 0:HD]
                s_out_ref[s, 2 * q + 1] = st_scr[s, q, HD:2 * HD, HD:2 * HD]


def _rwkv(p_rw, row0, n_b, t_len, shift0, s0, prm, pp):
    if t_len >= RWKV_ROWS:
        seq_rows, n_seq = RWKV_ROWS, 1
    else:
        seq_rows, n_seq = t_len, RWKV_ROWS // t_len
    assert t_len % seq_rows == 0 and n_b % n_seq == 0 and row0 % (seq_rows * n_seq) == 0
    rows = seq_rows * n_seq
    n_c = t_len // seq_rows
    assert n_seq == 1 or n_c == 1
    blk0 = row0 // rows
    has_state = s0 is not None
    n_grp = n_b // n_seq
    n_par = math.gcd(n_grp, RWKV_GROUPS)
    seqs = n_par * n_seq

    wda = jnp.zeros((LANES, 2 * RW_WIDTH), F32)
    wda = wda.at[0:RW_DECAY_RANK, 0:RW_WIDTH].set(prm["w_decay_up"])
    wda = wda.at[RW_DECAY_RANK:, RW_WIDTH:].set(prm["w_a_up"])
    wa0 = jnp.concatenate([prm["w0_decay"], prm["a0"]]).reshape(1, 2 * RW_WIDTH)
    vecs = jnp.zeros((8, RW_WIDTH), F32)
    for i, name in enumerate(("k_k", "k_a", "r_k", "ln_x_w", "ln_x_b")):
        vecs = vecs.at[i].set(prm[name].reshape(RW_WIDTH))

    const = lambda *shape: pl.BlockSpec(shape, lambda bi, c: (0,) * len(shape))
    in_specs = [pl.BlockSpec((rows, RW_PROJ), lambda bi, c, u=u: (blk0 + (bi * n_par + u) * n_c + c, 0))
                for u in range(n_par)]
    args = [p_rw] * n_par
    st_block = (seqs, RW_HEADS, RW_HEAD_DIM, RW_HEAD_DIM)
    if has_state:
        in_specs += [pl.BlockSpec((seqs, 1, RW_PROJ), lambda bi, c: (bi, 0, 0)),
                     pl.BlockSpec(st_block, lambda bi, c: (bi, 0, 0, 0))]
        args += [shift0.reshape(n_b, 1, RW_PROJ), s0]
    in_specs += [const(1, RW_PROJ), const(LANES, 2 * RW_WIDTH), const(1, 2 * RW_WIDTH),
                 const(RW_GATE_RANK, RW_WIDTH), const(8, RW_WIDTH)]
    args += [prm["mu_rwkv"].reshape(1, RW_PROJ), wda.astype(BF16), wa0,
             prm["w_gate_up"].astype(BF16), vecs]

    o, sh, st = pl.pallas_call(
        functools.partial(_rwkv_kernel, seq_rows=seq_rows, n_seq=n_seq, n_par=n_par, has_state=has_state, pp=pp),
        grid=(n_grp // n_par, n_c),
        in_specs=in_specs,
        out_specs=[
            pl.BlockSpec((n_par, rows, RW_WIDTH), lambda bi, c: (bi, c, 0)),
            pl.BlockSpec((seqs, 1, RW_PROJ), lambda bi, c: (bi, 0, 0)),
            pl.BlockSpec(st_block, lambda bi, c: (bi, 0, 0, 0)),
        ],
        out_shape=[
            jax.ShapeDtypeStruct((n_grp, n_c * rows, RW_WIDTH), BF16),
            jax.ShapeDtypeStruct((n_b, 1, RW_PROJ), F32),
            jax.ShapeDtypeStruct((n_b, RW_HEADS, RW_HEAD_DIM, RW_HEAD_DIM), F32),
        ],
        scratch_shapes=[pltpu.VMEM((seqs, 1, RW_PROJ), F32),
                        pltpu.VMEM((seqs, RW_PAIRS, LANES, LANES), F32)],
        compiler_params=_params("arbitrary", "arbitrary"),
        name="rwkv_state" if has_state else "rwkv_fresh",
    )(*args)
    return o.reshape(n_b * t_len, RW_WIDTH), sh.reshape(n_b, RW_PROJ), st


def _sink_probs(s_parts, sink):
    m = sink
    for s in s_parts:
        m = jnp.maximum(m, jnp.max(s, axis=-1, keepdims=True))
    den = jnp.exp(sink - m)
    es = []
    for s in s_parts:
        e = jnp.exp(s - m)
        den = den + jnp.sum(e, axis=-1, keepdims=True)
        es.append(e)
    inv = 1.0 / den
    return [(e * inv).astype(BF16) for e in es]


def _group_queries(q, hk):
    return jnp.concatenate(
        [q[:, (hk * SWA_GROUP + g) * SWA_HEAD_DIM:(hk * SWA_GROUP + g + 1) * SWA_HEAD_DIM]
         for g in range(SWA_GROUP)], axis=0).astype(BF16)


def _group_sinks(sink_ref, hk, grp):
    sink = jnp.zeros(grp.shape, F32)
    for g in range(SWA_GROUP):
        sink = jnp.where(grp == g, sink_ref[hk * SWA_GROUP + g], sink)
    return sink


def _swa_prompt_kernel(sink_ref, q_ref, kp_ref, kc_ref, vp_ref, vc_ref, o_ref, *, n_win):
    n = pl.program_id(1)
    Wn = WINDOW
    kf = jnp.concatenate([kp_ref[...], kc_ref[...]], axis=0).astype(BF16)
    vf = jnp.concatenate([vp_ref[...], vc_ref[...]], axis=0).astype(BF16)
    gi = lax.broadcasted_iota(jnp.int32, (SWA_GROUP * Wn, 2 * Wn), 0)
    i = _mod_pow2(gi, Wn)
    j = lax.broadcasted_iota(jnp.int32, (SWA_GROUP * Wn, 2 * Wn), 1)
    ok = (j > i) & (j <= i + Wn)
    first_key = jnp.where(n > 0, 0, Wn)
    ok_first = ok & (j >= first_key)
    grp = _div_pow2(lax.broadcasted_iota(jnp.int32, (SWA_GROUP * Wn, 1), 0), Wn)
    ks = [slice(hk * SWA_HEAD_DIM, (hk + 1) * SWA_HEAD_DIM) for hk in range(SWA_KV_HEADS)]
    sinks = [_group_sinks(sink_ref, hk, grp) for hk in range(SWA_KV_HEADS)]
    work = [(w, hk) for w in range(n_win) for hk in range(SWA_KV_HEADS)]
    qw = [q_ref[w * Wn:(w + 1) * Wn, :] * SWA_SCALE for w in range(n_win)]
    sc = {(w, hk): lax.dot_general(_group_queries(qw[w], hk), kf[w * Wn:(w + 2) * Wn, ks[hk]], _NT,
                                   preferred_element_type=F32) for w, hk in work}
    pr = {(w, hk): _sink_probs([jnp.where(ok_first if w == 0 else ok, sc[w, hk], -jnp.inf)], sinks[hk])[0]
          for w, hk in work}
    ov = {(w, hk): jnp.dot(pr[w, hk], vf[w * Wn:(w + 2) * Wn, ks[hk]], preferred_element_type=F32)
          for w, hk in work}
    for w, hk in work:
        for g in range(SWA_GROUP):
            h = hk * SWA_GROUP + g
            o_ref[w * Wn:(w + 1) * Wn, h * SWA_HEAD_DIM:(h + 1) * SWA_HEAD_DIM] = (
                ov[w, hk][g * Wn:(g + 1) * Wn].astype(o_ref.dtype))


def _swa_prompt(p_att, n_b, t_len, sinks):
    nb = t_len // WINDOW
    n_win = 2 if nb % 2 == 0 else 1
    ns = nb // n_win
    rows = n_win * WINDOW
    kcol = SWA_Q_WIDTH // SWA_KV_WIDTH
    vcol = kcol + 1
    cur = lambda col: (lambda b, n: (b * ns + n, col))
    prv = lambda col: (lambda b, n: (b * nb + jnp.maximum(n * n_win - 1, 0), col))
    return pl.pallas_call(
        functools.partial(_swa_prompt_kernel, n_win=n_win),
        grid=(n_b, ns),
        in_specs=[
            pl.BlockSpec(memory_space=pltpu.SMEM),
            pl.BlockSpec((rows, SWA_Q_WIDTH), cur(0)),
            pl.BlockSpec((WINDOW, SWA_KV_WIDTH), prv(kcol)),
            pl.BlockSpec((rows, SWA_KV_WIDTH), cur(kcol)),
            pl.BlockSpec((WINDOW, SWA_KV_WIDTH), prv(vcol)),
            pl.BlockSpec((rows, SWA_KV_WIDTH), cur(vcol)),
        ],
        out_specs=pl.BlockSpec((rows, SWA_Q_WIDTH), lambda b, n: (b * ns + n, 0)),
        out_shape=jax.ShapeDtypeStruct((n_b * t_len, SWA_Q_WIDTH), BF16),
        compiler_params=_params("arbitrary", "arbitrary"),
        name="swa_prompt",
    )(sinks, p_att, p_att, p_att, p_att, p_att)


def _swa_step_kernel(sink_ref, q_ref, kn_ref, vn_ref, ck_ref, cv_ref, o_ref, ok_ref, ov_ref, *, n_seq, t_len):
    T = t_len
    wb = ck_ref.shape[2]
    t_row = _mod_pow2(lax.broadcasted_iota(jnp.int32, (SWA_GROUP * T, 1), 0), T)
    grp = _div_pow2(lax.broadcasted_iota(jnp.int32, (SWA_GROUP * T, 1), 0), T)
    jc = lax.broadcasted_iota(jnp.int32, (SWA_GROUP * T, wb), 1)
    jn = lax.broadcasted_iota(jnp.int32, (SWA_GROUP * T, T), 1)
    ok_c = (t_row + wb - jc < WINDOW)
    ok_n = (jn <= t_row)
    ks = [slice(hk * SWA_HEAD_DIM, (hk + 1) * SWA_HEAD_DIM) for hk in range(SWA_KV_HEADS)]
    sinks = [_group_sinks(sink_ref, hk, grp) for hk in range(SWA_KV_HEADS)]
    work = [(s, hk) for s in range(n_seq) for hk in range(SWA_KV_HEADS)]
    rows = [slice(s * T, (s + 1) * T) for s in range(n_seq)]
    lane_w = lax.broadcasted_iota(jnp.int32, (1, wb), 1)
    pad = jnp.zeros((wb - T, SWA_KV_WIDTH), F32)
    knb, vnb, ckb, cvb = [], [], [], []
    for s in range(n_seq):
        kn = kn_ref[rows[s], :]
        vn = vn_ref[rows[s], :]
        ck = ck_ref[s]
        cv = cv_ref[s]
        kn_t = jnp.transpose(jnp.concatenate([pad, kn], axis=0))
        vn_t = jnp.transpose(jnp.concatenate([pad, vn], axis=0))
        ok_ref[s] = jnp.where(lane_w >= wb - T, kn_t, pltpu.roll(ck, wb - T, axis=1))
        ov_ref[s] = jnp.where(lane_w >= wb - T, vn_t, pltpu.roll(cv, wb - T, axis=1))
        knb.append(kn.astype(BF16))
        vnb.append(vn.astype(BF16))
        ckb.append(ck.astype(BF16))
        cvb.append(cv.astype(BF16))
    qs = [q_ref[rows[s], :] * SWA_SCALE for s in range(n_seq)]
    qh = {(s, hk): _group_queries(qs[s], hk) for s, hk in work}
    sc = {w: jnp.dot(qh[w], ckb[w[0]][ks[w[1]], :], preferred_element_type=F32) for w in work}
    sn = {w: lax.dot_general(qh[w], knb[w[0]][:, ks[w[1]]], _NT, preferred_element_type=F32) for w in work}
    pr = {w: _sink_probs([jnp.where(ok_c, sc[w], -jnp.inf), jnp.where(ok_n, sn[w], -jnp.inf)], sinks[w[1]])
          for w in work}
    ov = {w: lax.dot_general(pr[w][0], cvb[w[0]][ks[w[1]], :], _NT, preferred_element_type=F32)
          + jnp.dot(pr[w][1], vnb[w[0]][:, ks[w[1]]], preferred_element_type=F32) for w in work}
    for s, hk in work:
        for g in range(SWA_GROUP):
            h = hk * SWA_GROUP + g
            o_ref[rows[s], h * SWA_HEAD_DIM:(h + 1) * SWA_HEAD_DIM] = ov[s, hk][g * T:(g + 1) * T].astype(o_ref.dtype)


def _swa_step(p_att, row0, n_b, t_len, cache_k, cache_v, sinks):
    wb = cache_k.shape[1]
    assert wb == WINDOW and t_len % 8 == 0 and t_len < wb
    n_seq = max(1, 64 // t_len)
    while n_b % n_seq:
        n_seq //= 2
    rows = n_seq * t_len
    assert rows % 16 == 0 and row0 % rows == 0
    blk0 = row0 // rows
    kcol = SWA_Q_WIDTH // SWA_KV_WIDTH
    to_cm = lambda c: jnp.transpose(c, (0, 2, 3, 1)).reshape(n_b, SWA_KV_WIDTH, wb)
    from_cm = lambda c: jnp.transpose(c.reshape(n_b, SWA_KV_HEADS, SWA_HEAD_DIM, wb), (0, 3, 1, 2))
    ck = to_cm(cache_k)
    cv = to_cm(cache_v)
    o, nk, nv = pl.pallas_call(
        functools.partial(_swa_step_kernel, n_seq=n_seq, t_len=t_len),
        grid=(n_b // n_seq,),
        in_specs=[
            pl.BlockSpec(memory_space=pltpu.SMEM),
            pl.BlockSpec((rows, SWA_Q_WIDTH), lambda i: (blk0 + i, 0)),
            pl.BlockSpec((rows, SWA_KV_WIDTH), lambda i: (blk0 + i, kcol)),
            pl.BlockSpec((rows, SWA_KV_WIDTH), lambda i: (blk0 + i, kcol + 1)),
            pl.BlockSpec((n_seq, SWA_KV_WIDTH, wb), lambda i: (i, 0, 0)),
            pl.BlockSpec((n_seq, SWA_KV_WIDTH, wb), lambda i: (i, 0, 0)),
        ],
        out_specs=[
            pl.BlockSpec((rows, SWA_Q_WIDTH), lambda i: (i, 0)),
            pl.BlockSpec((n_seq, SWA_KV_WIDTH, wb), lambda i: (i, 0, 0)),
            pl.BlockSpec((n_seq, SWA_KV_WIDTH, wb), lambda i: (i, 0, 0)),
        ],
        out_shape=[
            jax.ShapeDtypeStruct((n_b * t_len, SWA_Q_WIDTH), BF16),
            jax.ShapeDtypeStruct((n_b, SWA_KV_WIDTH, wb), F32),
            jax.ShapeDtypeStruct((n_b, SWA_KV_WIDTH, wb), F32),
        ],
        compiler_params=_params("arbitrary"),
        name="swa_step",
    )(sinks, p_att, p_att, p_att, ck, cv)
    return o, from_cm(nk), from_cm(nv)


def _softmax(s):
    m = jnp.max(s, axis=-1, keepdims=True)
    e = jnp.exp(s - m)
    return (e / jnp.sum(e, axis=-1, keepdims=True)).astype(BF16)


def _mem_attend(items, o_ref):
    scale = MEM_HEAD_DIM ** -0.5
    sc = [lax.dot_general(q, k, _NT, preferred_element_type=F32) * scale for q, k, _, _, _ in items]
    pr = [_softmax(s) for s in sc]
    ov = [jnp.dot(p, it[2], preferred_element_type=F32) for p, it in zip(pr, items)]
    for o, it in zip(ov, items):
        o_ref[it[3], it[4]] = o.astype(o_ref.dtype)


def _mem_prompt_kernel(qa_ref, qb_ref, k_ref, v_ref, o_ref):
    kb = k_ref[...].astype(BF16)
    vb = v_ref[...].astype(BF16)
    half = MEM_HEADS // 2
    items = []
    for h in range(MEM_HEADS):
        q_ref = qa_ref if h < half else qb_ref
        hs = slice((h % half) * MEM_HEAD_DIM, (h % half + 1) * MEM_HEAD_DIM)
        ms = slice(h * MEM_HEAD_DIM, (h + 1) * MEM_HEAD_DIM)
        items.append((q_ref[:, hs].astype(BF16), kb[:, ms], vb[:, ms], slice(None), ms))
    _mem_attend(items, o_ref)


def _mem_prompt(p_att, n_b, t_len, mem_kv):
    m_tok = mem_kv.shape[0] // n_b
    tq = math.gcd(t_len, 512)
    nq = t_len // tq
    half_w = MEM_WIDTH // 2
    qcol = (SWA_Q_WIDTH + 2 * SWA_KV_WIDTH) // half_w
    return pl.pallas_call(
        _mem_prompt_kernel,
        grid=(n_b, nq),
        in_specs=[
            pl.BlockSpec((tq, half_w), lambda b, n: (b * nq + n, qcol)),
            pl.BlockSpec((tq, half_w), lambda b, n: (b * nq + n, qcol + 1)),
            pl.BlockSpec((m_tok, MEM_WIDTH), lambda b, n: (b, 0)),
            pl.BlockSpec((m_tok, MEM_WIDTH), lambda b, n: (b, 1)),
        ],
        out_specs=pl.BlockSpec((tq, MEM_WIDTH), lambda b, n: (b * nq + n, 0)),
        out_shape=jax.ShapeDtypeStruct((n_b * t_len, MEM_WIDTH), BF16),
        compiler_params=_params("arbitrary", "arbitrary"),
        name="mem_prompt",
    )(p_att, p_att, mem_kv, mem_kv)


def _mem_step_kernel(qa_ref, qb_ref, k_ref, v_ref, o_ref, *, n_seq, t_len):
    half = MEM_HEADS // 2
    m_tok = k_ref.shape[1] // MEM_HEADS
    items = []
    for s in range(n_seq):
        rs = slice(s * t_len, (s + 1) * t_len)
        for h in range(MEM_HEADS):
            q_ref = qa_ref if h < half else qb_ref
            hs = slice((h % half) * MEM_HEAD_DIM, (h % half + 1) * MEM_HEAD_DIM)
            ms = slice(h * MEM_HEAD_DIM, (h + 1) * MEM_HEAD_DIM)
            hrows = pl.ds(h, m_tok, stride=MEM_HEADS)
            items.append((q_ref[rs, hs].astype(BF16), k_ref[s, hrows, :].astype(BF16),
                          v_ref[s, hrows, :].astype(BF16), rs, ms))
    _mem_attend(items, o_ref)


def _mem_step(p_att, row0, n_b, t_len, cache_k, cache_v):
    m_tok = cache_k.shape[1]
    n_seq = max(1, 64 // t_len)
    while n_b % n_seq:
        n_seq //= 2
    rows = n_seq * t_len
    assert rows % 16 == 0 and row0 % rows == 0
    blk0 = row0 // rows
    half_w = MEM_WIDTH // 2
    qcol = (SWA_Q_WIDTH + 2 * SWA_KV_WIDTH) // half_w
    cache_block = (n_seq, m_tok * MEM_HEADS, MEM_HEAD_DIM)
    ck = cache_k.reshape(n_b, m_tok * MEM_HEADS, MEM_HEAD_DIM)
    cv = cache_v.reshape(n_b, m_tok * MEM_HEADS, MEM_HEAD_DIM)
    return pl.pallas_call(
        functools.partial(_mem_step_kernel, n_seq=n_seq, t_len=t_len),
        grid=(n_b // n_seq,),
        in_specs=[
            pl.BlockSpec((rows, half_w), lambda i: (blk0 + i, qcol)),
            pl.BlockSpec((rows, half_w), lambda i: (blk0 + i, qcol + 1)),
            pl.BlockSpec(cache_block, lambda i: (i, 0, 0)),
            pl.BlockSpec(cache_block, lambda i: (i, 0, 0)),
        ],
        out_specs=pl.BlockSpec((rows, MEM_WIDTH), lambda i: (i, 0)),
        out_shape=jax.ShapeDtypeStruct((n_b * t_len, MEM_WIDTH), BF16),
        compiler_params=_params("arbitrary"),
        name="mem_step",
    )(p_att, p_att, ck, cv)


def _merge_kernel(xn_ref, rwp_ref, rws_ref, swp_ref, sws_ref, mmp_ref, mms_ref,
                  wg0_ref, wg1_ref, wg2_ref, wb0_ref, wb1_ref, wb2_ref, o_ref,
                  g0_scr, g1_scr, g2_scr, b0_scr, b1_scr, b2_scr, *, n_p):
    i = pl.program_id(1)

    @pl.when(i == 0)
    def _():
        for src, dst in ((wg0_ref, g0_scr), (wg1_ref, g1_scr), (wg2_ref, g2_scr),
                         (wb0_ref, b0_scr), (wb1_ref, b1_scr), (wb2_ref, b2_scr)):
            dst[...] = src[...].astype(BF16)

    def run(o_rw_ref, o_sw_ref, o_mm_ref):
        xn = xn_ref[...]
        acc = None
        for g_scr, b_scr, br_ref in ((g0_scr, b0_scr, o_rw_ref), (g1_scr, b1_scr, o_sw_ref),
                                     (g2_scr, b2_scr, o_mm_ref)):
            gate = _sigmoid(jnp.dot(xn, g_scr[...], preferred_element_type=F32))
            t = gate * jnp.dot(br_ref[...], b_scr[...], preferred_element_type=F32)
            acc = t if acc is None else acc + t
        o_ref[...] = acc.astype(o_ref.dtype)

    @pl.when(i < n_p)
    def _():
        run(rwp_ref, swp_ref, mmp_ref)

    @pl.when(i >= n_p)
    def _():
        run(rws_ref, sws_ref, mms_ref)


def _merge(xn, o_p, o_s, w_in, gate_col0, w_br, n_rows_p):
    m, d = xn.shape
    tn = MXU_COLS
    tm = _row_tile(n_rows_p, m - n_rows_p, ROW_TILE)
    n_p = n_rows_p // tm
    n_i = m // tm
    assert gate_col0 % tn == 0 and d % tn == 0
    g0 = gate_col0 // tn
    nj = d // tn
    p_spec = lambda w: pl.BlockSpec((tm, w), lambda j, i: (jnp.minimum(i, n_p - 1), 0))
    s_spec = lambda w: pl.BlockSpec((tm, w), lambda j, i: (jnp.maximum(i - n_p, 0), 0))
    in_specs = [pl.BlockSpec((tm, d), lambda j, i: (i, 0))]
    args = [xn]
    for op, os_ in zip(o_p, o_s):
        in_specs += [p_spec(op.shape[1]), s_spec(os_.shape[1])]
        args += [op, os_]
    for br in range(N_BRANCH):
        in_specs.append(pl.BlockSpec((d, tn), lambda j, i, br=br: (0, g0 + br * nj + j)))
        args.append(w_in)
    for w in w_br:
        in_specs.append(pl.BlockSpec((w.shape[0], tn), lambda j, i: (0, j)))
        args.append(w)
    scratch = [pltpu.VMEM((d, tn), BF16) for _ in range(N_BRANCH)]
    scratch += [pltpu.VMEM((w.shape[0], tn), BF16) for w in w_br]
    return pl.pallas_call(
        functools.partial(_merge_kernel, n_p=n_p),
        grid=(nj, n_i),
        in_specs=in_specs,
        out_specs=pl.BlockSpec((tm, tn), lambda j, i: (i, j)),
        out_shape=jax.ShapeDtypeStruct((m, d), BF16),
        scratch_shapes=scratch,
        compiler_params=_params("arbitrary", "arbitrary"),
        name="gated_merge",
    )(*args)


def _cast_kernel(x_ref, o_ref):
    o_ref[...] = x_ref[...].astype(o_ref.dtype)


def _to_bf16(w, name):
    r, c = w.shape
    tr = math.gcd(r, 512)
    return pl.pallas_call(
        _cast_kernel,
        grid=(r // tr,),
        in_specs=[pl.BlockSpec((tr, c), lambda i: (i, 0))],
        out_specs=pl.BlockSpec((tr, c), lambda i: (i, 0)),
        out_shape=jax.ShapeDtypeStruct((r, c), BF16),
        compiler_params=_params("arbitrary"),
        name=name,
    )(w)


def _wo_kernel(m_ref, w_ref, xp_ref, xs_ref, g1_ref, g2_ref, h_ref, hn_ref, *, n_p):
    i = pl.program_id(0)
    tm = m_ref.shape[0]
    halves = [slice(0, tm // 2), slice(tm // 2, tm)] if tm % 32 == 0 else [slice(0, tm)]

    def run(x_ref):
        for rs in halves:
            f = jnp.dot(m_ref[rs, :], w_ref[...], preferred_element_type=F32)
            h = x_ref[rs, :] + _rms(f, g1_ref[...])
            h_ref[rs, :] = h
            hn_ref[rs, :] = _rms(h, g2_ref[...]).astype(hn_ref.dtype)

    @pl.when(i < n_p)
    def _():
        run(xp_ref)

    @pl.when(i >= n_p)
    def _():
        run(xs_ref)


def _wo_block(merged, w_o, xp, xs, g_post, g_pre_ffn):
    m, d = merged.shape
    n_rows_p = xp.shape[0]
    tm = _row_tile(n_rows_p, m - n_rows_p, NORM_ROW_TILE)
    n_p = n_rows_p // tm
    return pl.pallas_call(
        functools.partial(_wo_kernel, n_p=n_p),
        grid=(m // tm,),
        in_specs=[
            pl.BlockSpec((tm, d), lambda i: (i, 0)),
            pl.BlockSpec((d, d), lambda i: (0, 0), pipeline_mode=pl.Buffered(1)),
            pl.BlockSpec((tm, d), lambda i: (jnp.minimum(i, n_p - 1), 0)),
            pl.BlockSpec((tm, d), lambda i: (jnp.maximum(i - n_p, 0), 0)),
            pl.BlockSpec((1, d), lambda i: (0, 0)),
            pl.BlockSpec((1, d), lambda i: (0, 0)),
        ],
        out_specs=[pl.BlockSpec((tm, d), lambda i: (i, 0)), pl.BlockSpec((tm, d), lambda i: (i, 0))],
        out_shape=[jax.ShapeDtypeStruct((m, d), F32), jax.ShapeDtypeStruct((m, d), BF16)],
        compiler_params=_params("arbitrary"),
        name="wo_residual",
    )(merged, _to_bf16(w_o, "cast_w_o"), xp, xs, g_post.reshape(1, d), g_pre_ffn.reshape(1, d))


def _gelu_tanh(x):
    return 0.5 * x * (1.0 + jnp.tanh(math.sqrt(2.0 / math.pi) * (x + 0.044715 * (x * x * x))))


def _ffn_up_kernel(hp_ref, hs_ref, wg_ref, wv_ref, cw_ref, cb_ref, st_ref, ap_ref, as_ref, cp_ref, cs_ref,
                   wgb_scr, wvb_scr, *, n_p, t_s):
    i = pl.program_id(1)

    @pl.when(i == 0)
    def _():
        wgb_scr[...] = wg_ref[...].astype(BF16)
        wvb_scr[...] = wv_ref[...].astype(BF16)

    tn = wgb_scr.shape[1]
    subs = [slice(c0, c0 + MXU_COLS) for c0 in range(0, tn, MXU_COLS)] if tn % MXU_COLS == 0 else [slice(0, tn)]

    def row_pieces(rows):
        size = NORM_ROW_TILE if rows % NORM_ROW_TILE == 0 and NORM_ROW_TILE % (8 * t_s) == 0 else rows
        return [(r0, r0 + size) for r0 in range(0, rows, size)]

    def conv_glu(cs, zg, zm1, zm2, zv):
        conv = cb_ref[:, cs] + cw_ref[0:1, cs] * zm2 + cw_ref[1:2, cs] * zm1 + cw_ref[2:3, cs] * zg
        return _gelu_tanh(conv) * zv

    def run(h_ref, epilogue):
        items = [(pc, cs) for cs in subs for pc in row_pieces(h_ref.shape[0])]

        def gate_val(pc, cs):
            hn = h_ref[slice(*pc), :]
            return (jnp.dot(hn, wgb_scr[:, cs], preferred_element_type=F32),
                    jnp.dot(hn, wvb_scr[:, cs], preferred_element_type=F32))

        z = {items[0]: gate_val(*items[0])}
        carry = None
        for n, (pc, cs) in enumerate(items):
            if n + 1 < len(items):
                z[items[n + 1]] = gate_val(*items[n + 1])
            zg, zv = z.pop((pc, cs))
            carry = epilogue(pc, cs, zg, zv, carry if pc[0] else None)

    def prompt_epilogue(pc, cs, zg, zv, carry):
        tr = pc[1] - pc[0]
        p2, p1 = carry if carry is not None else (jnp.zeros((1, zg.shape[1]), F32),) * 2
        row = lax.broadcasted_iota(jnp.int32, (tr, 1), 0)
        zm1 = jnp.where(row == 0, p1, pltpu.roll(zg, 1, axis=0))
        zm2 = jnp.where(row == 0, p2, jnp.where(row == 1, p1, pltpu.roll(zg, 2, axis=0)))
        ap_ref[slice(*pc), cs] = conv_glu(cs, zg, zm1, zm2, zv).astype(ap_ref.dtype)
        if pc[1] == ap_ref.shape[0]:
            cp_ref[0, :, cs] = zg[tr - 2:tr, :]
        return zg[tr - 2:tr - 1, :], zg[tr - 1:tr, :]

    def sample_epilogue(pc, cs, zg, zv, carry):
        tr = pc[1] - pc[0]
        nq = tr // t_s
        qs = slice(pc[0] // t_s, pc[1] // t_s)
        tc = zg.shape[1]
        t = lax.broadcasted_iota(jnp.int32, (1, t_s, 1), 1)
        z3 = zg.reshape(nq, t_s, tc)
        s1 = st_ref[qs, 1:2, cs]
        s0 = st_ref[qs, 0:1, cs]
        zm1 = jnp.where(t == 0, s1, pltpu.roll(z3, 1, axis=1))
        zm2 = jnp.where(t == 0, s0, jnp.where(t == 1, s1, pltpu.roll(z3, 2, axis=1)))
        act = conv_glu(cs, z3, zm1, zm2, zv.reshape(nq, t_s, tc))
        as_ref[slice(*pc), cs] = act.reshape(tr, tc).astype(as_ref.dtype)
        cs_ref[qs, :, cs] = z3[:, t_s - 2:t_s, :]
        return None

    @pl.when(i < n_p)
    def _():
        run(hp_ref, prompt_epilogue)

    @pl.when(i >= n_p)
    def _():
        run(hs_ref, sample_epilogue)


def _ffn_up(hn, w_up, conv_w, conv_b, conv_state, n_b_p, t_p, n_b_s, t_s):
    m, d = hn.shape
    d_ff = conv_b.shape[0]
    n_rows_p = n_b_p * t_p
    n_rows_s = n_b_s * t_s
    assert m == n_rows_p + n_rows_s and n_rows_p % n_rows_s == 0 and t_s % 8 == 0 and t_s >= 2
    assert t_p % 16 == 0 and n_rows_s % 16 == 0
    tn = math.gcd(d_ff, 512)
    nj = d_ff // tn
    last = n_b_p - 1
    s_blk = n_rows_p // n_rows_s
    return pl.pallas_call(
        functools.partial(_ffn_up_kernel, n_p=n_b_p, t_s=t_s),
        grid=(nj, n_b_p + 1),
        in_specs=[
            pl.BlockSpec((t_p, d), lambda j, i: (jnp.minimum(i, last), 0)),
            pl.BlockSpec((n_rows_s, d), lambda j, i: (s_blk, 0)),
            pl.BlockSpec((d, tn), lambda j, i: (0, j)),
            pl.BlockSpec((d, tn), lambda j, i: (0, nj + j)),
            pl.BlockSpec((CONV_W, tn), lambda j, i: (0, j)),
            pl.BlockSpec((1, tn), lambda j, i: (0, j)),
            pl.BlockSpec((n_b_s, CONV_W - 1, tn), lambda j, i: (0, 0, j)),
        ],
        out_specs=[
            pl.BlockSpec((t_p, tn), lambda j, i: (jnp.minimum(i, last), j)),
            pl.BlockSpec((n_rows_s, tn), lambda j, i: (0, j)),
            pl.BlockSpec((1, CONV_W - 1, tn), lambda j, i: (jnp.minimum(i, last), 0, j)),
            pl.BlockSpec((n_b_s, CONV_W - 1, tn), lambda j, i: (0, 0, j)),
        ],
        out_shape=[
            jax.ShapeDtypeStruct((n_rows_p, d_ff), BF16),
            jax.ShapeDtypeStruct((n_rows_s, d_ff), BF16),
            jax.ShapeDtypeStruct((n_b_p, CONV_W - 1, d_ff), F32),
            jax.ShapeDtypeStruct((n_b_s, CONV_W - 1, d_ff), F32),
        ],
        scratch_shapes=[pltpu.VMEM((d, tn), BF16), pltpu.VMEM((d, tn), BF16)],
        compiler_params=_params("arbitrary", "arbitrary"),
        name="ffn_up_convglu",
    )(hn, hn, w_up, w_up, conv_w, conv_b.reshape(1, d_ff), conv_state)


def _ffn_down_kernel(a_ref, w_ref, h_ref, g_ref, y_ref):
    kk = pl.program_id(1)

    @pl.when(kk == 0)
    def _():
        y_ref[...] = jnp.zeros(y_ref.shape, F32)

    last = kk == pl.num_programs(1) - 1

    @pl.when(jnp.logical_not(last))
    def _():
        y_ref[...] += jnp.dot(a_ref[...], w_ref[...].astype(BF16), preferred_element_type=F32)

    @pl.when(last)
    def _():
        tm = y_ref.shape[0]
        n_pieces = 4 if tm % 64 == 0 else 1
        tr = tm // n_pieces
        wb = w_ref[...].astype(BF16)
        for r in range(n_pieces):
            rs = slice(r * tr, (r + 1) * tr)
            f = y_ref[rs, :] + jnp.dot(a_ref[rs, :], wb, preferred_element_type=F32)
            y_ref[rs, :] = h_ref[rs, :] + _rms(f, g_ref[...])


def _ffn_down(act, w_down, h, g, row0):
    d_ff, d = w_down.shape
    n_rows = act.shape[0]
    tm = math.gcd(math.gcd(n_rows, row0) if row0 else n_rows, ROW_TILE)
    tk = math.gcd(d_ff, 512)
    i0 = row0 // tm
    return pl.pallas_call(
        _ffn_down_kernel,
        grid=(n_rows // tm, d_ff // tk),
        in_specs=[
            pl.BlockSpec((tm, tk), lambda i, k: (i, k)),
            pl.BlockSpec((tk, d), lambda i, k: (k, 0)),
            pl.BlockSpec((tm, d), lambda i, k: (i0 + i, 0)),
            pl.BlockSpec((1, d), lambda i, k: (0, 0)),
        ],
        out_specs=pl.BlockSpec((tm, d), lambda i, k: (i, 0)),
        out_shape=jax.ShapeDtypeStruct((n_rows, d), F32),
        compiler_params=_params("arbitrary", "arbitrary"),
        name="ffn_down",
    )(act, w_down, h, g.reshape(1, d))


def kernel(x_prompt, x_sample, cache_swa_k, cache_swa_v, cache_mem_k, cache_mem_v, state_rwkv_shift, state_rwkv_wkv, state_ffn_conv, mem_prompt, g_pre_mix, w_in, mu_rwkv, w_decay_up, w0_decay, w_a_up, a0, w_gate_up, k_k, k_a, r_k, ln_x_w, ln_x_b, swa_sinks, g_mem, w_mem_kv, w_br_rwkv, w_br_swa, w_br_mem, w_o, g_post_mix, g_pre_ffn, w_ffn_up, conv_w, conv_b, w_ffn_down, g_post_ffn):
    b_p, t_p, d = x_prompt.shape
    b_s, t_s, _ = x_sample.shape
    n_p, n_s = b_p * t_p, b_s * t_s
    m_tok = mem_prompt.shape[1]
    rw_prm = dict(mu_rwkv=mu_rwkv, w_decay_up=w_decay_up, w0_decay=w0_decay, w_a_up=w_a_up, a0=a0,
                  w_gate_up=w_gate_up, k_k=k_k, k_a=k_a, r_k=r_k, ln_x_w=ln_x_w, ln_x_b=ln_x_b)
    xp = x_prompt.reshape(n_p, d)
    xs = x_sample.reshape(n_s, d)

    xn, p_att = _norm_matmul(xp, xs, g_pre_mix, w_in, RW_PROJ, ATT_WIDTH, "norm_proj_attn")
    p_rw = _matmul(xn, w_in, 0, RW_PROJ, RW_PROJ // 2, ROW_TILE, "proj_rwkv")

    _, mem_kv = _norm_matmul(mem_prompt.reshape(b_p * m_tok, d), None, g_mem, w_mem_kv,
                             0, 2 * MEM_WIDTH, "norm_proj_mem_kv")

    o_rw_p, shift_p, wkv_p = _rwkv(p_rw, 0, b_p, t_p, None, None, rw_prm, RWKV_PARTS)
    o_rw_s, shift_s, wkv_s = _rwkv(p_rw, n_p, b_s, t_s, state_rwkv_shift, state_rwkv_wkv, rw_prm, RWKV_PARTS)

    o_sw_p = _swa_prompt(p_att, b_p, t_p, swa_sinks)
    o_sw_s, swa_k_s, swa_v_s = _swa_step(p_att, n_p, b_s, t_s, cache_swa_k, cache_swa_v, swa_sinks)

    o_mm_p = _mem_prompt(p_att, b_p, t_p, mem_kv)
    o_mm_s = _mem_step(p_att, n_p, b_s, t_s, cache_mem_k, cache_mem_v)

    merged = _merge(xn, (o_rw_p, o_sw_p, o_mm_p), (o_rw_s, o_sw_s, o_mm_s), w_in,
                    RW_PROJ + ATT_WIDTH, (w_br_rwkv, w_br_swa, w_br_mem), n_p)
    h, hn = _wo_block(merged, w_o, xp, xs, g_post_mix, g_pre_ffn)
    act_p, act_s, conv_p, conv_s = _ffn_up(hn, w_ffn_up, conv_w, conv_b, state_ffn_conv, b_p, t_p, b_s, t_s)
    y_p = _ffn_down(act_p, w_ffn_down, h, g_post_ffn, 0)
    y_s = _ffn_down(act_s, w_ffn_down, h, g_post_ffn, n_p)

    keep = min(WINDOW, t_p)
    kv_p = jnp.stack([lax.slice(p_att, ((b + 1) * t_p - keep, SWA_Q_WIDTH),
                                ((b + 1) * t_p, SWA_Q_WIDTH + 2 * SWA_KV_WIDTH)) for b in range(b_p)])
    swa_k_p = kv_p[..., :SWA_KV_WIDTH].reshape(b_p, keep, SWA_KV_HEADS, SWA_HEAD_DIM)
    swa_v_p = kv_p[..., SWA_KV_WIDTH:].reshape(b_p, keep, SWA_KV_HEADS, SWA_HEAD_DIM)
    mem_k_p = mem_kv[:, :MEM_WIDTH].reshape(b_p, m_tok, MEM_HEADS, MEM_HEAD_DIM)
    mem_v_p = mem_kv[:, MEM_WIDTH:].reshape(b_p, m_tok, MEM_HEADS, MEM_HEAD_DIM)
    return (y_p.reshape(b_p, t_p, d), y_s.reshape(b_s, t_s, d), swa_k_p, swa_v_p, mem_k_p, mem_v_p,
            shift_p, wkv_p, conv_p, swa_k_s, swa_v_s, shift_s, wkv_s, conv_s)
```

```python
import functools
import math

import jax
import jax.numpy as jnp
from jax import lax
from jax.experimental import pallas as pl
from jax.experimental.pallas import tpu as pltpu

F32 = jnp.float32
BF16 = jnp.bfloat16

RW_HEADS = 12
RW_HEAD_DIM = 64
RW_WIDTH = RW_HEADS * RW_HEAD_DIM
RW_DECAY_RANK = 64
RW_A_RANK = 64
RW_GATE_RANK = 128
RW_PROJ = 3 * RW_WIDTH + RW_DECAY_RANK + RW_A_RANK + RW_GATE_RANK
RW_GN_EPS = 6.4e-4
RW_PAIRS = RW_HEADS // 2
SWA_Q_HEADS = 12
SWA_KV_HEADS = 4
SWA_GROUP = SWA_Q_HEADS // SWA_KV_HEADS
SWA_HEAD_DIM = 64
SWA_Q_WIDTH = SWA_Q_HEADS * SWA_HEAD_DIM
SWA_KV_WIDTH = SWA_KV_HEADS * SWA_HEAD_DIM
WINDOW = 128
MEM_HEADS = 4
MEM_HEAD_DIM = 128
MEM_WIDTH = MEM_HEADS * MEM_HEAD_DIM
N_BRANCH = 3
CONV_W = 3
NORM_EPS = 1e-6
ATT_WIDTH = SWA_Q_WIDTH + 2 * SWA_KV_WIDTH + MEM_WIDTH
SWA_SCALE = SWA_HEAD_DIM ** -0.5
assert math.frexp(SWA_SCALE)[0] == 0.5

LANES = 128
MXU_COLS = 256
VMEM_LIMIT_BYTES = 56 * 1024 * 1024

RWKV_ROWS = 64
RWKV_GROUPS = 2
RWKV_PARTS = dict(z=1, lv=1, pow=1, app=1, om=1, hg=1, out=1, st=1)
RWKV_SUM_PARTS = 1
ROW_TILE = 1024
NORM_ROW_TILE = 512


def _params(*sem):
    return pltpu.CompilerParams(dimension_semantics=sem, vmem_limit_bytes=VMEM_LIMIT_BYTES)


def _row_tile(n_p, n_s, target):
    t = math.gcd(math.gcd(n_p, n_s), target)
    assert t % 16 == 0, (n_p, n_s, target)
    return t


def _div_pow2(x, n):
    assert n & (n - 1) == 0
    return lax.shift_right_logical(x, n.bit_length() - 1)


def _mod_pow2(x, n):
    assert n & (n - 1) == 0
    return jnp.bitwise_and(x, n - 1)


def _rms(x, g):
    ms = jnp.mean(x * x, axis=-1, keepdims=True)
    return x * lax.rsqrt(ms + NORM_EPS) * g


def _norm_mm_kernel(*refs, n_p, two):
    if two:
        xp_ref, xs_ref, g_ref, w_ref, xn_ref, o_ref, wb_scr = refs
    else:
        xp_ref, g_ref, w_ref, xn_ref, o_ref, wb_scr = refs
    i = pl.program_id(0)

    @pl.when(i == 0)
    def _():
        wb_scr[...] = w_ref[...].astype(BF16)

    def run(x_ref):
        xn = _rms(x_ref[...], g_ref[...]).astype(BF16)
        xn_ref[...] = xn
        o_ref[...] = jnp.dot(xn, wb_scr[...], preferred_element_type=F32)

    if two:
        @pl.when(i < n_p)
        def _():
            run(xp_ref)

        @pl.when(i >= n_p)
        def _():
            run(xs_ref)
    else:
        run(xp_ref)


def _norm_matmul(xp, xs, g, w, col0, n_cols, name):
    n_rows_p, d = xp.shape
    two = xs is not None
    n_rows_s = xs.shape[0] if two else 0
    tm = _row_tile(n_rows_p, n_rows_s, NORM_ROW_TILE) if two else math.gcd(n_rows_p, NORM_ROW_TILE)
    n_p, n_s = n_rows_p // tm, n_rows_s // tm
    assert col0 % LANES == 0 and n_cols % LANES == 0
    once = pl.Buffered(1)
    in_specs = [pl.BlockSpec((tm, d), lambda i: (jnp.minimum(i, n_p - 1), 0))]
    args = [xp]
    if two:
        in_specs.append(pl.BlockSpec((tm, d), lambda i: (jnp.maximum(i - n_p, 0), 0)))
        args.append(xs)
    in_specs += [pl.BlockSpec((1, d), lambda i: (0, 0)),
                 pl.BlockSpec((pl.Element(d), pl.Element(n_cols)), lambda i: (0, col0), pipeline_mode=once)]
    args += [g.reshape(1, d), w]
    m = n_rows_p + n_rows_s
    return pl.pallas_call(
        functools.partial(_norm_mm_kernel, n_p=n_p, two=two),
        grid=(n_p + n_s,),
        in_specs=in_specs,
        out_specs=[pl.BlockSpec((tm, d), lambda i: (i, 0)), pl.BlockSpec((tm, n_cols), lambda i: (i, 0))],
        out_shape=[jax.ShapeDtypeStruct((m, d), BF16), jax.ShapeDtypeStruct((m, n_cols), F32)],
        scratch_shapes=[pltpu.VMEM((d, n_cols), BF16)],
        compiler_params=_params("arbitrary"),
        name=name,
    )(*args)


def _mm_kernel(a_ref, w_ref, o_ref, wb_ref):
    @pl.when(pl.program_id(1) == 0)
    def _():
        wb_ref[...] = w_ref[...].astype(BF16)

    o_ref[...] = jnp.dot(a_ref[...], wb_ref[...], preferred_element_type=F32)


def _matmul(a, w, col0, n_cols, tn, tm, name):
    m, k = a.shape
    tm = math.gcd(m, tm)
    assert n_cols % tn == 0 and col0 % LANES == 0 and tn % LANES == 0
    return pl.pallas_call(
        _mm_kernel,
        grid=(n_cols // tn, m // tm),
        in_specs=[
            pl.BlockSpec((tm, k), lambda j, i: (i, 0)),
            pl.BlockSpec((pl.Element(k), pl.Element(tn)),
                         lambda j, i: (0, pl.multiple_of(col0 + j * tn, LANES))),
        ],
        out_specs=pl.BlockSpec((tm, tn), lambda j, i: (i, j)),
        out_shape=jax.ShapeDtypeStruct((m, n_cols), F32),
        scratch_shapes=[pltpu.VMEM((k, tn), BF16)],
        compiler_params=_params("arbitrary", "arbitrary"),
        name=name,
    )(a, w)


_NN = (((1,), (0,)), ((), ()))
_NT = (((1,), (1,)), ((), ()))
_TN = (((0,), (0,)), ((), ()))


def _bf_parts(x, n):
    parts = []
    r = x
    for i in range(n):
        h = r.astype(BF16)
        parts.append(h)
        if i + 1 < n:
            r = r - h.astype(F32)
    return parts


def _dotp(a, b, dn, pa, pb):
    pa_parts = _bf_parts(a, pa)
    pb_parts = _bf_parts(b, pb)
    acc = None
    for i in range(pa):
        for j in range(pb):
            if i + j < max(pa, pb):
                t = lax.dot_general(pa_parts[i], pb_parts[j], dn, preferred_element_type=F32)
                acc = t if acc is None else acc + t
    return acc


def _softplus(x):
    return jnp.maximum(x, 0.0) + jnp.log(1.0 + jnp.exp(-jnp.abs(x)))


def _sigmoid(x):
    return 1.0 / (1.0 + jnp.exp(-x))


def _rwkv_kernel(*refs, seq_rows, n_seq, n_par, has_state, pp):
    n_in = n_par + (2 if has_state else 0)
    p_refs = refs[0:n_par]
    if has_state:
        sh0_ref, s0_ref = refs[n_par:n_in]
    (mu_ref, wda_ref, wa0_ref, wg_ref, vec_ref,
     o_ref, sh_out_ref, s_out_ref, prev_scr, st_scr) = refs[n_in:]
    c = pl.program_id(1)
    n_c = pl.num_programs(1)
    C = seq_rows
    R = seq_rows * n_seq
    W = RW_WIDTH
    HD = RW_HEAD_DIM
    groups = range(n_par)

    @pl.when(c == 0)
    def _():
        st_scr[...] = jnp.zeros(st_scr.shape, F32)
        if has_state:
            prev_scr[...] = sh0_ref[...]
            for s in range(n_par * n_seq):
                for q in range(RW_PAIRS):
                    st_scr[s, q, 0:HD, 0:HD] = s0_ref[s, 2 * q]
                    st_scr[s, q, HD:2 * HD, HD:2 * HD] = s0_ref[s, 2 * q + 1]
        else:
            prev_scr[...] = jnp.zeros(prev_scr.shape, F32)

    k_k = vec_ref[0:1, :]
    k_a = vec_ref[1:2, :]
    r_k = vec_ref[2:3, :]
    ln_w = vec_ref[3:4, :]
    ln_b = vec_ref[4:5, :]
    lane = lax.broadcasted_iota(jnp.int32, (1, LANES), 1)
    row = lax.broadcasted_iota(jnp.int32, (R, 1), 0)

    li = lax.broadcasted_iota(jnp.int32, (LANES, LANES), 0)
    lj = lax.broadcasted_iota(jnp.int32, (LANES, LANES), 1)
    same_head = _div_pow2(li, RW_HEAD_DIM) == _div_pow2(lj, RW_HEAD_DIM)
    ones_bd = jnp.where(same_head, 1.0, 0.0).astype(BF16)

    def head_sum(*zs):
        stack = jnp.concatenate([z[:, q * LANES:(q + 1) * LANES] for z in zs for q in range(RW_PAIRS)], axis=0)
        acc = None
        for part in _bf_parts(stack, RWKV_SUM_PARTS):
            t = jnp.dot(part, ones_bd, preferred_element_type=F32)
            acc = t if acc is None else acc + t
        outs = [jnp.concatenate([acc[(n * RW_PAIRS + q) * R:(n * RW_PAIRS + q + 1) * R] for q in range(RW_PAIRS)],
                                axis=1) for n in range(len(zs))]
        return outs[0] if len(zs) == 1 else outs

    ri = lax.broadcasted_iota(jnp.int32, (R, R), 0)
    rj = lax.broadcasted_iota(jnp.int32, (R, R), 1)
    same_seq = _div_pow2(ri, C) == _div_pow2(rj, C)
    causal = jnp.where(same_seq & (rj <= ri), 1.0, 0.0).astype(BF16)

    def prepare_proj(u):
        p = p_refs[u][...]
        prev = pltpu.roll(p, 1, axis=0)
        for s in range(n_seq):
            prev = jnp.where(row == s * C, prev_scr[u * n_seq + s], prev)
        for s in range(n_seq):
            prev_scr[u * n_seq + s] = p[(s + 1) * C - 1:(s + 1) * C, :]
            sh_out_ref[u * n_seq + s] = p[(s + 1) * C - 1:(s + 1) * C, :]
        x = p + mu_ref[...] * (prev - p)
        r = x[:, 0:W]
        k = x[:, W:2 * W]
        v = x[:, 2 * W:3 * W]
        wa_in = x[:, 3 * W:3 * W + LANES]
        g_in = x[:, 3 * W + LANES:3 * W + 2 * LANES]
        wa_act = jnp.where(lane < RW_DECAY_RANK, jnp.tanh(wa_in), wa_in)
        za = jnp.dot(wa_act.astype(BF16), wda_ref[...], preferred_element_type=F32) + wa0_ref[...]
        w_log = -_softplus(-za[:, 0:W]) - 0.5
        lw = -jnp.exp(w_log)
        a = _sigmoid(za[:, W:2 * W])
        g = jnp.dot(_sigmoid(g_in).astype(BF16), wg_ref[...], preferred_element_type=F32)
        kk = k * k_k
        k = k * (1.0 + (a - 1.0) * k_a)
        return dict(r=r, k=k, v=v, g=g, a=a, lw=lw, kk=kk)

    def prepare_sums(d):
        d["kk_sq"], d["rk"] = head_sum(d["kk"] * d["kk"], d["r"] * d["k"] * r_k)
        cl = None
        for part in _bf_parts(d["lw"], 2):
            t1 = jnp.dot(causal, part, preferred_element_type=F32)
            cl = t1 if cl is None else cl + t1
        d["cl"] = cl
        return d

    def prepare_factors(d):
        r, k, v, g, a, lw, cl, rk = (d[n] for n in ("r", "k", "v", "g", "a", "lw", "cl", "rk"))
        kk = d["kk"] / jnp.maximum(jnp.sqrt(d["kk_sq"]), 1e-12)
        b = kk * a
        if n_seq == 1:
            ct = cl[R - 1:R, :]
        else:
            last = cl.reshape(n_seq, C, W)[:, C - 1:C, :]
            ct = jnp.broadcast_to(last, (n_seq, C, W)).reshape(R, W)
        e_neg = jnp.exp(-cl)
        e_end = jnp.exp(ct - cl)
        return dict(v=v, g=g, bonus=rk * v, at=-kk * jnp.exp(cl - lw), rt=r * jnp.exp(cl), bt=b * e_neg,
                    kt=k * e_neg, bts=b * e_end, kts=k * e_end,
                    p_end=jnp.exp(ct))

    prep = [prepare_factors(prepare_sums(prepare_proj(u))) for u in groups]

    strict = jnp.where(same_seq & (rj < ri), 1.0, 0.0)
    incl = jnp.where(same_seq & (rj <= ri), 1.0, 0.0)
    strict2 = jnp.concatenate([strict, strict], axis=1)
    incl2 = jnp.concatenate([incl, incl], axis=1)
    m0 = jnp.where(lane < RW_HEAD_DIM, 1.0, 0.0)
    m1 = jnp.where(lane >= RW_HEAD_DIM, 1.0, 0.0)
    diag = jnp.where(li == lj, 1.0, 0.0)
    bd_mask = jnp.where(same_head, 1.0, 0.0)
    zeros_rl = jnp.zeros((R, LANES), F32)
    zeros_r2 = jnp.zeros((R, 2 * R), F32)
    n_dbl = max(1, (C - 1).bit_length())
    sls = [slice(q * LANES, (q + 1) * LANES) for q in range(RW_PAIRS)]
    items = [(u, q) for u in groups for q in range(RW_PAIRS)]

    def col(name, it):
        return prep[it[0]][name][:, sls[it[1]]]

    v0 = {it: jnp.concatenate([col("v", it), zeros_rl], axis=1) for it in items}
    def against_keys(name, it):
        x, bq, kq = col(name, it), col("bt", it), col("kt", it)
        x2 = jnp.concatenate([x * m0, x * m1], axis=0)
        y4 = jnp.concatenate([bq, kq, kq, bq], axis=0)
        return _dotp(x2, y4, _NT, pp["z"], pp["z"])

    zs = {it: against_keys("at", it) for it in items}
    zt0 = {it: zs[it][0:R, 0:2 * R] * strict2 for it in items}
    zt1 = {it: zs[it][R:2 * R, 2 * R:4 * R] * strict2 for it in items}
    half0 = jnp.where(lax.broadcasted_iota(jnp.int32, (1, 2 * R), 1) < R, 1.0, 0.0)
    half1 = 1.0 - half0
    pw = {it: jnp.concatenate([zt0[it] * half0, zt1[it] * half1], axis=0) for it in items}
    wmat = {}
    for it in items:
        lhs = jnp.concatenate([jnp.concatenate([zt0[it], zeros_r2], axis=1),
                               jnp.concatenate([zeros_r2, zt1[it]], axis=1)], axis=0)
        vq = col("v", it)
        rhs = jnp.concatenate([zeros_rl, vq, vq, zeros_rl], axis=0)
        lv = _dotp(lhs, rhs, _NN, pp["lv"], pp["lv"])
        xa = col("at", it)
        wmat[it] = jnp.concatenate([lv, jnp.concatenate([xa * m0, xa * m1], axis=0)], axis=1)
    def lower_rows(a, lo):
        return jnp.concatenate([a[lo:R], a[R + lo:2 * R]], axis=0)

    def scatter_rows(base, upd, lo):
        h = R - lo
        top = [jnp.zeros((lo, upd.shape[1]), F32)] * 2 if base is None else [base[0:lo], base[R:R + lo]]
        mid = [upd[0:h], upd[h:2 * h]] if base is None else [base[lo:R] + upd[0:h], base[R + lo:2 * R] + upd[h:2 * h]]
        return jnp.concatenate([top[0], mid[0], top[1], mid[1]], axis=0)

    for step in range(n_dbl):
        lo = 1 << step
        if n_seq == 1 and lo % 8 == 0:
            wmat = {it: scatter_rows(wmat[it], _dotp(lower_rows(pw[it], lo), wmat[it], _NN, pp["app"], pp["app"]), lo)
                    for it in items}
        else:
            wmat = {it: wmat[it] + _dotp(pw[it], wmat[it], _NN, pp["app"], pp["app"]) for it in items}
        if step + 1 < n_dbl:
            if n_seq == 1 and (2 * lo) % 8 == 0:
                pw = {it: scatter_rows(None, _dotp(lower_rows(pw[it], 2 * lo), pw[it], _NN, pp["pow"], pp["pow"]),
                                       2 * lo) for it in items}
            else:
                pw = {it: _dotp(pw[it], pw[it], _NN, pp["pow"], pp["pow"]) for it in items}
    zs = {it: against_keys("rt", it) for it in items}
    rhat, o0 = {}, {}
    for it in items:
        zb0 = zs[it][0:R, 0:2 * R] * incl2
        zb1 = zs[it][R:2 * R, 2 * R:4 * R] * incl2
        lhs = jnp.concatenate([jnp.concatenate([zb0, zeros_r2], axis=1),
                               jnp.concatenate([zeros_r2, zb1], axis=1)], axis=0)
        rhs = jnp.concatenate([wmat[it][0:R], v0[it], v0[it], wmat[it][R:2 * R]], axis=0)
        om = _dotp(lhs, rhs, _NN, pp["om"], pp["om"])
        rhat[it] = col("rt", it) + om[0:R, LANES:] + om[R:2 * R, LANES:]
        o0[it] = om[0:R, 0:LANES] * m0 + om[R:2 * R, 0:LANES] * m1
    hgt = {}
    for it in items:
        bq, kq = col("bts", it), col("kts", it)
        for s in range(n_seq):
            rs = slice(s * C, (s + 1) * C)
            ys4 = jnp.concatenate([bq[rs] * m0, kq[rs] * m0, kq[rs] * m1, bq[rs] * m1], axis=0)
            r4 = jnp.concatenate([wmat[it][rs], v0[it][rs], v0[it][rs],
                                  wmat[it][R + s * C:R + (s + 1) * C]], axis=0)
            hgt[it, s] = _dotp(r4, ys4, _TN, pp["hg"], pp["hg"])
    o_pair = {}
    for it in items:
        u, q = it
        o_seq = []
        for s in range(n_seq):
            rs = slice(s * C, (s + 1) * C)
            s_old = st_scr[u * n_seq + s, q]
            ht = hgt[it, s][0:LANES] * bd_mask
            gt = hgt[it, s][LANES:] + diag * prep[u]["p_end"][s * C:s * C + 1, sls[q]]
            o_seq.append(_dotp(rhat[it][rs], s_old, _NT, pp["out"], pp["out"]) + o0[it][rs])
            st_scr[u * n_seq + s, q] = _dotp(s_old, gt, _NN, pp["st"], pp["st"]) + ht
        o_pair[it] = o_seq[0] if n_seq == 1 else jnp.concatenate(o_seq, axis=0)

    inv_n = 1.0 / RW_HEAD_DIM
    o_all = [jnp.concatenate([o_pair[u, q] for q in range(RW_PAIRS)], axis=1) for u in groups]
    sums = head_sum(*o_all)
    dev = [o_all[u] - (sums if n_par == 1 else sums[u]) * inv_n for u in groups]
    sums = head_sum(*[d * d for d in dev])
    for u in groups:
        var = (sums if n_par == 1 else sums[u]) * inv_n
        o = dev[u] * lax.rsqrt(var + RW_GN_EPS) * ln_w + ln_b
        o_ref[u] = ((o + prep[u]["bonus"]) * prep[u]["g"]).astype(o_ref.dtype)

    @pl.when(c == n_c - 1)
    def _():
        for s in range(n_par * n_seq):
            for q in range(RW_PAIRS):
                s_out_ref[s, 2 * q] = st_scr[s, q, 0:HD, 0:HD]
                s_out_ref[s, 2 * q + 1] = st_scr[s, q, HD:2 * HD, HD:2 * HD]


def _rwkv(p_rw, row0, n_b, t_len, shift0, s0, prm, pp):
    if t_len >= RWKV_ROWS:
        seq_rows, n_seq = RWKV_ROWS, 1
    else:
        seq_rows, n_seq = t_len, RWKV_ROWS // t_len
    assert t_len % seq_rows == 0 and n_b % n_seq == 0 and row0 % (seq_rows * n_seq) == 0
    rows = seq_rows * n_seq
    n_c = t_len // seq_rows
    assert n_seq == 1 or n_c == 1
    blk0 = row0 // rows
    has_state = s0 is not None
    n_grp = n_b // n_seq
    n_par = math.gcd(n_grp, RWKV_GROUPS)
    seqs = n_par * n_seq

    wda = jnp.zeros((LANES, 2 * RW_WIDTH), F32)
    wda = wda.at[0:RW_DECAY_RANK, 0:RW_WIDTH].set(prm["w_decay_up"])
    wda = wda.at[RW_DECAY_RANK:, RW_WIDTH:].set(prm["w_a_up"])
    wa0 = jnp.concatenate([prm["w0_decay"], prm["a0"]]).reshape(1, 2 * RW_WIDTH)
    vecs = jnp.zeros((8, RW_WIDTH), F32)
    for i, name in enumerate(("k_k", "k_a", "r_k", "ln_x_w", "ln_x_b")):
        vecs = vecs.at[i].set(prm[name].reshape(RW_WIDTH))

    const = lambda *shape: pl.BlockSpec(shape, lambda bi, c: (0,) * len(shape))
    in_specs = [pl.BlockSpec((rows, RW_PROJ), lambda bi, c, u=u: (blk0 + (bi * n_par + u) * n_c + c, 0))
                for u in range(n_par)]
    args = [p_rw] * n_par
    st_block = (seqs, RW_HEADS, RW_HEAD_DIM, RW_HEAD_DIM)
    if has_state:
        in_specs += [pl.BlockSpec((seqs, 1, RW_PROJ), lambda bi, c: (bi, 0, 0)),
                     pl.BlockSpec(st_block, lambda bi, c: (bi, 0, 0, 0))]
        args += [shift0.reshape(n_b, 1, RW_PROJ), s0]
    in_specs += [const(1, RW_PROJ), const(LANES, 2 * RW_WIDTH), const(1, 2 * RW_WIDTH),
                 const(RW_GATE_RANK, RW_WIDTH), const(8, RW_WIDTH)]
    args += [prm["mu_rwkv"].reshape(1, RW_PROJ), wda.astype(BF16), wa0,
             prm["w_gate_up"].astype(BF16), vecs]

    o, sh, st = pl.pallas_call(
        functools.partial(_rwkv_kernel, seq_rows=seq_rows, n_seq=n_seq, n_par=n_par, has_state=has_state, pp=pp),
        grid=(n_grp // n_par, n_c),
        in_specs=in_specs,
        out_specs=[
            pl.BlockSpec((n_par, rows, RW_WIDTH), lambda bi, c: (bi, c, 0)),
            pl.BlockSpec((seqs, 1, RW_PROJ), lambda bi, c: (bi, 0, 0)),
            pl.BlockSpec(st_block, lambda bi, c: (bi, 0, 0, 0)),
        ],
        out_shape=[
            jax.ShapeDtypeStruct((n_grp, n_c * rows, RW_WIDTH), BF16),
            jax.ShapeDtypeStruct((n_b, 1, RW_PROJ), F32),
            jax.ShapeDtypeStruct((n_b, RW_HEADS, RW_HEAD_DIM, RW_HEAD_DIM), F32),
        ],
        scratch_shapes=[pltpu.VMEM((seqs, 1, RW_PROJ), F32),
                        pltpu.VMEM((seqs, RW_PAIRS, LANES, LANES), F32)],
        compiler_params=_params("arbitrary", "arbitrary"),
        name="rwkv_state" if has_state else "rwkv_fresh",
    )(*args)
    return o.reshape(n_b * t_len, RW_WIDTH), sh.reshape(n_b, RW_PROJ), st


def _sink_probs(s_parts, sink):
    m = sink
    for s in s_parts:
        m = jnp.maximum(m, jnp.max(s, axis=-1, keepdims=True))
    den = jnp.exp(sink - m)
    es = []
    for s in s_parts:
        e = jnp.exp(s - m)
        den = den + jnp.sum(e, axis=-1, keepdims=True)
        es.append(e)
    inv = 1.0 / den
    return [(e * inv).astype(BF16) for e in es]


def _group_queries(q, hk):
    return jnp.concatenate(
        [q[:, (hk * SWA_GROUP + g) * SWA_HEAD_DIM:(hk * SWA_GROUP + g + 1) * SWA_HEAD_DIM]
         for g in range(SWA_GROUP)], axis=0).astype(BF16)


def _group_sinks(sink_ref, hk, grp):
    sink = jnp.zeros(grp.shape, F32)
    for g in range(SWA_GROUP):
        sink = jnp.where(grp == g, sink_ref[hk * SWA_GROUP + g], sink)
    return sink


def _swa_prompt_kernel(sink_ref, q_ref, kp_ref, kc_ref, vp_ref, vc_ref, o_ref, *, n_win):
    n = pl.program_id(1)
    Wn = WINDOW
    kf = jnp.concatenate([kp_ref[...], kc_ref[...]], axis=0).astype(BF16)
    vf = jnp.concatenate([vp_ref[...], vc_ref[...]], axis=0).astype(BF16)
    gi = lax.broadcasted_iota(jnp.int32, (SWA_GROUP * Wn, 2 * Wn), 0)
    i = _mod_pow2(gi, Wn)
    j = lax.broadcasted_iota(jnp.int32, (SWA_GROUP * Wn, 2 * Wn), 1)
    ok = (j > i) & (j <= i + Wn)
    first_key = jnp.where(n > 0, 0, Wn)
    ok_first = ok & (j >= first_key)
    grp = _div_pow2(lax.broadcasted_iota(jnp.int32, (SWA_GROUP * Wn, 1), 0), Wn)
    ks = [slice(hk * SWA_HEAD_DIM, (hk + 1) * SWA_HEAD_DIM) for hk in range(SWA_KV_HEADS)]
    sinks = [_group_sinks(sink_ref, hk, grp) for hk in range(SWA_KV_HEADS)]
    work = [(w, hk) for w in range(n_win) for hk in range(SWA_KV_HEADS)]
    qw = [q_ref[w * Wn:(w + 1) * Wn, :] * SWA_SCALE for w in range(n_win)]
    sc = {(w, hk): lax.dot_general(_group_queries(qw[w], hk), kf[w * Wn:(w + 2) * Wn, ks[hk]], _NT,
                                   preferred_element_type=F32) for w, hk in work}
    pr = {(w, hk): _sink_probs([jnp.where(ok_first if w == 0 else ok, sc[w, hk], -jnp.inf)], sinks[hk])[0]
          for w, hk in work}
    ov = {(w, hk): jnp.dot(pr[w, hk], vf[w * Wn:(w + 2) * Wn, ks[hk]], preferred_element_type=F32)
          for w, hk in work}
    for w, hk in work:
        for g in range(SWA_GROUP):
            h = hk * SWA_GROUP + g
            o_ref[w * Wn:(w + 1) * Wn, h * SWA_HEAD_DIM:(h + 1) * SWA_HEAD_DIM] = (
                ov[w, hk][g * Wn:(g + 1) * Wn].astype(o_ref.dtype))


def _swa_prompt(p_att, n_b, t_len, sinks):
    nb = t_len // WINDOW
    n_win = 2 if nb % 2 == 0 else 1
    ns = nb // n_win
    rows = n_win * WINDOW
    kcol = SWA_Q_WIDTH // SWA_KV_WIDTH
    vcol = kcol + 1
    cur = lambda col: (lambda b, n: (b * ns + n, col))
    prv = lambda col: (lambda b, n: (b * nb + jnp.maximum(n * n_win - 1, 0), col))
    return pl.pallas_call(
        functools.partial(_swa_prompt_kernel, n_win=n_win),
        grid=(n_b, ns),
        in_specs=[
            pl.BlockSpec(memory_space=pltpu.SMEM),
            pl.BlockSpec((rows, SWA_Q_WIDTH), cur(0)),
            pl.BlockSpec((WINDOW, SWA_KV_WIDTH), prv(kcol)),
            pl.BlockSpec((rows, SWA_KV_WIDTH), cur(kcol)),
            pl.BlockSpec((WINDOW, SWA_KV_WIDTH), prv(vcol)),
            pl.BlockSpec((rows, SWA_KV_WIDTH), cur(vcol)),
        ],
        out_specs=pl.BlockSpec((rows, SWA_Q_WIDTH), lambda b, n: (b * ns + n, 0)),
        out_shape=jax.ShapeDtypeStruct((n_b * t_len, SWA_Q_WIDTH), BF16),
        compiler_params=_params("arbitrary", "arbitrary"),
        name="swa_prompt",
    )(sinks, p_att, p_att, p_att, p_att, p_att)


def _swa_step_kernel(sink_ref, q_ref, kn_ref, vn_ref, ck_ref, cv_ref, o_ref, ok_ref, ov_ref, *, n_seq, t_len):
    T = t_len
    wb = ck_ref.shape[2]
    t_row = _mod_pow2(lax.broadcasted_iota(jnp.int32, (SWA_GROUP * T, 1), 0), T)
    grp = _div_pow2(lax.broadcasted_iota(jnp.int32, (SWA_GROUP * T, 1), 0), T)
    jc = lax.broadcasted_iota(jnp.int32, (SWA_GROUP * T, wb), 1)
    jn = lax.broadcasted_iota(jnp.int32, (SWA_GROUP * T, T), 1)
    ok_c = (t_row + wb - jc < WINDOW)
    ok_n = (jn <= t_row)
    ks = [slice(hk * SWA_HEAD_DIM, (hk + 1) * SWA_HEAD_DIM) for hk in range(SWA_KV_HEADS)]
    sinks = [_group_sinks(sink_ref, hk, grp) for hk in range(SWA_KV_HEADS)]
    work = [(s, hk) for s in range(n_seq) for hk in range(SWA_KV_HEADS)]
    rows = [slice(s * T, (s + 1) * T) for s in range(n_seq)]
    lane_w = lax.broadcasted_iota(jnp.int32, (1, wb), 1)
    pad = jnp.zeros((wb - T, SWA_KV_WIDTH), F32)
    knb, vnb, ckb, cvb = [], [], [], []
    for s in range(n_seq):
        kn = kn_ref[rows[s], :]
        vn = vn_ref[rows[s], :]
        ck = ck_ref[s]
        cv = cv_ref[s]
        kn_t = jnp.transpose(jnp.concatenate([pad, kn], axis=0))
        vn_t = jnp.transpose(jnp.concatenate([pad, vn], axis=0))
        ok_ref[s] = jnp.where(lane_w >= wb - T, kn_t, pltpu.roll(ck, wb - T, axis=1))
        ov_ref[s] = jnp.where(lane_w >= wb - T, vn_t, pltpu.roll(cv, wb - T, axis=1))
        knb.append(kn.astype(BF16))
        vnb.append(vn.astype(BF16))
        ckb.append(ck.astype(BF16))
        cvb.append(cv.astype(BF16))
    qs = [q_ref[rows[s], :] * SWA_SCALE for s in range(n_seq)]
    qh = {(s, hk): _group_queries(qs[s], hk) for s, hk in work}
    sc = {w: jnp.dot(qh[w], ckb[w[0]][ks[w[1]], :], preferred_element_type=F32) for w in work}
    sn = {w: lax.dot_general(qh[w], knb[w[0]][:, ks[w[1]]], _NT, preferred_element_type=F32) for w in work}
    pr = {w: _sink_probs([jnp.where(ok_c, sc[w], -jnp.inf), jnp.where(ok_n, sn[w], -jnp.inf)], sinks[w[1]])
          for w in work}
    ov = {w: lax.dot_general(pr[w][0], cvb[w[0]][ks[w[1]], :], _NT, preferred_element_type=F32)
          + jnp.dot(pr[w][1], vnb[w[0]][:, ks[w[1]]], preferred_element_type=F32) for w in work}
    for s, hk in work:
        for g in range(SWA_GROUP):
            h = hk * SWA_GROUP + g
            o_ref[rows[s], h * SWA_HEAD_DIM:(h + 1) * SWA_HEAD_DIM] = ov[s, hk][g * T:(g + 1) * T].astype(o_ref.dtype)


def _swa_step(p_att, row0, n_b, t_len, cache_k, cache_v, sinks):
    wb = cache_k.shape[1]
    assert wb == WINDOW and t_len % 8 == 0 and t_len < wb
    n_seq = max(1, 64 // t_len)
    while n_b % n_seq:
        n_seq //= 2
    rows = n_seq * t_len
    assert rows % 16 == 0 and row0 % rows == 0
    blk0 = row0 // rows
    kcol = SWA_Q_WIDTH // SWA_KV_WIDTH
    to_cm = lambda c: jnp.transpose(c, (0, 2, 3, 1)).reshape(n_b, SWA_KV_WIDTH, wb)
    from_cm = lambda c: jnp.transpose(c.reshape(n_b, SWA_KV_HEADS, SWA_HEAD_DIM, wb), (0, 3, 1, 2))
    ck = to_cm(cache_k)
    cv = to_cm(cache_v)
    o, nk, nv = pl.pallas_call(
        functools.partial(_swa_step_kernel, n_seq=n_seq, t_len=t_len),
        grid=(n_b // n_seq,),
        in_specs=[
            pl.BlockSpec(memory_space=pltpu.SMEM),
            pl.BlockSpec((rows, SWA_Q_WIDTH), lambda i: (blk0 + i, 0)),
            pl.BlockSpec((rows, SWA_KV_WIDTH), lambda i: (blk0 + i, kcol)),
            pl.BlockSpec((rows, SWA_KV_WIDTH), lambda i: (blk0 + i, kcol + 1)),
            pl.BlockSpec((n_seq, SWA_KV_WIDTH, wb), lambda i: (i, 0, 0)),
            pl.BlockSpec((n_seq, SWA_KV_WIDTH, wb), lambda i: (i, 0, 0)),
        ],
        out_specs=[
            pl.BlockSpec((rows, SWA_Q_WIDTH), lambda i: (i, 0)),
            pl.BlockSpec((n_seq, SWA_KV_WIDTH, wb), lambda i: (i, 0, 0)),
            pl.BlockSpec((n_seq, SWA_KV_WIDTH, wb), lambda i: (i, 0, 0)),
        ],
        out_shape=[
            jax.ShapeDtypeStruct((n_b * t_len, SWA_Q_WIDTH), BF16),
            jax.ShapeDtypeStruct((n_b, SWA_KV_WIDTH, wb), F32),
            jax.ShapeDtypeStruct((n_b, SWA_KV_WIDTH, wb), F32),
        ],
        compiler_params=_params("arbitrary"),
        name="swa_step",
    )(sinks, p_att, p_att, p_att, ck, cv)
    return o, from_cm(nk), from_cm(nv)


def _softmax(s):
    m = jnp.max(s, axis=-1, keepdims=True)
    e = jnp.exp(s - m)
    return (e / jnp.sum(e, axis=-1, keepdims=True)).astype(BF16)


def _mem_attend(items, o_ref):
    scale = MEM_HEAD_DIM ** -0.5
    sc = [lax.dot_general(q, k, _NT, preferred_element_type=F32) * scale for q, k, _, _, _ in items]
    pr = [_softmax(s) for s in sc]
    ov = [jnp.dot(p, it[2], preferred_element_type=F32) for p, it in zip(pr, items)]
    for o, it in zip(ov, items):
        o_ref[it[3], it[4]] = o.astype(o_ref.dtype)


def _mem_prompt_kernel(qa_ref, qb_ref, k_ref, v_ref, o_ref):
    kb = k_ref[...].astype(BF16)
    vb = v_ref[...].astype(BF16)
    half = MEM_HEADS // 2
    items = []
    for h in range(MEM_HEADS):
        q_ref = qa_ref if h < half else qb_ref
        hs = slice((h % half) * MEM_HEAD_DIM, (h % half + 1) * MEM_HEAD_DIM)
        ms = slice(h * MEM_HEAD_DIM, (h + 1) * MEM_HEAD_DIM)
        items.append((q_ref[:, hs].astype(BF16), kb[:, ms], vb[:, ms], slice(None), ms))
    _mem_attend(items, o_ref)


def _mem_prompt(p_att, n_b, t_len, mem_kv):
    m_tok = mem_kv.shape[0] // n_b
    tq = math.gcd(t_len, 512)
    nq = t_len // tq
    half_w = MEM_WIDTH // 2
    qcol = (SWA_Q_WIDTH + 2 * SWA_KV_WIDTH) // half_w
    return pl.pallas_call(
        _mem_prompt_kernel,
        grid=(n_b, nq),
        in_specs=[
            pl.BlockSpec((tq, half_w), lambda b, n: (b * nq + n, qcol)),
            pl.BlockSpec((tq, half_w), lambda b, n: (b * nq + n, qcol + 1)),
            pl.BlockSpec((m_tok, MEM_WIDTH), lambda b, n: (b, 0)),
            pl.BlockSpec((m_tok, MEM_WIDTH), lambda b, n: (b, 1)),
        ],
        out_specs=pl.BlockSpec((tq, MEM_WIDTH), lambda b, n: (b * nq + n, 0)),
        out_shape=jax.ShapeDtypeStruct((n_b * t_len, MEM_WIDTH), BF16),
        compiler_params=_params("arbitrary", "arbitrary"),
        name="mem_prompt",
    )(p_att, p_att, mem_kv, mem_kv)


def _mem_step_kernel(qa_ref, qb_ref, k_ref, v_ref, o_ref, *, n_seq, t_len):
    half = MEM_HEADS // 2
    m_tok = k_ref.shape[1] // MEM_HEADS
    items = []
    for s in range(n_seq):
        rs = slice(s * t_len, (s + 1) * t_len)
        for h in range(MEM_HEADS):
            q_ref = qa_ref if h < half else qb_ref
            hs = slice((h % half) * MEM_HEAD_DIM, (h % half + 1) * MEM_HEAD_DIM)
            ms = slice(h * MEM_HEAD_DIM, (h + 1) * MEM_HEAD_DIM)
            hrows = pl.ds(h, m_tok, stride=MEM_HEADS)
            items.append((q_ref[rs, hs].astype(BF16), k_ref[s, hrows, :].astype(BF16),
                          v_ref[s, hrows, :].astype(BF16), rs, ms))
    _mem_attend(items, o_ref)


def _mem_step(p_att, row0, n_b, t_len, cache_k, cache_v):
    m_tok = cache_k.shape[1]
    n_seq = max(1, 64 // t_len)
    while n_b % n_seq:
        n_seq //= 2
    rows = n_seq * t_len
    assert rows % 16 == 0 and row0 % rows == 0
    blk0 = row0 // rows
    half_w = MEM_WIDTH // 2
    qcol = (SWA_Q_WIDTH + 2 * SWA_KV_WIDTH) // half_w
    cache_block = (n_seq, m_tok * MEM_HEADS, MEM_HEAD_DIM)
    ck = cache_k.reshape(n_b, m_tok * MEM_HEADS, MEM_HEAD_DIM)
    cv = cache_v.reshape(n_b, m_tok * MEM_HEADS, MEM_HEAD_DIM)
    return pl.pallas_call(
        functools.partial(_mem_step_kernel, n_seq=n_seq, t_len=t_len),
        grid=(n_b // n_seq,),
        in_specs=[
            pl.BlockSpec((rows, half_w), lambda i: (blk0 + i, qcol)),
            pl.BlockSpec((rows, half_w), lambda i: (blk0 + i, qcol + 1)),
            pl.BlockSpec(cache_block, lambda i: (i, 0, 0)),
            pl.BlockSpec(cache_block, lambda i: (i, 0, 0)),
        ],
        out_specs=pl.BlockSpec((rows, MEM_WIDTH), lambda i: (i, 0)),
        out_shape=jax.ShapeDtypeStruct((n_b * t_len, MEM_WIDTH), BF16),
        compiler_params=_params("arbitrary"),
        name="mem_step",
    )(p_att, p_att, ck, cv)


def _merge_kernel(xn_ref, rwp_ref, rws_ref, swp_ref, sws_ref, mmp_ref, mms_ref,
                  wg0_ref, wg1_ref, wg2_ref, wb0_ref, wb1_ref, wb2_ref, o_ref,
                  g0_scr, g1_scr, g2_scr, b0_scr, b1_scr, b2_scr, *, n_p):
    i = pl.program_id(1)

    @pl.when(i == 0)
    def _():
        for src, dst in ((wg0_ref, g0_scr), (wg1_ref, g1_scr), (wg2_ref, g2_scr),
                         (wb0_ref, b0_scr), (wb1_ref, b1_scr), (wb2_ref, b2_scr)):
            dst[...] = src[...].astype(BF16)

    def run(o_rw_ref, o_sw_ref, o_mm_ref):
        xn = xn_ref[...]
        acc = None
        for g_scr, b_scr, br_ref in ((g0_scr, b0_scr, o_rw_ref), (g1_scr, b1_scr, o_sw_ref),
                                     (g2_scr, b2_scr, o_mm_ref)):
            gate = _sigmoid(jnp.dot(xn, g_scr[...], preferred_element_type=F32))
            t = gate * jnp.dot(br_ref[...], b_scr[...], preferred_element_type=F32)
            acc = t if acc is None else acc + t
        o_ref[...] = acc.astype(o_ref.dtype)

    @pl.when(i < n_p)
    def _():
        run(rwp_ref, swp_ref, mmp_ref)

    @pl.when(i >= n_p)
    def _():
        run(rws_ref, sws_ref, mms_ref)


def _merge(xn, o_p, o_s, w_in, gate_col0, w_br, n_rows_p):
    m, d = xn.shape
    tn = MXU_COLS
    tm = _row_tile(n_rows_p, m - n_rows_p, ROW_TILE)
    n_p = n_rows_p // tm
    n_i = m // tm
    assert gate_col0 % tn == 0 and d % tn == 0
    g0 = gate_col0 // tn
    nj = d // tn
    p_spec = lambda w: pl.BlockSpec((tm, w), lambda j, i: (jnp.minimum(i, n_p - 1), 0))
    s_spec = lambda w: pl.BlockSpec((tm, w), lambda j, i: (jnp.maximum(i - n_p, 0), 0))
    in_specs = [pl.BlockSpec((tm, d), lambda j, i: (i, 0))]
    args = [xn]
    for op, os_ in zip(o_p, o_s):
        in_specs += [p_spec(op.shape[1]), s_spec(os_.shape[1])]
        args += [op, os_]
    for br in range(N_BRANCH):
        in_specs.append(pl.BlockSpec((d, tn), lambda j, i, br=br: (0, g0 + br * nj + j)))
        args.append(w_in)
    for w in w_br:
        in_specs.append(pl.BlockSpec((w.shape[0], tn), lambda j, i: (0, j)))
        args.append(w)
    scratch = [pltpu.VMEM((d, tn), BF16) for _ in range(N_BRANCH)]
    scratch += [pltpu.VMEM((w.shape[0], tn), BF16) for w in w_br]
    return pl.pallas_call(
        functools.partial(_merge_kernel, n_p=n_p),
        grid=(nj, n_i),
        in_specs=in_specs,
        out_specs=pl.BlockSpec((tm, tn), lambda j, i: (i, j)),
        out_shape=jax.ShapeDtypeStruct((m, d), BF16),
        scratch_shapes=scratch,
        compiler_params=_params("arbitrary", "arbitrary"),
        name="gated_merge",
    )(*args)


def _cast_kernel(x_ref, o_ref):
    o_ref[...] = x_ref[...].astype(o_ref.dtype)


def _to_bf16(w, name):
    r, c = w.shape
    tr = math.gcd(r, 512)
    return pl.pallas_call(
        _cast_kernel,
        grid=(r // tr,),
        in_specs=[pl.BlockSpec((tr, c), lambda i: (i, 0))],
        out_specs=pl.BlockSpec((tr, c), lambda i: (i, 0)),
        out_shape=jax.ShapeDtypeStruct((r, c), BF16),
        compiler_params=_params("arbitrary"),
        name=name,
    )(w)


def _wo_kernel(m_ref, w_ref, xp_ref, xs_ref, g1_ref, g2_ref, h_ref, hn_ref, *, n_p):
    i = pl.program_id(0)
    tm = m_ref.shape[0]
    halves = [slice(0, tm // 2), slice(tm // 2, tm)] if tm % 32 == 0 else [slice(0, tm)]

    def run(x_ref):
        for rs in halves:
            f = jnp.dot(m_ref[rs, :], w_ref[...], preferred_element_type=F32)
            h = x_ref[rs, :] + _rms(f, g1_ref[...])
            h_ref[rs, :] = h
            hn_ref[rs, :] = _rms(h, g2_ref[...]).astype(hn_ref.dtype)

    @pl.when(i < n_p)
    def _():
        run(xp_ref)

    @pl.when(i >= n_p)
    def _():
        run(xs_ref)


def _wo_block(merged, w_o, xp, xs, g_post, g_pre_ffn):
    m, d = merged.shape
    n_rows_p = xp.shape[0]
    tm = _row_tile(n_rows_p, m - n_rows_p, NORM_ROW_TILE)
    n_p = n_rows_p // tm
    return pl.pallas_call(
        functools.partial(_wo_kernel, n_p=n_p),
        grid=(m // tm,),
        in_specs=[
            pl.BlockSpec((tm, d), lambda i: (i, 0)),
            pl.BlockSpec((d, d), lambda i: (0, 0), pipeline_mode=pl.Buffered(1)),
            pl.BlockSpec((tm, d), lambda i: (jnp.minimum(i, n_p - 1), 0)),
            pl.BlockSpec((tm, d), lambda i: (jnp.maximum(i - n_p, 0), 0)),
            pl.BlockSpec((1, d), lambda i: (0, 0)),
            pl.BlockSpec((1, d), lambda i: (0, 0)),
        ],
        out_specs=[pl.BlockSpec((tm, d), lambda i: (i, 0)), pl.BlockSpec((tm, d), lambda i: (i, 0))],
        out_shape=[jax.ShapeDtypeStruct((m, d), F32), jax.ShapeDtypeStruct((m, d), BF16)],
        compiler_params=_params("arbitrary"),
        name="wo_residual",
    )(merged, _to_bf16(w_o, "cast_w_o"), xp, xs, g_post.reshape(1, d), g_pre_ffn.reshape(1, d))


def _gelu_tanh(x):
    return 0.5 * x * (1.0 + jnp.tanh(math.sqrt(2.0 / math.pi) * (x + 0.044715 * (x * x * x))))


def _ffn_up_kernel(hp_ref, hs_ref, wg_ref, wv_ref, cw_ref, cb_ref, st_ref, ap_ref, as_ref, cp_ref, cs_ref,
                   wgb_scr, wvb_scr, *, n_p, t_s):
    i = pl.program_id(1)

    @pl.when(i == 0)
    def _():
        wgb_scr[...] = wg_ref[...].astype(BF16)
        wvb_scr[...] = wv_ref[...].astype(BF16)

    tn = wgb_scr.shape[1]
    subs = [slice(c0, c0 + MXU_COLS) for c0 in range(0, tn, MXU_COLS)] if tn % MXU_COLS == 0 else [slice(0, tn)]

    def row_pieces(rows):
        size = NORM_ROW_TILE if rows % NORM_ROW_TILE == 0 and NORM_ROW_TILE % (8 * t_s) == 0 else rows
        return [(r0, r0 + size) for r0 in range(0, rows, size)]

    def conv_glu(cs, zg, zm1, zm2, zv):
        conv = cb_ref[:, cs] + cw_ref[0:1, cs] * zm2 + cw_ref[1:2, cs] * zm1 + cw_ref[2:3, cs] * zg
        return _gelu_tanh(conv) * zv

    def run(h_ref, epilogue):
        items = [(pc, cs) for cs in subs for pc in row_pieces(h_ref.shape[0])]

        def gate_val(pc, cs):
            hn = h_ref[slice(*pc), :]
            return (jnp.dot(hn, wgb_scr[:, cs], preferred_element_type=F32),
                    jnp.dot(hn, wvb_scr[:, cs], preferred_element_type=F32))

        z = {items[0]: gate_val(*items[0])}
        carry = None
        for n, (pc, cs) in enumerate(items):
            if n + 1 < len(items):
                z[items[n + 1]] = gate_val(*items[n + 1])
            zg, zv = z.pop((pc, cs))
            carry = epilogue(pc, cs, zg, zv, carry if pc[0] else None)

    def prompt_epilogue(pc, cs, zg, zv, carry):
        tr = pc[1] - pc[0]
        p2, p1 = carry if carry is not None else (jnp.zeros((1, zg.shape[1]), F32),) * 2
        row = lax.broadcasted_iota(jnp.int32, (tr, 1), 0)
        zm1 = jnp.where(row == 0, p1, pltpu.roll(zg, 1, axis=0))
        zm2 = jnp.where(row == 0, p2, jnp.where(row == 1, p1, pltpu.roll(zg, 2, axis=0)))
        ap_ref[slice(*pc), cs] = conv_glu(cs, zg, zm1, zm2, zv).astype(ap_ref.dtype)
        if pc[1] == ap_ref.shape[0]:
            cp_ref[0, :, cs] = zg[tr - 2:tr, :]
        return zg[tr - 2:tr - 1, :], zg[tr - 1:tr, :]

    def sample_epilogue(pc, cs, zg, zv, carry):
        tr = pc[1] - pc[0]
        nq = tr // t_s
        qs = slice(pc[0] // t_s, pc[1] // t_s)
        tc = zg.shape[1]
        t = lax.broadcasted_iota(jnp.int32, (1, t_s, 1), 1)
        z3 = zg.reshape(nq, t_s, tc)
        s1 = st_ref[qs, 1:2, cs]
        s0 = st_ref[qs, 0:1, cs]
        zm1 = jnp.where(t == 0, s1, pltpu.roll(z3, 1, axis=1))
        zm2 = jnp.where(t == 0, s0, jnp.where(t == 1, s1, pltpu.roll(z3, 2, axis=1)))
        act = conv_glu(cs, z3, zm1, zm2, zv.reshape(nq, t_s, tc))
        as_ref[slice(*pc), cs] = act.reshape(tr, tc).astype(as_ref.dtype)
        cs_ref[qs, :, cs] = z3[:, t_s - 2:t_s, :]
        return None

    @pl.when(i < n_p)
    def _():
        run(hp_ref, prompt_epilogue)

    @pl.when(i >= n_p)
    def _():
        run(hs_ref, sample_epilogue)


def _ffn_up(hn, w_up, conv_w, conv_b, conv_state, n_b_p, t_p, n_b_s, t_s):
    m, d = hn.shape
    d_ff = conv_b.shape[0]
    n_rows_p = n_b_p * t_p
    n_rows_s = n_b_s * t_s
    assert m == n_rows_p + n_rows_s and n_rows_p % n_rows_s == 0 and t_s % 8 == 0 and t_s >= 2
    assert t_p % 16 == 0 and n_rows_s % 16 == 0
    tn = math.gcd(d_ff, 512)
    nj = d_ff // tn
    last = n_b_p - 1
    s_blk = n_rows_p // n_rows_s
    return pl.pallas_call(
        functools.partial(_ffn_up_kernel, n_p=n_b_p, t_s=t_s),
        grid=(nj, n_b_p + 1),
        in_specs=[
            pl.BlockSpec((t_p, d), lambda j, i: (jnp.minimum(i, last), 0)),
            pl.BlockSpec((n_rows_s, d), lambda j, i: (s_blk, 0)),
            pl.BlockSpec((d, tn), lambda j, i: (0, j)),
            pl.BlockSpec((d, tn), lambda j, i: (0, nj + j)),
            pl.BlockSpec((CONV_W, tn), lambda j, i: (0, j)),
            pl.BlockSpec((1, tn), lambda j, i: (0, j)),
            pl.BlockSpec((n_b_s, CONV_W - 1, tn), lambda j, i: (0, 0, j)),
        ],
        out_specs=[
            pl.BlockSpec((t_p, tn), lambda j, i: (jnp.minimum(i, last), j)),
            pl.BlockSpec((n_rows_s, tn), lambda j, i: (0, j)),
            pl.BlockSpec((1, CONV_W - 1, tn), lambda j, i: (jnp.minimum(i, last), 0, j)),
            pl.BlockSpec((n_b_s, CONV_W - 1, tn), lambda j, i: (0, 0, j)),
        ],
        out_shape=[
            jax.ShapeDtypeStruct((n_rows_p, d_ff), BF16),
            jax.ShapeDtypeStruct((n_rows_s, d_ff), BF16),
            jax.ShapeDtypeStruct((n_b_p, CONV_W - 1, d_ff), F32),
            jax.ShapeDtypeStruct((n_b_s, CONV_W - 1, d_ff), F32),
        ],
        scratch_shapes=[pltpu.VMEM((d, tn), BF16), pltpu.VMEM((d, tn), BF16)],
        compiler_params=_params("arbitrary", "arbitrary"),
        name="ffn_up_convglu",
    )(hn, hn, w_up, w_up, conv_w, conv_b.reshape(1, d_ff), conv_state)


def _ffn_down_kernel(a_ref, w_ref, h_ref, g_ref, y_ref):
    kk = pl.program_id(1)

    @pl.when(kk == 0)
    def _():
        y_ref[...] = jnp.zeros(y_ref.shape, F32)

    last = kk == pl.num_programs(1) - 1

    @pl.when(jnp.logical_not(last))
    def _():
        y_ref[...] += jnp.dot(a_ref[...], w_ref[...], preferred_element_type=F32)

    @pl.when(last)
    def _():
        tm = y_ref.shape[0]
        n_pieces = 4 if tm % 64 == 0 else 1
        tr = tm // n_pieces
        wb = w_ref[...]
        for r in range(n_pieces):
            rs = slice(r * tr, (r + 1) * tr)
            f = y_ref[rs, :] + jnp.dot(a_ref[rs, :], wb, preferred_element_type=F32)
            y_ref[rs, :] = h_ref[rs, :] + _rms(f, g_ref[...])


def _ffn_down(act, w_down, h, g, row0):
    d_ff, d = w_down.shape
    n_rows = act.shape[0]
    tm = math.gcd(math.gcd(n_rows, row0) if row0 else n_rows, ROW_TILE)
    tk = math.gcd(d_ff, 512)
    i0 = row0 // tm
    return pl.pallas_call(
        _ffn_down_kernel,
        grid=(n_rows // tm, d_ff // tk),
        in_specs=[
            pl.BlockSpec((tm, tk), lambda i, k: (i, k)),
            pl.BlockSpec((tk, d), lambda i, k: (k, 0)),
            pl.BlockSpec((tm, d), lambda i, k: (i0 + i, 0)),
            pl.BlockSpec((1, d), lambda i, k: (0, 0)),
        ],
        out_specs=pl.BlockSpec((tm, d), lambda i, k: (i, 0)),
        out_shape=jax.ShapeDtypeStruct((n_rows, d), F32),
        compiler_params=_params("arbitrary", "arbitrary"),
        name="ffn_down",
    )(act, w_down, h, g.reshape(1, d))


def kernel(x_prompt, x_sample, cache_swa_k, cache_swa_v, cache_mem_k, cache_mem_v, state_rwkv_shift, state_rwkv_wkv, state_ffn_conv, mem_prompt, g_pre_mix, w_in, mu_rwkv, w_decay_up, w0_decay, w_a_up, a0, w_gate_up, k_k, k_a, r_k, ln_x_w, ln_x_b, swa_sinks, g_mem, w_mem_kv, w_br_rwkv, w_br_swa, w_br_mem, w_o, g_post_mix, g_pre_ffn, w_ffn_up, conv_w, conv_b, w_ffn_down, g_post_ffn):
    b_p, t_p, d = x_prompt.shape
    b_s, t_s, _ = x_sample.shape
    n_p, n_s = b_p * t_p, b_s * t_s
    m_tok = mem_prompt.shape[1]
    rw_prm = dict(mu_rwkv=mu_rwkv, w_decay_up=w_decay_up, w0_decay=w0_decay, w_a_up=w_a_up, a0=a0,
                  w_gate_up=w_gate_up, k_k=k_k, k_a=k_a, r_k=r_k, ln_x_w=ln_x_w, ln_x_b=ln_x_b)
    xp = x_prompt.reshape(n_p, d)
    xs = x_sample.reshape(n_s, d)

    xn, p_att = _norm_matmul(xp, xs, g_pre_mix, w_in, RW_PROJ, ATT_WIDTH, "norm_proj_attn")
    p_rw = _matmul(xn, w_in, 0, RW_PROJ, RW_PROJ // 2, ROW_TILE, "proj_rwkv")

    _, mem_kv = _norm_matmul(mem_prompt.reshape(b_p * m_tok, d), None, g_mem, w_mem_kv,
                             0, 2 * MEM_WIDTH, "norm_proj_mem_kv")

    o_rw_p, shift_p, wkv_p = _rwkv(p_rw, 0, b_p, t_p, None, None, rw_prm, RWKV_PARTS)
    o_rw_s, shift_s, wkv_s = _rwkv(p_rw, n_p, b_s, t_s, state_rwkv_shift, state_rwkv_wkv, rw_prm, RWKV_PARTS)

    o_sw_p = _swa_prompt(p_att, b_p, t_p, swa_sinks)
    o_sw_s, swa_k_s, swa_v_s = _swa_step(p_att, n_p, b_s, t_s, cache_swa_k, cache_swa_v, swa_sinks)

    o_mm_p = _mem_prompt(p_att, b_p, t_p, mem_kv)
    o_mm_s = _mem_step(p_att, n_p, b_s, t_s, cache_mem_k, cache_mem_v)

    merged = _merge(xn, (o_rw_p, o_sw_p, o_mm_p), (o_rw_s, o_sw_s, o_mm_s), w_in,
                    RW_PROJ + ATT_WIDTH, (w_br_rwkv, w_br_swa, w_br_mem), n_p)
    h, hn = _wo_block(merged, w_o, xp, xs, g_post_mix, g_pre_ffn)
    act_p, act_s, conv_p, conv_s = _ffn_up(hn, w_ffn_up, conv_w, conv_b, state_ffn_conv, b_p, t_p, b_s, t_s)
    w_down = _to_bf16(w_ffn_down, "cast_w_down")
    y_p = _ffn_down(act_p, w_down, h, g_post_ffn, 0)
    y_s = _ffn_down(act_s, w_down, h, g_post_ffn, n_p)

    keep = min(WINDOW, t_p)
    kv_p = jnp.stack([lax.slice(p_att, ((b + 1) * t_p - keep, SWA_Q_WIDTH),
                                ((b + 1) * t_p, SWA_Q_WIDTH + 2 * SWA_KV_WIDTH)) for b in range(b_p)])
    swa_k_p = kv_p[..., :SWA_KV_WIDTH].reshape(b_p, keep, SWA_KV_HEADS, SWA_HEAD_DIM)
    swa_v_p = kv_p[..., SWA_KV_WIDTH:].reshape(b_p, keep, SWA_KV_HEADS, SWA_HEAD_DIM)
    mem_k_p = mem_kv[:, :MEM_WIDTH].reshape(b_p, m_tok, MEM_HEADS, MEM_HEAD_DIM)
    mem_v_p = mem_kv[:, MEM_WIDTH:].reshape(b_p, m_tok, MEM_HEADS, MEM_HEAD_DIM)
    return (y_p.reshape(b_p, t_p, d), y_s.reshape(b_s, t_s, d), swa_k_p, swa_v_p, mem_k_p, mem_v_p,
            shift_p, wkv_p, conv_p, swa_k_s, swa_v_s, shift_s, wkv_s, conv_s)
```

```python
import functools
import math

import jax
import jax.numpy as jnp
from jax import lax
from jax.experimental import pallas as pl
from jax.experimental.pallas import tpu as pltpu

F32 = jnp.float32
BF16 = jnp.bfloat16

RW_HEADS = 12
RW_HEAD_DIM = 64
RW_WIDTH = RW_HEADS * RW_HEAD_DIM
RW_DECAY_RANK = 64
RW_A_RANK = 64
RW_GATE_RANK = 128
RW_PROJ = 3 * RW_WIDTH + RW_DECAY_RANK + RW_A_RANK + RW_GATE_RANK
RW_GN_EPS = 6.4e-4
RW_PAIRS = RW_HEADS // 2
SWA_Q_HEADS = 12
SWA_KV_HEADS = 4
SWA_GROUP = SWA_Q_HEADS // SWA_KV_HEADS
SWA_HEAD_DIM = 64
SWA_Q_WIDTH = SWA_Q_HEADS * SWA_HEAD_DIM
SWA_KV_WIDTH = SWA_KV_HEADS * SWA_HEAD_DIM
WINDOW = 128
MEM_HEADS = 4
MEM_HEAD_DIM = 128
MEM_WIDTH = MEM_HEADS * MEM_HEAD_DIM
N_BRANCH = 3
CONV_W = 3
NORM_EPS = 1e-6
ATT_WIDTH = SWA_Q_WIDTH + 2 * SWA_KV_WIDTH + MEM_WIDTH
SWA_SCALE = SWA_HEAD_DIM ** -0.5
assert math.frexp(SWA_SCALE)[0] == 0.5

LANES = 128
BF16_SUBLANES = 16
MXU_COLS = 256
VMEM_LIMIT_BYTES = 56 * 1024 * 1024

RWKV_ROWS = 64
RWKV_GROUPS = 2
RWKV_PARTS = dict(z=1, lv=1, pow=1, app=1, om=1, hg=1, out=1, st=1)
RWKV_SUM_PARTS = 1
ROW_TILE = 1024
NORM_ROW_TILE = 512
FFN_TILE = 512
STEP_ROWS = 64
SWA_WINDOWS = 4


def _params(*sem):
    return pltpu.CompilerParams(dimension_semantics=sem, vmem_limit_bytes=VMEM_LIMIT_BYTES)


def _row_tile(n_p, n_s, target):
    t = math.gcd(math.gcd(n_p, n_s), target)
    assert t % BF16_SUBLANES == 0, (n_p, n_s, target)
    return t


def _div_pow2(x, n):
    assert n & (n - 1) == 0
    return lax.shift_right_logical(x, n.bit_length() - 1)


def _mod_pow2(x, n):
    assert n & (n - 1) == 0
    return jnp.bitwise_and(x, n - 1)


def _rms(x, g):
    ms = jnp.mean(x * x, axis=-1, keepdims=True)
    return x * lax.rsqrt(ms + NORM_EPS) * g


def _norm_mm_kernel(*refs, n_p, two):
    if two:
        xp_ref, xs_ref, g_ref, w_ref, xn_ref, o_ref, wb_scr = refs
    else:
        xp_ref, g_ref, w_ref, xn_ref, o_ref, wb_scr = refs
    i = pl.program_id(0)

    @pl.when(i == 0)
    def _():
        wb_scr[...] = w_ref[...].astype(BF16)

    def run(x_ref):
        xn = _rms(x_ref[...], g_ref[...]).astype(BF16)
        xn_ref[...] = xn
        o_ref[...] = jnp.dot(xn, wb_scr[...], preferred_element_type=F32)

    if two:
        @pl.when(i < n_p)
        def _():
            run(xp_ref)

        @pl.when(i >= n_p)
        def _():
            run(xs_ref)
    else:
        run(xp_ref)


def _norm_matmul(xp, xs, g, w, col0, n_cols, name):
    n_rows_p, d = xp.shape
    two = xs is not None
    n_rows_s = xs.shape[0] if two else 0
    tm = _row_tile(n_rows_p, n_rows_s, NORM_ROW_TILE) if two else math.gcd(n_rows_p, NORM_ROW_TILE)
    n_p, n_s = n_rows_p // tm, n_rows_s // tm
    assert col0 % LANES == 0 and n_cols % LANES == 0
    once = pl.Buffered(1)
    in_specs = [pl.BlockSpec((tm, d), lambda i: (jnp.minimum(i, n_p - 1), 0))]
    args = [xp]
    if two:
        in_specs.append(pl.BlockSpec((tm, d), lambda i: (jnp.maximum(i - n_p, 0), 0)))
        args.append(xs)
    in_specs += [pl.BlockSpec((1, d), lambda i: (0, 0)),
                 pl.BlockSpec((pl.Element(d), pl.Element(n_cols)), lambda i: (0, col0), pipeline_mode=once)]
    args += [g.reshape(1, d), w]
    m = n_rows_p + n_rows_s
    return pl.pallas_call(
        functools.partial(_norm_mm_kernel, n_p=n_p, two=two),
        grid=(n_p + n_s,),
        in_specs=in_specs,
        out_specs=[pl.BlockSpec((tm, d), lambda i: (i, 0)), pl.BlockSpec((tm, n_cols), lambda i: (i, 0))],
        out_shape=[jax.ShapeDtypeStruct((m, d), BF16), jax.ShapeDtypeStruct((m, n_cols), F32)],
        scratch_shapes=[pltpu.VMEM((d, n_cols), BF16)],
        compiler_params=_params("arbitrary"),
        name=name,
    )(*args)


def _mm_kernel(a_ref, w_ref, o_ref, wb_ref):
    @pl.when(pl.program_id(1) == 0)
    def _():
        wb_ref[...] = w_ref[...].astype(BF16)

    o_ref[...] = jnp.dot(a_ref[...], wb_ref[...], preferred_element_type=F32)


def _matmul(a, w, col0, n_cols, tn, tm, name):
    m, k = a.shape
    tm = math.gcd(m, tm)
    assert n_cols % tn == 0 and col0 % LANES == 0 and tn % LANES == 0
    return pl.pallas_call(
        _mm_kernel,
        grid=(n_cols // tn, m // tm),
        in_specs=[
            pl.BlockSpec((tm, k), lambda j, i: (i, 0)),
            pl.BlockSpec((pl.Element(k), pl.Element(tn)),
                         lambda j, i: (0, pl.multiple_of(col0 + j * tn, LANES))),
        ],
        out_specs=pl.BlockSpec((tm, tn), lambda j, i: (i, j)),
        out_shape=jax.ShapeDtypeStruct((m, n_cols), F32),
        scratch_shapes=[pltpu.VMEM((k, tn), BF16)],
        compiler_params=_params("arbitrary", "arbitrary"),
        name=name,
    )(a, w)


_NN = (((1,), (0,)), ((), ()))
_NT = (((1,), (1,)), ((), ()))
_TN = (((0,), (0,)), ((), ()))


def _bf_parts(x, n):
    parts = []
    r = x
    for i in range(n):
        h = r.astype(BF16)
        parts.append(h)
        if i + 1 < n:
            r = r - h.astype(F32)
    return parts


def _dotp(a, b, dn, pa, pb):
    pa_parts = _bf_parts(a, pa)
    pb_parts = _bf_parts(b, pb)
    acc = None
    for i in range(pa):
        for j in range(pb):
            if i + j < max(pa, pb):
                t = lax.dot_general(pa_parts[i], pb_parts[j], dn, preferred_element_type=F32)
                acc = t if acc is None else acc + t
    return acc


def _softplus(x):
    return jnp.maximum(x, 0.0) + jnp.log(1.0 + jnp.exp(-jnp.abs(x)))


def _sigmoid(x):
    return 1.0 / (1.0 + jnp.exp(-x))


def _rwkv_kernel(*refs, seq_rows, n_seq, n_par, has_state, pp):
    n_in = n_par + (2 if has_state else 0)
    p_refs = refs[0:n_par]
    if has_state:
        sh0_ref, s0_ref = refs[n_par:n_in]
    (mu_ref, wda_ref, wa0_ref, wg_ref, vec_ref,
     o_ref, sh_out_ref, s_out_ref, prev_scr, st_scr) = refs[n_in:]
    c = pl.program_id(1)
    n_c = pl.num_programs(1)
    C = seq_rows
    R = seq_rows * n_seq
    W = RW_WIDTH
    HD = RW_HEAD_DIM
    groups = range(n_par)

    @pl.when(c == 0)
    def _():
        st_scr[...] = jnp.zeros(st_scr.shape, F32)
        if has_state:
            prev_scr[...] = sh0_ref[...]
            for s in range(n_par * n_seq):
                for q in range(RW_PAIRS):
                    st_scr[s, q, 0:HD, 0:HD] = s0_ref[s, 2 * q]
                    st_scr[s, q, HD:2 * HD, HD:2 * HD] = s0_ref[s, 2 * q + 1]
        else:
            prev_scr[...] = jnp.zeros(prev_scr.shape, F32)

    k_k = vec_ref[0:1, :]
    k_a = vec_ref[1:2, :]
    r_k = vec_ref[2:3, :]
    ln_w = vec_ref[3:4, :]
    ln_b = vec_ref[4:5, :]
    lane = lax.broadcasted_iota(jnp.int32, (1, LANES), 1)
    row = lax.broadcasted_iota(jnp.int32, (R, 1), 0)

    li = lax.broadcasted_iota(jnp.int32, (LANES, LANES), 0)
    lj = lax.broadcasted_iota(jnp.int32, (LANES, LANES), 1)
    same_head = _div_pow2(li, RW_HEAD_DIM) == _div_pow2(lj, RW_HEAD_DIM)
    ones_bd = jnp.where(same_head, 1.0, 0.0).astype(BF16)

    def head_sum(*zs):
        stack = jnp.concatenate([z[:, q * LANES:(q + 1) * LANES] for z in zs for q in range(RW_PAIRS)], axis=0)
        acc = None
        for part in _bf_parts(stack, RWKV_SUM_PARTS):
            t = jnp.dot(part, ones_bd, preferred_element_type=F32)
            acc = t if acc is None else acc + t
        outs = [jnp.concatenate([acc[(n * RW_PAIRS + q) * R:(n * RW_PAIRS + q + 1) * R] for q in range(RW_PAIRS)],
                                axis=1) for n in range(len(zs))]
        return outs[0] if len(zs) == 1 else outs

    ri = lax.broadcasted_iota(jnp.int32, (R, R), 0)
    rj = lax.broadcasted_iota(jnp.int32, (R, R), 1)
    same_seq = _div_pow2(ri, C) == _div_pow2(rj, C)
    causal = jnp.where(same_seq & (rj <= ri), 1.0, 0.0).astype(BF16)

    def prepare_proj(u):
        p = p_refs[u][...]
        prev = pltpu.roll(p, 1, axis=0)
        for s in range(n_seq):
            prev = jnp.where(row == s * C, prev_scr[u * n_seq + s], prev)
        for s in range(n_seq):
            prev_scr[u * n_seq + s] = p[(s + 1) * C - 1:(s + 1) * C, :]
            sh_out_ref[u * n_seq + s] = p[(s + 1) * C - 1:(s + 1) * C, :]
        x = p + mu_ref[...] * (prev - p)
        r = x[:, 0:W]
        k = x[:, W:2 * W]
        v = x[:, 2 * W:3 * W]
        wa_in = x[:, 3 * W:3 * W + LANES]
        g_in = x[:, 3 * W + LANES:3 * W + 2 * LANES]
        wa_act = jnp.where(lane < RW_DECAY_RANK, jnp.tanh(wa_in), wa_in)
        za = jnp.dot(wa_act.astype(BF16), wda_ref[...], preferred_element_type=F32) + wa0_ref[...]
        w_log = -_softplus(-za[:, 0:W]) - 0.5
        lw = -jnp.exp(w_log)
        a = _sigmoid(za[:, W:2 * W])
        g = jnp.dot(_sigmoid(g_in).astype(BF16), wg_ref[...], preferred_element_type=F32)
        kk = k * k_k
        k = k * (1.0 + (a - 1.0) * k_a)
        return dict(r=r, k=k, v=v, g=g, a=a, lw=lw, kk=kk)

    def prepare_sums(d):
        d["kk_sq"], d["rk"] = head_sum(d["kk"] * d["kk"], d["r"] * d["k"] * r_k)
        cl = None
        for part in _bf_parts(d["lw"], 2):
            t1 = jnp.dot(causal, part, preferred_element_type=F32)
            cl = t1 if cl is None else cl + t1
        d["cl"] = cl
        return d

    def prepare_factors(d):
        r, k, v, g, a, lw, cl, rk = (d[n] for n in ("r", "k", "v", "g", "a", "lw", "cl", "rk"))
        kk = d["kk"] / jnp.maximum(jnp.sqrt(d["kk_sq"]), 1e-12)
        b = kk * a
        if n_seq == 1:
            ct = cl[R - 1:R, :]
        else:
            last = cl.reshape(n_seq, C, W)[:, C - 1:C, :]
            ct = jnp.broadcast_to(last, (n_seq, C, W)).reshape(R, W)
        e_neg = jnp.exp(-cl)
        e_end = jnp.exp(ct - cl)
        return dict(v=v, g=g, bonus=rk * v, at=-kk * jnp.exp(cl - lw), rt=r * jnp.exp(cl), bt=b * e_neg,
                    kt=k * e_neg, bts=b * e_end, kts=k * e_end,
                    p_end=jnp.exp(ct))

    prep = [prepare_factors(prepare_sums(prepare_proj(u))) for u in groups]

    strict = jnp.where(same_seq & (rj < ri), 1.0, 0.0)
    incl = jnp.where(same_seq & (rj <= ri), 1.0, 0.0)
    strict2 = jnp.concatenate([strict, strict], axis=1)
    incl2 = jnp.concatenate([incl, incl], axis=1)
    m0 = jnp.where(lane < RW_HEAD_DIM, 1.0, 0.0)
    m1 = jnp.where(lane >= RW_HEAD_DIM, 1.0, 0.0)
    diag = jnp.where(li == lj, 1.0, 0.0)
    bd_mask = jnp.where(same_head, 1.0, 0.0)
    zeros_rl = jnp.zeros((R, LANES), F32)
    zeros_r2 = jnp.zeros((R, 2 * R), F32)
    n_dbl = max(1, (C - 1).bit_length())
    sls = [slice(q * LANES, (q + 1) * LANES) for q in range(RW_PAIRS)]
    items = [(u, q) for u in groups for q in range(RW_PAIRS)]

    def col(name, it):
        return prep[it[0]][name][:, sls[it[1]]]

    v0 = {it: jnp.concatenate([col("v", it), zeros_rl], axis=1) for it in items}
    def against_keys(name, it):
        x, bq, kq = col(name, it), col("bt", it), col("kt", it)
        x2 = jnp.concatenate([x * m0, x * m1], axis=0)
        y4 = jnp.concatenate([bq, kq, kq, bq], axis=0)
        return _dotp(x2, y4, _NT, pp["z"], pp["z"])

    zs = {it: against_keys("at", it) for it in items}
    zt0 = {it: zs[it][0:R, 0:2 * R] * strict2 for it in items}
    zt1 = {it: zs[it][R:2 * R, 2 * R:4 * R] * strict2 for it in items}
    half0 = jnp.where(lax.broadcasted_iota(jnp.int32, (1, 2 * R), 1) < R, 1.0, 0.0)
    half1 = 1.0 - half0
    pw = {it: jnp.concatenate([zt0[it] * half0, zt1[it] * half1], axis=0) for it in items}
    wmat = {}
    for it in items:
        lhs = jnp.concatenate([jnp.concatenate([zt0[it], zeros_r2], axis=1),
                               jnp.concatenate([zeros_r2, zt1[it]], axis=1)], axis=0)
        vq = col("v", it)
        rhs = jnp.concatenate([zeros_rl, vq, vq, zeros_rl], axis=0)
        lv = _dotp(lhs, rhs, _NN, pp["lv"], pp["lv"])
        xa = col("at", it)
        wmat[it] = jnp.concatenate([lv, jnp.concatenate([xa * m0, xa * m1], axis=0)], axis=1)
    def lower_rows(a, lo):
        return jnp.concatenate([a[lo:R], a[R + lo:2 * R]], axis=0)

    def scatter_rows(base, upd, lo):
        h = R - lo
        top = [jnp.zeros((lo, upd.shape[1]), F32)] * 2 if base is None else [base[0:lo], base[R:R + lo]]
        mid = [upd[0:h], upd[h:2 * h]] if base is None else [base[lo:R] + upd[0:h], base[R + lo:2 * R] + upd[h:2 * h]]
        return jnp.concatenate([top[0], mid[0], top[1], mid[1]], axis=0)

    for step in range(n_dbl):
        lo = 1 << step
        if n_seq == 1 and lo % 8 == 0:
            wmat = {it: scatter_rows(wmat[it], _dotp(lower_rows(pw[it], lo), wmat[it], _NN, pp["app"], pp["app"]), lo)
                    for it in items}
        else:
            wmat = {it: wmat[it] + _dotp(pw[it], wmat[it], _NN, pp["app"], pp["app"]) for it in items}
        if step + 1 < n_dbl:
            if n_seq == 1 and (2 * lo) % 8 == 0:
                pw = {it: scatter_rows(None, _dotp(lower_rows(pw[it], 2 * lo), pw[it], _NN, pp["pow"], pp["pow"]),
                                       2 * lo) for it in items}
            else:
                pw = {it: _dotp(pw[it], pw[it], _NN, pp["pow"], pp["pow"]) for it in items}
    zs = {it: against_keys("rt", it) for it in items}
    rhat, o0 = {}, {}
    for it in items:
        zb0 = zs[it][0:R, 0:2 * R] * incl2
        zb1 = zs[it][R:2 * R, 2 * R:4 * R] * incl2
        lhs = jnp.concatenate([jnp.concatenate([zb0, zeros_r2], axis=1),
                               jnp.concatenate([zeros_r2, zb1], axis=1)], axis=0)
        rhs = jnp.concatenate([wmat[it][0:R], v0[it], v0[it], wmat[it][R:2 * R]], axis=0)
        om = _dotp(lhs, rhs, _NN, pp["om"], pp["om"])
        rhat[it] = col("rt", it) + om[0:R, LANES:] + om[R:2 * R, LANES:]
        o0[it] = om[0:R, 0:LANES] * m0 + om[R:2 * R, 0:LANES] * m1
    hgt = {}
    for it in items:
        bq, kq = col("bts", it), col("kts", it)
        for s in range(n_seq):
            rs = slice(s * C, (s + 1) * C)
            ys4 = jnp.concatenate([bq[rs] * m0, kq[rs] * m0, kq[rs] * m1, bq[rs] * m1], axis=0)
            r4 = jnp.concatenate([wmat[it][rs], v0[it][rs], v0[it][rs],
                                  wmat[it][R + s * C:R + (s + 1) * C]], axis=0)
            hgt[it, s] = _dotp(r4, ys4, _TN, pp["hg"], pp["hg"])
    o_pair = {}
    for it in items:
        u, q = it
        o_seq = []
        for s in range(n_seq):
            rs = slice(s * C, (s + 1) * C)
            s_old = st_scr[u * n_seq + s, q]
            ht = hgt[it, s][0:LANES] * bd_mask
            gt = hgt[it, s][LANES:] + diag * prep[u]["p_end"][s * C:s * C + 1, sls[q]]
            o_seq.append(_dotp(rhat[it][rs], s_old, _NT, pp["out"], pp["out"]) + o0[it][rs])
            st_scr[u * n_seq + s, q] = _dotp(s_old, gt, _NN, pp["st"], pp["st"]) + ht
        o_pair[it] = o_seq[0] if n_seq == 1 else jnp.concatenate(o_seq, axis=0)

    inv_n = 1.0 / RW_HEAD_DIM
    o_all = [jnp.concatenate([o_pair[u, q] for q in range(RW_PAIRS)], axis=1) for u in groups]
    sums = head_sum(*o_all)
    dev = [o_all[u] - (sums if n_par == 1 else sums[u]) * inv_n for u in groups]
    sums = head_sum(*[d * d for d in dev])
    for u in groups:
        var = (sums if n_par == 1 else sums[u]) * inv_n
        o = dev[u] * lax.rsqrt(var + RW_GN_EPS) * ln_w + ln_b
        o_ref[u] = ((o + prep[u]["bonus"]) * prep[u]["g"]).astype(o_ref.dtype)

    @pl.when(c == n_c - 1)
    def _():
        for s in range(n_par * n_seq):
            for q in range(RW_PAIRS):
                s_out_ref[s, 2 * q] = st_scr[s, q, 0:HD, 0:HD]
                s_out_ref[s, 2 * q + 1] = st_scr[s, q, HD:2 * HD, HD:2 * HD]


def _rwkv(p_rw, row0, n_b, t_len, shift0, s0, prm, pp):
    if t_len >= RWKV_ROWS:
        seq_rows, n_seq = RWKV_ROWS, 1
    else:
        seq_rows, n_seq = t_len, RWKV_ROWS // t_len
    assert t_len % seq_rows == 0 and n_b % n_seq == 0 and row0 % (seq_rows * n_seq) == 0
    rows = seq_rows * n_seq
    n_c = t_len // seq_rows
    assert n_seq == 1 or n_c == 1
    blk0 = row0 // rows
    has_state = s0 is not None
    n_grp = n_b // n_seq
    n_par = math.gcd(n_grp, RWKV_GROUPS)
    seqs = n_par * n_seq

    wda = jnp.zeros((LANES, 2 * RW_WIDTH), F32)
    wda = wda.at[0:RW_DECAY_RANK, 0:RW_WIDTH].set(prm["w_decay_up"])
    wda = wda.at[RW_DECAY_RANK:, RW_WIDTH:].set(prm["w_a_up"])
    wa0 = jnp.concatenate([prm["w0_decay"], prm["a0"]]).reshape(1, 2 * RW_WIDTH)
    vecs = jnp.zeros((8, RW_WIDTH), F32)
    for i, name in enumerate(("k_k", "k_a", "r_k", "ln_x_w", "ln_x_b")):
        vecs = vecs.at[i].set(prm[name].reshape(RW_WIDTH))

    const = lambda *shape: pl.BlockSpec(shape, lambda bi, c: (0,) * len(shape))
    in_specs = [pl.BlockSpec((rows, RW_PROJ), lambda bi, c, u=u: (blk0 + (bi * n_par + u) * n_c + c, 0))
                for u in range(n_par)]
    args = [p_rw] * n_par
    st_block = (seqs, RW_HEADS, RW_HEAD_DIM, RW_HEAD_DIM)
    if has_state:
        in_specs += [pl.BlockSpec((seqs, 1, RW_PROJ), lambda bi, c: (bi, 0, 0)),
                     pl.BlockSpec(st_block, lambda bi, c: (bi, 0, 0, 0))]
        args += [shift0.reshape(n_b, 1, RW_PROJ), s0]
    in_specs += [const(1, RW_PROJ), const(LANES, 2 * RW_WIDTH), const(1, 2 * RW_WIDTH),
                 const(RW_GATE_RANK, RW_WIDTH), const(8, RW_WIDTH)]
    args += [prm["mu_rwkv"].reshape(1, RW_PROJ), wda.astype(BF16), wa0,
             prm["w_gate_up"].astype(BF16), vecs]

    o, sh, st = pl.pallas_call(
        functools.partial(_rwkv_kernel, seq_rows=seq_rows, n_seq=n_seq, n_par=n_par, has_state=has_state, pp=pp),
        grid=(n_grp // n_par, n_c),
        in_specs=in_specs,
        out_specs=[
            pl.BlockSpec((n_par, rows, RW_WIDTH), lambda bi, c: (bi, c, 0)),
            pl.BlockSpec((seqs, 1, RW_PROJ), lambda bi, c: (bi, 0, 0)),
            pl.BlockSpec(st_block, lambda bi, c: (bi, 0, 0, 0)),
        ],
        out_shape=[
            jax.ShapeDtypeStruct((n_grp, n_c * rows, RW_WIDTH), BF16),
            jax.ShapeDtypeStruct((n_b, 1, RW_PROJ), F32),
            jax.ShapeDtypeStruct((n_b, RW_HEADS, RW_HEAD_DIM, RW_HEAD_DIM), F32),
        ],
        scratch_shapes=[pltpu.VMEM((seqs, 1, RW_PROJ), F32),
                        pltpu.VMEM((seqs, RW_PAIRS, LANES, LANES), F32)],
        compiler_params=_params("arbitrary", "arbitrary"),
        name="rwkv_state" if has_state else "rwkv_fresh",
    )(*args)
    return o.reshape(n_b * t_len, RW_WIDTH), sh.reshape(n_b, RW_PROJ), st


def _sink_probs(s_parts, sink):
    m = sink
    for s in s_parts:
        m = jnp.maximum(m, jnp.max(s, axis=-1, keepdims=True))
    den = jnp.exp(sink - m)
    es = []
    for s in s_parts:
        e = jnp.exp(s - m)
        den = den + jnp.sum(e, axis=-1, keepdims=True)
        es.append(e)
    inv = 1.0 / den
    return [(e * inv).astype(BF16) for e in es]


def _group_queries(q, hk):
    return jnp.concatenate(
        [q[:, (hk * SWA_GROUP + g) * SWA_HEAD_DIM:(hk * SWA_GROUP + g + 1) * SWA_HEAD_DIM]
         for g in range(SWA_GROUP)], axis=0).astype(BF16)


def _group_sinks(sink_ref, hk, grp):
    sink = jnp.zeros(grp.shape, F32)
    for g in range(SWA_GROUP):
        sink = jnp.where(grp == g, sink_ref[hk * SWA_GROUP + g], sink)
    return sink


def _swa_prompt_kernel(sink_ref, q_ref, kp_ref, kc_ref, vp_ref, vc_ref, o_ref, *, n_win):
    n = pl.program_id(1)
    Wn = WINDOW
    kf = jnp.concatenate([kp_ref[...], kc_ref[...]], axis=0).astype(BF16)
    vf = jnp.concatenate([vp_ref[...], vc_ref[...]], axis=0).astype(BF16)
    gi = lax.broadcasted_iota(jnp.int32, (SWA_GROUP * Wn, 2 * Wn), 0)
    i = _mod_pow2(gi, Wn)
    j = lax.broadcasted_iota(jnp.int32, (SWA_GROUP * Wn, 2 * Wn), 1)
    ok = (j > i) & (j <= i + Wn)
    first_key = jnp.where(n > 0, 0, Wn)
    ok_first = ok & (j >= first_key)
    grp = _div_pow2(lax.broadcasted_iota(jnp.int32, (SWA_GROUP * Wn, 1), 0), Wn)
    ks = [slice(hk * SWA_HEAD_DIM, (hk + 1) * SWA_HEAD_DIM) for hk in range(SWA_KV_HEADS)]
    sinks = [_group_sinks(sink_ref, hk, grp) for hk in range(SWA_KV_HEADS)]
    work = [(w, hk) for w in range(n_win) for hk in range(SWA_KV_HEADS)]
    qw = [q_ref[w * Wn:(w + 1) * Wn, :] * SWA_SCALE for w in range(n_win)]
    sc = {(w, hk): lax.dot_general(_group_queries(qw[w], hk), kf[w * Wn:(w + 2) * Wn, ks[hk]], _NT,
                                   preferred_element_type=F32) for w, hk in work}
    pr = {(w, hk): _sink_probs([jnp.where(ok_first if w == 0 else ok, sc[w, hk], -jnp.inf)], sinks[hk])[0]
          for w, hk in work}
    ov = {(w, hk): jnp.dot(pr[w, hk], vf[w * Wn:(w + 2) * Wn, ks[hk]], preferred_element_type=F32)
          for w, hk in work}
    for w, hk in work:
        for g in range(SWA_GROUP):
            h = hk * SWA_GROUP + g
            o_ref[w * Wn:(w + 1) * Wn, h * SWA_HEAD_DIM:(h + 1) * SWA_HEAD_DIM] = (
                ov[w, hk][g * Wn:(g + 1) * Wn].astype(o_ref.dtype))


def _swa_prompt(p_att, n_b, t_len, sinks):
    nb = t_len // WINDOW
    n_win = math.gcd(nb, SWA_WINDOWS)
    ns = nb // n_win
    rows = n_win * WINDOW
    kcol = SWA_Q_WIDTH // SWA_KV_WIDTH
    vcol = kcol + 1
    cur = lambda col: (lambda b, n: (b * ns + n, col))
    prv = lambda col: (lambda b, n: (b * nb + jnp.maximum(n * n_win - 1, 0), col))
    return pl.pallas_call(
        functools.partial(_swa_prompt_kernel, n_win=n_win),
        grid=(n_b, ns),
        in_specs=[
            pl.BlockSpec(memory_space=pltpu.SMEM),
            pl.BlockSpec((rows, SWA_Q_WIDTH), cur(0)),
            pl.BlockSpec((WINDOW, SWA_KV_WIDTH), prv(kcol)),
            pl.BlockSpec((rows, SWA_KV_WIDTH), cur(kcol)),
            pl.BlockSpec((WINDOW, SWA_KV_WIDTH), prv(vcol)),
            pl.BlockSpec((rows, SWA_KV_WIDTH), cur(vcol)),
        ],
        out_specs=pl.BlockSpec((rows, SWA_Q_WIDTH), lambda b, n: (b * ns + n, 0)),
        out_shape=jax.ShapeDtypeStruct((n_b * t_len, SWA_Q_WIDTH), BF16),
        compiler_params=_params("arbitrary", "arbitrary"),
        name="swa_prompt",
    )(sinks, p_att, p_att, p_att, p_att, p_att)


def _swa_step_kernel(sink_ref, q_ref, kn_ref, vn_ref, ck_ref, cv_ref, o_ref, ok_ref, ov_ref, *, n_seq, t_len):
    T = t_len
    wb = ck_ref.shape[2]
    t_row = _mod_pow2(lax.broadcasted_iota(jnp.int32, (SWA_GROUP * T, 1), 0), T)
    grp = _div_pow2(lax.broadcasted_iota(jnp.int32, (SWA_GROUP * T, 1), 0), T)
    jc = lax.broadcasted_iota(jnp.int32, (SWA_GROUP * T, wb), 1)
    jn = lax.broadcasted_iota(jnp.int32, (SWA_GROUP * T, T), 1)
    ok_c = (t_row + wb - jc < WINDOW)
    ok_n = (jn <= t_row)
    ks = [slice(hk * SWA_HEAD_DIM, (hk + 1) * SWA_HEAD_DIM) for hk in range(SWA_KV_HEADS)]
    sinks = [_group_sinks(sink_ref, hk, grp) for hk in range(SWA_KV_HEADS)]
    work = [(s, hk) for s in range(n_seq) for hk in range(SWA_KV_HEADS)]
    rows = [slice(s * T, (s + 1) * T) for s in range(n_seq)]
    lane_w = lax.broadcasted_iota(jnp.int32, (1, wb), 1)
    pad = jnp.zeros((wb - T, SWA_KV_WIDTH), F32)
    knb, vnb, ckb, cvb = [], [], [], []
    for s in range(n_seq):
        kn = kn_ref[rows[s], :]
        vn = vn_ref[rows[s], :]
        ck = ck_ref[s]
        cv = cv_ref[s]
        kn_t = jnp.transpose(jnp.concatenate([pad, kn], axis=0))
        vn_t = jnp.transpose(jnp.concatenate([pad, vn], axis=0))
        ok_ref[s] = jnp.where(lane_w >= wb - T, kn_t, pltpu.roll(ck, wb - T, axis=1))
        ov_ref[s] = jnp.where(lane_w >= wb - T, vn_t, pltpu.roll(cv, wb - T, axis=1))
        knb.append(kn.astype(BF16))
        vnb.append(vn.astype(BF16))
        ckb.append(ck.astype(BF16))
        cvb.append(cv.astype(BF16))
    qs = [q_ref[rows[s], :] * SWA_SCALE for s in range(n_seq)]
    qh = {(s, hk): _group_queries(qs[s], hk) for s, hk in work}
    sc = {w: jnp.dot(qh[w], ckb[w[0]][ks[w[1]], :], preferred_element_type=F32) for w in work}
    sn = {w: lax.dot_general(qh[w], knb[w[0]][:, ks[w[1]]], _NT, preferred_element_type=F32) for w in work}
    pr = {w: _sink_probs([jnp.where(ok_c, sc[w], -jnp.inf), jnp.where(ok_n, sn[w], -jnp.inf)], sinks[w[1]])
          for w in work}
    ov = {w: lax.dot_general(pr[w][0], cvb[w[0]][ks[w[1]], :], _NT, preferred_element_type=F32)
          + jnp.dot(pr[w][1], vnb[w[0]][:, ks[w[1]]], preferred_element_type=F32) for w in work}
    for s, hk in work:
        for g in range(SWA_GROUP):
            h = hk * SWA_GROUP + g
            o_ref[rows[s], h * SWA_HEAD_DIM:(h + 1) * SWA_HEAD_DIM] = ov[s, hk][g * T:(g + 1) * T].astype(o_ref.dtype)


def _swa_step(p_att, row0, n_b, t_len, cache_k, cache_v, sinks):
    wb = cache_k.shape[1]
    assert wb == WINDOW and t_len % 8 == 0 and t_len < wb
    n_seq = max(1, STEP_ROWS // t_len)
    while n_b % n_seq:
        n_seq //= 2
    rows = n_seq * t_len
    assert rows % BF16_SUBLANES == 0 and row0 % rows == 0
    blk0 = row0 // rows
    kcol = SWA_Q_WIDTH // SWA_KV_WIDTH
    to_cm = lambda c: jnp.transpose(c, (0, 2, 3, 1)).reshape(n_b, SWA_KV_WIDTH, wb)
    from_cm = lambda c: jnp.transpose(c.reshape(n_b, SWA_KV_HEADS, SWA_HEAD_DIM, wb), (0, 3, 1, 2))
    ck = to_cm(cache_k)
    cv = to_cm(cache_v)
    o, nk, nv = pl.pallas_call(
        functools.partial(_swa_step_kernel, n_seq=n_seq, t_len=t_len),
        grid=(n_b // n_seq,),
        in_specs=[
            pl.BlockSpec(memory_space=pltpu.SMEM),
            pl.BlockSpec((rows, SWA_Q_WIDTH), lambda i: (blk0 + i, 0)),
            pl.BlockSpec((rows, SWA_KV_WIDTH), lambda i: (blk0 + i, kcol)),
            pl.BlockSpec((rows, SWA_KV_WIDTH), lambda i: (blk0 + i, kcol + 1)),
            pl.BlockSpec((n_seq, SWA_KV_WIDTH, wb), lambda i: (i, 0, 0)),
            pl.BlockSpec((n_seq, SWA_KV_WIDTH, wb), lambda i: (i, 0, 0)),
        ],
        out_specs=[
            pl.BlockSpec((rows, SWA_Q_WIDTH), lambda i: (i, 0)),
            pl.BlockSpec((n_seq, SWA_KV_WIDTH, wb), lambda i: (i, 0, 0)),
            pl.BlockSpec((n_seq, SWA_KV_WIDTH, wb), lambda i: (i, 0, 0)),
        ],
        out_shape=[
            jax.ShapeDtypeStruct((n_b * t_len, SWA_Q_WIDTH), BF16),
            jax.ShapeDtypeStruct((n_b, SWA_KV_WIDTH, wb), F32),
            jax.ShapeDtypeStruct((n_b, SWA_KV_WIDTH, wb), F32),
        ],
        compiler_params=_params("arbitrary"),
        name="swa_step",
    )(sinks, p_att, p_att, p_att, ck, cv)
    return o, from_cm(nk), from_cm(nv)


def _softmax(s):
    m = jnp.max(s, axis=-1, keepdims=True)
    e = jnp.exp(s - m)
    return (e / jnp.sum(e, axis=-1, keepdims=True)).astype(BF16)


def _mem_attend(items, o_ref):
    scale = MEM_HEAD_DIM ** -0.5
    sc = [lax.dot_general(q, k, _NT, preferred_element_type=F32) * scale for q, k, _, _, _ in items]
    pr = [_softmax(s) for s in sc]
    ov = [jnp.dot(p, it[2], preferred_element_type=F32) for p, it in zip(pr, items)]
    for o, it in zip(ov, items):
        o_ref[it[3], it[4]] = o.astype(o_ref.dtype)


def _mem_prompt_kernel(qa_ref, qb_ref, k_ref, v_ref, o_ref):
    kb = k_ref[...].astype(BF16)
    vb = v_ref[...].astype(BF16)
    half = MEM_HEADS // 2
    items = []
    for h in range(MEM_HEADS):
        q_ref = qa_ref if h < half else qb_ref
        hs = slice((h % half) * MEM_HEAD_DIM, (h % half + 1) * MEM_HEAD_DIM)
        ms = slice(h * MEM_HEAD_DIM, (h + 1) * MEM_HEAD_DIM)
        items.append((q_ref[:, hs].astype(BF16), kb[:, ms], vb[:, ms], slice(None), ms))
    _mem_attend(items, o_ref)


def _mem_prompt(p_att, n_b, t_len, mem_kv):
    m_tok = mem_kv.shape[0] // n_b
    tq = math.gcd(t_len, ROW_TILE)
    nq = t_len // tq
    half_w = MEM_WIDTH // 2
    qcol = (SWA_Q_WIDTH + 2 * SWA_KV_WIDTH) // half_w
    return pl.pallas_call(
        _mem_prompt_kernel,
        grid=(n_b, nq),
        in_specs=[
            pl.BlockSpec((tq, half_w), lambda b, n: (b * nq + n, qcol)),
            pl.BlockSpec((tq, half_w), lambda b, n: (b * nq + n, qcol + 1)),
            pl.BlockSpec((m_tok, MEM_WIDTH), lambda b, n: (b, 0)),
            pl.BlockSpec((m_tok, MEM_WIDTH), lambda b, n: (b, 1)),
        ],
        out_specs=pl.BlockSpec((tq, MEM_WIDTH), lambda b, n: (b * nq + n, 0)),
        out_shape=jax.ShapeDtypeStruct((n_b * t_len, MEM_WIDTH), BF16),
        compiler_params=_params("arbitrary", "arbitrary"),
        name="mem_prompt",
    )(p_att, p_att, mem_kv, mem_kv)


def _mem_step_kernel(qa_ref, qb_ref, k_ref, v_ref, o_ref, *, n_seq, t_len):
    half = MEM_HEADS // 2
    m_tok = k_ref.shape[1] // MEM_HEADS
    items = []
    for s in range(n_seq):
        rs = slice(s * t_len, (s + 1) * t_len)
        for h in range(MEM_HEADS):
            q_ref = qa_ref if h < half else qb_ref
            hs = slice((h % half) * MEM_HEAD_DIM, (h % half + 1) * MEM_HEAD_DIM)
            ms = slice(h * MEM_HEAD_DIM, (h + 1) * MEM_HEAD_DIM)
            hrows = pl.ds(h, m_tok, stride=MEM_HEADS)
            items.append((q_ref[rs, hs].astype(BF16), k_ref[s, hrows, :].astype(BF16),
                          v_ref[s, hrows, :].astype(BF16), rs, ms))
    _mem_attend(items, o_ref)


def _mem_step(p_att, row0, n_b, t_len, cache_k, cache_v):
    m_tok = cache_k.shape[1]
    n_seq = max(1, STEP_ROWS // t_len)
    while n_b % n_seq:
        n_seq //= 2
    rows = n_seq * t_len
    assert rows % BF16_SUBLANES == 0 and row0 % rows == 0
    blk0 = row0 // rows
    half_w = MEM_WIDTH // 2
    qcol = (SWA_Q_WIDTH + 2 * SWA_KV_WIDTH) // half_w
    cache_block = (n_seq, m_tok * MEM_HEADS, MEM_HEAD_DIM)
    ck = cache_k.reshape(n_b, m_tok * MEM_HEADS, MEM_HEAD_DIM)
    cv = cache_v.reshape(n_b, m_tok * MEM_HEADS, MEM_HEAD_DIM)
    return pl.pallas_call(
        functools.partial(_mem_step_kernel, n_seq=n_seq, t_len=t_len),
        grid=(n_b // n_seq,),
        in_specs=[
            pl.BlockSpec((rows, half_w), lambda i: (blk0 + i, qcol)),
            pl.BlockSpec((rows, half_w), lambda i: (blk0 + i, qcol + 1)),
            pl.BlockSpec(cache_block, lambda i: (i, 0, 0)),
            pl.BlockSpec(cache_block, lambda i: (i, 0, 0)),
        ],
        out_specs=pl.BlockSpec((rows, MEM_WIDTH), lambda i: (i, 0)),
        out_shape=jax.ShapeDtypeStruct((n_b * t_len, MEM_WIDTH), BF16),
        compiler_params=_params("arbitrary"),
        name="mem_step",
    )(p_att, p_att, ck, cv)


def _merge_kernel(xn_ref, rwp_ref, rws_ref, swp_ref, sws_ref, mmp_ref, mms_ref,
                  wg0_ref, wg1_ref, wg2_ref, wb0_ref, wb1_ref, wb2_ref, o_ref,
                  g0_scr, g1_scr, g2_scr, b0_scr, b1_scr, b2_scr, *, n_p):
    i = pl.program_id(1)

    @pl.when(i == 0)
    def _():
        for src, dst in ((wg0_ref, g0_scr), (wg1_ref, g1_scr), (wg2_ref, g2_scr),
                         (wb0_ref, b0_scr), (wb1_ref, b1_scr), (wb2_ref, b2_scr)):
            dst[...] = src[...].astype(BF16)

    def run(o_rw_ref, o_sw_ref, o_mm_ref):
        xn = xn_ref[...]
        acc = None
        for g_scr, b_scr, br_ref in ((g0_scr, b0_scr, o_rw_ref), (g1_scr, b1_scr, o_sw_ref),
                                     (g2_scr, b2_scr, o_mm_ref)):
            gate = _sigmoid(jnp.dot(xn, g_scr[...], preferred_element_type=F32))
            t = gate * jnp.dot(br_ref[...], b_scr[...], preferred_element_type=F32)
            acc = t if acc is None else acc + t
        o_ref[...] = acc.astype(o_ref.dtype)

    @pl.when(i < n_p)
    def _():
        run(rwp_ref, swp_ref, mmp_ref)

    @pl.when(i >= n_p)
    def _():
        run(rws_ref, sws_ref, mms_ref)


def _merge(xn, o_p, o_s, w_in, gate_col0, w_br, n_rows_p):
    m, d = xn.shape
    tn = MXU_COLS
    tm = _row_tile(n_rows_p, m - n_rows_p, ROW_TILE)
    n_p = n_rows_p // tm
    n_i = m // tm
    assert gate_col0 % tn == 0 and d % tn == 0
    g0 = gate_col0 // tn
    nj = d // tn
    p_spec = lambda w: pl.BlockSpec((tm, w), lambda j, i: (jnp.minimum(i, n_p - 1), 0))
    s_spec = lambda w: pl.BlockSpec((tm, w), lambda j, i: (jnp.maximum(i - n_p, 0), 0))
    in_specs = [pl.BlockSpec((tm, d), lambda j, i: (i, 0))]
    args = [xn]
    for op, os_ in zip(o_p, o_s):
        in_specs += [p_spec(op.shape[1]), s_spec(os_.shape[1])]
        args += [op, os_]
    for br in range(N_BRANCH):
        in_specs.append(pl.BlockSpec((d, tn), lambda j, i, br=br: (0, g0 + br * nj + j)))
        args.append(w_in)
    for w in w_br:
        in_specs.append(pl.BlockSpec((w.shape[0], tn), lambda j, i: (0, j)))
        args.append(w)
    scratch = [pltpu.VMEM((d, tn), BF16) for _ in range(N_BRANCH)]
    scratch += [pltpu.VMEM((w.shape[0], tn), BF16) for w in w_br]
    return pl.pallas_call(
        functools.partial(_merge_kernel, n_p=n_p),
        grid=(nj, n_i),
        in_specs=in_specs,
        out_specs=pl.BlockSpec((tm, tn), lambda j, i: (i, j)),
        out_shape=jax.ShapeDtypeStruct((m, d), BF16),
        scratch_shapes=scratch,
        compiler_params=_params("arbitrary", "arbitrary"),
        name="gated_merge",
    )(*args)


def _cast_kernel(x_ref, o_ref):
    o_ref[...] = x_ref[...].astype(o_ref.dtype)


def _to_bf16(w, name):
    r, c = w.shape
    tr = math.gcd(r, NORM_ROW_TILE)
    return pl.pallas_call(
        _cast_kernel,
        grid=(r // tr,),
        in_specs=[pl.BlockSpec((tr, c), lambda i: (i, 0))],
        out_specs=pl.BlockSpec((tr, c), lambda i: (i, 0)),
        out_shape=jax.ShapeDtypeStruct((r, c), BF16),
        compiler_params=_params("arbitrary"),
        name=name,
    )(w)


def _wo_kernel(m_ref, w_ref, xp_ref, xs_ref, g1_ref, g2_ref, h_ref, hn_ref, *, n_p):
    i = pl.program_id(0)
    tm = m_ref.shape[0]
    halves = [slice(0, tm // 2), slice(tm // 2, tm)] if tm % (2 * BF16_SUBLANES) == 0 else [slice(0, tm)]

    def run(x_ref):
        for rs in halves:
            f = jnp.dot(m_ref[rs, :], w_ref[...], preferred_element_type=F32)
            h = x_ref[rs, :] + _rms(f, g1_ref[...])
            h_ref[rs, :] = h
            hn_ref[rs, :] = _rms(h, g2_ref[...]).astype(hn_ref.dtype)

    @pl.when(i < n_p)
    def _():
        run(xp_ref)

    @pl.when(i >= n_p)
    def _():
        run(xs_ref)


def _wo_block(merged, w_o, xp, xs, g_post, g_pre_ffn):
    m, d = merged.shape
    n_rows_p = xp.shape[0]
    tm = _row_tile(n_rows_p, m - n_rows_p, NORM_ROW_TILE)
    n_p = n_rows_p // tm
    return pl.pallas_call(
        functools.partial(_wo_kernel, n_p=n_p),
        grid=(m // tm,),
        in_specs=[
            pl.BlockSpec((tm, d), lambda i: (i, 0)),
            pl.BlockSpec((d, d), lambda i: (0, 0), pipeline_mode=pl.Buffered(1)),
            pl.BlockSpec((tm, d), lambda i: (jnp.minimum(i, n_p - 1), 0)),
            pl.BlockSpec((tm, d), lambda i: (jnp.maximum(i - n_p, 0), 0)),
            pl.BlockSpec((1, d), lambda i: (0, 0)),
            pl.BlockSpec((1, d), lambda i: (0, 0)),
        ],
        out_specs=[pl.BlockSpec((tm, d), lambda i: (i, 0)), pl.BlockSpec((tm, d), lambda i: (i, 0))],
        out_shape=[jax.ShapeDtypeStruct((m, d), F32), jax.ShapeDtypeStruct((m, d), BF16)],
        compiler_params=_params("arbitrary"),
        name="wo_residual",
    )(merged, _to_bf16(w_o, "cast_w_o"), xp, xs, g_post.reshape(1, d), g_pre_ffn.reshape(1, d))


def _gelu_tanh(x):
    return 0.5 * x * (1.0 + jnp.tanh(math.sqrt(2.0 / math.pi) * (x + 0.044715 * (x * x * x))))


def _ffn_up_kernel(hp_ref, hs_ref, wg_ref, wv_ref, cw_ref, cb_ref, st_ref, ap_ref, as_ref, cp_ref, cs_ref,
                   wgb_scr, wvb_scr, *, n_p, t_s):
    i = pl.program_id(1)

    @pl.when(i == 0)
    def _():
        wgb_scr[...] = wg_ref[...].astype(BF16)
        wvb_scr[...] = wv_ref[...].astype(BF16)

    tn = wgb_scr.shape[1]
    subs = [slice(c0, c0 + MXU_COLS) for c0 in range(0, tn, MXU_COLS)] if tn % MXU_COLS == 0 else [slice(0, tn)]

    def row_pieces(rows):
        size = NORM_ROW_TILE if rows % NORM_ROW_TILE == 0 and NORM_ROW_TILE % (8 * t_s) == 0 else rows
        return [(r0, r0 + size) for r0 in range(0, rows, size)]

    def conv_glu(cs, zg, zm1, zm2, zv):
        conv = cb_ref[:, cs] + cw_ref[0:1, cs] * zm2 + cw_ref[1:2, cs] * zm1 + cw_ref[2:3, cs] * zg
        return _gelu_tanh(conv) * zv

    def run(h_ref, epilogue):
        items = [(pc, cs) for cs in subs for pc in row_pieces(h_ref.shape[0])]

        def gate_val(pc, cs):
            hn = h_ref[slice(*pc), :]
            return (jnp.dot(hn, wgb_scr[:, cs], preferred_element_type=F32),
                    jnp.dot(hn, wvb_scr[:, cs], preferred_element_type=F32))

        z = {items[0]: gate_val(*items[0])}
        carry = None
        for n, (pc, cs) in enumerate(items):
            if n + 1 < len(items):
                z[items[n + 1]] = gate_val(*items[n + 1])
            zg, zv = z.pop((pc, cs))
            carry = epilogue(pc, cs, zg, zv, carry if pc[0] else None)

    def prompt_epilogue(pc, cs, zg, zv, carry):
        tr = pc[1] - pc[0]
        p2, p1 = carry if carry is not None else (jnp.zeros((1, zg.shape[1]), F32),) * 2
        row = lax.broadcasted_iota(jnp.int32, (tr, 1), 0)
        zm1 = jnp.where(row == 0, p1, pltpu.roll(zg, 1, axis=0))
        zm2 = jnp.where(row == 0, p2, jnp.where(row == 1, p1, pltpu.roll(zg, 2, axis=0)))
        ap_ref[slice(*pc), cs] = conv_glu(cs, zg, zm1, zm2, zv).astype(ap_ref.dtype)
        if pc[1] == ap_ref.shape[0]:
            cp_ref[0, :, cs] = zg[tr - 2:tr, :]
        return zg[tr - 2:tr - 1, :], zg[tr - 1:tr, :]

    def sample_epilogue(pc, cs, zg, zv, carry):
        tr = pc[1] - pc[0]
        nq = tr // t_s
        qs = slice(pc[0] // t_s, pc[1] // t_s)
        tc = zg.shape[1]
        t = lax.broadcasted_iota(jnp.int32, (1, t_s, 1), 1)
        z3 = zg.reshape(nq, t_s, tc)
        s1 = st_ref[qs, 1:2, cs]
        s0 = st_ref[qs, 0:1, cs]
        zm1 = jnp.where(t == 0, s1, pltpu.roll(z3, 1, axis=1))
        zm2 = jnp.where(t == 0, s0, jnp.where(t == 1, s1, pltpu.roll(z3, 2, axis=1)))
        act = conv_glu(cs, z3, zm1, zm2, zv.reshape(nq, t_s, tc))
        as_ref[slice(*pc), cs] = act.reshape(tr, tc).astype(as_ref.dtype)
        cs_ref[qs, :, cs] = z3[:, t_s - 2:t_s, :]
        return None

    @pl.when(i < n_p)
    def _():
        run(hp_ref, prompt_epilogue)

    @pl.when(i >= n_p)
    def _():
        run(hs_ref, sample_epilogue)


def _ffn_up(hn, w_up, conv_w, conv_b, conv_state, n_b_p, t_p, n_b_s, t_s):
    m, d = hn.shape
    d_ff = conv_b.shape[0]
    n_rows_p = n_b_p * t_p
    n_rows_s = n_b_s * t_s
    assert m == n_rows_p + n_rows_s and n_rows_p % n_rows_s == 0 and t_s % 8 == 0 and t_s >= 2
    assert t_p % BF16_SUBLANES == 0 and n_rows_s % BF16_SUBLANES == 0
    tn = math.gcd(d_ff, FFN_TILE)
    nj = d_ff // tn
    last = n_b_p - 1
    s_blk = n_rows_p // n_rows_s
    return pl.pallas_call(
        functools.partial(_ffn_up_kernel, n_p=n_b_p, t_s=t_s),
        grid=(nj, n_b_p + 1),
        in_specs=[
            pl.BlockSpec((t_p, d), lambda j, i: (jnp.minimum(i, last), 0)),
            pl.BlockSpec((n_rows_s, d), lambda j, i: (s_blk, 0)),
            pl.BlockSpec((d, tn), lambda j, i: (0, j)),
            pl.BlockSpec((d, tn), lambda j, i: (0, nj + j)),
            pl.BlockSpec((CONV_W, tn), lambda j, i: (0, j)),
            pl.BlockSpec((1, tn), lambda j, i: (0, j)),
            pl.BlockSpec((n_b_s, CONV_W - 1, tn), lambda j, i: (0, 0, j)),
        ],
        out_specs=[
            pl.BlockSpec((t_p, tn), lambda j, i: (jnp.minimum(i, last), j)),
            pl.BlockSpec((n_rows_s, tn), lambda j, i: (0, j)),
            pl.BlockSpec((1, CONV_W - 1, tn), lambda j, i: (jnp.minimum(i, last), 0, j)),
            pl.BlockSpec((n_b_s, CONV_W - 1, tn), lambda j, i: (0, 0, j)),
        ],
        out_shape=[
            jax.ShapeDtypeStruct((n_rows_p, d_ff), BF16),
            jax.ShapeDtypeStruct((n_rows_s, d_ff), BF16),
            jax.ShapeDtypeStruct((n_b_p, CONV_W - 1, d_ff), F32),
            jax.ShapeDtypeStruct((n_b_s, CONV_W - 1, d_ff), F32),
        ],
        scratch_shapes=[pltpu.VMEM((d, tn), BF16), pltpu.VMEM((d, tn), BF16)],
        compiler_params=_params("arbitrary", "arbitrary"),
        name="ffn_up_convglu",
    )(hn, hn, w_up, w_up, conv_w, conv_b.reshape(1, d_ff), conv_state)


def _ffn_down_kernel(a_ref, w_ref, h_ref, g_ref, y_ref):
    kk = pl.program_id(1)

    @pl.when(kk == 0)
    def _():
        y_ref[...] = jnp.zeros(y_ref.shape, F32)

    last = kk == pl.num_programs(1) - 1

    @pl.when(jnp.logical_not(last))
    def _():
        y_ref[...] += jnp.dot(a_ref[...], w_ref[...].astype(BF16), preferred_element_type=F32)

    @pl.when(last)
    def _():
        tm = y_ref.shape[0]
        n_pieces = 4 if tm % (4 * BF16_SUBLANES) == 0 else 1
        tr = tm // n_pieces
        wb = w_ref[...].astype(BF16)
        for r in range(n_pieces):
            rs = slice(r * tr, (r + 1) * tr)
            f = y_ref[rs, :] + jnp.dot(a_ref[rs, :], wb, preferred_element_type=F32)
            y_ref[rs, :] = h_ref[rs, :] + _rms(f, g_ref[...])


def _ffn_down(act, w_down, h, g, row0):
    d_ff, d = w_down.shape
    n_rows = act.shape[0]
    tm = math.gcd(math.gcd(n_rows, row0) if row0 else n_rows, ROW_TILE)
    tk = math.gcd(d_ff, FFN_TILE)
    i0 = row0 // tm
    return pl.pallas_call(
        _ffn_down_kernel,
        grid=(n_rows // tm, d_ff // tk),
        in_specs=[
            pl.BlockSpec((tm, tk), lambda i, k: (i, k)),
            pl.BlockSpec((tk, d), lambda i, k: (k, 0)),
            pl.BlockSpec((tm, d), lambda i, k: (i0 + i, 0)),
            pl.BlockSpec((1, d), lambda i, k: (0, 0)),
        ],
        out_specs=pl.BlockSpec((tm, d), lambda i, k: (i, 0)),
        out_shape=jax.ShapeDtypeStruct((n_rows, d), F32),
        compiler_params=_params("arbitrary", "arbitrary"),
        name="ffn_down",
    )(act, w_down, h, g.reshape(1, d))


def kernel(x_prompt, x_sample, cache_swa_k, cache_swa_v, cache_mem_k, cache_mem_v, state_rwkv_shift, state_rwkv_wkv, state_ffn_conv, mem_prompt, g_pre_mix, w_in, mu_rwkv, w_decay_up, w0_decay, w_a_up, a0, w_gate_up, k_k, k_a, r_k, ln_x_w, ln_x_b, swa_sinks, g_mem, w_mem_kv, w_br_rwkv, w_br_swa, w_br_mem, w_o, g_post_mix, g_pre_ffn, w_ffn_up, conv_w, conv_b, w_ffn_down, g_post_ffn):
    b_p, t_p, d = x_prompt.shape
    b_s, t_s, _ = x_sample.shape
    n_p, n_s = b_p * t_p, b_s * t_s
    m_tok = mem_prompt.shape[1]
    rw_prm = dict(mu_rwkv=mu_rwkv, w_decay_up=w_decay_up, w0_decay=w0_decay, w_a_up=w_a_up, a0=a0,
                  w_gate_up=w_gate_up, k_k=k_k, k_a=k_a, r_k=r_k, ln_x_w=ln_x_w, ln_x_b=ln_x_b)
    xp = x_prompt.reshape(n_p, d)
    xs = x_sample.reshape(n_s, d)

    xn, p_att = _norm_matmul(xp, xs, g_pre_mix, w_in, RW_PROJ, ATT_WIDTH, "norm_proj_attn")
    p_rw = _matmul(xn, w_in, 0, RW_PROJ, RW_PROJ // 2, ROW_TILE, "proj_rwkv")

    _, mem_kv = _norm_matmul(mem_prompt.reshape(b_p * m_tok, d), None, g_mem, w_mem_kv,
                             0, 2 * MEM_WIDTH, "norm_proj_mem_kv")

    o_rw_p, shift_p, wkv_p = _rwkv(p_rw, 0, b_p, t_p, None, None, rw_prm, RWKV_PARTS)
    o_rw_s, shift_s, wkv_s = _rwkv(p_rw, n_p, b_s, t_s, state_rwkv_shift, state_rwkv_wkv, rw_prm, RWKV_PARTS)

    o_sw_p = _swa_prompt(p_att, b_p, t_p, swa_sinks)
    o_sw_s, swa_k_s, swa_v_s = _swa_step(p_att, n_p, b_s, t_s, cache_swa_k, cache_swa_v, swa_sinks)

    o_mm_p = _mem_prompt(p_att, b_p, t_p, mem_kv)
    o_mm_s = _mem_step(p_att, n_p, b_s, t_s, cache_mem_k, cache_mem_v)

    merged = _merge(xn, (o_rw_p, o_sw_p, o_mm_p), (o_rw_s, o_sw_s, o_mm_s), w_in,
                    RW_PROJ + ATT_WIDTH, (w_br_rwkv, w_br_swa, w_br_mem), n_p)
    h, hn = _wo_block(merged, w_o, xp, xs, g_post_mix, g_pre_ffn)
    act_p, act_s, conv_p, conv_s = _ffn_up(hn, w_ffn_up, conv_w, conv_b, state_ffn_conv, b_p, t_p, b_s, t_s)
    y_p = _ffn_down(act_p, w_ffn_down, h, g_post_ffn, 0)
    y_s = _ffn_down(act_s, w_ffn_down, h, g_post_ffn, n_p)

    keep = min(WINDOW, t_p)
    kv_p = jnp.stack([lax.slice(p_att, ((b + 1) * t_p - keep, SWA_Q_WIDTH),
                                ((b + 1) * t_p, SWA_Q_WIDTH + 2 * SWA_KV_WIDTH)) for b in range(b_p)])
    swa_k_p = kv_p[..., :SWA_KV_WIDTH].reshape(b_p, keep, SWA_KV_HEADS, SWA_HEAD_DIM)
    swa_v_p = kv_p[..., SWA_KV_WIDTH:].reshape(b_p, keep, SWA_KV_HEADS, SWA_HEAD_DIM)
    mem_k_p = mem_kv[:, :MEM_WIDTH].reshape(b_p, m_tok, MEM_HEADS, MEM_HEAD_DIM)
    mem_v_p = mem_kv[:, MEM_WIDTH:].reshape(b_p, m_tok, MEM_HEADS, MEM_HEAD_DIM)
    return (y_p.reshape(b_p, t_p, d), y_s.reshape(b_s, t_s, d), swa_k_p, swa_v_p, mem_k_p, mem_v_p,
            shift_p, wkv_p, conv_p, swa_k_s, swa_v_s, shift_s, wkv_s, conv_s)
```

```python
import functools
import math

import jax
import jax.numpy as jnp
from jax import lax
from jax.experimental import pallas as pl
from jax.experimental.pallas import tpu as pltpu

F32 = jnp.float32
BF16 = jnp.bfloat16

RW_HEADS = 12
RW_HEAD_DIM = 64
RW_WIDTH = RW_HEADS * RW_HEAD_DIM
RW_DECAY_RANK = 64
RW_A_RANK = 64
RW_GATE_RANK = 128
RW_PROJ = 3 * RW_WIDTH + RW_DECAY_RANK + RW_A_RANK + RW_GATE_RANK
RW_GN_EPS = 6.4e-4
RW_PAIRS = RW_HEADS // 2
SWA_Q_HEADS = 12
SWA_KV_HEADS = 4
SWA_GROUP = SWA_Q_HEADS // SWA_KV_HEADS
SWA_HEAD_DIM = 64
SWA_Q_WIDTH = SWA_Q_HEADS * SWA_HEAD_DIM
SWA_KV_WIDTH = SWA_KV_HEADS * SWA_HEAD_DIM
WINDOW = 128
MEM_HEADS = 4
MEM_HEAD_DIM = 128
MEM_WIDTH = MEM_HEADS * MEM_HEAD_DIM
N_BRANCH = 3
CONV_W = 3
NORM_EPS = 1e-6
ATT_WIDTH = SWA_Q_WIDTH + 2 * SWA_KV_WIDTH + MEM_WIDTH
SWA_SCALE = SWA_HEAD_DIM ** -0.5
assert math.frexp(SWA_SCALE)[0] == 0.5

LANES = 128
BF16_SUBLANES = 16
MXU_COLS = 256
VMEM_LIMIT_BYTES = 56 * 1024 * 1024

RWKV_ROWS = 64
RWKV_GROUPS = 2
RWKV_PARTS = dict(z=1, lv=1, pow=1, app=1, om=1, hg=1, out=1, st=1)
RWKV_SUM_PARTS = 1
ROW_TILE = 1024
NORM_ROW_TILE = 512
FFN_TILE = 512
STEP_ROWS = 64
SWA_WINDOWS = 4


def _params(*sem):
    return pltpu.CompilerParams(dimension_semantics=sem, vmem_limit_bytes=VMEM_LIMIT_BYTES)


def _row_tile(n_p, n_s, target):
    t = math.gcd(math.gcd(n_p, n_s), target)
    assert t % BF16_SUBLANES == 0, (n_p, n_s, target)
    return t


def _div_pow2(x, n):
    assert n & (n - 1) == 0
    return lax.shift_right_logical(x, n.bit_length() - 1)


def _mod_pow2(x, n):
    assert n & (n - 1) == 0
    return jnp.bitwise_and(x, n - 1)


def _rms(x, g):
    ms = jnp.mean(x * x, axis=-1, keepdims=True)
    return x * lax.rsqrt(ms + NORM_EPS) * g


def _norm_mm_kernel(*refs, n_p, two):
    if two:
        xp_ref, xs_ref, g_ref, w_ref, xn_ref, o_ref, wb_scr = refs
    else:
        xp_ref, g_ref, w_ref, xn_ref, o_ref, wb_scr = refs
    i = pl.program_id(0)

    @pl.when(i == 0)
    def _():
        wb_scr[...] = w_ref[...].astype(BF16)

    def run(x_ref):
        xn = _rms(x_ref[...], g_ref[...]).astype(BF16)
        xn_ref[...] = xn
        o_ref[...] = jnp.dot(xn, wb_scr[...], preferred_element_type=F32)

    if two:
        @pl.when(i < n_p)
        def _():
            run(xp_ref)

        @pl.when(i >= n_p)
        def _():
            run(xs_ref)
    else:
        run(xp_ref)


def _norm_matmul(xp, xs, g, w, col0, n_cols, name):
    n_rows_p, d = xp.shape
    two = xs is not None
    n_rows_s = xs.shape[0] if two else 0
    tm = _row_tile(n_rows_p, n_rows_s, NORM_ROW_TILE) if two else math.gcd(n_rows_p, NORM_ROW_TILE)
    n_p, n_s = n_rows_p // tm, n_rows_s // tm
    assert col0 % LANES == 0 and n_cols % LANES == 0
    once = pl.Buffered(1)
    in_specs = [pl.BlockSpec((tm, d), lambda i: (jnp.minimum(i, n_p - 1), 0))]
    args = [xp]
    if two:
        in_specs.append(pl.BlockSpec((tm, d), lambda i: (jnp.maximum(i - n_p, 0), 0)))
        args.append(xs)
    in_specs += [pl.BlockSpec((1, d), lambda i: (0, 0)),
                 pl.BlockSpec((pl.Element(d), pl.Element(n_cols)), lambda i: (0, col0), pipeline_mode=once)]
    args += [g.reshape(1, d), w]
    m = n_rows_p + n_rows_s
    return pl.pallas_call(
        functools.partial(_norm_mm_kernel, n_p=n_p, two=two),
        grid=(n_p + n_s,),
        in_specs=in_specs,
        out_specs=[pl.BlockSpec((tm, d), lambda i: (i, 0)), pl.BlockSpec((tm, n_cols), lambda i: (i, 0))],
        out_shape=[jax.ShapeDtypeStruct((m, d), BF16), jax.ShapeDtypeStruct((m, n_cols), F32)],
        scratch_shapes=[pltpu.VMEM((d, n_cols), BF16)],
        compiler_params=_params("arbitrary"),
        name=name,
    )(*args)


def _mm_kernel(a_ref, w_ref, o_ref, wb_ref):
    @pl.when(pl.program_id(1) == 0)
    def _():
        wb_ref[...] = w_ref[...].astype(BF16)

    o_ref[...] = jnp.dot(a_ref[...], wb_ref[...], preferred_element_type=F32)


def _matmul(a, w, col0, n_cols, tn, tm, name):
    m, k = a.shape
    tm = math.gcd(m, tm)
    assert n_cols % tn == 0 and col0 % LANES == 0 and tn % LANES == 0
    return pl.pallas_call(
        _mm_kernel,
        grid=(n_cols // tn, m // tm),
        in_specs=[
            pl.BlockSpec((tm, k), lambda j, i: (i, 0)),
            pl.BlockSpec((pl.Element(k), pl.Element(tn)),
                         lambda j, i: (0, pl.multiple_of(col0 + j * tn, LANES))),
        ],
        out_specs=pl.BlockSpec((tm, tn), lambda j, i: (i, j)),
        out_shape=jax.ShapeDtypeStruct((m, n_cols), F32),
        scratch_shapes=[pltpu.VMEM((k, tn), BF16)],
        compiler_params=_params("arbitrary", "arbitrary"),
        name=name,
    )(a, w)


_NN = (((1,), (0,)), ((), ()))
_NT = (((1,), (1,)), ((), ()))
_TN = (((0,), (0,)), ((), ()))


def _bf_parts(x, n):
    parts = []
    r = x
    for i in range(n):
        h = r.astype(BF16)
        parts.append(h)
        if i + 1 < n:
            r = r - h.astype(F32)
    return parts


def _dotp(a, b, dn, pa, pb):
    pa_parts = _bf_parts(a, pa)
    pb_parts = _bf_parts(b, pb)
    acc = None
    for i in range(pa):
        for j in range(pb):
            if i + j < max(pa, pb):
                t = lax.dot_general(pa_parts[i], pb_parts[j], dn, preferred_element_type=F32)
                acc = t if acc is None else acc + t
    return acc


def _softplus(x):
    return jnp.maximum(x, 0.0) + jnp.log(1.0 + jnp.exp(-jnp.abs(x)))


def _sigmoid(x):
    return 1.0 / (1.0 + jnp.exp(-x))


def _rwkv_kernel(*refs, seq_rows, n_seq, n_par, has_state, pp):
    n_in = n_par + (2 if has_state else 0)
    p_refs = refs[0:n_par]
    if has_state:
        sh0_ref, s0_ref = refs[n_par:n_in]
    (mu_ref, wda_ref, wa0_ref, wg_ref, vec_ref,
     o_ref, sh_out_ref, s_out_ref, prev_scr, st_scr) = refs[n_in:]
    c = pl.program_id(1)
    n_c = pl.num_programs(1)
    C = seq_rows
    R = seq_rows * n_seq
    W = RW_WIDTH
    HD = RW_HEAD_DIM
    groups = range(n_par)

    @pl.when(c == 0)
    def _():
        st_scr[...] = jnp.zeros(st_scr.shape, F32)
        if has_state:
            prev_scr[...] = sh0_ref[...]
            for s in range(n_par * n_seq):
                for q in range(RW_PAIRS):
                    st_scr[s, q, 0:HD, 0:HD] = s0_ref[s, 2 * q]
                    st_scr[s, q, HD:2 * HD, HD:2 * HD] = s0_ref[s, 2 * q + 1]
        else:
            prev_scr[...] = jnp.zeros(prev_scr.shape, F32)

    k_k = vec_ref[0:1, :]
    k_a = vec_ref[1:2, :]
    r_k = vec_ref[2:3, :]
    ln_w = vec_ref[3:4, :]
    ln_b = vec_ref[4:5, :]
    lane = lax.broadcasted_iota(jnp.int32, (1, LANES), 1)
    row = lax.broadcasted_iota(jnp.int32, (R, 1), 0)

    li = lax.broadcasted_iota(jnp.int32, (LANES, LANES), 0)
    lj = lax.broadcasted_iota(jnp.int32, (LANES, LANES), 1)
    same_head = _div_pow2(li, RW_HEAD_DIM) == _div_pow2(lj, RW_HEAD_DIM)
    ones_bd = jnp.where(same_head, 1.0, 0.0).astype(BF16)

    def head_sum(*zs):
        stack = jnp.concatenate([z[:, q * LANES:(q + 1) * LANES] for z in zs for q in range(RW_PAIRS)], axis=0)
        acc = None
        for part in _bf_parts(stack, RWKV_SUM_PARTS):
            t = jnp.dot(part, ones_bd, preferred_element_type=F32)
            acc = t if acc is None else acc + t
        outs = [jnp.concatenate([acc[(n * RW_PAIRS + q) * R:(n * RW_PAIRS + q + 1) * R] for q in range(RW_PAIRS)],
                                axis=1) for n in range(len(zs))]
        return outs[0] if len(zs) == 1 else outs

    ri = lax.broadcasted_iota(jnp.int32, (R, R), 0)
    rj = lax.broadcasted_iota(jnp.int32, (R, R), 1)
    same_seq = _div_pow2(ri, C) == _div_pow2(rj, C)
    causal = jnp.where(same_seq & (rj <= ri), 1.0, 0.0).astype(BF16)

    def prepare_proj(u):
        p = p_refs[u][...]
        prev = pltpu.roll(p, 1, axis=0)
        for s in range(n_seq):
            prev = jnp.where(row == s * C, prev_scr[u * n_seq + s], prev)
        for s in range(n_seq):
            prev_scr[u * n_seq + s] = p[(s + 1) * C - 1:(s + 1) * C, :]
            sh_out_ref[u * n_seq + s] = p[(s + 1) * C - 1:(s + 1) * C, :]
        x = p + mu_ref[...] * (prev - p)
        r = x[:, 0:W]
        k = x[:, W:2 * W]
        v = x[:, 2 * W:3 * W]
        wa_in = x[:, 3 * W:3 * W + LANES]
        g_in = x[:, 3 * W + LANES:3 * W + 2 * LANES]
        wa_act = jnp.where(lane < RW_DECAY_RANK, jnp.tanh(wa_in), wa_in)
        za = jnp.dot(wa_act.astype(BF16), wda_ref[...], preferred_element_type=F32) + wa0_ref[...]
        w_log = -_softplus(-za[:, 0:W]) - 0.5
        lw = -jnp.exp(w_log)
        a = _sigmoid(za[:, W:2 * W])
        g = jnp.dot(_sigmoid(g_in).astype(BF16), wg_ref[...], preferred_element_type=F32)
        kk = k * k_k
        k = k * (1.0 + (a - 1.0) * k_a)
        return dict(r=r, k=k, v=v, g=g, a=a, lw=lw, kk=kk)

    def prepare_sums(d):
        d["kk_sq"], d["rk"] = head_sum(d["kk"] * d["kk"], d["r"] * d["k"] * r_k)
        cl = None
        for part in _bf_parts(d["lw"], 2):
            t1 = jnp.dot(causal, part, preferred_element_type=F32)
            cl = t1 if cl is None else cl + t1
        d["cl"] = cl
        return d

    def prepare_factors(d):
        r, k, v, g, a, lw, cl, rk = (d[n] for n in ("r", "k", "v", "g", "a", "lw", "cl", "rk"))
        kk = d["kk"] / jnp.maximum(jnp.sqrt(d["kk_sq"]), 1e-12)
        b = kk * a
        if n_seq == 1:
            ct = cl[R - 1:R, :]
        else:
            last = cl.reshape(n_seq, C, W)[:, C - 1:C, :]
            ct = jnp.broadcast_to(last, (n_seq, C, W)).reshape(R, W)
        e_neg = jnp.exp(-cl)
        e_end = jnp.exp(ct - cl)
        return dict(v=v, g=g, bonus=rk * v, at=-kk * jnp.exp(cl - lw), rt=r * jnp.exp(cl), bt=b * e_neg,
                    kt=k * e_neg, bts=b * e_end, kts=k * e_end,
                    p_end=jnp.exp(ct))

    prep = [prepare_factors(prepare_sums(prepare_proj(u))) for u in groups]

    strict = jnp.where(same_seq & (rj < ri), 1.0, 0.0)
    incl = jnp.where(same_seq & (rj <= ri), 1.0, 0.0)
    strict2 = jnp.concatenate([strict, strict], axis=1)
    incl2 = jnp.concatenate([incl, incl], axis=1)
    m0 = jnp.where(lane < RW_HEAD_DIM, 1.0, 0.0)
    m1 = jnp.where(lane >= RW_HEAD_DIM, 1.0, 0.0)
    bd_mask = jnp.where(same_head, 1.0, 0.0)
    zeros_rl = jnp.zeros((R, LANES), F32)
    zeros_r2 = jnp.zeros((R, 2 * R), F32)
    n_dbl = max(1, (C - 1).bit_length())
    sls = [slice(q * LANES, (q + 1) * LANES) for q in range(RW_PAIRS)]
    items = [(u, q) for u in groups for q in range(RW_PAIRS)]

    def col(name, it):
        return prep[it[0]][name][:, sls[it[1]]]

    v0 = {it: jnp.concatenate([col("v", it), zeros_rl], axis=1) for it in items}
    def against_keys(name, it):
        x, bq, kq = col(name, it), col("bt", it), col("kt", it)
        x2 = jnp.concatenate([x * m0, x * m1], axis=0)
        y4 = jnp.concatenate([bq, kq, kq, bq], axis=0)
        return _dotp(x2, y4, _NT, pp["z"], pp["z"])

    zs = {it: against_keys("at", it) for it in items}
    zt0 = {it: zs[it][0:R, 0:2 * R] * strict2 for it in items}
    zt1 = {it: zs[it][R:2 * R, 2 * R:4 * R] * strict2 for it in items}
    half0 = jnp.where(lax.broadcasted_iota(jnp.int32, (1, 2 * R), 1) < R, 1.0, 0.0)
    half1 = 1.0 - half0
    pw = {it: jnp.concatenate([zt0[it] * half0, zt1[it] * half1], axis=0) for it in items}
    wmat = {}
    for it in items:
        lhs = jnp.concatenate([jnp.concatenate([zt0[it], zeros_r2], axis=1),
                               jnp.concatenate([zeros_r2, zt1[it]], axis=1)], axis=0)
        vq = col("v", it)
        rhs = jnp.concatenate([zeros_rl, vq, vq, zeros_rl], axis=0)
        lv = _dotp(lhs, rhs, _NN, pp["lv"], pp["lv"])
        xa = col("at", it)
        wmat[it] = jnp.concatenate([lv, jnp.concatenate([xa * m0, xa * m1], axis=0)], axis=1)
    def lower_rows(a, lo):
        return jnp.concatenate([a[lo:R], a[R + lo:2 * R]], axis=0)

    def scatter_rows(base, upd, lo):
        h = R - lo
        top = [jnp.zeros((lo, upd.shape[1]), F32)] * 2 if base is None else [base[0:lo], base[R:R + lo]]
        mid = [upd[0:h], upd[h:2 * h]] if base is None else [base[lo:R] + upd[0:h], base[R + lo:2 * R] + upd[h:2 * h]]
        return jnp.concatenate([top[0], mid[0], top[1], mid[1]], axis=0)

    for step in range(n_dbl):
        lo = 1 << step
        if n_seq == 1 and lo % 8 == 0:
            wmat = {it: scatter_rows(wmat[it], _dotp(lower_rows(pw[it], lo), wmat[it], _NN, pp["app"], pp["app"]), lo)
                    for it in items}
        else:
            wmat = {it: wmat[it] + _dotp(pw[it], wmat[it], _NN, pp["app"], pp["app"]) for it in items}
        if step + 1 < n_dbl:
            if n_seq == 1 and (2 * lo) % 8 == 0:
                pw = {it: scatter_rows(None, _dotp(lower_rows(pw[it], 2 * lo), pw[it], _NN, pp["pow"], pp["pow"]),
                                       2 * lo) for it in items}
            else:
                pw = {it: _dotp(pw[it], pw[it], _NN, pp["pow"], pp["pow"]) for it in items}
    zs = {it: against_keys("rt", it) for it in items}
    rhat, o0 = {}, {}
    for it in items:
        zb0 = zs[it][0:R, 0:2 * R] * incl2
        zb1 = zs[it][R:2 * R, 2 * R:4 * R] * incl2
        lhs = jnp.concatenate([jnp.concatenate([zb0, zeros_r2], axis=1),
                               jnp.concatenate([zeros_r2, zb1], axis=1)], axis=0)
        rhs = jnp.concatenate([wmat[it][0:R], v0[it], v0[it], wmat[it][R:2 * R]], axis=0)
        om = _dotp(lhs, rhs, _NN, pp["om"], pp["om"])
        rhat[it] = col("rt", it) + om[0:R, LANES:] + om[R:2 * R, LANES:]
        o0[it] = om[0:R, 0:LANES] * m0 + om[R:2 * R, 0:LANES] * m1
    hgt = {}
    for it in items:
        bq, kq = col("bts", it), col("kts", it)
        for s in range(n_seq):
            rs = slice(s * C, (s + 1) * C)
            ys4 = jnp.concatenate([bq[rs] * m0, kq[rs] * m0, kq[rs] * m1, bq[rs] * m1], axis=0)
            r4 = jnp.concatenate([wmat[it][rs], v0[it][rs], v0[it][rs],
                                  wmat[it][R + s * C:R + (s + 1) * C]], axis=0)
            hgt[it, s] = _dotp(r4, ys4, _TN, pp["hg"], pp["hg"])
    o_pair = {}
    for it in items:
        u, q = it
        o_seq = []
        for s in range(n_seq):
            rs = slice(s * C, (s + 1) * C)
            s_old = st_scr[u * n_seq + s, q]
            ht = hgt[it, s][0:LANES] * bd_mask
            o_seq.append(_dotp(rhat[it][rs], s_old, _NT, pp["out"], pp["out"]) + o0[it][rs])
            decayed = s_old * prep[u]["p_end"][s * C:s * C + 1, sls[q]]
            st_scr[u * n_seq + s, q] = decayed + _dotp(s_old, hgt[it, s][LANES:], _NN, pp["st"], pp["st"]) + ht
        o_pair[it] = o_seq[0] if n_seq == 1 else jnp.concatenate(o_seq, axis=0)

    inv_n = 1.0 / RW_HEAD_DIM
    o_all = [jnp.concatenate([o_pair[u, q] for q in range(RW_PAIRS)], axis=1) for u in groups]
    sums = head_sum(*o_all)
    dev = [o_all[u] - (sums if n_par == 1 else sums[u]) * inv_n for u in groups]
    sums = head_sum(*[d * d for d in dev])
    for u in groups:
        var = (sums if n_par == 1 else sums[u]) * inv_n
        o = dev[u] * lax.rsqrt(var + RW_GN_EPS) * ln_w + ln_b
        o_ref[u] = ((o + prep[u]["bonus"]) * prep[u]["g"]).astype(o_ref.dtype)

    @pl.when(c == n_c - 1)
    def _():
        for s in range(n_par * n_seq):
            for q in range(RW_PAIRS):
                s_out_ref[s, 2 * q] = st_scr[s, q, 0:HD, 0:HD]
                s_out_ref[s, 2 * q + 1] = st_scr[s, q, HD:2 * HD, HD:2 * HD]


def _rwkv(p_rw, row0, n_b, t_len, shift0, s0, prm, pp):
    if t_len >= RWKV_ROWS:
        seq_rows, n_seq = RWKV_ROWS, 1
    else:
        seq_rows, n_seq = t_len, RWKV_ROWS // t_len
    assert t_len % seq_rows == 0 and n_b % n_seq == 0 and row0 % (seq_rows * n_seq) == 0
    rows = seq_rows * n_seq
    n_c = t_len // seq_rows
    assert n_seq == 1 or n_c == 1
    blk0 = row0 // rows
    has_state = s0 is not None
    n_grp = n_b // n_seq
    n_par = math.gcd(n_grp, RWKV_GROUPS)
    seqs = n_par * n_seq

    wda = jnp.zeros((LANES, 2 * RW_WIDTH), F32)
    wda = wda.at[0:RW_DECAY_RANK, 0:RW_WIDTH].set(prm["w_decay_up"])
    wda = wda.at[RW_DECAY_RANK:, RW_WIDTH:].set(prm["w_a_up"])
    wa0 = jnp.concatenate([prm["w0_decay"], prm["a0"]]).reshape(1, 2 * RW_WIDTH)
    vecs = jnp.zeros((8, RW_WIDTH), F32)
    for i, name in enumerate(("k_k", "k_a", "r_k", "ln_x_w", "ln_x_b")):
        vecs = vecs.at[i].set(prm[name].reshape(RW_WIDTH))

    const = lambda *shape: pl.BlockSpec(shape, lambda bi, c: (0,) * len(shape))
    in_specs = [pl.BlockSpec((rows, RW_PROJ), lambda bi, c, u=u: (blk0 + (bi * n_par + u) * n_c + c, 0))
                for u in range(n_par)]
    args = [p_rw] * n_par
    st_block = (seqs, RW_HEADS, RW_HEAD_DIM, RW_HEAD_DIM)
    if has_state:
        in_specs += [pl.BlockSpec((seqs, 1, RW_PROJ), lambda bi, c: (bi, 0, 0)),
                     pl.BlockSpec(st_block, lambda bi, c: (bi, 0, 0, 0))]
        args += [shift0.reshape(n_b, 1, RW_PROJ), s0]
    in_specs += [const(1, RW_PROJ), const(LANES, 2 * RW_WIDTH), const(1, 2 * RW_WIDTH),
                 const(RW_GATE_RANK, RW_WIDTH), const(8, RW_WIDTH)]
    args += [prm["mu_rwkv"].reshape(1, RW_PROJ), wda.astype(BF16), wa0,
             prm["w_gate_up"].astype(BF16), vecs]

    o, sh, st = pl.pallas_call(
        functools.partial(_rwkv_kernel, seq_rows=seq_rows, n_seq=n_seq, n_par=n_par, has_state=has_state, pp=pp),
        grid=(n_grp // n_par, n_c),
        in_specs=in_specs,
        out_specs=[
            pl.BlockSpec((n_par, rows, RW_WIDTH), lambda bi, c: (bi, c, 0)),
            pl.BlockSpec((seqs, 1, RW_PROJ), lambda bi, c: (bi, 0, 0)),
            pl.BlockSpec(st_block, lambda bi, c: (bi, 0, 0, 0)),
        ],
        out_shape=[
            jax.ShapeDtypeStruct((n_grp, n_c * rows, RW_WIDTH), BF16),
            jax.ShapeDtypeStruct((n_b, 1, RW_PROJ), F32),
            jax.ShapeDtypeStruct((n_b, RW_HEADS, RW_HEAD_DIM, RW_HEAD_DIM), F32),
        ],
        scratch_shapes=[pltpu.VMEM((seqs, 1, RW_PROJ), F32),
                        pltpu.VMEM((seqs, RW_PAIRS, LANES, LANES), F32)],
        compiler_params=_params("arbitrary", "arbitrary"),
        name="rwkv_state" if has_state else "rwkv_fresh",
    )(*args)
    return o.reshape(n_b * t_len, RW_WIDTH), sh.reshape(n_b, RW_PROJ), st


def _sink_probs(s_parts, sink):
    m = sink
    for s in s_parts:
        m = jnp.maximum(m, jnp.max(s, axis=-1, keepdims=True))
    den = jnp.exp(sink - m)
    es = []
    for s in s_parts:
        e = jnp.exp(s - m)
        den = den + jnp.sum(e, axis=-1, keepdims=True)
        es.append(e)
    inv = 1.0 / den
    return [(e * inv).astype(BF16) for e in es]


def _group_queries(q, hk):
    return jnp.concatenate(
        [q[:, (hk * SWA_GROUP + g) * SWA_HEAD_DIM:(hk * SWA_GROUP + g + 1) * SWA_HEAD_DIM]
         for g in range(SWA_GROUP)], axis=0).astype(BF16)


def _group_sinks(sink_ref, hk, grp):
    sink = jnp.zeros(grp.shape, F32)
    for g in range(SWA_GROUP):
        sink = jnp.where(grp == g, sink_ref[hk * SWA_GROUP + g], sink)
    return sink


def _swa_prompt_kernel(sink_ref, q_ref, kp_ref, kc_ref, vp_ref, vc_ref, o_ref, *, n_win):
    n = pl.program_id(1)
    Wn = WINDOW
    kf = jnp.concatenate([kp_ref[...], kc_ref[...]], axis=0).astype(BF16)
    vf = jnp.concatenate([vp_ref[...], vc_ref[...]], axis=0).astype(BF16)
    gi = lax.broadcasted_iota(jnp.int32, (SWA_GROUP * Wn, 2 * Wn), 0)
    i = _mod_pow2(gi, Wn)
    j = lax.broadcasted_iota(jnp.int32, (SWA_GROUP * Wn, 2 * Wn), 1)
    ok = (j > i) & (j <= i + Wn)
    first_key = jnp.where(n > 0, 0, Wn)
    ok_first = ok & (j >= first_key)
    grp = _div_pow2(lax.broadcasted_iota(jnp.int32, (SWA_GROUP * Wn, 1), 0), Wn)
    ks = [slice(hk * SWA_HEAD_DIM, (hk + 1) * SWA_HEAD_DIM) for hk in range(SWA_KV_HEADS)]
    sinks = [_group_sinks(sink_ref, hk, grp) for hk in range(SWA_KV_HEADS)]
    work = [(w, hk) for w in range(n_win) for hk in range(SWA_KV_HEADS)]
    qw = [q_ref[w * Wn:(w + 1) * Wn, :] * SWA_SCALE for w in range(n_win)]
    sc = {(w, hk): lax.dot_general(_group_queries(qw[w], hk), kf[w * Wn:(w + 2) * Wn, ks[hk]], _NT,
                                   preferred_element_type=F32) for w, hk in work}
    pr = {(w, hk): _sink_probs([jnp.where(ok_first if w == 0 else ok, sc[w, hk], -jnp.inf)], sinks[hk])[0]
          for w, hk in work}
    ov = {(w, hk): jnp.dot(pr[w, hk], vf[w * Wn:(w + 2) * Wn, ks[hk]], preferred_element_type=F32)
          for w, hk in work}
    for w, hk in work:
        for g in range(SWA_GROUP):
            h = hk * SWA_GROUP + g
            o_ref[w * Wn:(w + 1) * Wn, h * SWA_HEAD_DIM:(h + 1) * SWA_HEAD_DIM] = (
                ov[w, hk][g * Wn:(g + 1) * Wn].astype(o_ref.dtype))


def _swa_prompt(p_att, n_b, t_len, sinks):
    nb = t_len // WINDOW
    n_win = math.gcd(nb, SWA_WINDOWS)
    ns = nb // n_win
    rows = n_win * WINDOW
    kcol = SWA_Q_WIDTH // SWA_KV_WIDTH
    vcol = kcol + 1
    cur = lambda col: (lambda b, n: (b * ns + n, col))
    prv = lambda col: (lambda b, n: (b * nb + jnp.maximum(n * n_win - 1, 0), col))
    return pl.pallas_call(
        functools.partial(_swa_prompt_kernel, n_win=n_win),
        grid=(n_b, ns),
        in_specs=[
            pl.BlockSpec(memory_space=pltpu.SMEM),
            pl.BlockSpec((rows, SWA_Q_WIDTH), cur(0)),
            pl.BlockSpec((WINDOW, SWA_KV_WIDTH), prv(kcol)),
            pl.BlockSpec((rows, SWA_KV_WIDTH), cur(kcol)),
            pl.BlockSpec((WINDOW, SWA_KV_WIDTH), prv(vcol)),
            pl.BlockSpec((rows, SWA_KV_WIDTH), cur(vcol)),
        ],
        out_specs=pl.BlockSpec((rows, SWA_Q_WIDTH), lambda b, n: (b * ns + n, 0)),
        out_shape=jax.ShapeDtypeStruct((n_b * t_len, SWA_Q_WIDTH), BF16),
        compiler_params=_params("arbitrary", "arbitrary"),
        name="swa_prompt",
    )(sinks, p_att, p_att, p_att, p_att, p_att)


def _swa_step_kernel(sink_ref, q_ref, kn_ref, vn_ref, ck_ref, cv_ref, o_ref, ok_ref, ov_ref, *, n_seq, t_len):
    T = t_len
    wb = ck_ref.shape[2]
    t_row = _mod_pow2(lax.broadcasted_iota(jnp.int32, (SWA_GROUP * T, 1), 0), T)
    grp = _div_pow2(lax.broadcasted_iota(jnp.int32, (SWA_GROUP * T, 1), 0), T)
    jc = lax.broadcasted_iota(jnp.int32, (SWA_GROUP * T, wb), 1)
    jn = lax.broadcasted_iota(jnp.int32, (SWA_GROUP * T, T), 1)
    ok_c = (t_row + wb - jc < WINDOW)
    ok_n = (jn <= t_row)
    ks = [slice(hk * SWA_HEAD_DIM, (hk + 1) * SWA_HEAD_DIM) for hk in range(SWA_KV_HEADS)]
    sinks = [_group_sinks(sink_ref, hk, grp) for hk in range(SWA_KV_HEADS)]
    work = [(s, hk) for s in range(n_seq) for hk in range(SWA_KV_HEADS)]
    rows = [slice(s * T, (s + 1) * T) for s in range(n_seq)]
    lane_w = lax.broadcasted_iota(jnp.int32, (1, wb), 1)
    pad = jnp.zeros((wb - T, SWA_KV_WIDTH), F32)
    knb, vnb, ckb, cvb = [], [], [], []
    for s in range(n_seq):
        kn = kn_ref[rows[s], :]
        vn = vn_ref[rows[s], :]
        ck = ck_ref[s]
        cv = cv_ref[s]
        kn_t = jnp.transpose(jnp.concatenate([pad, kn], axis=0))
        vn_t = jnp.transpose(jnp.concatenate([pad, vn], axis=0))
        ok_ref[s] = jnp.where(lane_w >= wb - T, kn_t, pltpu.roll(ck, wb - T, axis=1))
        ov_ref[s] = jnp.where(lane_w >= wb - T, vn_t, pltpu.roll(cv, wb - T, axis=1))
        knb.append(kn.astype(BF16))
        vnb.append(vn.astype(BF16))
        ckb.append(ck.astype(BF16))
        cvb.append(cv.astype(BF16))
    qs = [q_ref[rows[s], :] * SWA_SCALE for s in range(n_seq)]
    qh = {(s, hk): _group_queries(qs[s], hk) for s, hk in work}
    sc = {w: jnp.dot(qh[w], ckb[w[0]][ks[w[1]], :], preferred_element_type=F32) for w in work}
    sn = {w: lax.dot_general(qh[w], knb[w[0]][:, ks[w[1]]], _NT, preferred_element_type=F32) for w in work}
    pr = {w: _sink_probs([jnp.where(ok_c, sc[w], -jnp.inf), jnp.where(ok_n, sn[w], -jnp.inf)], sinks[w[1]])
          for w in work}
    ov = {w: lax.dot_general(pr[w][0], cvb[w[0]][ks[w[1]], :], _NT, preferred_element_type=F32)
          + jnp.dot(pr[w][1], vnb[w[0]][:, ks[w[1]]], preferred_element_type=F32) for w in work}
    for s, hk in work:
        for g in range(SWA_GROUP):
            h = hk * SWA_GROUP + g
            o_ref[rows[s], h * SWA_HEAD_DIM:(h + 1) * SWA_HEAD_DIM] = ov[s, hk][g * T:(g + 1) * T].astype(o_ref.dtype)


def _swa_step(p_att, row0, n_b, t_len, cache_k, cache_v, sinks):
    wb = cache_k.shape[1]
    assert wb == WINDOW and t_len % 8 == 0 and t_len < wb
    n_seq = max(1, STEP_ROWS // t_len)
    while n_b % n_seq:
        n_seq //= 2
    rows = n_seq * t_len
    assert rows % BF16_SUBLANES == 0 and row0 % rows == 0
    blk0 = row0 // rows
    kcol = SWA_Q_WIDTH // SWA_KV_WIDTH
    to_cm = lambda c: jnp.transpose(c, (0, 2, 3, 1)).reshape(n_b, SWA_KV_WIDTH, wb)
    from_cm = lambda c: jnp.transpose(c.reshape(n_b, SWA_KV_HEADS, SWA_HEAD_DIM, wb), (0, 3, 1, 2))
    ck = to_cm(cache_k)
    cv = to_cm(cache_v)
    o, nk, nv = pl.pallas_call(
        functools.partial(_swa_step_kernel, n_seq=n_seq, t_len=t_len),
        grid=(n_b // n_seq,),
        in_specs=[
            pl.BlockSpec(memory_space=pltpu.SMEM),
            pl.BlockSpec((rows, SWA_Q_WIDTH), lambda i: (blk0 + i, 0)),
            pl.BlockSpec((rows, SWA_KV_WIDTH), lambda i: (blk0 + i, kcol)),
            pl.BlockSpec((rows, SWA_KV_WIDTH), lambda i: (blk0 + i, kcol + 1)),
            pl.BlockSpec((n_seq, SWA_KV_WIDTH, wb), lambda i: (i, 0, 0)),
            pl.BlockSpec((n_seq, SWA_KV_WIDTH, wb), lambda i: (i, 0, 0)),
        ],
        out_specs=[
            pl.BlockSpec((rows, SWA_Q_WIDTH), lambda i: (i, 0)),
            pl.BlockSpec((n_seq, SWA_KV_WIDTH, wb), lambda i: (i, 0, 0)),
            pl.BlockSpec((n_seq, SWA_KV_WIDTH, wb), lambda i: (i, 0, 0)),
        ],
        out_shape=[
            jax.ShapeDtypeStruct((n_b * t_len, SWA_Q_WIDTH), BF16),
            jax.ShapeDtypeStruct((n_b, SWA_KV_WIDTH, wb), F32),
            jax.ShapeDtypeStruct((n_b, SWA_KV_WIDTH, wb), F32),
        ],
        compiler_params=_params("arbitrary"),
        name="swa_step",
    )(sinks, p_att, p_att, p_att, ck, cv)
    return o, from_cm(nk), from_cm(nv)


def _softmax(s):
    m = jnp.max(s, axis=-1, keepdims=True)
    e = jnp.exp(s - m)
    return (e / jnp.sum(e, axis=-1, keepdims=True)).astype(BF16)


def _mem_attend(items, o_ref):
    scale = MEM_HEAD_DIM ** -0.5
    sc = [lax.dot_general(q, k, _NT, preferred_element_type=F32) * scale for q, k, _, _, _ in items]
    pr = [_softmax(s) for s in sc]
    ov = [jnp.dot(p, it[2], preferred_element_type=F32) for p, it in zip(pr, items)]
    for o, it in zip(ov, items):
        o_ref[it[3], it[4]] = o.astype(o_ref.dtype)


def _mem_prompt_kernel(qa_ref, qb_ref, k_ref, v_ref, o_ref):
    kb = k_ref[...].astype(BF16)
    vb = v_ref[...].astype(BF16)
    half = MEM_HEADS // 2
    items = []
    for h in range(MEM_HEADS):
        q_ref = qa_ref if h < half else qb_ref
        hs = slice((h % half) * MEM_HEAD_DIM, (h % half + 1) * MEM_HEAD_DIM)
        ms = slice(h * MEM_HEAD_DIM, (h + 1) * MEM_HEAD_DIM)
        items.append((q_ref[:, hs].astype(BF16), kb[:, ms], vb[:, ms], slice(None), ms))
    _mem_attend(items, o_ref)


def _mem_prompt(p_att, n_b, t_len, mem_kv):
    m_tok = mem_kv.shape[0] // n_b
    tq = math.gcd(t_len, ROW_TILE)
    nq = t_len // tq
    half_w = MEM_WIDTH // 2
    qcol = (SWA_Q_WIDTH + 2 * SWA_KV_WIDTH) // half_w
    return pl.pallas_call(
        _mem_prompt_kernel,
        grid=(n_b, nq),
        in_specs=[
            pl.BlockSpec((tq, half_w), lambda b, n: (b * nq + n, qcol)),
            pl.BlockSpec((tq, half_w), lambda b, n: (b * nq + n, qcol + 1)),
            pl.BlockSpec((m_tok, MEM_WIDTH), lambda b, n: (b, 0)),
            pl.BlockSpec((m_tok, MEM_WIDTH), lambda b, n: (b, 1)),
        ],
        out_specs=pl.BlockSpec((tq, MEM_WIDTH), lambda b, n: (b * nq + n, 0)),
        out_shape=jax.ShapeDtypeStruct((n_b * t_len, MEM_WIDTH), BF16),
        compiler_params=_params("arbitrary", "arbitrary"),
        name="mem_prompt",
    )(p_att, p_att, mem_kv, mem_kv)


def _mem_step_kernel(qa_ref, qb_ref, k_ref, v_ref, o_ref, *, n_seq, t_len):
    half = MEM_HEADS // 2
    m_tok = k_ref.shape[1] // MEM_HEADS
    items = []
    for s in range(n_seq):
        rs = slice(s * t_len, (s + 1) * t_len)
        for h in range(MEM_HEADS):
            q_ref = qa_ref if h < half else qb_ref
            hs = slice((h % half) * MEM_HEAD_DIM, (h % half + 1) * MEM_HEAD_DIM)
            ms = slice(h * MEM_HEAD_DIM, (h + 1) * MEM_HEAD_DIM)
            hrows = pl.ds(h, m_tok, stride=MEM_HEADS)
            items.append((q_ref[rs, hs].astype(BF16), k_ref[s, hrows, :].astype(BF16),
                          v_ref[s, hrows, :].astype(BF16), rs, ms))
    _mem_attend(items, o_ref)


def _mem_step(p_att, row0, n_b, t_len, cache_k, cache_v):
    m_tok = cache_k.shape[1]
    n_seq = max(1, STEP_ROWS // t_len)
    while n_b % n_seq:
        n_seq //= 2
    rows = n_seq * t_len
    assert rows % BF16_SUBLANES == 0 and row0 % rows == 0
    blk0 = row0 // rows
    half_w = MEM_WIDTH // 2
    qcol = (SWA_Q_WIDTH + 2 * SWA_KV_WIDTH) // half_w
    cache_block = (n_seq, m_tok * MEM_HEADS, MEM_HEAD_DIM)
    ck = cache_k.reshape(n_b, m_tok * MEM_HEADS, MEM_HEAD_DIM)
    cv = cache_v.reshape(n_b, m_tok * MEM_HEADS, MEM_HEAD_DIM)
    return pl.pallas_call(
        functools.partial(_mem_step_kernel, n_seq=n_seq, t_len=t_len),
        grid=(n_b // n_seq,),
        in_specs=[
            pl.BlockSpec((rows, half_w), lambda i: (blk0 + i, qcol)),
            pl.BlockSpec((rows, half_w), lambda i: (blk0 + i, qcol + 1)),
            pl.BlockSpec(cache_block, lambda i: (i, 0, 0)),
            pl.BlockSpec(cache_block, lambda i: (i, 0, 0)),
        ],
        out_specs=pl.BlockSpec((rows, MEM_WIDTH), lambda i: (i, 0)),
        out_shape=jax.ShapeDtypeStruct((n_b * t_len, MEM_WIDTH), BF16),
        compiler_params=_params("arbitrary"),
        name="mem_step",
    )(p_att, p_att, ck, cv)


def _merge_kernel(xn_ref, rwp_ref, rws_ref, swp_ref, sws_ref, mmp_ref, mms_ref,
                  wg0_ref, wg1_ref, wg2_ref, wb0_ref, wb1_ref, wb2_ref, o_ref,
                  g0_scr, g1_scr, g2_scr, b0_scr, b1_scr, b2_scr, *, n_p):
    i = pl.program_id(1)

    @pl.when(i == 0)
    def _():
        for src, dst in ((wg0_ref, g0_scr), (wg1_ref, g1_scr), (wg2_ref, g2_scr),
                         (wb0_ref, b0_scr), (wb1_ref, b1_scr), (wb2_ref, b2_scr)):
            dst[...] = src[...].astype(BF16)

    def run(o_rw_ref, o_sw_ref, o_mm_ref):
        xn = xn_ref[...]
        acc = None
        for g_scr, b_scr, br_ref in ((g0_scr, b0_scr, o_rw_ref), (g1_scr, b1_scr, o_sw_ref),
                                     (g2_scr, b2_scr, o_mm_ref)):
            gate = _sigmoid(jnp.dot(xn, g_scr[...], preferred_element_type=F32))
            t = gate * jnp.dot(br_ref[...], b_scr[...], preferred_element_type=F32)
            acc = t if acc is None else acc + t
        o_ref[...] = acc.astype(o_ref.dtype)

    @pl.when(i < n_p)
    def _():
        run(rwp_ref, swp_ref, mmp_ref)

    @pl.when(i >= n_p)
    def _():
        run(rws_ref, sws_ref, mms_ref)


def _merge(xn, o_p, o_s, w_in, gate_col0, w_br, n_rows_p):
    m, d = xn.shape
    tn = MXU_COLS
    tm = _row_tile(n_rows_p, m - n_rows_p, ROW_TILE)
    n_p = n_rows_p // tm
    n_i = m // tm
    assert gate_col0 % tn == 0 and d % tn == 0
    g0 = gate_col0 // tn
    nj = d // tn
    p_spec = lambda w: pl.BlockSpec((tm, w), lambda j, i: (jnp.minimum(i, n_p - 1), 0))
    s_spec = lambda w: pl.BlockSpec((tm, w), lambda j, i: (jnp.maximum(i - n_p, 0), 0))
    in_specs = [pl.BlockSpec((tm, d), lambda j, i: (i, 0))]
    args = [xn]
    for op, os_ in zip(o_p, o_s):
        in_specs += [p_spec(op.shape[1]), s_spec(os_.shape[1])]
        args += [op, os_]
    for br in range(N_BRANCH):
        in_specs.append(pl.BlockSpec((d, tn), lambda j, i, br=br: (0, g0 + br * nj + j)))
        args.append(w_in)
    for w in w_br:
        in_specs.append(pl.BlockSpec((w.shape[0], tn), lambda j, i: (0, j)))
        args.append(w)
    scratch = [pltpu.VMEM((d, tn), BF16) for _ in range(N_BRANCH)]
    scratch += [pltpu.VMEM((w.shape[0], tn), BF16) for w in w_br]
    return pl.pallas_call(
        functools.partial(_merge_kernel, n_p=n_p),
        grid=(nj, n_i),
        in_specs=in_specs,
        out_specs=pl.BlockSpec((tm, tn), lambda j, i: (i, j)),
        out_shape=jax.ShapeDtypeStruct((m, d), BF16),
        scratch_shapes=scratch,
        compiler_params=_params("arbitrary", "arbitrary"),
        name="gated_merge",
    )(*args)


def _cast_kernel(x_ref, o_ref):
    o_ref[...] = x_ref[...].astype(o_ref.dtype)


def _to_bf16(w, name):
    r, c = w.shape
    tr = math.gcd(r, NORM_ROW_TILE)
    return pl.pallas_call(
        _cast_kernel,
        grid=(r // tr,),
        in_specs=[pl.BlockSpec((tr, c), lambda i: (i, 0))],
        out_specs=pl.BlockSpec((tr, c), lambda i: (i, 0)),
        out_shape=jax.ShapeDtypeStruct((r, c), BF16),
        compiler_params=_params("arbitrary"),
        name=name,
    )(w)


def _wo_kernel(m_ref, w_ref, xp_ref, xs_ref, g1_ref, g2_ref, h_ref, hn_ref, *, n_p):
    i = pl.program_id(0)
    tm = m_ref.shape[0]
    halves = [slice(0, tm // 2), slice(tm // 2, tm)] if tm % (2 * BF16_SUBLANES) == 0 else [slice(0, tm)]

    def run(x_ref):
        for rs in halves:
            f = jnp.dot(m_ref[rs, :], w_ref[...], preferred_element_type=F32)
            h = x_ref[rs, :] + _rms(f, g1_ref[...])
            h_ref[rs, :] = h
            hn_ref[rs, :] = _rms(h, g2_ref[...]).astype(hn_ref.dtype)

    @pl.when(i < n_p)
    def _():
        run(xp_ref)

    @pl.when(i >= n_p)
    def _():
        run(xs_ref)


def _wo_block(merged, w_o, xp, xs, g_post, g_pre_ffn):
    m, d = merged.shape
    n_rows_p = xp.shape[0]
    tm = _row_tile(n_rows_p, m - n_rows_p, NORM_ROW_TILE)
    n_p = n_rows_p // tm
    return pl.pallas_call(
        functools.partial(_wo_kernel, n_p=n_p),
        grid=(m // tm,),
        in_specs=[
            pl.BlockSpec((tm, d), lambda i: (i, 0)),
            pl.BlockSpec((d, d), lambda i: (0, 0), pipeline_mode=pl.Buffered(1)),
            pl.BlockSpec((tm, d), lambda i: (jnp.minimum(i, n_p - 1), 0)),
            pl.BlockSpec((tm, d), lambda i: (jnp.maximum(i - n_p, 0), 0)),
            pl.BlockSpec((1, d), lambda i: (0, 0)),
            pl.BlockSpec((1, d), lambda i: (0, 0)),
        ],
        out_specs=[pl.BlockSpec((tm, d), lambda i: (i, 0)), pl.BlockSpec((tm, d), lambda i: (i, 0))],
        out_shape=[jax.ShapeDtypeStruct((m, d), F32), jax.ShapeDtypeStruct((m, d), BF16)],
        compiler_params=_params("arbitrary"),
        name="wo_residual",
    )(merged, _to_bf16(w_o, "cast_w_o"), xp, xs, g_post.reshape(1, d), g_pre_ffn.reshape(1, d))


def _gelu_tanh(x):
    return 0.5 * x * (1.0 + jnp.tanh(math.sqrt(2.0 / math.pi) * (x + 0.044715 * (x * x * x))))


def _ffn_up_kernel(hp_ref, hs_ref, wg_ref, wv_ref, cw_ref, cb_ref, st_ref, ap_ref, as_ref, cp_ref, cs_ref,
                   wgb_scr, wvb_scr, *, n_p, t_s):
    i = pl.program_id(1)

    @pl.when(i == 0)
    def _():
        wgb_scr[...] = wg_ref[...].astype(BF16)
        wvb_scr[...] = wv_ref[...].astype(BF16)

    tn = wgb_scr.shape[1]
    subs = [slice(c0, c0 + MXU_COLS) for c0 in range(0, tn, MXU_COLS)] if tn % MXU_COLS == 0 else [slice(0, tn)]

    def row_pieces(rows):
        size = NORM_ROW_TILE if rows % NORM_ROW_TILE == 0 and NORM_ROW_TILE % (8 * t_s) == 0 else rows
        return [(r0, r0 + size) for r0 in range(0, rows, size)]

    def conv_glu(cs, zg, zm1, zm2, zv):
        conv = cb_ref[:, cs] + cw_ref[0:1, cs] * zm2 + cw_ref[1:2, cs] * zm1 + cw_ref[2:3, cs] * zg
        return _gelu_tanh(conv) * zv

    def run(h_ref, epilogue):
        items = [(pc, cs) for cs in subs for pc in row_pieces(h_ref.shape[0])]

        def gate_val(pc, cs):
            hn = h_ref[slice(*pc), :]
            return (jnp.dot(hn, wgb_scr[:, cs], preferred_element_type=F32),
                    jnp.dot(hn, wvb_scr[:, cs], preferred_element_type=F32))

        z = {items[0]: gate_val(*items[0])}
        carry = None
        for n, (pc, cs) in enumerate(items):
            if n + 1 < len(items):
                z[items[n + 1]] = gate_val(*items[n + 1])
            zg, zv = z.pop((pc, cs))
            carry = epilogue(pc, cs, zg, zv, carry if pc[0] else None)

    def prompt_epilogue(pc, cs, zg, zv, carry):
        tr = pc[1] - pc[0]
        p2, p1 = carry if carry is not None else (jnp.zeros((1, zg.shape[1]), F32),) * 2
        row = lax.broadcasted_iota(jnp.int32, (tr, 1), 0)
        zm1 = jnp.where(row == 0, p1, pltpu.roll(zg, 1, axis=0))
        zm2 = jnp.where(row == 0, p2, jnp.where(row == 1, p1, pltpu.roll(zg, 2, axis=0)))
        ap_ref[slice(*pc), cs] = conv_glu(cs, zg, zm1, zm2, zv).astype(ap_ref.dtype)
        if pc[1] == ap_ref.shape[0]:
            cp_ref[0, :, cs] = zg[tr - 2:tr, :]
        return zg[tr - 2:tr - 1, :], zg[tr - 1:tr, :]

    def sample_epilogue(pc, cs, zg, zv, carry):
        tr = pc[1] - pc[0]
        nq = tr // t_s
        qs = slice(pc[0] // t_s, pc[1] // t_s)
        tc = zg.shape[1]
        t = lax.broadcasted_iota(jnp.int32, (1, t_s, 1), 1)
        z3 = zg.reshape(nq, t_s, tc)
        s1 = st_ref[qs, 1:2, cs]
        s0 = st_ref[qs, 0:1, cs]
        zm1 = jnp.where(t == 0, s1, pltpu.roll(z3, 1, axis=1))
        zm2 = jnp.where(t == 0, s0, jnp.where(t == 1, s1, pltpu.roll(z3, 2, axis=1)))
        act = conv_glu(cs, z3, zm1, zm2, zv.reshape(nq, t_s, tc))
        as_ref[slice(*pc), cs] = act.reshape(tr, tc).astype(as_ref.dtype)
        cs_ref[qs, :, cs] = z3[:, t_s - 2:t_s, :]
        return None

    @pl.when(i < n_p)
    def _():
        run(hp_ref, prompt_epilogue)

    @pl.when(i >= n_p)
    def _():
        run(hs_ref, sample_epilogue)


def _ffn_up(hn, w_up, conv_w, conv_b, conv_state, n_b_p, t_p, n_b_s, t_s):
    m, d = hn.shape
    d_ff = conv_b.shape[0]
    n_rows_p = n_b_p * t_p
    n_rows_s = n_b_s * t_s
    assert m == n_rows_p + n_rows_s and n_rows_p % n_rows_s == 0 and t_s % 8 == 0 and t_s >= 2
    assert t_p % BF16_SUBLANES == 0 and n_rows_s % BF16_SUBLANES == 0
    tn = math.gcd(d_ff, FFN_TILE)
    nj = d_ff // tn
    last = n_b_p - 1
    s_blk = n_rows_p // n_rows_s
    return pl.pallas_call(
        functools.partial(_ffn_up_kernel, n_p=n_b_p, t_s=t_s),
        grid=(nj, n_b_p + 1),
        in_specs=[
            pl.BlockSpec((t_p, d), lambda j, i: (jnp.minimum(i, last), 0)),
            pl.BlockSpec((n_rows_s, d), lambda j, i: (s_blk, 0)),
            pl.BlockSpec((d, tn), lambda j, i: (0, j)),
            pl.BlockSpec((d, tn), lambda j, i: (0, nj + j)),
            pl.BlockSpec((CONV_W, tn), lambda j, i: (0, j)),
            pl.BlockSpec((1, tn), lambda j, i: (0, j)),
            pl.BlockSpec((n_b_s, CONV_W - 1, tn), lambda j, i: (0, 0, j)),
        ],
        out_specs=[
            pl.BlockSpec((t_p, tn), lambda j, i: (jnp.minimum(i, last), j)),
            pl.BlockSpec((n_rows_s, tn), lambda j, i: (0, j)),
            pl.BlockSpec((1, CONV_W - 1, tn), lambda j, i: (jnp.minimum(i, last), 0, j)),
            pl.BlockSpec((n_b_s, CONV_W - 1, tn), lambda j, i: (0, 0, j)),
        ],
        out_shape=[
            jax.ShapeDtypeStruct((n_rows_p, d_ff), BF16),
            jax.ShapeDtypeStruct((n_rows_s, d_ff), BF16),
            jax.ShapeDtypeStruct((n_b_p, CONV_W - 1, d_ff), F32),
            jax.ShapeDtypeStruct((n_b_s, CONV_W - 1, d_ff), F32),
        ],
        scratch_shapes=[pltpu.VMEM((d, tn), BF16), pltpu.VMEM((d, tn), BF16)],
        compiler_params=_params("arbitrary", "arbitrary"),
        name="ffn_up_convglu",
    )(hn, hn, w_up, w_up, conv_w, conv_b.reshape(1, d_ff), conv_state)


def _ffn_down_kernel(a_ref, w_ref, h_ref, g_ref, y_ref):
    kk = pl.program_id(1)

    @pl.when(kk == 0)
    def _():
        y_ref[...] = jnp.zeros(y_ref.shape, F32)

    last = kk == pl.num_programs(1) - 1

    @pl.when(jnp.logical_not(last))
    def _():
        y_ref[...] += jnp.dot(a_ref[...], w_ref[...].astype(BF16), preferred_element_type=F32)

    @pl.when(last)
    def _():
        tm = y_ref.shape[0]
        n_pieces = 4 if tm % (4 * BF16_SUBLANES) == 0 else 1
        tr = tm // n_pieces
        wb = w_ref[...].astype(BF16)
        for r in range(n_pieces):
            rs = slice(r * tr, (r + 1) * tr)
            f = y_ref[rs, :] + jnp.dot(a_ref[rs, :], wb, preferred_element_type=F32)
            y_ref[rs, :] = h_ref[rs, :] + _rms(f, g_ref[...])


def _ffn_down(act, w_down, h, g, row0):
    d_ff, d = w_down.shape
    n_rows = act.shape[0]
    tm = math.gcd(math.gcd(n_rows, row0) if row0 else n_rows, ROW_TILE)
    tk = math.gcd(d_ff, FFN_TILE)
    i0 = row0 // tm
    return pl.pallas_call(
        _ffn_down_kernel,
        grid=(n_rows // tm, d_ff // tk),
        in_specs=[
            pl.BlockSpec((tm, tk), lambda i, k: (i, k)),
            pl.BlockSpec((tk, d), lambda i, k: (k, 0)),
            pl.BlockSpec((tm, d), lambda i, k: (i0 + i, 0)),
            pl.BlockSpec((1, d), lambda i, k: (0, 0)),
        ],
        out_specs=pl.BlockSpec((tm, d), lambda i, k: (i, 0)),
        out_shape=jax.ShapeDtypeStruct((n_rows, d), F32),
        compiler_params=_params("arbitrary", "arbitrary"),
        name="ffn_down",
    )(act, w_down, h, g.reshape(1, d))


def kernel(x_prompt, x_sample, cache_swa_k, cache_swa_v, cache_mem_k, cache_mem_v, state_rwkv_shift, state_rwkv_wkv, state_ffn_conv, mem_prompt, g_pre_mix, w_in, mu_rwkv, w_decay_up, w0_decay, w_a_up, a0, w_gate_up, k_k, k_a, r_k, ln_x_w, ln_x_b, swa_sinks, g_mem, w_mem_kv, w_br_rwkv, w_br_swa, w_br_mem, w_o, g_post_mix, g_pre_ffn, w_ffn_up, conv_w, conv_b, w_ffn_down, g_post_ffn):
    b_p, t_p, d = x_prompt.shape
    b_s, t_s, _ = x_sample.shape
    n_p, n_s = b_p * t_p, b_s * t_s
    m_tok = mem_prompt.shape[1]
    rw_prm = dict(mu_rwkv=mu_rwkv, w_decay_up=w_decay_up, w0_decay=w0_decay, w_a_up=w_a_up, a0=a0,
                  w_gate_up=w_gate_up, k_k=k_k, k_a=k_a, r_k=r_k, ln_x_w=ln_x_w, ln_x_b=ln_x_b)
    xp = x_prompt.reshape(n_p, d)
    xs = x_sample.reshape(n_s, d)

    xn, p_att = _norm_matmul(xp, xs, g_pre_mix, w_in, RW_PROJ, ATT_WIDTH, "norm_proj_attn")
    p_rw = _matmul(xn, w_in, 0, RW_PROJ, RW_PROJ // 2, ROW_TILE, "proj_rwkv")

    _, mem_kv = _norm_matmul(mem_prompt.reshape(b_p * m_tok, d), None, g_mem, w_mem_kv,
                             0, 2 * MEM_WIDTH, "norm_proj_mem_kv")

    o_rw_p, shift_p, wkv_p = _rwkv(p_rw, 0, b_p, t_p, None, None, rw_prm, RWKV_PARTS)
    o_rw_s, shift_s, wkv_s = _rwkv(p_rw, n_p, b_s, t_s, state_rwkv_shift, state_rwkv_wkv, rw_prm, RWKV_PARTS)

    o_sw_p = _swa_prompt(p_att, b_p, t_p, swa_sinks)
    o_sw_s, swa_k_s, swa_v_s = _swa_step(p_att, n_p, b_s, t_s, cache_swa_k, cache_swa_v, swa_sinks)

    o_mm_p = _mem_prompt(p_att, b_p, t_p, mem_kv)
    o_mm_s = _mem_step(p_att, n_p, b_s, t_s, cache_mem_k, cache_mem_v)

    merged = _merge(xn, (o_rw_p, o_sw_p, o_mm_p), (o_rw_s, o_sw_s, o_mm_s), w_in,
                    RW_PROJ + ATT_WIDTH, (w_br_rwkv, w_br_swa, w_br_mem), n_p)
    h, hn = _wo_block(merged, w_o, xp, xs, g_post_mix, g_pre_ffn)
    act_p, act_s, conv_p, conv_s = _ffn_up(hn, w_ffn_up, conv_w, conv_b, state_ffn_conv, b_p, t_p, b_s, t_s)
    y_p = _ffn_down(act_p, w_ffn_down, h, g_post_ffn, 0)
    y_s = _ffn_down(act_s, w_ffn_down, h, g_post_ffn, n_p)

    keep = min(WINDOW, t_p)
    kv_p = jnp.stack([lax.slice(p_att, ((b + 1) * t_p - keep, SWA_Q_WIDTH),
                                ((b + 1) * t_p, SWA_Q_WIDTH + 2 * SWA_KV_WIDTH)) for b in range(b_p)])
    swa_k_p = kv_p[..., :SWA_KV_WIDTH].reshape(b_p, keep, SWA_KV_HEADS, SWA_HEAD_DIM)
    swa_v_p = kv_p[..., SWA_KV_WIDTH:].reshape(b_p, keep, SWA_KV_HEADS, SWA_HEAD_DIM)
    mem_k_p = mem_kv[:, :MEM_WIDTH].reshape(b_p, m_tok, MEM_HEADS, MEM_HEAD_DIM)
    mem_v_p = mem_kv[:, MEM_WIDTH:].reshape(b_p, m_tok, MEM_HEADS, MEM_HEAD_DIM)
    return (y_p.reshape(b_p, t_p, d), y_s.reshape(b_s, t_s, d), swa_k_p, swa_v_p, mem_k_p, mem_v_p,
            shift_p, wkv_p, conv_p, swa_k_s, swa_v_s, shift_s, wkv_s, conv_s)
```

```python
import functools
import math

import jax
import jax.numpy as jnp
from jax import lax
from jax.experimental import pallas as pl
from jax.experimental.pallas import tpu as pltpu

F32 = jnp.float32
BF16 = jnp.bfloat16

RW_HEADS = 12
RW_HEAD_DIM = 64
RW_WIDTH = RW_HEADS * RW_HEAD_DIM
RW_DECAY_RANK = 64
RW_A_RANK = 64
RW_GATE_RANK = 128
RW_PROJ = 3 * RW_WIDTH + RW_DECAY_RANK + RW_A_RANK + RW_GATE_RANK
RW_GN_EPS = 6.4e-4
RW_PAIRS = RW_HEADS // 2
SWA_Q_HEADS = 12
SWA_KV_HEADS = 4
SWA_GROUP = SWA_Q_HEADS // SWA_KV_HEADS
SWA_HEAD_DIM = 64
SWA_Q_WIDTH = SWA_Q_HEADS * SWA_HEAD_DIM
SWA_KV_WIDTH = SWA_KV_HEADS * SWA_HEAD_DIM
WINDOW = 128
MEM_HEADS = 4
MEM_HEAD_DIM = 128
MEM_WIDTH = MEM_HEADS * MEM_HEAD_DIM
N_BRANCH = 3
CONV_W = 3
NORM_EPS = 1e-6
ATT_WIDTH = SWA_Q_WIDTH + 2 * SWA_KV_WIDTH + MEM_WIDTH
SWA_SCALE = SWA_HEAD_DIM ** -0.5
assert math.frexp(SWA_SCALE)[0] == 0.5

LANES = 128
BF16_SUBLANES = 16
MXU_COLS = 256
VMEM_LIMIT_BYTES = 56 * 1024 * 1024

RWKV_ROWS = 64
RWKV_GROUPS = 2
RWKV_PARTS = dict(z=1, lv=1, pow=1, app=1, om=1, hg=1, out=1, st=1)
RWKV_SUM_PARTS = 1
ROW_TILE = 1024
NORM_ROW_TILE = 512
FFN_TILE = 512
STEP_ROWS = 64
SWA_WINDOWS = 4


def _params(*sem):
    return pltpu.CompilerParams(dimension_semantics=sem, vmem_limit_bytes=VMEM_LIMIT_BYTES)


def _row_tile(n_p, n_s, target):
    t = math.gcd(math.gcd(n_p, n_s), target)
    assert t % BF16_SUBLANES == 0, (n_p, n_s, target)
    return t


def _div_pow2(x, n):
    assert n & (n - 1) == 0
    return lax.shift_right_logical(x, n.bit_length() - 1)


def _mod_pow2(x, n):
    assert n & (n - 1) == 0
    return jnp.bitwise_and(x, n - 1)


def _rms(x, g):
    ms = jnp.mean(x * x, axis=-1, keepdims=True)
    return x * lax.rsqrt(ms + NORM_EPS) * g


def _norm_mm_kernel(*refs, n_p, two):
    if two:
        xp_ref, xs_ref, g_ref, w_ref, xn_ref, o_ref, wb_scr = refs
    else:
        xp_ref, g_ref, w_ref, xn_ref, o_ref, wb_scr = refs
    i = pl.program_id(0)

    @pl.when(i == 0)
    def _():
        wb_scr[...] = w_ref[...].astype(BF16)

    def run(x_ref):
        xn = _rms(x_ref[...], g_ref[...]).astype(BF16)
        xn_ref[...] = xn
        o_ref[...] = jnp.dot(xn, wb_scr[...], preferred_element_type=F32)

    if two:
        @pl.when(i < n_p)
        def _():
            run(xp_ref)

        @pl.when(i >= n_p)
        def _():
            run(xs_ref)
    else:
        run(xp_ref)


def _norm_matmul(xp, xs, g, w, col0, n_cols, name):
    n_rows_p, d = xp.shape
    two = xs is not None
    n_rows_s = xs.shape[0] if two else 0
    tm = _row_tile(n_rows_p, n_rows_s, NORM_ROW_TILE) if two else math.gcd(n_rows_p, NORM_ROW_TILE)
    n_p, n_s = n_rows_p // tm, n_rows_s // tm
    assert col0 % LANES == 0 and n_cols % LANES == 0
    once = pl.Buffered(1)
    in_specs = [pl.BlockSpec((tm, d), lambda i: (jnp.minimum(i, n_p - 1), 0))]
    args = [xp]
    if two:
        in_specs.append(pl.BlockSpec((tm, d), lambda i: (jnp.maximum(i - n_p, 0), 0)))
        args.append(xs)
    in_specs += [pl.BlockSpec((1, d), lambda i: (0, 0)),
                 pl.BlockSpec((pl.Element(d), pl.Element(n_cols)), lambda i: (0, col0), pipeline_mode=once)]
    args += [g.reshape(1, d), w]
    m = n_rows_p + n_rows_s
    return pl.pallas_call(
        functools.partial(_norm_mm_kernel, n_p=n_p, two=two),
        grid=(n_p + n_s,),
        in_specs=in_specs,
        out_specs=[pl.BlockSpec((tm, d), lambda i: (i, 0)), pl.BlockSpec((tm, n_cols), lambda i: (i, 0))],
        out_shape=[jax.ShapeDtypeStruct((m, d), BF16), jax.ShapeDtypeStruct((m, n_cols), F32)],
        scratch_shapes=[pltpu.VMEM((d, n_cols), BF16)],
        compiler_params=_params("arbitrary"),
        name=name,
    )(*args)


def _mm_kernel(a_ref, w_ref, o_ref, wb_ref):
    @pl.when(pl.program_id(1) == 0)
    def _():
        wb_ref[...] = w_ref[...].astype(BF16)

    o_ref[...] = jnp.dot(a_ref[...], wb_ref[...], preferred_element_type=F32)


def _matmul(a, w, col0, n_cols, tn, tm, name):
    m, k = a.shape
    tm = math.gcd(m, tm)
    assert n_cols % tn == 0 and col0 % LANES == 0 and tn % LANES == 0
    return pl.pallas_call(
        _mm_kernel,
        grid=(n_cols // tn, m // tm),
        in_specs=[
            pl.BlockSpec((tm, k), lambda j, i: (i, 0)),
            pl.BlockSpec((pl.Element(k), pl.Element(tn)),
                         lambda j, i: (0, pl.multiple_of(col0 + j * tn, LANES))),
        ],
        out_specs=pl.BlockSpec((tm, tn), lambda j, i: (i, j)),
        out_shape=jax.ShapeDtypeStruct((m, n_cols), F32),
        scratch_shapes=[pltpu.VMEM((k, tn), BF16)],
        compiler_params=_params("arbitrary", "arbitrary"),
        name=name,
    )(a, w)


_NN = (((1,), (0,)), ((), ()))
_NT = (((1,), (1,)), ((), ()))
_TN = (((0,), (0,)), ((), ()))


def _bf_parts(x, n):
    parts = []
    r = x
    for i in range(n):
        h = r.astype(BF16)
        parts.append(h)
        if i + 1 < n:
            r = r - h.astype(F32)
    return parts


def _dotp(a, b, dn, pa, pb):
    pa_parts = _bf_parts(a, pa)
    pb_parts = _bf_parts(b, pb)
    acc = None
    for i in range(pa):
        for j in range(pb):
            if i + j < max(pa, pb):
                t = lax.dot_general(pa_parts[i], pb_parts[j], dn, preferred_element_type=F32)
                acc = t if acc is None else acc + t
    return acc


def _softplus(x):
    return jnp.maximum(x, 0.0) + jnp.log(1.0 + jnp.exp(-jnp.abs(x)))


def _sigmoid(x):
    return 1.0 / (1.0 + jnp.exp(-x))


def _rwkv_kernel(*refs, seq_rows, n_seq, n_par, has_state, pp):
    n_in = n_par + (2 if has_state else 0)
    p_refs = refs[0:n_par]
    if has_state:
        sh0_ref, s0_ref = refs[n_par:n_in]
    (mu_ref, wda_ref, wa0_ref, wg_ref, vec_ref,
     o_ref, sh_out_ref, s_out_ref, prev_scr, st_scr) = refs[n_in:]
    c = pl.program_id(1)
    n_c = pl.num_programs(1)
    C = seq_rows
    R = seq_rows * n_seq
    W = RW_WIDTH
    HD = RW_HEAD_DIM
    groups = range(n_par)
    lane = lax.broadcasted_iota(jnp.int32, (1, LANES), 1)
    m0 = jnp.where(lane < RW_HEAD_DIM, 1.0, 0.0)
    m1 = jnp.where(lane >= RW_HEAD_DIM, 1.0, 0.0)
    even = lambda h: pl.ds(h * HD, HD // 2, stride=2)
    odd = lambda h: pl.ds(h * HD + 1, HD // 2, stride=2)
    swap = lambda z: pltpu.roll(z, RW_HEAD_DIM, axis=1)

    @pl.when(c == 0)
    def _():
        if has_state:
            prev_scr[...] = sh0_ref[...]
            for s in range(n_par * n_seq):
                for q in range(RW_PAIRS):
                    x0 = s0_ref[s, 2 * q]
                    x1 = s0_ref[s, 2 * q + 1]
                    st_scr[s, q, even(0), :] = x0 * m0
                    st_scr[s, q, odd(0), :] = swap(x0) * m0
                    st_scr[s, q, even(1), :] = swap(x1) * m1
                    st_scr[s, q, odd(1), :] = x1 * m1
        else:
            st_scr[...] = jnp.zeros(st_scr.shape, F32)
            prev_scr[...] = jnp.zeros(prev_scr.shape, F32)

    k_k = vec_ref[0:1, :]
    k_a = vec_ref[1:2, :]
    r_k = vec_ref[2:3, :]
    ln_w = vec_ref[3:4, :]
    ln_b = vec_ref[4:5, :]
    row = lax.broadcasted_iota(jnp.int32, (R, 1), 0)

    li = lax.broadcasted_iota(jnp.int32, (LANES, LANES), 0)
    lj = lax.broadcasted_iota(jnp.int32, (LANES, LANES), 1)
    same_head = _div_pow2(li, RW_HEAD_DIM) == _div_pow2(lj, RW_HEAD_DIM)
    ones_bd = jnp.where(same_head, 1.0, 0.0).astype(BF16)

    def head_sum(*zs):
        stack = jnp.concatenate([z[:, q * LANES:(q + 1) * LANES] for z in zs for q in range(RW_PAIRS)], axis=0)
        acc = None
        for part in _bf_parts(stack, RWKV_SUM_PARTS):
            t = jnp.dot(part, ones_bd, preferred_element_type=F32)
            acc = t if acc is None else acc + t
        outs = [jnp.concatenate([acc[(n * RW_PAIRS + q) * R:(n * RW_PAIRS + q + 1) * R] for q in range(RW_PAIRS)],
                                axis=1) for n in range(len(zs))]
        return outs[0] if len(zs) == 1 else outs

    ri = lax.broadcasted_iota(jnp.int32, (R, R), 0)
    rj = lax.broadcasted_iota(jnp.int32, (R, R), 1)
    same_seq = _div_pow2(ri, C) == _div_pow2(rj, C)
    causal = jnp.where(same_seq & (rj <= ri), 1.0, 0.0).astype(BF16)

    def prepare_proj(u):
        p = p_refs[u][...]
        prev = pltpu.roll(p, 1, axis=0)
        for s in range(n_seq):
            prev = jnp.where(row == s * C, prev_scr[u * n_seq + s], prev)
        for s in range(n_seq):
            prev_scr[u * n_seq + s] = p[(s + 1) * C - 1:(s + 1) * C, :]
            sh_out_ref[u * n_seq + s] = p[(s + 1) * C - 1:(s + 1) * C, :]
        x = p + mu_ref[...] * (prev - p)
        r = x[:, 0:W]
        k = x[:, W:2 * W]
        v = x[:, 2 * W:3 * W]
        wa_in = x[:, 3 * W:3 * W + LANES]
        g_in = x[:, 3 * W + LANES:3 * W + 2 * LANES]
        wa_act = jnp.where(lane < RW_DECAY_RANK, jnp.tanh(wa_in), wa_in)
        za = jnp.dot(wa_act.astype(BF16), wda_ref[...], preferred_element_type=F32) + wa0_ref[...]
        w_log = -_softplus(-za[:, 0:W]) - 0.5
        lw = -jnp.exp(w_log)
        a = _sigmoid(za[:, W:2 * W])
        g = jnp.dot(_sigmoid(g_in).astype(BF16), wg_ref[...], preferred_element_type=F32)
        kk = k * k_k
        k = k * (1.0 + (a - 1.0) * k_a)
        return dict(r=r, k=k, v=v, g=g, a=a, lw=lw, kk=kk)

    def prepare_sums(d):
        d["kk_sq"], d["rk"] = head_sum(d["kk"] * d["kk"], d["r"] * d["k"] * r_k)
        cl = None
        for part in _bf_parts(d["lw"], 2):
            t1 = jnp.dot(causal, part, preferred_element_type=F32)
            cl = t1 if cl is None else cl + t1
        d["cl"] = cl
        return d

    def prepare_factors(d):
        r, k, v, g, a, lw, cl, rk = (d[n] for n in ("r", "k", "v", "g", "a", "lw", "cl", "rk"))
        kk = d["kk"] / jnp.maximum(jnp.sqrt(d["kk_sq"]), 1e-12)
        b = kk * a
        if n_seq == 1:
            ct = cl[R - 1:R, :]
        else:
            last = cl.reshape(n_seq, C, W)[:, C - 1:C, :]
            ct = jnp.broadcast_to(last, (n_seq, C, W)).reshape(R, W)
        e_neg = jnp.exp(-cl)
        e_end = jnp.exp(ct - cl)
        return dict(v=v, g=g, bonus=rk * v, at=-kk * jnp.exp(cl - lw), rt=r * jnp.exp(cl), bt=b * e_neg,
                    kt=k * e_neg, bts=b * e_end, kts=k * e_end,
                    p_end=jnp.exp(ct))

    prep = [prepare_factors(prepare_sums(prepare_proj(u))) for u in groups]

    strict = jnp.where(same_seq & (rj < ri), 1.0, 0.0)
    incl = jnp.where(same_seq & (rj <= ri), 1.0, 0.0)
    strict2 = jnp.concatenate([strict, strict], axis=1)
    incl2 = jnp.concatenate([incl, incl], axis=1)
    bd_mask = jnp.where(same_head, 1.0, 0.0)
    zeros_rl = jnp.zeros((R, LANES), F32)
    zeros_r2 = jnp.zeros((R, 2 * R), F32)
    n_dbl = max(1, (C - 1).bit_length())
    sls = [slice(q * LANES, (q + 1) * LANES) for q in range(RW_PAIRS)]
    items = [(u, q) for u in groups for q in range(RW_PAIRS)]

    def col(name, it):
        return prep[it[0]][name][:, sls[it[1]]]

    v0 = {it: jnp.concatenate([col("v", it), zeros_rl], axis=1) for it in items}
    def against_keys(name, it):
        x, bq, kq = col(name, it), col("bt", it), col("kt", it)
        x2 = jnp.concatenate([x * m0, x * m1], axis=0)
        y4 = jnp.concatenate([bq, kq, kq, bq], axis=0)
        return _dotp(x2, y4, _NT, pp["z"], pp["z"])

    zs = {it: against_keys("at", it) for it in items}
    zt0 = {it: zs[it][0:R, 0:2 * R] * strict2 for it in items}
    zt1 = {it: zs[it][R:2 * R, 2 * R:4 * R] * strict2 for it in items}
    half0 = jnp.where(lax.broadcasted_iota(jnp.int32, (1, 2 * R), 1) < R, 1.0, 0.0)
    half1 = 1.0 - half0
    pw = {it: jnp.concatenate([zt0[it] * half0, zt1[it] * half1], axis=0) for it in items}
    wmat = {}
    for it in items:
        lhs = jnp.concatenate([jnp.concatenate([zt0[it], zeros_r2], axis=1),
                               jnp.concatenate([zeros_r2, zt1[it]], axis=1)], axis=0)
        vq = col("v", it)
        rhs = jnp.concatenate([zeros_rl, vq, vq, zeros_rl], axis=0)
        lv = _dotp(lhs, rhs, _NN, pp["lv"], pp["lv"])
        xa = col("at", it)
        wmat[it] = jnp.concatenate([lv, jnp.concatenate([xa * m0, xa * m1], axis=0)], axis=1)
    def lower_rows(a, lo):
        return jnp.concatenate([a[lo:R], a[R + lo:2 * R]], axis=0)

    def scatter_rows(base, upd, lo):
        h = R - lo
        top = [jnp.zeros((lo, upd.shape[1]), F32)] * 2 if base is None else [base[0:lo], base[R:R + lo]]
        mid = [upd[0:h], upd[h:2 * h]] if base is None else [base[lo:R] + upd[0:h], base[R + lo:2 * R] + upd[h:2 * h]]
        return jnp.concatenate([top[0], mid[0], top[1], mid[1]], axis=0)

    for step in range(n_dbl):
        lo = 1 << step
        if n_seq == 1 and lo % 8 == 0:
            wmat = {it: scatter_rows(wmat[it], _dotp(lower_rows(pw[it], lo), wmat[it], _NN, pp["app"], pp["app"]), lo)
                    for it in items}
        else:
            wmat = {it: wmat[it] + _dotp(pw[it], wmat[it], _NN, pp["app"], pp["app"]) for it in items}
        if step + 1 < n_dbl:
            if n_seq == 1 and (2 * lo) % 8 == 0:
                pw = {it: scatter_rows(None, _dotp(lower_rows(pw[it], 2 * lo), pw[it], _NN, pp["pow"], pp["pow"]),
                                       2 * lo) for it in items}
            else:
                pw = {it: _dotp(pw[it], pw[it], _NN, pp["pow"], pp["pow"]) for it in items}
    zs = {it: against_keys("rt", it) for it in items}
    rhat, o0 = {}, {}
    for it in items:
        zb0 = zs[it][0:R, 0:2 * R] * incl2
        zb1 = zs[it][R:2 * R, 2 * R:4 * R] * incl2
        lhs = jnp.concatenate([jnp.concatenate([zb0, zeros_r2], axis=1),
                               jnp.concatenate([zeros_r2, zb1], axis=1)], axis=0)
        rhs = jnp.concatenate([wmat[it][0:R], v0[it], v0[it], wmat[it][R:2 * R]], axis=0)
        om = _dotp(lhs, rhs, _NN, pp["om"], pp["om"])
        rhat[it] = col("rt", it) + om[0:R, LANES:] + om[R:2 * R, LANES:]
        o0[it] = om[0:R, 0:LANES] * m0 + om[R:2 * R, 0:LANES] * m1
    hgt = {}
    for it in items:
        bq, kq = col("bts", it), col("kts", it)
        for s in range(n_seq):
            rs = slice(s * C, (s + 1) * C)
            ys4 = jnp.concatenate([bq[rs] * m0, kq[rs] * m0, kq[rs] * m1, bq[rs] * m1], axis=0)
            r4 = jnp.concatenate([wmat[it][rs], v0[it][rs], v0[it][rs],
                                  wmat[it][R + s * C:R + (s + 1) * C]], axis=0)
            hgt[it, s] = _dotp(r4, ys4, _TN, pp["hg"], pp["hg"])
    o_pair = {}
    for it in items:
        u, q = it
        o_seq = []
        for s in range(n_seq):
            rs = slice(s * C, (s + 1) * C)
            s_old = st_scr[u * n_seq + s, q]
            ht = hgt[it, s][0:LANES] * bd_mask
            o_seq.append(_dotp(rhat[it][rs], s_old, _NT, pp["out"], pp["out"]) + o0[it][rs])
            decayed = s_old * prep[u]["p_end"][s * C:s * C + 1, sls[q]]
            st_scr[u * n_seq + s, q] = decayed + _dotp(s_old, hgt[it, s][LANES:], _NN, pp["st"], pp["st"]) + ht
        o_pair[it] = o_seq[0] if n_seq == 1 else jnp.concatenate(o_seq, axis=0)

    inv_n = 1.0 / RW_HEAD_DIM
    o_all = [jnp.concatenate([o_pair[u, q] for q in range(RW_PAIRS)], axis=1) for u in groups]
    sums = head_sum(*o_all)
    dev = [o_all[u] - (sums if n_par == 1 else sums[u]) * inv_n for u in groups]
    sums = head_sum(*[d * d for d in dev])
    for u in groups:
        var = (sums if n_par == 1 else sums[u]) * inv_n
        o = dev[u] * lax.rsqrt(var + RW_GN_EPS) * ln_w + ln_b
        o_ref[u] = ((o + prep[u]["bonus"]) * prep[u]["g"]).astype(o_ref.dtype)

    @pl.when(c == n_c - 1)
    def _():
        for s in range(n_par * n_seq):
            for q in range(RW_PAIRS):
                s_out_ref[s, 2 * q] = st_scr[s, q, even(0), :] * m0 + swap(st_scr[s, q, odd(0), :]) * m1
                s_out_ref[s, 2 * q + 1] = swap(st_scr[s, q, even(1), :]) * m0 + st_scr[s, q, odd(1), :] * m1


def _rwkv(p_rw, row0, n_b, t_len, shift0, s0, prm, pp):
    if t_len >= RWKV_ROWS:
        seq_rows, n_seq = RWKV_ROWS, 1
    else:
        seq_rows, n_seq = t_len, RWKV_ROWS // t_len
    assert t_len % seq_rows == 0 and n_b % n_seq == 0 and row0 % (seq_rows * n_seq) == 0
    rows = seq_rows * n_seq
    n_c = t_len // seq_rows
    assert n_seq == 1 or n_c == 1
    blk0 = row0 // rows
    has_state = s0 is not None
    n_grp = n_b // n_seq
    n_par = math.gcd(n_grp, RWKV_GROUPS)
    seqs = n_par * n_seq

    wda = jnp.zeros((LANES, 2 * RW_WIDTH), F32)
    wda = wda.at[0:RW_DECAY_RANK, 0:RW_WIDTH].set(prm["w_decay_up"])
    wda = wda.at[RW_DECAY_RANK:, RW_WIDTH:].set(prm["w_a_up"])
    wa0 = jnp.concatenate([prm["w0_decay"], prm["a0"]]).reshape(1, 2 * RW_WIDTH)
    vecs = jnp.zeros((8, RW_WIDTH), F32)
    for i, name in enumerate(("k_k", "k_a", "r_k", "ln_x_w", "ln_x_b")):
        vecs = vecs.at[i].set(prm[name].reshape(RW_WIDTH))

    const = lambda *shape: pl.BlockSpec(shape, lambda bi, c: (0,) * len(shape))
    in_specs = [pl.BlockSpec((rows, RW_PROJ), lambda bi, c, u=u: (blk0 + (bi * n_par + u) * n_c + c, 0))
                for u in range(n_par)]
    args = [p_rw] * n_par
    st_packed = (RW_HEADS, RW_HEAD_DIM // 2, 2 * RW_HEAD_DIM)
    st_block = (seqs,) + st_packed
    if has_state:
        in_specs += [pl.BlockSpec((seqs, 1, RW_PROJ), lambda bi, c: (bi, 0, 0)),
                     pl.BlockSpec(st_block, lambda bi, c: (bi, 0, 0, 0))]
        args += [shift0.reshape(n_b, 1, RW_PROJ), s0.reshape((n_b,) + st_packed)]
    in_specs += [const(1, RW_PROJ), const(LANES, 2 * RW_WIDTH), const(1, 2 * RW_WIDTH),
                 const(RW_GATE_RANK, RW_WIDTH), const(8, RW_WIDTH)]
    args += [prm["mu_rwkv"].reshape(1, RW_PROJ), wda.astype(BF16), wa0,
             prm["w_gate_up"].astype(BF16), vecs]

    o, sh, st = pl.pallas_call(
        functools.partial(_rwkv_kernel, seq_rows=seq_rows, n_seq=n_seq, n_par=n_par, has_state=has_state, pp=pp),
        grid=(n_grp // n_par, n_c),
        in_specs=in_specs,
        out_specs=[
            pl.BlockSpec((n_par, rows, RW_WIDTH), lambda bi, c: (bi, c, 0)),
            pl.BlockSpec((seqs, 1, RW_PROJ), lambda bi, c: (bi, 0, 0)),
            pl.BlockSpec(st_block, lambda bi, c: (bi, 0, 0, 0)),
        ],
        out_shape=[
            jax.ShapeDtypeStruct((n_grp, n_c * rows, RW_WIDTH), BF16),
            jax.ShapeDtypeStruct((n_b, 1, RW_PROJ), F32),
            jax.ShapeDtypeStruct((n_b,) + st_packed, F32),
        ],
        scratch_shapes=[pltpu.VMEM((seqs, 1, RW_PROJ), F32),
                        pltpu.VMEM((seqs, RW_PAIRS, LANES, LANES), F32)],
        compiler_params=_params("arbitrary", "arbitrary"),
        name="rwkv_state" if has_state else "rwkv_fresh",
    )(*args)
    return (o.reshape(n_b * t_len, RW_WIDTH), sh.reshape(n_b, RW_PROJ),
            st.reshape(n_b, RW_HEADS, RW_HEAD_DIM, RW_HEAD_DIM))


def _sink_probs(s_parts, sink):
    m = sink
    for s in s_parts:
        m = jnp.maximum(m, jnp.max(s, axis=-1, keepdims=True))
    den = jnp.exp(sink - m)
    es = []
    for s in s_parts:
        e = jnp.exp(s - m)
        den = den + jnp.sum(e, axis=-1, keepdims=True)
        es.append(e)
    inv = 1.0 / den
    return [(e * inv).astype(BF16) for e in es]


def _group_queries(q, hk):
    return jnp.concatenate(
        [q[:, (hk * SWA_GROUP + g) * SWA_HEAD_DIM:(hk * SWA_GROUP + g + 1) * SWA_HEAD_DIM]
         for g in range(SWA_GROUP)], axis=0).astype(BF16)


def _group_sinks(sink_ref, hk, grp):
    sink = jnp.zeros(grp.shape, F32)
    for g in range(SWA_GROUP):
        sink = jnp.where(grp == g, sink_ref[hk * SWA_GROUP + g], sink)
    return sink


def _swa_prompt_kernel(sink_ref, q_ref, kp_ref, kc_ref, vp_ref, vc_ref, o_ref, *, n_win):
    n = pl.program_id(1)
    Wn = WINDOW
    kf = jnp.concatenate([kp_ref[...], kc_ref[...]], axis=0).astype(BF16)
    vf = jnp.concatenate([vp_ref[...], vc_ref[...]], axis=0).astype(BF16)
    gi = lax.broadcasted_iota(jnp.int32, (SWA_GROUP * Wn, 2 * Wn), 0)
    i = _mod_pow2(gi, Wn)
    j = lax.broadcasted_iota(jnp.int32, (SWA_GROUP * Wn, 2 * Wn), 1)
    ok = (j > i) & (j <= i + Wn)
    first_key = jnp.where(n > 0, 0, Wn)
    ok_first = ok & (j >= first_key)
    grp = _div_pow2(lax.broadcasted_iota(jnp.int32, (SWA_GROUP * Wn, 1), 0), Wn)
    ks = [slice(hk * SWA_HEAD_DIM, (hk + 1) * SWA_HEAD_DIM) for hk in range(SWA_KV_HEADS)]
    sinks = [_group_sinks(sink_ref, hk, grp) for hk in range(SWA_KV_HEADS)]
    work = [(w, hk) for w in range(n_win) for hk in range(SWA_KV_HEADS)]
    qw = [q_ref[w * Wn:(w + 1) * Wn, :] * SWA_SCALE for w in range(n_win)]
    sc = {(w, hk): lax.dot_general(_group_queries(qw[w], hk), kf[w * Wn:(w + 2) * Wn, ks[hk]], _NT,
                                   preferred_element_type=F32) for w, hk in work}
    pr = {(w, hk): _sink_probs([jnp.where(ok_first if w == 0 else ok, sc[w, hk], -jnp.inf)], sinks[hk])[0]
          for w, hk in work}
    ov = {(w, hk): jnp.dot(pr[w, hk], vf[w * Wn:(w + 2) * Wn, ks[hk]], preferred_element_type=F32)
          for w, hk in work}
    for w, hk in work:
        for g in range(SWA_GROUP):
            h = hk * SWA_GROUP + g
            o_ref[w * Wn:(w + 1) * Wn, h * SWA_HEAD_DIM:(h + 1) * SWA_HEAD_DIM] = (
                ov[w, hk][g * Wn:(g + 1) * Wn].astype(o_ref.dtype))


def _swa_prompt(p_att, n_b, t_len, sinks):
    nb = t_len // WINDOW
    n_win = math.gcd(nb, SWA_WINDOWS)
    ns = nb // n_win
    rows = n_win * WINDOW
    kcol = SWA_Q_WIDTH // SWA_KV_WIDTH
    vcol = kcol + 1
    cur = lambda col: (lambda b, n: (b * ns + n, col))
    prv = lambda col: (lambda b, n: (b * nb + jnp.maximum(n * n_win - 1, 0), col))
    return pl.pallas_call(
        functools.partial(_swa_prompt_kernel, n_win=n_win),
        grid=(n_b, ns),
        in_specs=[
            pl.BlockSpec(memory_space=pltpu.SMEM),
            pl.BlockSpec((rows, SWA_Q_WIDTH), cur(0)),
            pl.BlockSpec((WINDOW, SWA_KV_WIDTH), prv(kcol)),
            pl.BlockSpec((rows, SWA_KV_WIDTH), cur(kcol)),
            pl.BlockSpec((WINDOW, SWA_KV_WIDTH), prv(vcol)),
            pl.BlockSpec((rows, SWA_KV_WIDTH), cur(vcol)),
        ],
        out_specs=pl.BlockSpec((rows, SWA_Q_WIDTH), lambda b, n: (b * ns + n, 0)),
        out_shape=jax.ShapeDtypeStruct((n_b * t_len, SWA_Q_WIDTH), BF16),
        compiler_params=_params("arbitrary", "arbitrary"),
        name="swa_prompt",
    )(sinks, p_att, p_att, p_att, p_att, p_att)


def _swa_step_kernel(sink_ref, q_ref, kn_ref, vn_ref, ck_ref, cv_ref, o_ref, ok_ref, ov_ref, *, n_seq, t_len):
    T = t_len
    wb = ck_ref.shape[2]
    t_row = _mod_pow2(lax.broadcasted_iota(jnp.int32, (SWA_GROUP * T, 1), 0), T)
    grp = _div_pow2(lax.broadcasted_iota(jnp.int32, (SWA_GROUP * T, 1), 0), T)
    jc = lax.broadcasted_iota(jnp.int32, (SWA_GROUP * T, wb), 1)
    jn = lax.broadcasted_iota(jnp.int32, (SWA_GROUP * T, T), 1)
    ok_c = (t_row + wb - jc < WINDOW)
    ok_n = (jn <= t_row)
    ks = [slice(hk * SWA_HEAD_DIM, (hk + 1) * SWA_HEAD_DIM) for hk in range(SWA_KV_HEADS)]
    sinks = [_group_sinks(sink_ref, hk, grp) for hk in range(SWA_KV_HEADS)]
    work = [(s, hk) for s in range(n_seq) for hk in range(SWA_KV_HEADS)]
    rows = [slice(s * T, (s + 1) * T) for s in range(n_seq)]
    lane_w = lax.broadcasted_iota(jnp.int32, (1, wb), 1)
    pad = jnp.zeros((wb - T, SWA_KV_WIDTH), F32)
    knb, vnb, ckb, cvb = [], [], [], []
    for s in range(n_seq):
        kn = kn_ref[rows[s], :]
        vn = vn_ref[rows[s], :]
        ck = ck_ref[s]
        cv = cv_ref[s]
        kn_t = jnp.transpose(jnp.concatenate([pad, kn], axis=0))
        vn_t = jnp.transpose(jnp.concatenate([pad, vn], axis=0))
        ok_ref[s] = jnp.where(lane_w >= wb - T, kn_t, pltpu.roll(ck, wb - T, axis=1))
        ov_ref[s] = jnp.where(lane_w >= wb - T, vn_t, pltpu.roll(cv, wb - T, axis=1))
        knb.append(kn.astype(BF16))
        vnb.append(vn.astype(BF16))
        ckb.append(ck.astype(BF16))
        cvb.append(cv.astype(BF16))
    qs = [q_ref[rows[s], :] * SWA_SCALE for s in range(n_seq)]
    qh = {(s, hk): _group_queries(qs[s], hk) for s, hk in work}
    sc = {w: jnp.dot(qh[w], ckb[w[0]][ks[w[1]], :], preferred_element_type=F32) for w in work}
    sn = {w: lax.dot_general(qh[w], knb[w[0]][:, ks[w[1]]], _NT, preferred_element_type=F32) for w in work}
    pr = {w: _sink_probs([jnp.where(ok_c, sc[w], -jnp.inf), jnp.where(ok_n, sn[w], -jnp.inf)], sinks[w[1]])
          for w in work}
    ov = {w: lax.dot_general(pr[w][0], cvb[w[0]][ks[w[1]], :], _NT, preferred_element_type=F32)
          + jnp.dot(pr[w][1], vnb[w[0]][:, ks[w[1]]], preferred_element_type=F32) for w in work}
    for s, hk in work:
        for g in range(SWA_GROUP):
            h = hk * SWA_GROUP + g
            o_ref[rows[s], h * SWA_HEAD_DIM:(h + 1) * SWA_HEAD_DIM] = ov[s, hk][g * T:(g + 1) * T].astype(o_ref.dtype)


def _swa_step(p_att, row0, n_b, t_len, cache_k, cache_v, sinks):
    wb = cache_k.shape[1]
    assert wb == WINDOW and t_len % 8 == 0 and t_len < wb
    n_seq = max(1, STEP_ROWS // t_len)
    while n_b % n_seq:
        n_seq //= 2
    rows = n_seq * t_len
    assert rows % BF16_SUBLANES == 0 and row0 % rows == 0
    blk0 = row0 // rows
    kcol = SWA_Q_WIDTH // SWA_KV_WIDTH
    to_cm = lambda c: jnp.transpose(c, (0, 2, 3, 1)).reshape(n_b, SWA_KV_WIDTH, wb)
    from_cm = lambda c: jnp.transpose(c.reshape(n_b, SWA_KV_HEADS, SWA_HEAD_DIM, wb), (0, 3, 1, 2))
    ck = to_cm(cache_k)
    cv = to_cm(cache_v)
    o, nk, nv = pl.pallas_call(
        functools.partial(_swa_step_kernel, n_seq=n_seq, t_len=t_len),
        grid=(n_b // n_seq,),
        in_specs=[
            pl.BlockSpec(memory_space=pltpu.SMEM),
            pl.BlockSpec((rows, SWA_Q_WIDTH), lambda i: (blk0 + i, 0)),
            pl.BlockSpec((rows, SWA_KV_WIDTH), lambda i: (blk0 + i, kcol)),
            pl.BlockSpec((rows, SWA_KV_WIDTH), lambda i: (blk0 + i, kcol + 1)),
            pl.BlockSpec((n_seq, SWA_KV_WIDTH, wb), lambda i: (i, 0, 0)),
            pl.BlockSpec((n_seq, SWA_KV_WIDTH, wb), lambda i: (i, 0, 0)),
        ],
        out_specs=[
            pl.BlockSpec((rows, SWA_Q_WIDTH), lambda i: (i, 0)),
            pl.BlockSpec((n_seq, SWA_KV_WIDTH, wb), lambda i: (i, 0, 0)),
            pl.BlockSpec((n_seq, SWA_KV_WIDTH, wb), lambda i: (i, 0, 0)),
        ],
        out_shape=[
            jax.ShapeDtypeStruct((n_b * t_len, SWA_Q_WIDTH), BF16),
            jax.ShapeDtypeStruct((n_b, SWA_KV_WIDTH, wb), F32),
            jax.ShapeDtypeStruct((n_b, SWA_KV_WIDTH, wb), F32),
        ],
        compiler_params=_params("arbitrary"),
        name="swa_step",
    )(sinks, p_att, p_att, p_att, ck, cv)
    return o, from_cm(nk), from_cm(nv)


def _softmax(s):
    m = jnp.max(s, axis=-1, keepdims=True)
    e = jnp.exp(s - m)
    return (e / jnp.sum(e, axis=-1, keepdims=True)).astype(BF16)


def _mem_attend(items, o_ref):
    scale = MEM_HEAD_DIM ** -0.5
    sc = [lax.dot_general(q, k, _NT, preferred_element_type=F32) * scale for q, k, _, _, _ in items]
    pr = [_softmax(s) for s in sc]
    ov = [jnp.dot(p, it[2], preferred_element_type=F32) for p, it in zip(pr, items)]
    for o, it in zip(ov, items):
        o_ref[it[3], it[4]] = o.astype(o_ref.dtype)


def _mem_prompt_kernel(qa_ref, qb_ref, k_ref, v_ref, o_ref):
    kb = k_ref[...].astype(BF16)
    vb = v_ref[...].astype(BF16)
    half = MEM_HEADS // 2
    items = []
    for h in range(MEM_HEADS):
        q_ref = qa_ref if h < half else qb_ref
        hs = slice((h % half) * MEM_HEAD_DIM, (h % half + 1) * MEM_HEAD_DIM)
        ms = slice(h * MEM_HEAD_DIM, (h + 1) * MEM_HEAD_DIM)
        items.append((q_ref[:, hs].astype(BF16), kb[:, ms], vb[:, ms], slice(None), ms))
    _mem_attend(items, o_ref)


def _mem_prompt(p_att, n_b, t_len, mem_kv):
    m_tok = mem_kv.shape[0] // n_b
    tq = math.gcd(t_len, ROW_TILE)
    nq = t_len // tq
    half_w = MEM_WIDTH // 2
    qcol = (SWA_Q_WIDTH + 2 * SWA_KV_WIDTH) // half_w
    return pl.pallas_call(
        _mem_prompt_kernel,
        grid=(n_b, nq),
        in_specs=[
            pl.BlockSpec((tq, half_w), lambda b, n: (b * nq + n, qcol)),
            pl.BlockSpec((tq, half_w), lambda b, n: (b * nq + n, qcol + 1)),
            pl.BlockSpec((m_tok, MEM_WIDTH), lambda b, n: (b, 0)),
            pl.BlockSpec((m_tok, MEM_WIDTH), lambda b, n: (b, 1)),
        ],
        out_specs=pl.BlockSpec((tq, MEM_WIDTH), lambda b, n: (b * nq + n, 0)),
        out_shape=jax.ShapeDtypeStruct((n_b * t_len, MEM_WIDTH), BF16),
        compiler_params=_params("arbitrary", "arbitrary"),
        name="mem_prompt",
    )(p_att, p_att, mem_kv, mem_kv)


def _mem_step_kernel(qa_ref, qb_ref, k_ref, v_ref, o_ref, *, n_seq, t_len):
    half = MEM_HEADS // 2
    m_tok = k_ref.shape[1] // MEM_HEADS
    items = []
    for s in range(n_seq):
        rs = slice(s * t_len, (s + 1) * t_len)
        for h in range(MEM_HEADS):
            q_ref = qa_ref if h < half else qb_ref
            hs = slice((h % half) * MEM_HEAD_DIM, (h % half + 1) * MEM_HEAD_DIM)
            ms = slice(h * MEM_HEAD_DIM, (h + 1) * MEM_HEAD_DIM)
            hrows = pl.ds(h, m_tok, stride=MEM_HEADS)
            items.append((q_ref[rs, hs].astype(BF16), k_ref[s, hrows, :].astype(BF16),
                          v_ref[s, hrows, :].astype(BF16), rs, ms))
    _mem_attend(items, o_ref)


def _mem_step(p_att, row0, n_b, t_len, cache_k, cache_v):
    m_tok = cache_k.shape[1]
    n_seq = max(1, STEP_ROWS // t_len)
    while n_b % n_seq:
        n_seq //= 2
    rows = n_seq * t_len
    assert rows % BF16_SUBLANES == 0 and row0 % rows == 0
    blk0 = row0 // rows
    half_w = MEM_WIDTH // 2
    qcol = (SWA_Q_WIDTH + 2 * SWA_KV_WIDTH) // half_w
    cache_block = (n_seq, m_tok * MEM_HEADS, MEM_HEAD_DIM)
    ck = cache_k.reshape(n_b, m_tok * MEM_HEADS, MEM_HEAD_DIM)
    cv = cache_v.reshape(n_b, m_tok * MEM_HEADS, MEM_HEAD_DIM)
    return pl.pallas_call(
        functools.partial(_mem_step_kernel, n_seq=n_seq, t_len=t_len),
        grid=(n_b // n_seq,),
        in_specs=[
            pl.BlockSpec((rows, half_w), lambda i: (blk0 + i, qcol)),
            pl.BlockSpec((rows, half_w), lambda i: (blk0 + i, qcol + 1)),
            pl.BlockSpec(cache_block, lambda i: (i, 0, 0)),
            pl.BlockSpec(cache_block, lambda i: (i, 0, 0)),
        ],
        out_specs=pl.BlockSpec((rows, MEM_WIDTH), lambda i: (i, 0)),
        out_shape=jax.ShapeDtypeStruct((n_b * t_len, MEM_WIDTH), BF16),
        compiler_params=_params("arbitrary"),
        name="mem_step",
    )(p_att, p_att, ck, cv)


def _merge_kernel(xn_ref, rwp_ref, rws_ref, swp_ref, sws_ref, mmp_ref, mms_ref,
                  wg0_ref, wg1_ref, wg2_ref, wb0_ref, wb1_ref, wb2_ref, o_ref,
                  g0_scr, g1_scr, g2_scr, b0_scr, b1_scr, b2_scr, *, n_p):
    i = pl.program_id(1)

    @pl.when(i == 0)
    def _():
        for src, dst in ((wg0_ref, g0_scr), (wg1_ref, g1_scr), (wg2_ref, g2_scr),
                         (wb0_ref, b0_scr), (wb1_ref, b1_scr), (wb2_ref, b2_scr)):
            dst[...] = src[...].astype(BF16)

    def run(o_rw_ref, o_sw_ref, o_mm_ref):
        xn = xn_ref[...]
        acc = None
        for g_scr, b_scr, br_ref in ((g0_scr, b0_scr, o_rw_ref), (g1_scr, b1_scr, o_sw_ref),
                                     (g2_scr, b2_scr, o_mm_ref)):
            gate = _sigmoid(jnp.dot(xn, g_scr[...], preferred_element_type=F32))
            t = gate * jnp.dot(br_ref[...], b_scr[...], preferred_element_type=F32)
            acc = t if acc is None else acc + t
        o_ref[...] = acc.astype(o_ref.dtype)

    @pl.when(i < n_p)
    def _():
        run(rwp_ref, swp_ref, mmp_ref)

    @pl.when(i >= n_p)
    def _():
        run(rws_ref, sws_ref, mms_ref)


def _merge(xn, o_p, o_s, w_in, gate_col0, w_br, n_rows_p):
    m, d = xn.shape
    tn = MXU_COLS
    tm = _row_tile(n_rows_p, m - n_rows_p, ROW_TILE)
    n_p = n_rows_p // tm
    n_i = m // tm
    assert gate_col0 % tn == 0 and d % tn == 0
    g0 = gate_col0 // tn
    nj = d // tn
    p_spec = lambda w: pl.BlockSpec((tm, w), lambda j, i: (jnp.minimum(i, n_p - 1), 0))
    s_spec = lambda w: pl.BlockSpec((tm, w), lambda j, i: (jnp.maximum(i - n_p, 0), 0))
    in_specs = [pl.BlockSpec((tm, d), lambda j, i: (i, 0))]
    args = [xn]
    for op, os_ in zip(o_p, o_s):
        in_specs += [p_spec(op.shape[1]), s_spec(os_.shape[1])]
        args += [op, os_]
    for br in range(N_BRANCH):
        in_specs.append(pl.BlockSpec((d, tn), lambda j, i, br=br: (0, g0 + br * nj + j)))
        args.append(w_in)
    for w in w_br:
        in_specs.append(pl.BlockSpec((w.shape[0], tn), lambda j, i: (0, j)))
        args.append(w)
    scratch = [pltpu.VMEM((d, tn), BF16) for _ in range(N_BRANCH)]
    scratch += [pltpu.VMEM((w.shape[0], tn), BF16) for w in w_br]
    return pl.pallas_call(
        functools.partial(_merge_kernel, n_p=n_p),
        grid=(nj, n_i),
        in_specs=in_specs,
        out_specs=pl.BlockSpec((tm, tn), lambda j, i: (i, j)),
        out_shape=jax.ShapeDtypeStruct((m, d), BF16),
        scratch_shapes=scratch,
        compiler_params=_params("arbitrary", "arbitrary"),
        name="gated_merge",
    )(*args)


def _cast_kernel(x_ref, o_ref):
    o_ref[...] = x_ref[...].astype(o_ref.dtype)


def _to_bf16(w, name):
    r, c = w.shape
    tr = math.gcd(r, NORM_ROW_TILE)
    return pl.pallas_call(
        _cast_kernel,
        grid=(r // tr,),
        in_specs=[pl.BlockSpec((tr, c), lambda i: (i, 0))],
        out_specs=pl.BlockSpec((tr, c), lambda i: (i, 0)),
        out_shape=jax.ShapeDtypeStruct((r, c), BF16),
        compiler_params=_params("arbitrary"),
        name=name,
    )(w)


def _wo_kernel(m_ref, w_ref, xp_ref, xs_ref, g1_ref, g2_ref, h_ref, hn_ref, *, n_p):
    i = pl.program_id(0)
    tm = m_ref.shape[0]
    halves = [slice(0, tm // 2), slice(tm // 2, tm)] if tm % (2 * BF16_SUBLANES) == 0 else [slice(0, tm)]

    def run(x_ref):
        for rs in halves:
            f = jnp.dot(m_ref[rs, :], w_ref[...], preferred_element_type=F32)
            h = x_ref[rs, :] + _rms(f, g1_ref[...])
            h_ref[rs, :] = h
            hn_ref[rs, :] = _rms(h, g2_ref[...]).astype(hn_ref.dtype)

    @pl.when(i < n_p)
    def _():
        run(xp_ref)

    @pl.when(i >= n_p)
    def _():
        run(xs_ref)


def _wo_block(merged, w_o, xp, xs, g_post, g_pre_ffn):
    m, d = merged.shape
    n_rows_p = xp.shape[0]
    tm = _row_tile(n_rows_p, m - n_rows_p, NORM_ROW_TILE)
    n_p = n_rows_p // tm
    return pl.pallas_call(
        functools.partial(_wo_kernel, n_p=n_p),
        grid=(m // tm,),
        in_specs=[
            pl.BlockSpec((tm, d), lambda i: (i, 0)),
            pl.BlockSpec((d, d), lambda i: (0, 0), pipeline_mode=pl.Buffered(1)),
            pl.BlockSpec((tm, d), lambda i: (jnp.minimum(i, n_p - 1), 0)),
            pl.BlockSpec((tm, d), lambda i: (jnp.maximum(i - n_p, 0), 0)),
            pl.BlockSpec((1, d), lambda i: (0, 0)),
            pl.BlockSpec((1, d), lambda i: (0, 0)),
        ],
        out_specs=[pl.BlockSpec((tm, d), lambda i: (i, 0)), pl.BlockSpec((tm, d), lambda i: (i, 0))],
        out_shape=[jax.ShapeDtypeStruct((m, d), F32), jax.ShapeDtypeStruct((m, d), BF16)],
        compiler_params=_params("arbitrary"),
        name="wo_residual",
    )(merged, _to_bf16(w_o, "cast_w_o"), xp, xs, g_post.reshape(1, d), g_pre_ffn.reshape(1, d))


def _gelu_tanh(x):
    return 0.5 * x * (1.0 + jnp.tanh(math.sqrt(2.0 / math.pi) * (x + 0.044715 * (x * x * x))))


def _ffn_up_kernel(hp_ref, hs_ref, wg_ref, wv_ref, cw_ref, cb_ref, st_ref, ap_ref, as_ref, cp_ref, cs_ref,
                   wgb_scr, wvb_scr, *, n_p, t_s):
    i = pl.program_id(1)

    @pl.when(i == 0)
    def _():
        wgb_scr[...] = wg_ref[...].astype(BF16)
        wvb_scr[...] = wv_ref[...].astype(BF16)

    tn = wgb_scr.shape[1]
    subs = [slice(c0, c0 + MXU_COLS) for c0 in range(0, tn, MXU_COLS)] if tn % MXU_COLS == 0 else [slice(0, tn)]

    def row_pieces(rows):
        size = NORM_ROW_TILE if rows % NORM_ROW_TILE == 0 and NORM_ROW_TILE % (8 * t_s) == 0 else rows
        return [(r0, r0 + size) for r0 in range(0, rows, size)]

    def conv_glu(cs, zg, zm1, zm2, zv):
        conv = cb_ref[:, cs] + cw_ref[0:1, cs] * zm2 + cw_ref[1:2, cs] * zm1 + cw_ref[2:3, cs] * zg
        return _gelu_tanh(conv) * zv

    def run(h_ref, epilogue):
        items = [(pc, cs) for cs in subs for pc in row_pieces(h_ref.shape[0])]

        def gate_val(pc, cs):
            hn = h_ref[slice(*pc), :]
            return (jnp.dot(hn, wgb_scr[:, cs], preferred_element_type=F32),
                    jnp.dot(hn, wvb_scr[:, cs], preferred_element_type=F32))

        z = {items[0]: gate_val(*items[0])}
        carry = None
        for n, (pc, cs) in enumerate(items):
            if n + 1 < len(items):
                z[items[n + 1]] = gate_val(*items[n + 1])
            zg, zv = z.pop((pc, cs))
            carry = epilogue(pc, cs, zg, zv, carry if pc[0] else None)

    def prompt_epilogue(pc, cs, zg, zv, carry):
        tr = pc[1] - pc[0]
        p2, p1 = carry if carry is not None else (jnp.zeros((1, zg.shape[1]), F32),) * 2
        row = lax.broadcasted_iota(jnp.int32, (tr, 1), 0)
        zm1 = jnp.where(row == 0, p1, pltpu.roll(zg, 1, axis=0))
        zm2 = jnp.where(row == 0, p2, jnp.where(row == 1, p1, pltpu.roll(zg, 2, axis=0)))
        ap_ref[slice(*pc), cs] = conv_glu(cs, zg, zm1, zm2, zv).astype(ap_ref.dtype)
        if pc[1] == ap_ref.shape[0]:
            cp_ref[0, :, cs] = zg[tr - 2:tr, :]
        return zg[tr - 2:tr - 1, :], zg[tr - 1:tr, :]

    def sample_epilogue(pc, cs, zg, zv, carry):
        tr = pc[1] - pc[0]
        nq = tr // t_s
        qs = slice(pc[0] // t_s, pc[1] // t_s)
        tc = zg.shape[1]
        t = lax.broadcasted_iota(jnp.int32, (1, t_s, 1), 1)
        z3 = zg.reshape(nq, t_s, tc)
        s1 = st_ref[qs, 1:2, cs]
        s0 = st_ref[qs, 0:1, cs]
        zm1 = jnp.where(t == 0, s1, pltpu.roll(z3, 1, axis=1))
        zm2 = jnp.where(t == 0, s0, jnp.where(t == 1, s1, pltpu.roll(z3, 2, axis=1)))
        act = conv_glu(cs, z3, zm1, zm2, zv.reshape(nq, t_s, tc))
        as_ref[slice(*pc), cs] = act.reshape(tr, tc).astype(as_ref.dtype)
        cs_ref[qs, :, cs] = z3[:, t_s - 2:t_s, :]
        return None

    @pl.when(i < n_p)
    def _():
        run(hp_ref, prompt_epilogue)

    @pl.when(i >= n_p)
    def _():
        run(hs_ref, sample_epilogue)


def _ffn_up(hn, w_up, conv_w, conv_b, conv_state, n_b_p, t_p, n_b_s, t_s):
    m, d = hn.shape
    d_ff = conv_b.shape[0]
    n_rows_p = n_b_p * t_p
    n_rows_s = n_b_s * t_s
    assert m == n_rows_p + n_rows_s and n_rows_p % n_rows_s == 0 and t_s % 8 == 0 and t_s >= 2
    assert t_p % BF16_SUBLANES == 0 and n_rows_s % BF16_SUBLANES == 0
    tn = math.gcd(d_ff, FFN_TILE)
    nj = d_ff // tn
    last = n_b_p - 1
    s_blk = n_rows_p // n_rows_s
    return pl.pallas_call(
        functools.partial(_ffn_up_kernel, n_p=n_b_p, t_s=t_s),
        grid=(nj, n_b_p + 1),
        in_specs=[
            pl.BlockSpec((t_p, d), lambda j, i: (jnp.minimum(i, last), 0)),
            pl.BlockSpec((n_rows_s, d), lambda j, i: (s_blk, 0)),
            pl.BlockSpec((d, tn), lambda j, i: (0, j)),
            pl.BlockSpec((d, tn), lambda j, i: (0, nj + j)),
            pl.BlockSpec((CONV_W, tn), lambda j, i: (0, j)),
            pl.BlockSpec((1, tn), lambda j, i: (0, j)),
            pl.BlockSpec((n_b_s, CONV_W - 1, tn), lambda j, i: (0, 0, j)),
        ],
        out_specs=[
            pl.BlockSpec((t_p, tn), lambda j, i: (jnp.minimum(i, last), j)),
            pl.BlockSpec((n_rows_s, tn), lambda j, i: (0, j)),
            pl.BlockSpec((1, CONV_W - 1, tn), lambda j, i: (jnp.minimum(i, last), 0, j)),
            pl.BlockSpec((n_b_s, CONV_W - 1, tn), lambda j, i: (0, 0, j)),
        ],
        out_shape=[
            jax.ShapeDtypeStruct((n_rows_p, d_ff), BF16),
            jax.ShapeDtypeStruct((n_rows_s, d_ff), BF16),
            jax.ShapeDtypeStruct((n_b_p, CONV_W - 1, d_ff), F32),
            jax.ShapeDtypeStruct((n_b_s, CONV_W - 1, d_ff), F32),
        ],
        scratch_shapes=[pltpu.VMEM((d, tn), BF16), pltpu.VMEM((d, tn), BF16)],
        compiler_params=_params("arbitrary", "arbitrary"),
        name="ffn_up_convglu",
    )(hn, hn, w_up, w_up, conv_w, conv_b.reshape(1, d_ff), conv_state)


def _ffn_down_kernel(a_ref, w_ref, h_ref, g_ref, y_ref):
    kk = pl.program_id(1)

    @pl.when(kk == 0)
    def _():
        y_ref[...] = jnp.zeros(y_ref.shape, F32)

    last = kk == pl.num_programs(1) - 1

    @pl.when(jnp.logical_not(last))
    def _():
        y_ref[...] += jnp.dot(a_ref[...], w_ref[...].astype(BF16), preferred_element_type=F32)

    @pl.when(last)
    def _():
        tm = y_ref.shape[0]
        n_pieces = 4 if tm % (4 * BF16_SUBLANES) == 0 else 1
        tr = tm // n_pieces
        wb = w_ref[...].astype(BF16)
        for r in range(n_pieces):
            rs = slice(r * tr, (r + 1) * tr)
            f = y_ref[rs, :] + jnp.dot(a_ref[rs, :], wb, preferred_element_type=F32)
            y_ref[rs, :] = h_ref[rs, :] + _rms(f, g_ref[...])


def _ffn_down(act, w_down, h, g, row0):
    d_ff, d = w_down.shape
    n_rows = act.shape[0]
    tm = math.gcd(math.gcd(n_rows, row0) if row0 else n_rows, ROW_TILE)
    tk = math.gcd(d_ff, FFN_TILE)
    i0 = row0 // tm
    return pl.pallas_call(
        _ffn_down_kernel,
        grid=(n_rows // tm, d_ff // tk),
        in_specs=[
            pl.BlockSpec((tm, tk), lambda i, k: (i, k)),
            pl.BlockSpec((tk, d), lambda i, k: (k, 0)),
            pl.BlockSpec((tm, d), lambda i, k: (i0 + i, 0)),
            pl.BlockSpec((1, d), lambda i, k: (0, 0)),
        ],
        out_specs=pl.BlockSpec((tm, d), lambda i, k: (i, 0)),
        out_shape=jax.ShapeDtypeStruct((n_rows, d), F32),
        compiler_params=_params("arbitrary", "arbitrary"),
        name="ffn_down",
    )(act, w_down, h, g.reshape(1, d))


def kernel(x_prompt, x_sample, cache_swa_k, cache_swa_v, cache_mem_k, cache_mem_v, state_rwkv_shift, state_rwkv_wkv, state_ffn_conv, mem_prompt, g_pre_mix, w_in, mu_rwkv, w_decay_up, w0_decay, w_a_up, a0, w_gate_up, k_k, k_a, r_k, ln_x_w, ln_x_b, swa_sinks, g_mem, w_mem_kv, w_br_rwkv, w_br_swa, w_br_mem, w_o, g_post_mix, g_pre_ffn, w_ffn_up, conv_w, conv_b, w_ffn_down, g_post_ffn):
    b_p, t_p, d = x_prompt.shape
    b_s, t_s, _ = x_sample.shape
    n_p, n_s = b_p * t_p, b_s * t_s
    m_tok = mem_prompt.shape[1]
    rw_prm = dict(mu_rwkv=mu_rwkv, w_decay_up=w_decay_up, w0_decay=w0_decay, w_a_up=w_a_up, a0=a0,
                  w_gate_up=w_gate_up, k_k=k_k, k_a=k_a, r_k=r_k, ln_x_w=ln_x_w, ln_x_b=ln_x_b)
    xp = x_prompt.reshape(n_p, d)
    xs = x_sample.reshape(n_s, d)

    xn, p_att = _norm_matmul(xp, xs, g_pre_mix, w_in, RW_PROJ, ATT_WIDTH, "norm_proj_attn")
    p_rw = _matmul(xn, w_in, 0, RW_PROJ, RW_PROJ // 2, ROW_TILE, "proj_rwkv")

    _, mem_kv = _norm_matmul(mem_prompt.reshape(b_p * m_tok, d), None, g_mem, w_mem_kv,
                             0, 2 * MEM_WIDTH, "norm_proj_mem_kv")

    o_rw_p, shift_p, wkv_p = _rwkv(p_rw, 0, b_p, t_p, None, None, rw_prm, RWKV_PARTS)
    o_rw_s, shift_s, wkv_s = _rwkv(p_rw, n_p, b_s, t_s, state_rwkv_shift, state_rwkv_wkv, rw_prm, RWKV_PARTS)

    o_sw_p = _swa_prompt(p_att, b_p, t_p, swa_sinks)
    o_sw_s, swa_k_s, swa_v_s = _swa_step(p_att, n_p, b_s, t_s, cache_swa_k, cache_swa_v, swa_sinks)

    o_mm_p = _mem_prompt(p_att, b_p, t_p, mem_kv)
    o_mm_s = _mem_step(p_att, n_p, b_s, t_s, cache_mem_k, cache_mem_v)

    merged = _merge(xn, (o_rw_p, o_sw_p, o_mm_p), (o_rw_s, o_sw_s, o_mm_s), w_in,
                    RW_PROJ + ATT_WIDTH, (w_br_rwkv, w_br_swa, w_br_mem), n_p)
    h, hn = _wo_block(merged, w_o, xp, xs, g_post_mix, g_pre_ffn)
    act_p, act_s, conv_p, conv_s = _ffn_up(hn, w_ffn_up, conv_w, conv_b, state_ffn_conv, b_p, t_p, b_s, t_s)
    y_p = _ffn_down(act_p, w_ffn_down, h, g_post_ffn, 0)
    y_s = _ffn_down(act_s, w_ffn_down, h, g_post_ffn, n_p)

    keep = min(WINDOW, t_p)
    kv_p = jnp.stack([lax.slice(p_att, ((b + 1) * t_p - keep, SWA_Q_WIDTH),
                                ((b + 1) * t_p, SWA_Q_WIDTH + 2 * SWA_KV_WIDTH)) for b in range(b_p)])
    swa_k_p = kv_p[..., :SWA_KV_WIDTH].reshape(b_p, keep, SWA_KV_HEADS, SWA_HEAD_DIM)
    swa_v_p = kv_p[..., SWA_KV_WIDTH:].reshape(b_p, keep, SWA_KV_HEADS, SWA_HEAD_DIM)
    mem_k_p = mem_kv[:, :MEM_WIDTH].reshape(b_p, m_tok, MEM_HEADS, MEM_HEAD_DIM)
    mem_v_p = mem_kv[:, MEM_WIDTH:].reshape(b_p, m_tok, MEM_HEADS, MEM_HEAD_DIM)
    return (y_p.reshape(b_p, t_p, d), y_s.reshape(b_s, t_s, d), swa_k_p, swa_v_p, mem_k_p, mem_v_p,
            shift_p, wkv_p, conv_p, swa_k_s, swa_v_s, shift_s, wkv_s, conv_s)
```
